```python
import math
import jax, jax.numpy as jnp
from jax import lax
import numpy as np

D_MODEL = 2048
BATCH = 8
SEQ = 4096
DEPTH = 4

CHUNK = 64
MIX_W = D_MODEL // 2
N_BRANCH = 3
LRU_W = MIX_W
LRU_BLOCKS = 16
LRU_BW = LRU_W // LRU_BLOCKS
CONV_W = 4
LRU_C = 8.0
ATT_HEADS = 8
ATT_HD = MIX_W // ATT_HEADS
ATT_LEFT_CHUNKS = 8
ATT_BAND = (ATT_LEFT_CHUNKS + 1) * CHUNK
MAX_REL = 128
N_REL = 2 * MAX_REL + 1
SSM_W = MIX_W
SSM_GROUP = 16
SSM_G = SSM_W // SSM_GROUP
SSM_P = 64
FFN_HIDDEN = -(-8 * D_MODEL // (3 * 256)) * 256
IN_SPLITS = [LRU_W, LRU_W, MIX_W, MIX_W, MIX_W, SSM_W]
IN_W = sum(IN_SPLITS) + N_BRANCH * D_MODEL
NORM_EPS = 1e-6
MASK_VALUE = -1e30

kernel_name = "hybrid_gated_lru_chunkattn_s5_encoder"


def rmsnorm(x, g):
    xf = x.astype(jnp.float32)
    xf = xf * lax.rsqrt(jnp.mean(xf * xf, axis=-1, keepdims=True) + NORM_EPS)
    return xf.astype(x.dtype) * g


def causal_dwconv(x, w, b):
    s = x.shape[1]
    xp = jnp.pad(x, ((0, 0), (CONV_W - 1, 0), (0, 0)))
    y = b
    for k in range(CONV_W):
        y = y + xp[:, k:k + s] * w[k]
    return y


def _lin_op(e1, e2):
    a1, b1 = e1
    a2, b2 = e2
    return a1 * a2, a2 * b1 + b2


def _complex_lin_op(e1, e2):
    ar1, ai1, br1, bi1 = e1
    ar2, ai2, br2, bi2 = e2
    return (ar2 * ar1 - ai2 * ai1,
            ar2 * ai1 + ai2 * ar1,
            ar2 * br1 - ai2 * bi1 + br2,
            ar2 * bi1 + ai2 * br1 + bi2)


def rg_lru_branch(xin, gate_in, conv_w, conv_b, wa, ba, wx, bx, lam):
    b_, s_, _ = xin.shape
    xc = causal_dwconv(xin, conv_w, conv_b).astype(jnp.float32)
    xb = xc.reshape(b_, s_, LRU_BLOCKS, LRU_BW)
    r = jax.nn.sigmoid(jnp.einsum("bsnk,nkj->bsnj", xb, wa.astype(jnp.float32)).reshape(b_, s_, LRU_W) + ba)
    i = jax.nn.sigmoid(jnp.einsum("bsnk,nkj->bsnj", xb, wx.astype(jnp.float32)).reshape(b_, s_, LRU_W) + bx)
    log_a = -LRU_C * r * jax.nn.softplus(-lam.astype(jnp.float32))
    a = jnp.exp(log_a)
    inp = jnp.sqrt(-jnp.expm1(2.0 * log_a)) * (i * xc)
    h = lax.associative_scan(_lin_op, (a, inp), axis=1)[1]
    return (h * jax.nn.gelu(gate_in.astype(jnp.float32))).astype(xin.dtype)


def chunk_attention_branch(q, k, v, rel_bias):
    b_, s_, _ = q.shape
    n_c = s_ // CHUNK
    qc = q.reshape(b_, n_c, CHUNK, ATT_HEADS, ATT_HD)
    pad = ((0, 0), (ATT_LEFT_CHUNKS, 0), (0, 0), (0, 0), (0, 0))
    kp = jnp.pad(k.reshape(b_, n_c, CHUNK, ATT_HEADS, ATT_HD), pad)
    vp = jnp.pad(v.reshape(b_, n_c, CHUNK, ATT_HEADS, ATT_HD), pad)
    band_idx = jnp.arange(n_c)[:, None] + jnp.arange(ATT_LEFT_CHUNKS + 1)[None, :]
    kb = kp[:, band_idx].reshape(b_, n_c, ATT_BAND, ATT_HEADS, ATT_HD)
    vb = vp[:, band_idx].reshape(b_, n_c, ATT_BAND, ATT_HEADS, ATT_HD)
    scores = jnp.einsum("bcqhd,bckhd->bchqk", qc, kb).astype(jnp.float32) * (ATT_HD ** -0.5)
    q_pos = ATT_LEFT_CHUNKS * CHUNK + jnp.arange(CHUNK)
    k_pos = jnp.arange(ATT_BAND)
    rel = jnp.clip(q_pos[:, None] - k_pos[None, :], -MAX_REL, MAX_REL) + MAX_REL
    bias = rel_bias.astype(jnp.float32)[:, rel]
    key_abs = (jnp.arange(n_c)[:, None] - ATT_LEFT_CHUNKS) * CHUNK + k_pos[None, :]
    valid = key_abs >= 0
    scores = jnp.where(valid[None, :, None, None, :], scores + bias[None, None], MASK_VALUE)
    p = jax.nn.softmax(scores, axis=-1).astype(v.dtype)
    o = jnp.einsum("bchqk,bckhd->bcqhd", p, vb)
    return o.reshape(b_, s_, MIX_W)


def s5_branch(u, a_re, a_im, b_re, b_im, c_re, c_im, d, log_step):
    b_, s_, _ = u.shape
    uf = u.astype(jnp.float32)
    ug = uf.reshape(b_, s_, SSM_G, SSM_GROUP)
    a_re = a_re.astype(jnp.float32)
    a_im = a_im.astype(jnp.float32)
    step = jnp.exp(log_step.astype(jnp.float32))[:, None]
    mag = jnp.exp(a_re * step)
    ang = a_im * step
    lb_re = mag * jnp.cos(ang)
    lb_im = mag * jnp.sin(ang)
    den = a_re * a_re + a_im * a_im
    nr = lb_re - 1.0
    coef_re = (nr * a_re + lb_im * a_im) / den
    coef_im = (lb_im * a_re - nr * a_im) / den
    b_re = b_re.astype(jnp.float32)
    b_im = b_im.astype(jnp.float32)
    bb_re = coef_re[..., None] * b_re - coef_im[..., None] * b_im
    bb_im = coef_re[..., None] * b_im + coef_im[..., None] * b_re
    bu_re = jnp.einsum("bsgh,gph->bsgp", ug, bb_re)
    bu_im = jnp.einsum("bsgh,gph->bsgp", ug, bb_im)
    ar = jnp.broadcast_to(lb_re, bu_re.shape)
    ai = jnp.broadcast_to(lb_im, bu_re.shape)
    _, _, xs_re, xs_im = lax.associative_scan(_complex_lin_op, (ar, ai, bu_re, bu_im), axis=1)
    y = (jnp.einsum("bsgp,ghp->bsgh", xs_re, c_re.astype(jnp.float32))
         - jnp.einsum("bsgp,ghp->bsgh", xs_im, c_im.astype(jnp.float32)))
    y = y.reshape(b_, s_, SSM_W) + d.astype(jnp.float32) * uf
    return y.astype(u.dtype)


def _fwd_setup_inputs(seed: int = 0) -> dict:
    key = jax.random.key(seed)
    ks = jax.random.split(key, 32)
    f32 = jnp.float32

    def nrm(k, shape, scale):
        return jax.random.normal(k, shape, f32) * scale

    u_lam = jax.random.uniform(ks[8], (DEPTH, LRU_W), f32, 0.9, 0.999)
    p_lam = u_lam ** (1.0 / LRU_C)
    lru_lambda = jnp.log(p_lam) - jnp.log1p(-p_lam)
    n_idx = jnp.arange(SSM_P, dtype=f32)
    return {
        "x": nrm(ks[0], (BATCH, SEQ, D_MODEL), 1.0),
        "norm_mix_g": 1.0 + nrm(ks[1], (DEPTH, D_MODEL), 0.01),
        "w_in": nrm(ks[2], (DEPTH, D_MODEL, IN_W), D_MODEL ** -0.5),
        "gate_bias": nrm(ks[3], (DEPTH, N_BRANCH, D_MODEL), 0.01),
        "lru_conv_w": nrm(ks[4], (DEPTH, CONV_W, LRU_W), CONV_W ** -0.5),
        "lru_conv_b": nrm(ks[5], (DEPTH, LRU_W), 0.01),
        "lru_wa": nrm(ks[6], (DEPTH, LRU_BLOCKS, LRU_BW, LRU_BW), LRU_BW ** -0.5),
        "lru_ba": nrm(ks[7], (DEPTH, LRU_W), 0.01),
        "lru_wx": nrm(ks[9], (DEPTH, LRU_BLOCKS, LRU_BW, LRU_BW), LRU_BW ** -0.5),
        "lru_bx": nrm(ks[10], (DEPTH, LRU_W), 0.01),
        "lru_lambda": lru_lambda,
        "attn_rel_bias": nrm(ks[11], (DEPTH, ATT_HEADS, N_REL), 0.1),
        "ssm_a_re": -0.5 + nrm(ks[12], (DEPTH, SSM_G, SSM_P), 0.01),
        "ssm_a_im": math.pi * n_idx + nrm(ks[13], (DEPTH, SSM_G, SSM_P), 0.01),
        "ssm_b_re": nrm(ks[14], (DEPTH, SSM_G, SSM_P, SSM_GROUP), (2 * SSM_GROUP) ** -0.5),
        "ssm_b_im": nrm(ks[15], (DEPTH, SSM_G, SSM_P, SSM_GROUP), (2 * SSM_GROUP) ** -0.5),
        "ssm_c_re": nrm(ks[16], (DEPTH, SSM_G, SSM_GROUP, SSM_P), (2 * SSM_P) ** -0.5),
        "ssm_c_im": nrm(ks[17], (DEPTH, SSM_G, SSM_GROUP, SSM_P), (2 * SSM_P) ** -0.5),
        "ssm_d": nrm(ks[18], (DEPTH, SSM_W), 1.0),
        "ssm_log_step": jax.random.uniform(ks[19], (DEPTH, SSM_G), f32, math.log(1e-3), math.log(1e-1)),
        "ssm_w_glu": nrm(ks[20], (DEPTH, SSM_W, D_MODEL), SSM_W ** -0.5),
        "w_branch": nrm(ks[21], (DEPTH, N_BRANCH, MIX_W, D_MODEL), MIX_W ** -0.5),
        "w_out": nrm(ks[22], (DEPTH, D_MODEL, D_MODEL), D_MODEL ** -0.5),
        "norm_ffn_g": 1.0 + nrm(ks[23], (DEPTH, D_MODEL), 0.01),
        "w_ffn_gate": nrm(ks[24], (DEPTH, D_MODEL, FFN_HIDDEN), D_MODEL ** -0.5),
        "w_ffn_up": nrm(ks[25], (DEPTH, D_MODEL, FFN_HIDDEN), D_MODEL ** -0.5),
        "w_ffn_down": nrm(ks[26], (DEPTH, FFN_HIDDEN, D_MODEL), FFN_HIDDEN ** -0.5),
        "norm_final_g": 1.0 + nrm(ks[27], (D_MODEL,), 0.01),
    }


def _fwd_reference(x, norm_mix_g, w_in, gate_bias, lru_conv_w, lru_conv_b, lru_wa, lru_ba, lru_wx, lru_bx,
              lru_lambda, attn_rel_bias, ssm_a_re, ssm_a_im, ssm_b_re, ssm_b_im, ssm_c_re, ssm_c_im,
              ssm_d, ssm_log_step, ssm_w_glu, w_branch, w_out, norm_ffn_g, w_ffn_gate, w_ffn_up,
              w_ffn_down, norm_final_g):
    b_, s_, _ = x.shape
    split_pts = [int(p) for p in np.cumsum(IN_SPLITS)]
    for l in range(DEPTH):
        h = rmsnorm(x, norm_mix_g[l])
        proj = h @ w_in[l]
        lru_x, lru_gate, q, k, v, ssm_u, gates = jnp.split(proj, split_pts, axis=-1)
        y_a = rg_lru_branch(lru_x, lru_gate, lru_conv_w[l], lru_conv_b[l], lru_wa[l], lru_ba[l],
                            lru_wx[l], lru_bx[l], lru_lambda[l])
        y_b = chunk_attention_branch(q, k, v, attn_rel_bias[l])
        y_c = jax.nn.gelu(s5_branch(ssm_u, ssm_a_re[l], ssm_a_im[l], ssm_b_re[l], ssm_b_im[l],
                                    ssm_c_re[l], ssm_c_im[l], ssm_d[l], ssm_log_step[l]))
        br_a = y_a @ w_branch[l, 0]
        br_b = y_b @ w_branch[l, 1]
        br_c = (y_c @ w_branch[l, 2]) * jax.nn.sigmoid(y_c @ ssm_w_glu[l])
        g = jax.nn.sigmoid(gates.reshape(b_, s_, N_BRANCH, D_MODEL) + gate_bias[l])
        merged = g[:, :, 0] * br_a + g[:, :, 1] * br_b + g[:, :, 2] * br_c
        x = x + merged @ w_out[l]
        h = rmsnorm(x, norm_ffn_g[l])
        x = x + (jax.nn.silu(h @ w_ffn_gate[l]) * (h @ w_ffn_up[l])) @ w_ffn_down[l]
    return rmsnorm(x, norm_final_g)


import jax as _jax
import jax.numpy as _jnp

TWIN_FORMAT = 'train_step'
FWD_PARAMS = ['x', 'norm_mix_g', 'w_in', 'gate_bias', 'lru_conv_w', 'lru_conv_b', 'lru_wa', 'lru_ba', 'lru_wx', 'lru_bx', 'lru_lambda', 'attn_rel_bias', 'ssm_a_re', 'ssm_a_im', 'ssm_b_re', 'ssm_b_im', 'ssm_c_re', 'ssm_c_im', 'ssm_d', 'ssm_log_step', 'ssm_w_glu', 'w_branch', 'w_out', 'norm_ffn_g', 'w_ffn_gate', 'w_ffn_up', 'w_ffn_down', 'norm_final_g']
TWIN_WEIGHTS = ['norm_mix_g', 'w_in', 'gate_bias', 'lru_conv_w', 'lru_conv_b', 'lru_wa', 'lru_ba', 'lru_wx', 'lru_bx', 'lru_lambda', 'attn_rel_bias', 'ssm_a_re', 'ssm_a_im', 'ssm_b_re', 'ssm_b_im', 'ssm_c_re', 'ssm_c_im', 'ssm_d', 'ssm_log_step', 'ssm_w_glu', 'w_branch', 'w_out', 'norm_ffn_g', 'w_ffn_gate', 'w_ffn_up', 'w_ffn_down', 'norm_final_g']
TWIN_DIFF_INPUT = 'x'
TWIN_INPUTS = ['x', 'norm_mix_g', 'w_in', 'gate_bias', 'lru_conv_w', 'lru_conv_b', 'lru_wa', 'lru_ba', 'lru_wx', 'lru_bx', 'lru_lambda', 'attn_rel_bias', 'ssm_a_re', 'ssm_a_im', 'ssm_b_re', 'ssm_b_im', 'ssm_c_re', 'ssm_c_im', 'ssm_d', 'ssm_log_step', 'ssm_w_glu', 'w_branch', 'w_out', 'norm_ffn_g', 'w_ffn_gate', 'w_ffn_up', 'w_ffn_down', 'norm_final_g', 'loss_target', 'm_norm_mix_g', 'm_w_in', 'm_gate_bias', 'm_lru_conv_w', 'm_lru_conv_b', 'm_lru_wa', 'm_lru_ba', 'm_lru_wx', 'm_lru_bx', 'm_lru_lambda', 'm_attn_rel_bias', 'm_ssm_a_re', 'm_ssm_a_im', 'm_ssm_b_re', 'm_ssm_b_im', 'm_ssm_c_re', 'm_ssm_c_im', 'm_ssm_d', 'm_ssm_log_step', 'm_ssm_w_glu', 'm_w_branch', 'm_w_out', 'm_norm_ffn_g', 'm_w_ffn_gate', 'm_w_ffn_up', 'm_w_ffn_down', 'm_norm_final_g', 'v_norm_mix_g', 'v_w_in', 'v_gate_bias', 'v_lru_conv_w', 'v_lru_conv_b', 'v_lru_wa', 'v_lru_ba', 'v_lru_wx', 'v_lru_bx', 'v_lru_lambda', 'v_attn_rel_bias', 'v_ssm_a_re', 'v_ssm_a_im', 'v_ssm_b_re', 'v_ssm_b_im', 'v_ssm_c_re', 'v_ssm_c_im', 'v_ssm_d', 'v_ssm_log_step', 'v_ssm_w_glu', 'v_w_branch', 'v_w_out', 'v_norm_ffn_g', 'v_w_ffn_gate', 'v_w_ffn_up', 'v_w_ffn_down', 'v_norm_final_g']
TWIN_OUTPUTS = ['loss', 'grad_x', 'grad_norm_mix_g', 'grad_w_in', 'grad_gate_bias', 'grad_lru_conv_w', 'grad_lru_conv_b', 'grad_lru_wa', 'grad_lru_ba', 'grad_lru_wx', 'grad_lru_bx', 'grad_lru_lambda', 'grad_attn_rel_bias', 'grad_ssm_a_re', 'grad_ssm_a_im', 'grad_ssm_b_re', 'grad_ssm_b_im', 'grad_ssm_c_re', 'grad_ssm_c_im', 'grad_ssm_d', 'grad_ssm_log_step', 'grad_ssm_w_glu', 'grad_w_branch', 'grad_w_out', 'grad_norm_ffn_g', 'grad_w_ffn_gate', 'grad_w_ffn_up', 'grad_w_ffn_down', 'grad_norm_final_g', 'delta_norm_mix_g', 'delta_w_in', 'delta_gate_bias', 'delta_lru_conv_w', 'delta_lru_conv_b', 'delta_lru_wa', 'delta_lru_ba', 'delta_lru_wx', 'delta_lru_bx', 'delta_lru_lambda', 'delta_attn_rel_bias', 'delta_ssm_a_re', 'delta_ssm_a_im', 'delta_ssm_b_re', 'delta_ssm_b_im', 'delta_ssm_c_re', 'delta_ssm_c_im', 'delta_ssm_d', 'delta_ssm_log_step', 'delta_ssm_w_glu', 'delta_w_branch', 'delta_w_out', 'delta_norm_ffn_g', 'delta_w_ffn_gate', 'delta_w_ffn_up', 'delta_w_ffn_down', 'delta_norm_final_g', 'new_m_norm_mix_g', 'new_m_w_in', 'new_m_gate_bias', 'new_m_lru_conv_w', 'new_m_lru_conv_b', 'new_m_lru_wa', 'new_m_lru_ba', 'new_m_lru_wx', 'new_m_lru_bx', 'new_m_lru_lambda', 'new_m_attn_rel_bias', 'new_m_ssm_a_re', 'new_m_ssm_a_im', 'new_m_ssm_b_re', 'new_m_ssm_b_im', 'new_m_ssm_c_re', 'new_m_ssm_c_im', 'new_m_ssm_d', 'new_m_ssm_log_step', 'new_m_ssm_w_glu', 'new_m_w_branch', 'new_m_w_out', 'new_m_norm_ffn_g', 'new_m_w_ffn_gate', 'new_m_w_ffn_up', 'new_m_w_ffn_down', 'new_m_norm_final_g', 'new_v_norm_mix_g', 'new_v_w_in', 'new_v_gate_bias', 'new_v_lru_conv_w', 'new_v_lru_conv_b', 'new_v_lru_wa', 'new_v_lru_ba', 'new_v_lru_wx', 'new_v_lru_bx', 'new_v_lru_lambda', 'new_v_attn_rel_bias', 'new_v_ssm_a_re', 'new_v_ssm_a_im', 'new_v_ssm_b_re', 'new_v_ssm_b_im', 'new_v_ssm_c_re', 'new_v_ssm_c_im', 'new_v_ssm_d', 'new_v_ssm_log_step', 'new_v_ssm_w_glu', 'new_v_w_branch', 'new_v_w_out', 'new_v_norm_ffn_g', 'new_v_w_ffn_gate', 'new_v_w_ffn_up', 'new_v_w_ffn_down', 'new_v_norm_final_g']
TWIN_LEAF_KINDS = {'loss': 'loss', 'grad_x': 'grad_x', 'grad_norm_mix_g': 'grad_w', 'grad_w_in': 'grad_w', 'grad_gate_bias': 'grad_w', 'grad_lru_conv_w': 'grad_w', 'grad_lru_conv_b': 'grad_w', 'grad_lru_wa': 'grad_w', 'grad_lru_ba': 'grad_w', 'grad_lru_wx': 'grad_w', 'grad_lru_bx': 'grad_w', 'grad_lru_lambda': 'grad_w', 'grad_attn_rel_bias': 'grad_w', 'grad_ssm_a_re': 'grad_w', 'grad_ssm_a_im': 'grad_w', 'grad_ssm_b_re': 'grad_w', 'grad_ssm_b_im': 'grad_w', 'grad_ssm_c_re': 'grad_w', 'grad_ssm_c_im': 'grad_w', 'grad_ssm_d': 'grad_w', 'grad_ssm_log_step': 'grad_w', 'grad_ssm_w_glu': 'grad_w', 'grad_w_branch': 'grad_w', 'grad_w_out': 'grad_w', 'grad_norm_ffn_g': 'grad_w', 'grad_w_ffn_gate': 'grad_w', 'grad_w_ffn_up': 'grad_w', 'grad_w_ffn_down': 'grad_w', 'grad_norm_final_g': 'grad_w', 'delta_norm_mix_g': 'delta_w', 'delta_w_in': 'delta_w', 'delta_gate_bias': 'delta_w', 'delta_lru_conv_w': 'delta_w', 'delta_lru_conv_b': 'delta_w', 'delta_lru_wa': 'delta_w', 'delta_lru_ba': 'delta_w', 'delta_lru_wx': 'delta_w', 'delta_lru_bx': 'delta_w', 'delta_lru_lambda': 'delta_w', 'delta_attn_rel_bias': 'delta_w', 'delta_ssm_a_re': 'delta_w', 'delta_ssm_a_im': 'delta_w', 'delta_ssm_b_re': 'delta_w', 'delta_ssm_b_im': 'delta_w', 'delta_ssm_c_re': 'delta_w', 'delta_ssm_c_im': 'delta_w', 'delta_ssm_d': 'delta_w', 'delta_ssm_log_step': 'delta_w', 'delta_ssm_w_glu': 'delta_w', 'delta_w_branch': 'delta_w', 'delta_w_out': 'delta_w', 'delta_norm_ffn_g': 'delta_w', 'delta_w_ffn_gate': 'delta_w', 'delta_w_ffn_up': 'delta_w', 'delta_w_ffn_down': 'delta_w', 'delta_norm_final_g': 'delta_w', 'new_m_norm_mix_g': 'new_m', 'new_m_w_in': 'new_m', 'new_m_gate_bias': 'new_m', 'new_m_lru_conv_w': 'new_m', 'new_m_lru_conv_b': 'new_m', 'new_m_lru_wa': 'new_m', 'new_m_lru_ba': 'new_m', 'new_m_lru_wx': 'new_m', 'new_m_lru_bx': 'new_m', 'new_m_lru_lambda': 'new_m', 'new_m_attn_rel_bias': 'new_m', 'new_m_ssm_a_re': 'new_m', 'new_m_ssm_a_im': 'new_m', 'new_m_ssm_b_re': 'new_m', 'new_m_ssm_b_im': 'new_m', 'new_m_ssm_c_re': 'new_m', 'new_m_ssm_c_im': 'new_m', 'new_m_ssm_d': 'new_m', 'new_m_ssm_log_step': 'new_m', 'new_m_ssm_w_glu': 'new_m', 'new_m_w_branch': 'new_m', 'new_m_w_out': 'new_m', 'new_m_norm_ffn_g': 'new_m', 'new_m_w_ffn_gate': 'new_m', 'new_m_w_ffn_up': 'new_m', 'new_m_w_ffn_down': 'new_m', 'new_m_norm_final_g': 'new_m', 'new_v_norm_mix_g': 'new_v', 'new_v_w_in': 'new_v', 'new_v_gate_bias': 'new_v', 'new_v_lru_conv_w': 'new_v', 'new_v_lru_conv_b': 'new_v', 'new_v_lru_wa': 'new_v', 'new_v_lru_ba': 'new_v', 'new_v_lru_wx': 'new_v', 'new_v_lru_bx': 'new_v', 'new_v_lru_lambda': 'new_v', 'new_v_attn_rel_bias': 'new_v', 'new_v_ssm_a_re': 'new_v', 'new_v_ssm_a_im': 'new_v', 'new_v_ssm_b_re': 'new_v', 'new_v_ssm_b_im': 'new_v', 'new_v_ssm_c_re': 'new_v', 'new_v_ssm_c_im': 'new_v', 'new_v_ssm_d': 'new_v', 'new_v_ssm_log_step': 'new_v', 'new_v_ssm_w_glu': 'new_v', 'new_v_w_branch': 'new_v', 'new_v_w_out': 'new_v', 'new_v_norm_ffn_g': 'new_v', 'new_v_w_ffn_gate': 'new_v', 'new_v_w_ffn_up': 'new_v', 'new_v_w_ffn_down': 'new_v', 'new_v_norm_final_g': 'new_v'}


def _forward(args):
    return _fwd_reference(*[args[k] for k in FWD_PARAMS])


def _output_shape():
    def fwd():
        inp = _fwd_setup_inputs(0)
        return _fwd_reference(*[inp[k] for k in FWD_PARAMS])
    out = _jax.eval_shape(fwd)
    return out.shape, out.dtype

N_MICROBATCH = 1
ADAM_LR = 0.001
ADAM_B1 = 0.9
ADAM_B2 = 0.999
ADAM_EPS = 1e-08
ADAM_WD = 0.01
ADAM_STEP = 10
PER_EXAMPLE_BATCH_AXIS = {'x': 0, 'loss_target': 0}
SHARED_INPUTS = []
_WEIGHT_DTYPES = {'norm_mix_g': _jnp.float32, 'w_in': _jnp.float32, 'gate_bias': _jnp.float32, 'lru_conv_w': _jnp.float32, 'lru_conv_b': _jnp.float32, 'lru_wa': _jnp.float32, 'lru_ba': _jnp.float32, 'lru_wx': _jnp.float32, 'lru_bx': _jnp.float32, 'lru_lambda': _jnp.float32, 'attn_rel_bias': _jnp.float32, 'ssm_a_re': _jnp.float32, 'ssm_a_im': _jnp.float32, 'ssm_b_re': _jnp.float32, 'ssm_b_im': _jnp.float32, 'ssm_c_re': _jnp.float32, 'ssm_c_im': _jnp.float32, 'ssm_d': _jnp.float32, 'ssm_log_step': _jnp.float32, 'ssm_w_glu': _jnp.float32, 'w_branch': _jnp.float32, 'w_out': _jnp.float32, 'norm_ffn_g': _jnp.float32, 'w_ffn_gate': _jnp.float32, 'w_ffn_up': _jnp.float32, 'w_ffn_down': _jnp.float32, 'norm_final_g': _jnp.float32}
MOMENT_SCALE = {'norm_mix_g': 4.568226e-02, 'w_in': 1.886905e-02, 'gate_bias': 8.787104e-03, 'lru_conv_w': 4.561227e-02, 'lru_conv_b': 3.559302e-01, 'lru_wa': 1.335342e-02, 'lru_ba': 1.079097e-02, 'lru_wx': 2.415140e-02, 'lru_bx': 1.607137e-02, 'lru_lambda': 2.252552e-02, 'attn_rel_bias': 6.891679e-03, 'ssm_a_re': 1.310528e-03, 'ssm_a_im': 1.355793e-03, 'ssm_b_re': 8.507083e-04, 'ssm_b_im': 8.626513e-04, 'ssm_c_re': 1.692690e-03, 'ssm_c_im': 1.711475e-03, 'ssm_d': 3.298180e-02, 'ssm_log_step': 1.035889e+00, 'ssm_w_glu': 5.448543e-03, 'w_branch': 2.209101e-02, 'w_out': 3.810363e-02, 'norm_ffn_g': 6.425399e-02, 'w_ffn_gate': 2.756501e-02, 'w_ffn_up': 2.671594e-02, 'w_ffn_down': 4.428827e-02, 'norm_final_g': 1.600464e+01}


def _to_microbatches(a, axis):
    t = _jnp.moveaxis(a, axis, 0)
    t = t.reshape((N_MICROBATCH, t.shape[0] // N_MICROBATCH) + t.shape[1:])
    return _jnp.moveaxis(t, 1, axis + 1)


def setup_inputs(seed: int = 0) -> dict:
    inp = _fwd_setup_inputs(seed)
    key = _jax.random.fold_in(_jax.random.key(seed), 7919)
    shape, _ = _output_shape()
    out = dict(inp)
    out["loss_target"] = _jax.random.normal(_jax.random.fold_in(key, 0), shape, _jnp.float32)
    for i, name in enumerate(TWIN_WEIGHTS):
        w = inp[name].astype(_jnp.float32)
        if MOMENT_SCALE is None:
            s = _jnp.sqrt(_jnp.mean(_jnp.square(w)) + 1e-30)
        else:
            s = MOMENT_SCALE[name]
        km, kv = _jax.random.split(_jax.random.fold_in(key, i + 1))
        out[name] = w
        out["m_" + name] = s * _jax.random.normal(km, w.shape, _jnp.float32)
        out["v_" + name] = (s * s) * _jax.random.uniform(kv, w.shape, _jnp.float32, 0.5, 1.5)
    if N_MICROBATCH > 1:
        for name, axis in PER_EXAMPLE_BATCH_AXIS.items():
            out[name] = _to_microbatches(out[name], axis)
    return {'x': out['x'], 'norm_mix_g': out['norm_mix_g'], 'w_in': out['w_in'], 'gate_bias': out['gate_bias'], 'lru_conv_w': out['lru_conv_w'], 'lru_conv_b': out['lru_conv_b'], 'lru_wa': out['lru_wa'], 'lru_ba': out['lru_ba'], 'lru_wx': out['lru_wx'], 'lru_bx': out['lru_bx'], 'lru_lambda': out['lru_lambda'], 'attn_rel_bias': out['attn_rel_bias'], 'ssm_a_re': out['ssm_a_re'], 'ssm_a_im': out['ssm_a_im'], 'ssm_b_re': out['ssm_b_re'], 'ssm_b_im': out['ssm_b_im'], 'ssm_c_re': out['ssm_c_re'], 'ssm_c_im': out['ssm_c_im'], 'ssm_d': out['ssm_d'], 'ssm_log_step': out['ssm_log_step'], 'ssm_w_glu': out['ssm_w_glu'], 'w_branch': out['w_branch'], 'w_out': out['w_out'], 'norm_ffn_g': out['norm_ffn_g'], 'w_ffn_gate': out['w_ffn_gate'], 'w_ffn_up': out['w_ffn_up'], 'w_ffn_down': out['w_ffn_down'], 'norm_final_g': out['norm_final_g'], 'loss_target': out['loss_target'], 'm_norm_mix_g': out['m_norm_mix_g'], 'm_w_in': out['m_w_in'], 'm_gate_bias': out['m_gate_bias'], 'm_lru_conv_w': out['m_lru_conv_w'], 'm_lru_conv_b': out['m_lru_conv_b'], 'm_lru_wa': out['m_lru_wa'], 'm_lru_ba': out['m_lru_ba'], 'm_lru_wx': out['m_lru_wx'], 'm_lru_bx': out['m_lru_bx'], 'm_lru_lambda': out['m_lru_lambda'], 'm_attn_rel_bias': out['m_attn_rel_bias'], 'm_ssm_a_re': out['m_ssm_a_re'], 'm_ssm_a_im': out['m_ssm_a_im'], 'm_ssm_b_re': out['m_ssm_b_re'], 'm_ssm_b_im': out['m_ssm_b_im'], 'm_ssm_c_re': out['m_ssm_c_re'], 'm_ssm_c_im': out['m_ssm_c_im'], 'm_ssm_d': out['m_ssm_d'], 'm_ssm_log_step': out['m_ssm_log_step'], 'm_ssm_w_glu': out['m_ssm_w_glu'], 'm_w_branch': out['m_w_branch'], 'm_w_out': out['m_w_out'], 'm_norm_ffn_g': out['m_norm_ffn_g'], 'm_w_ffn_gate': out['m_w_ffn_gate'], 'm_w_ffn_up': out['m_w_ffn_up'], 'm_w_ffn_down': out['m_w_ffn_down'], 'm_norm_final_g': out['m_norm_final_g'], 'v_norm_mix_g': out['v_norm_mix_g'], 'v_w_in': out['v_w_in'], 'v_gate_bias': out['v_gate_bias'], 'v_lru_conv_w': out['v_lru_conv_w'], 'v_lru_conv_b': out['v_lru_conv_b'], 'v_lru_wa': out['v_lru_wa'], 'v_lru_ba': out['v_lru_ba'], 'v_lru_wx': out['v_lru_wx'], 'v_lru_bx': out['v_lru_bx'], 'v_lru_lambda': out['v_lru_lambda'], 'v_attn_rel_bias': out['v_attn_rel_bias'], 'v_ssm_a_re': out['v_ssm_a_re'], 'v_ssm_a_im': out['v_ssm_a_im'], 'v_ssm_b_re': out['v_ssm_b_re'], 'v_ssm_b_im': out['v_ssm_b_im'], 'v_ssm_c_re': out['v_ssm_c_re'], 'v_ssm_c_im': out['v_ssm_c_im'], 'v_ssm_d': out['v_ssm_d'], 'v_ssm_log_step': out['v_ssm_log_step'], 'v_ssm_w_glu': out['v_ssm_w_glu'], 'v_w_branch': out['v_w_branch'], 'v_w_out': out['v_w_out'], 'v_norm_ffn_g': out['v_norm_ffn_g'], 'v_w_ffn_gate': out['v_w_ffn_gate'], 'v_w_ffn_up': out['v_w_ffn_up'], 'v_w_ffn_down': out['v_w_ffn_down'], 'v_norm_final_g': out['v_norm_final_g']}


def _loss(weights, diff, rest, loss_target):
    with _jax.named_scope("forward"):
        args = {**rest, TWIN_DIFF_INPUT: diff, **{k: w.astype(_WEIGHT_DTYPES[k]) for k, w in weights.items()}}
        y = _forward(args)
    with _jax.named_scope("loss_head"):
        err = _jnp.square(y.astype(_jnp.float32) - loss_target)
        return 0.5 * _jnp.sum(_jnp.mean(err, axis=-1)) if err.ndim else 0.5 * err


def _adamw(w, g, m, v):
    m = ADAM_B1 * m + (1.0 - ADAM_B1) * g
    v = ADAM_B2 * v + (1.0 - ADAM_B2) * _jnp.square(g)
    m_hat = m / (1.0 - ADAM_B1 ** ADAM_STEP)
    v_hat = v / (1.0 - ADAM_B2 ** ADAM_STEP)
    delta = -ADAM_LR * (m_hat / (_jnp.sqrt(v_hat) + ADAM_EPS) + ADAM_WD * w)
    return delta, m, v


def reference(x, norm_mix_g, w_in, gate_bias, lru_conv_w, lru_conv_b, lru_wa, lru_ba, lru_wx, lru_bx, lru_lambda, attn_rel_bias, ssm_a_re, ssm_a_im, ssm_b_re, ssm_b_im, ssm_c_re, ssm_c_im, ssm_d, ssm_log_step, ssm_w_glu, w_branch, w_out, norm_ffn_g, w_ffn_gate, w_ffn_up, w_ffn_down, norm_final_g, loss_target, m_norm_mix_g, m_w_in, m_gate_bias, m_lru_conv_w, m_lru_conv_b, m_lru_wa, m_lru_ba, m_lru_wx, m_lru_bx, m_lru_lambda, m_attn_rel_bias, m_ssm_a_re, m_ssm_a_im, m_ssm_b_re, m_ssm_b_im, m_ssm_c_re, m_ssm_c_im, m_ssm_d, m_ssm_log_step, m_ssm_w_glu, m_w_branch, m_w_out, m_norm_ffn_g, m_w_ffn_gate, m_w_ffn_up, m_w_ffn_down, m_norm_final_g, v_norm_mix_g, v_w_in, v_gate_bias, v_lru_conv_w, v_lru_conv_b, v_lru_wa, v_lru_ba, v_lru_wx, v_lru_bx, v_lru_lambda, v_attn_rel_bias, v_ssm_a_re, v_ssm_a_im, v_ssm_b_re, v_ssm_b_im, v_ssm_c_re, v_ssm_c_im, v_ssm_d, v_ssm_log_step, v_ssm_w_glu, v_w_branch, v_w_out, v_norm_ffn_g, v_w_ffn_gate, v_w_ffn_up, v_w_ffn_down, v_norm_final_g):
    given = dict(x=x, norm_mix_g=norm_mix_g, w_in=w_in, gate_bias=gate_bias, lru_conv_w=lru_conv_w, lru_conv_b=lru_conv_b, lru_wa=lru_wa, lru_ba=lru_ba, lru_wx=lru_wx, lru_bx=lru_bx, lru_lambda=lru_lambda, attn_rel_bias=attn_rel_bias, ssm_a_re=ssm_a_re, ssm_a_im=ssm_a_im, ssm_b_re=ssm_b_re, ssm_b_im=ssm_b_im, ssm_c_re=ssm_c_re, ssm_c_im=ssm_c_im, ssm_d=ssm_d, ssm_log_step=ssm_log_step, ssm_w_glu=ssm_w_glu, w_branch=w_branch, w_out=w_out, norm_ffn_g=norm_ffn_g, w_ffn_gate=w_ffn_gate, w_ffn_up=w_ffn_up, w_ffn_down=w_ffn_down, norm_final_g=norm_final_g, loss_target=loss_target, m_norm_mix_g=m_norm_mix_g, m_w_in=m_w_in, m_gate_bias=m_gate_bias, m_lru_conv_w=m_lru_conv_w, m_lru_conv_b=m_lru_conv_b, m_lru_wa=m_lru_wa, m_lru_ba=m_lru_ba, m_lru_wx=m_lru_wx, m_lru_bx=m_lru_bx, m_lru_lambda=m_lru_lambda, m_attn_rel_bias=m_attn_rel_bias, m_ssm_a_re=m_ssm_a_re, m_ssm_a_im=m_ssm_a_im, m_ssm_b_re=m_ssm_b_re, m_ssm_b_im=m_ssm_b_im, m_ssm_c_re=m_ssm_c_re, m_ssm_c_im=m_ssm_c_im, m_ssm_d=m_ssm_d, m_ssm_log_step=m_ssm_log_step, m_ssm_w_glu=m_ssm_w_glu, m_w_branch=m_w_branch, m_w_out=m_w_out, m_norm_ffn_g=m_norm_ffn_g, m_w_ffn_gate=m_w_ffn_gate, m_w_ffn_up=m_w_ffn_up, m_w_ffn_down=m_w_ffn_down, m_norm_final_g=m_norm_final_g, v_norm_mix_g=v_norm_mix_g, v_w_in=v_w_in, v_gate_bias=v_gate_bias, v_lru_conv_w=v_lru_conv_w, v_lru_conv_b=v_lru_conv_b, v_lru_wa=v_lru_wa, v_lru_ba=v_lru_ba, v_lru_wx=v_lru_wx, v_lru_bx=v_lru_bx, v_lru_lambda=v_lru_lambda, v_attn_rel_bias=v_attn_rel_bias, v_ssm_a_re=v_ssm_a_re, v_ssm_a_im=v_ssm_a_im, v_ssm_b_re=v_ssm_b_re, v_ssm_b_im=v_ssm_b_im, v_ssm_c_re=v_ssm_c_re, v_ssm_c_im=v_ssm_c_im, v_ssm_d=v_ssm_d, v_ssm_log_step=v_ssm_log_step, v_ssm_w_glu=v_ssm_w_glu, v_w_branch=v_w_branch, v_w_out=v_w_out, v_norm_ffn_g=v_norm_ffn_g, v_w_ffn_gate=v_w_ffn_gate, v_w_ffn_up=v_w_ffn_up, v_w_ffn_down=v_w_ffn_down, v_norm_final_g=v_norm_final_g)
    weights = {n: given[n] for n in TWIN_WEIGHTS}
    shared = {n: given[n] for n in SHARED_INPUTS}
    per_example = {n: given[n] for n in ['x']}
    grad_fn = _jax.value_and_grad(_loss, argnums=(0, 1))

    def one_microbatch(ex, loss_target):
        ex = dict(ex)
        diff = ex.pop(TWIN_DIFF_INPUT)
        return grad_fn(weights, diff, {**shared, **ex}, loss_target)

    if N_MICROBATCH == 1:
        loss, (grad_w, grad_x) = one_microbatch(per_example, given["loss_target"])
    else:
        def body(carry, xs):
            loss_sum, grad_sum = carry
            l_k, (gw_k, gx_k) = one_microbatch(xs[0], xs[1])
            with _jax.named_scope("update"):
                return (loss_sum + l_k, _jax.tree.map(_jnp.add, grad_sum, gw_k)), gx_k

        init = (_jnp.zeros((), _jnp.float32), _jax.tree.map(_jnp.zeros_like, weights))
        (loss, grad_w), grad_x = _jax.lax.scan(body, init, (per_example, given["loss_target"]))
    with _jax.named_scope("update"):
        delta_w, new_m, new_v = {}, {}, {}
        for n in TWIN_WEIGHTS:
            delta_w[n], new_m[n], new_v[n] = _adamw(weights[n], grad_w[n], given["m_" + n], given["v_" + n])
    return (loss, grad_x, *[grad_w[n] for n in TWIN_WEIGHTS], *[delta_w[n] for n in TWIN_WEIGHTS],
            *[new_m[n] for n in TWIN_WEIGHTS], *[new_v[n] for n in TWIN_WEIGHTS])
```

```python
import functools
import math

import numpy as np
import jax
import jax.numpy as jnp
from jax import lax
from jax.experimental import pallas as pl
from jax.experimental.pallas import tpu as pltpu

F32 = jnp.float32
BF16 = jnp.bfloat16
LANE = 128
NSEG = 8
NDEV = 8
NCHIP = 4
MESH = pl.DeviceIdType.MESH
VMEM_LIMIT = 56 * 1024 * 1024
FLAT_ROWS = 512

NORM_EPS = 1e-6
CHUNK = 64
ATT_LEFT = 8
ATT_BAND = (ATT_LEFT + 1) * CHUNK
MAX_REL = 128
N_REL = 2 * MAX_REL + 1
N_REL_PAD = 384
ATT_HEADS = 8
MASK_VALUE = -1e30
LRU_C = 8.0
LRU_BLOCKS = 16
SSM_GROUP = 16
SSM_P = 64
CONV_W = 4
N_BRANCH = 3

ADAM_LR = 0.001
ADAM_B1 = 0.9
ADAM_B2 = 0.999
ADAM_EPS = 1e-08
ADAM_WD = 0.01
ADAM_STEP = 10

NN = (((1,), (0,)), ((), ()))
NT = (((1,), (1,)), ((), ()))
TN = (((0,), (0,)), ((), ()))


def _cp(sem=None, vmem=VMEM_LIMIT, **kw):
    return pltpu.CompilerParams(dimension_semantics=sem, vmem_limit_bytes=vmem, **kw)


def _dot(a, b, dn=NN):
    return lax.dot_general(a, b, dn, preferred_element_type=F32)


def _gelu(x):
    c = math.sqrt(2.0 / math.pi)
    return 0.5 * x * (1.0 + jnp.tanh(c * (x + 0.044715 * x * x * x)))


def _gelu_grad(x):
    c = math.sqrt(2.0 / math.pi)
    t = jnp.tanh(c * (x + 0.044715 * x * x * x))
    return 0.5 * (1.0 + t) + 0.5 * x * (1.0 - t * t) * c * (1.0 + 3.0 * 0.044715 * x * x)


def _sigmoid(x):
    return 1.0 / (1.0 + jnp.exp(-x))


def _one_minus_exp(z):
    series = -(z * (1.0 + z * (0.5 + z * (1.0 / 6.0 + z * (1.0 / 24.0)))))
    return jnp.where(z > -0.02, series, 1.0 - jnp.exp(z))


def _softplus_neg(lam):
    e = jnp.exp(-jnp.abs(lam))
    series = e * (1.0 - e * (0.5 - e * (1.0 / 3.0 - e * 0.25)))
    log1p_e = jnp.where(e < 0.02, series, jnp.log(1.0 + e))
    return jnp.maximum(-lam, 0.0) + log1p_e


def _mm(name, dn, grid, n_red, ins, in_specs, out_shape, out_spec, acc_shape, epi=None, alias=None):
    n_extra = len(ins) - 2 - (1 if alias is not None else 0)
    red_axes = tuple(range(len(grid) - n_red, len(grid)))
    red_sizes = tuple(grid[r] for r in red_axes)

    def body(*refs):
        a_ref, b_ref = refs[0], refs[1]
        extra = refs[2:2 + n_extra]
        o_ref, acc = refs[-2], refs[-1]
        if n_red:
            first = functools.reduce(jnp.logical_and, [pl.program_id(r) == 0 for r in red_axes])
            last = functools.reduce(jnp.logical_and,
                                    [pl.program_id(r) == n - 1 for r, n in zip(red_axes, red_sizes)])

            @pl.when(first)
            def _():
                acc[...] = jnp.zeros_like(acc)

            acc[...] += _dot(a_ref[...], b_ref[...], dn)

            @pl.when(last)
            def _():
                r = acc[...]
                if epi is not None:
                    r = epi(r, *[e[...] for e in extra])
                o_ref[...] = r.astype(o_ref.dtype)
        else:
            r = _dot(a_ref[...], b_ref[...], dn)
            if epi is not None:
                r = epi(r, *[e[...] for e in extra])
            o_ref[...] = r.astype(o_ref.dtype)

    specs = list(in_specs)
    kw = {}
    if alias is not None:
        specs.append(pl.BlockSpec(memory_space=pl.ANY))
        kw["input_output_aliases"] = {len(ins) - 1: 0}
    sem = ("parallel",) * (len(grid) - n_red) + ("arbitrary",) * n_red
    return pl.pallas_call(
        body, name=name, grid=grid, in_specs=specs, out_specs=out_spec, out_shape=out_shape,
        scratch_shapes=[pltpu.VMEM(acc_shape, F32)], compiler_params=_cp(sem), **kw)(*ins)


def _tile(n, pref):
    if n <= pref:
        return n
    t = pref
    while t >= LANE:
        if n % t == 0 and t % LANE == 0:
            return t
        t -= LANE
    return n


def _rmsnorm_fwd(x, g, name):
    t, d = x.shape
    tm = min(512, t)

    def body(x_ref, g_ref, h_ref):
        xv = x_ref[...]
        r = lax.rsqrt(jnp.mean(xv * xv, axis=-1, keepdims=True) + NORM_EPS)
        h_ref[...] = (xv * r * g_ref[...]).astype(h_ref.dtype)

    return pl.pallas_call(
        body, name=name, grid=(t // tm,),
        in_specs=[pl.BlockSpec((tm, d), lambda i: (i, 0)), pl.BlockSpec((1, d), lambda i: (0, 0))],
        out_specs=pl.BlockSpec((tm, d), lambda i: (i, 0)),
        out_shape=jax.ShapeDtypeStruct((t, d), BF16), compiler_params=_cp(("parallel",)))(x, g)


def _rmsnorm_bwd(x, g, dhs, dres, name):
    t, d = x.shape
    tm = min(256, t)
    n_dh = len(dhs)

    def body(*refs):
        x_ref, g_ref = refs[0], refs[1]
        dh_refs = refs[2:2 + n_dh]
        dres_ref = refs[2 + n_dh]
        dx_ref, dxb_ref, dg_ref = refs[3 + n_dh:]
        xv = x_ref[...]
        r = lax.rsqrt(jnp.mean(xv * xv, axis=-1, keepdims=True) + NORM_EPS)
        xhat = xv * r
        dh = dh_refs[0][...].astype(F32)
        for e in dh_refs[1:]:
            dh = dh + e[...].astype(F32)
        dxh = dh * g_ref[...]
        dx = r * (dxh - xhat * jnp.mean(dxh * xhat, axis=-1, keepdims=True)) + dres_ref[...]
        dx_ref[...] = dx
        dxb_ref[...] = dx.astype(BF16)

        @pl.when(pl.program_id(0) == 0)
        def _():
            dg_ref[...] = jnp.zeros_like(dg_ref)

        dg_ref[...] += jnp.sum(dh * xhat, axis=0, keepdims=True)

    row = pl.BlockSpec((tm, d), lambda i: (i, 0))
    par = pl.BlockSpec((1, d), lambda i: (0, 0))
    return pl.pallas_call(
        body, name=name, grid=(t // tm,),
        in_specs=[row, par] + [row] * n_dh + [row],
        out_specs=[row, row, par],
        out_shape=[jax.ShapeDtypeStruct((t, d), F32), jax.ShapeDtypeStruct((t, d), BF16),
                   jax.ShapeDtypeStruct((1, d), F32)],
        compiler_params=_cp(("arbitrary",)))(x, g, *dhs, dres)


def _loss_head(x, g, target, name):
    t, d = x.shape
    tm = min(256, t)

    def body(x_ref, g_ref, t_ref, loss_ref, dx_ref, dxb_ref, dg_ref):
        xv = x_ref[...]
        r = lax.rsqrt(jnp.mean(xv * xv, axis=-1, keepdims=True) + NORM_EPS)
        xhat = xv * r
        y = xhat * g_ref[...]
        err = y - t_ref[...]
        dy = err * (1.0 / d)
        dxh = dy * g_ref[...]
        dx = r * (dxh - xhat * jnp.mean(dxh * xhat, axis=-1, keepdims=True))
        dx_ref[...] = dx
        dxb_ref[...] = dx.astype(BF16)

        @pl.when(pl.program_id(0) == 0)
        def _():
            dg_ref[...] = jnp.zeros_like(dg_ref)
            loss_ref[...] = jnp.zeros_like(loss_ref)

        dg_ref[...] += jnp.sum(dy * xhat, axis=0, keepdims=True)
        per_tok = jnp.sum(err * err, axis=-1, keepdims=True) * (0.5 / d)
        loss_ref[...] += jnp.sum(per_tok, axis=0, keepdims=True)

    row = pl.BlockSpec((tm, d), lambda i: (i, 0))
    par = pl.BlockSpec((1, d), lambda i: (0, 0))
    return pl.pallas_call(
        body, name=name, grid=(t // tm,),
        in_specs=[row, par, row],
        out_specs=[pl.BlockSpec((1, 1), lambda i: (0, 0)), row, row, par],
        out_shape=[jax.ShapeDtypeStruct((1, 1), F32), jax.ShapeDtypeStruct((t, d), F32),
                   jax.ShapeDtypeStruct((t, d), BF16), jax.ShapeDtypeStruct((1, d), F32)],
        compiler_params=_cp(("arbitrary",)))(x, g, target)


def _merge_fwd(proj, gbias, br_a, br_b, br_c0, gl, d, name):
    t = proj.shape[0]
    tm = min(128, t)
    off = proj.shape[1] // d - N_BRANCH

    def body(g0, g1, g2, gb, a_ref, b_ref, c_ref, gl_ref, o_ref):
        gbv = gb[...]
        s0 = _sigmoid(g0[...].astype(F32) + gbv[0:1])
        s1 = _sigmoid(g1[...].astype(F32) + gbv[1:2])
        s2 = _sigmoid(g2[...].astype(F32) + gbv[2:3])
        brc = c_ref[...].astype(F32) * _sigmoid(gl_ref[...].astype(F32))
        o_ref[...] = (s0 * a_ref[...].astype(F32) + s1 * b_ref[...].astype(F32) + s2 * brc).astype(BF16)

    row = pl.BlockSpec((tm, d), lambda i: (i, 0))
    gs = [pl.BlockSpec((tm, d), functools.partial(lambda i, b: (i, off + b), b=b)) for b in range(N_BRANCH)]
    return pl.pallas_call(
        body, name=name, grid=(t // tm,),
        in_specs=gs + [pl.BlockSpec((N_BRANCH, d), lambda i: (0, 0)), row, row, row, row],
        out_specs=row, out_shape=jax.ShapeDtypeStruct((t, d), BF16),
        compiler_params=_cp(("parallel",)))(proj, proj, proj, gbias, br_a, br_b, br_c0, gl)


def _merge_bwd(proj, gbias, br_a, br_b, br_c0, gl, dmerged, d, name):
    t = proj.shape[0]
    tm = min(128, t)
    off = proj.shape[1] // d - N_BRANCH

    def body(g0, g1, g2, gb, a_ref, b_ref, c_ref, gl_ref, dm_ref, da_ref, db_ref, dc_ref, dgl_ref, dg_ref, dgb_ref):
        gbv = gb[...]
        dm = dm_ref[...].astype(F32)
        s0 = _sigmoid(g0[...].astype(F32) + gbv[0:1])
        s1 = _sigmoid(g1[...].astype(F32) + gbv[1:2])
        s2 = _sigmoid(g2[...].astype(F32) + gbv[2:3])
        sg = _sigmoid(gl_ref[...].astype(F32))
        c0 = c_ref[...].astype(F32)
        brc = c0 * sg
        da_ref[...] = (dm * s0).astype(BF16)
        db_ref[...] = (dm * s1).astype(BF16)
        dbrc = dm * s2
        dc_ref[...] = (dbrc * sg).astype(BF16)
        dgl_ref[...] = (dbrc * c0 * sg * (1.0 - sg)).astype(BF16)
        dp0 = dm * a_ref[...].astype(F32) * s0 * (1.0 - s0)
        dp1 = dm * b_ref[...].astype(F32) * s1 * (1.0 - s1)
        dp2 = dm * brc * s2 * (1.0 - s2)
        dg_ref[:, 0:d] = dp0.astype(BF16)
        dg_ref[:, d:2 * d] = dp1.astype(BF16)
        dg_ref[:, 2 * d:3 * d] = dp2.astype(BF16)

        @pl.when(pl.program_id(0) == 0)
        def _():
            dgb_ref[...] = jnp.zeros_like(dgb_ref)

        dgb_ref[0:1, :] += jnp.sum(dp0, axis=0, keepdims=True)
        dgb_ref[1:2, :] += jnp.sum(dp1, axis=0, keepdims=True)
        dgb_ref[2:3, :] += jnp.sum(dp2, axis=0, keepdims=True)

    row = pl.BlockSpec((tm, d), lambda i: (i, 0))
    gs = [pl.BlockSpec((tm, d), functools.partial(lambda i, b: (i, off + b), b=b)) for b in range(N_BRANCH)]
    par = pl.BlockSpec((N_BRANCH, d), lambda i: (0, 0))
    bf = jax.ShapeDtypeStruct((t, d), BF16)
    return pl.pallas_call(
        body, name=name, grid=(t // tm,),
        in_specs=gs + [par, row, row, row, row, row],
        out_specs=[row, row, row, row, pl.BlockSpec((tm, N_BRANCH * d), lambda i: (i, 0)), par],
        out_shape=[bf, bf, bf, bf, jax.ShapeDtypeStruct((t, N_BRANCH * d), BF16),
                   jax.ShapeDtypeStruct((N_BRANCH, d), F32)],
        compiler_params=_cp(("arbitrary",)))(proj, proj, proj, gbias, br_a, br_b, br_c0, gl, dmerged)


def _swiglu_fwd(gp, up, name):
    s, t, f = gp.shape
    tm = min(512, t)

    def body(g_ref, u_ref, o_ref):
        g = g_ref[...].astype(F32)
        o_ref[...] = (g * _sigmoid(g) * u_ref[...].astype(F32)).astype(BF16)

    blk = pl.BlockSpec((None, tm, f), lambda j, i: (j, i, 0))
    return pl.pallas_call(
        body, name=name, grid=(s, t // tm), in_specs=[blk, blk], out_specs=blk,
        out_shape=jax.ShapeDtypeStruct((s, t, f), BF16), compiler_params=_cp(("parallel", "parallel")))(gp, up)


def _swiglu_bwd(gp, up, dact, name):
    s, t, f = gp.shape
    tm = min(512, t)

    def body(g_ref, u_ref, d_ref, dg_ref, du_ref):
        g = g_ref[...].astype(F32)
        u = u_ref[...].astype(F32)
        da = d_ref[...].astype(F32)
        sg = _sigmoid(g)
        silu = g * sg
        du_ref[...] = (da * silu).astype(BF16)
        dg_ref[...] = (da * u * (sg + silu * (1.0 - sg))).astype(BF16)

    blk = pl.BlockSpec((None, tm, f), lambda j, i: (j, i, 0))
    o = jax.ShapeDtypeStruct((s, t, f), BF16)
    return pl.pallas_call(
        body, name=name, grid=(s, t // tm), in_specs=[blk, blk, blk], out_specs=[blk, blk],
        out_shape=[o, o], compiler_params=_cp(("parallel", "parallel")))(gp, up, dact)


def _gelu_bwd_sum(s5o, dy1, dy2, name):
    t, w = s5o.shape
    tm = min(512, t)

    def body(s_ref, a_ref, b_ref, o_ref):
        o_ref[...] = ((a_ref[...] + b_ref[...]) * _gelu_grad(s_ref[...])).astype(BF16)

    row = pl.BlockSpec((tm, w), lambda i: (i, 0))
    return pl.pallas_call(
        body, name=name, grid=(t // tm,), in_specs=[row, row, row], out_specs=row,
        out_shape=jax.ShapeDtypeStruct((t, w), BF16), compiler_params=_cp(("parallel",)))(s5o, dy1, dy2)


def _lru_gates(xc, wa, ba, wx, bx, sp):
    xcb = xc.astype(BF16)
    r = _sigmoid(_dot(xcb, wa) + ba)
    i = _sigmoid(_dot(xcb, wx) + bx)
    la = -LRU_C * r * sp
    return xcb, r, i, la


def _lru_fwd(proj, cw, cb, wa_t, ba, wx_t, bx, lam, w, name):
    t = proj.shape[0]
    nt = w // LANE
    seg = t // NSEG
    rb = min(512, seg)

    def body(lx_ref, lg_ref, cw_ref, cb_ref, wa_ref, ba_ref, wx_ref, bx_ref, lam_ref, ya_ref, hs_ref, xp, a_s, b_s):
        xp[pl.ds(0, 8), :] = jnp.zeros((8, LANE), F32)
        for r0 in range(0, t, rb):
            xp[pl.ds(8 + r0, rb), :] = lx_ref[pl.ds(r0, rb), :].astype(F32)
        sp = _softplus_neg(lam_ref[...])
        cwv = cw_ref[...]
        for r0 in range(0, t, rb):
            xc = cb_ref[...] + sum(cwv[k:k + 1] * xp[pl.ds(8 + r0 - (CONV_W - 1) + k, rb), :] for k in range(CONV_W))
            _, _, i, la = _lru_gates(xc, wa_ref[...], ba_ref[...], wx_ref[...], bx_ref[...], sp)
            a_s[pl.ds(r0, rb), :] = jnp.exp(la)
            b_s[pl.ds(r0, rb), :] = jnp.sqrt(_one_minus_exp(2.0 * la)) * (i * xc)

        def step(k, c):
            h, p = c
            rows = pl.ds(k, NSEG, stride=seg)
            a = a_s[rows, :]
            h = a * h + b_s[rows, :]
            p = a * p
            b_s[rows, :] = h
            a_s[rows, :] = p
            return h, p

        lax.fori_loop(0, seg, step, (jnp.zeros((NSEG, LANE), F32), jnp.ones((NSEG, LANE), F32)))
        carry = jnp.zeros((1, LANE), F32)
        for s in range(NSEG):
            for r0 in range(s * seg, (s + 1) * seg, rb):
                rows = pl.ds(r0, rb)
                h = b_s[rows, :] + a_s[rows, :] * carry
                hs_ref[rows, :] = h
                ya_ref[rows, :] = (h * _gelu(lg_ref[rows, :].astype(F32))).astype(BF16)
            end = pl.ds((s + 1) * seg - 1, 1)
            carry = b_s[end, :] + a_s[end, :] * carry

    col = lambda c0: pl.BlockSpec((t, LANE), functools.partial(lambda i, c0: (0, c0 + i), c0=c0))
    par = lambda k: pl.BlockSpec((k, LANE), lambda i: (0, i))
    mat = pl.BlockSpec((None, LANE, LANE), lambda i: (i, 0, 0))
    return pl.pallas_call(
        body, name=name, grid=(nt,),
        in_specs=[col(0), col(nt), par(CONV_W), par(1), mat, par(1), mat, par(1), par(1)],
        out_specs=[col(0), col(0)],
        out_shape=[jax.ShapeDtypeStruct((t, w), BF16), jax.ShapeDtypeStruct((t, w), F32)],
        scratch_shapes=[pltpu.VMEM((t + 8, LANE), F32), pltpu.VMEM((t, LANE), F32), pltpu.VMEM((t, LANE), F32)],
        compiler_params=_cp(("parallel",)))(proj, proj, cw, cb, wa_t, ba, wx_t, bx, lam)


def _lru_bwd(proj, hs, dya, cw, cb, wa_t, ba, wx_t, bx, lam, wat_t, wxt_t, w, name):
    t = proj.shape[0]
    nt = w // LANE
    seg = t // NSEG
    rb = min(512, seg)

    def body(lx_ref, lg_ref, hs_ref, dy_ref, cw_ref, cb_ref, wa_ref, ba_ref, wx_ref, bx_ref, lam_ref, wat_ref, wxt_ref,
             dlx_ref, dlg_ref, dcw_ref, dcb_ref, dwa_ref, dba_ref, dwx_ref, dbx_ref, dlam_ref,
             xp, hp, a_s, g_s, q_s, dxc_s):
        z8 = jnp.zeros((8, LANE), F32)
        xp[pl.ds(0, 8), :] = z8
        hp[pl.ds(0, 8), :] = z8
        a_s[pl.ds(t, 8), :] = z8
        dxc_s[pl.ds(t, 8), :] = z8
        lamv = lam_ref[...]
        sp = _softplus_neg(lamv)
        cwv = cw_ref[...]

        def conv(r0):
            return cb_ref[...] + sum(cwv[k:k + 1] * xp[pl.ds(8 + r0 - (CONV_W - 1) + k, rb), :] for k in range(CONV_W))

        for r0 in range(0, t, rb):
            rows = pl.ds(r0, rb)
            xp[pl.ds(8 + r0, rb), :] = lx_ref[rows, :].astype(F32)
            hp[pl.ds(8 + r0, rb), :] = hs_ref[rows, :]
        for r0 in range(0, t, rb):
            rows = pl.ds(r0, rb)
            _, _, _, la = _lru_gates(conv(r0), wa_ref[...], ba_ref[...], wx_ref[...], bx_ref[...], sp)
            a_s[rows, :] = jnp.exp(la)
            g_s[rows, :] = dy_ref[rows, :].astype(F32) * _gelu(lg_ref[rows, :].astype(F32))

        def step(kk, c):
            g, q = c
            k = seg - 1 - kk
            rows = pl.ds(k, NSEG, stride=seg)
            an = a_s[pl.ds(k + 1, NSEG, stride=seg), :]
            g = g_s[rows, :] + an * g
            q = an * q
            g_s[rows, :] = g
            q_s[rows, :] = q
            return g, q

        lax.fori_loop(0, seg, step, (jnp.zeros((NSEG, LANE), F32), jnp.ones((NSEG, LANE), F32)))
        nxt = [None] * NSEG
        carry = jnp.zeros((1, LANE), F32)
        for s in range(NSEG - 1, -1, -1):
            nxt[s] = carry
            start = pl.ds(s * seg, 1)
            carry = g_s[start, :] + q_s[start, :] * carry

        zrow = jnp.zeros((1, LANE), F32)
        dsp = zrow
        dba = zrow
        dbx = zrow
        dcb = zrow
        dwa = jnp.zeros((LANE, LANE), F32)
        dwx = jnp.zeros((LANE, LANE), F32)
        for s in range(NSEG):
            for r0 in range(s * seg, (s + 1) * seg, rb):
                rows = pl.ds(r0, rb)
                g = g_s[rows, :] + q_s[rows, :] * nxt[s]
                xc = conv(r0)
                xcb, r, i, la = _lru_gates(xc, wa_ref[...], ba_ref[...], wx_ref[...], bx_ref[...], sp)
                a = jnp.exp(la)
                om = _one_minus_exp(2.0 * la)
                mult = jnp.sqrt(om)
                hprev = hp[pl.ds(8 + r0 - 1, rb), :]
                da = g * hprev
                dmult = g * i * xc
                di = g * mult * xc
                dxc = g * mult * i
                dla = da * a - dmult * (1.0 - om) / mult
                dr = dla * (-LRU_C) * sp
                dsp = dsp + jnp.sum(dla * (-LRU_C) * r, axis=0, keepdims=True)
                dpr = dr * r * (1.0 - r)
                dpi = di * i * (1.0 - i)
                dba = dba + jnp.sum(dpr, axis=0, keepdims=True)
                dbx = dbx + jnp.sum(dpi, axis=0, keepdims=True)
                dprb = dpr.astype(BF16)
                dpib = dpi.astype(BF16)
                dxc = dxc + _dot(dprb, wat_ref[...]) + _dot(dpib, wxt_ref[...])
                dwa = dwa + _dot(xcb, dprb, TN)
                dwx = dwx + _dot(xcb, dpib, TN)
                dxc_s[rows, :] = dxc
                dcb = dcb + jnp.sum(dxc, axis=0, keepdims=True)
                lg = lg_ref[rows, :].astype(F32)
                dlg_ref[rows, :] = (dy_ref[rows, :].astype(F32) * hs_ref[rows, :] * _gelu_grad(lg)).astype(BF16)
        dcw = [zrow] * CONV_W
        for r0 in range(0, t, rb):
            rows = pl.ds(r0, rb)
            dlx = sum(cwv[k:k + 1] * dxc_s[pl.ds(r0 + (CONV_W - 1) - k, rb), :] for k in range(CONV_W))
            dlx_ref[rows, :] = dlx.astype(BF16)
            dxc = dxc_s[rows, :]
            for k in range(CONV_W):
                dcw[k] = dcw[k] + jnp.sum(dxc * xp[pl.ds(8 + r0 - (CONV_W - 1) + k, rb), :], axis=0, keepdims=True)
        dcw_ref[...] = jnp.concatenate(dcw, axis=0)
        dcb_ref[...] = dcb
        dwa_ref[...] = dwa
        dwx_ref[...] = dwx
        dba_ref[...] = dba
        dbx_ref[...] = dbx
        dlam_ref[...] = dsp * (-_sigmoid(-lamv))

    col = lambda c0: pl.BlockSpec((t, LANE), functools.partial(lambda i, c0: (0, c0 + i), c0=c0))
    par = lambda k: pl.BlockSpec((k, LANE), lambda i: (0, i))
    mat = pl.BlockSpec((None, LANE, LANE), lambda i: (i, 0, 0))
    vec = jax.ShapeDtypeStruct((1, w), F32)
    big = lambda: pltpu.VMEM((t + 8, LANE), F32)
    return pl.pallas_call(
        body, name=name, grid=(nt,),
        in_specs=[col(0), col(nt), col(0), col(0), par(CONV_W), par(1), mat, par(1), mat, par(1), par(1), mat, mat],
        out_specs=[col(0), col(0), par(CONV_W), par(1), mat, par(1), mat, par(1), par(1)],
        out_shape=[jax.ShapeDtypeStruct((t, w), BF16), jax.ShapeDtypeStruct((t, w), BF16),
                   jax.ShapeDtypeStruct((CONV_W, w), F32), vec, jax.ShapeDtypeStruct((nt, LANE, LANE), F32), vec,
                   jax.ShapeDtypeStruct((nt, LANE, LANE), F32), vec, vec],
        scratch_shapes=[big(), big(), big(), pltpu.VMEM((t, LANE), F32), pltpu.VMEM((t, LANE), F32), big()],
        compiler_params=_cp(("parallel",)))(proj, proj, hs, dya, cw, cb, wa_t, ba, wx_t, bx, lam, wat_t, wxt_t)


def _attn_scores(q_ref, kp, bias_ref, c, hd):
    r0 = pl.multiple_of(c * CHUNK, CHUNK)
    qc = q_ref[pl.ds(r0, CHUNK), :]
    kb = kp[pl.ds(r0, ATT_BAND), :]
    s = _dot(qc, kb, NT) * (hd ** -0.5) + bias_ref[...]
    kpos = lax.broadcasted_iota(jnp.int32, (CHUNK, ATT_BAND), 1)
    s = jnp.where(kpos + (c - ATT_LEFT) * CHUNK >= 0, s, MASK_VALUE)
    m = jnp.max(s, axis=-1, keepdims=True)
    e = jnp.exp(s - m)
    p = e / jnp.sum(e, axis=-1, keepdims=True)
    return r0, qc, kb, p


def _attn_pad_copy(src_ref, dst, t, pad):
    dst[pl.ds(0, pad), :] = jnp.zeros((pad, dst.shape[1]), dst.dtype)
    rb = min(512, t)
    for r0 in range(0, t, rb):
        dst[pl.ds(pad + r0, rb), :] = src_ref[pl.ds(r0, rb), :]


def _attn_fwd(proj, bias, w, name):
    t = proj.shape[0]
    hd = w // ATT_HEADS
    pad = ATT_LEFT * CHUNK
    qo, ko, vo = 2 * w // hd, 3 * w // hd, 4 * w // hd

    def body(q_ref, k_ref, v_ref, bias_ref, o_ref, kp, vp):
        _attn_pad_copy(k_ref, kp, t, pad)
        _attn_pad_copy(v_ref, vp, t, pad)

        def chunk(c, _):
            r0, _, _, p = _attn_scores(q_ref, kp, bias_ref, c, hd)
            vb = vp[pl.ds(r0, ATT_BAND), :]
            o_ref[pl.ds(r0, CHUNK), :] = _dot(p.astype(BF16), vb).astype(BF16)
            return 0

        lax.fori_loop(0, t // CHUNK, chunk, 0)

    col = lambda c0: pl.BlockSpec((t, hd), functools.partial(lambda h, c0: (0, c0 + h), c0=c0))
    return pl.pallas_call(
        body, name=name, grid=(ATT_HEADS,),
        in_specs=[col(qo), col(ko), col(vo), pl.BlockSpec((None, CHUNK, ATT_BAND), lambda h: (h, 0, 0))],
        out_specs=col(0), out_shape=jax.ShapeDtypeStruct((t, w), BF16),
        scratch_shapes=[pltpu.VMEM((t + pad, hd), BF16), pltpu.VMEM((t + pad, hd), BF16)],
        compiler_params=_cp(("parallel",)))(proj, proj, proj, bias)


def _attn_bwd(proj, bias, do, w, name):
    t = proj.shape[0]
    hd = w // ATT_HEADS
    pad = ATT_LEFT * CHUNK
    qo, ko, vo = 2 * w // hd, 3 * w // hd, 4 * w // hd

    def body(q_ref, k_ref, v_ref, bias_ref, do_ref, dq_ref, dk_ref, dv_ref, db_ref, kp, vp, dkp, dvp):
        _attn_pad_copy(k_ref, kp, t, pad)
        _attn_pad_copy(v_ref, vp, t, pad)
        rb = min(512, t)
        for r0 in range(0, t + pad, rb):
            n = min(rb, t + pad - r0)
            dkp[pl.ds(r0, n), :] = jnp.zeros((n, hd), F32)
            dvp[pl.ds(r0, n), :] = jnp.zeros((n, hd), F32)
        db_ref[...] = jnp.zeros_like(db_ref)
        scale = hd ** -0.5

        def chunk(c, _):
            r0, qc, kb, p = _attn_scores(q_ref, kp, bias_ref, c, hd)
            band = pl.ds(r0, ATT_BAND)
            vb = vp[band, :]
            doc = do_ref[pl.ds(r0, CHUNK), :]
            dp = _dot(doc, vb, NT)
            ds = p * (dp - jnp.sum(dp * p, axis=-1, keepdims=True))
            db_ref[...] += ds
            dsb = ds.astype(BF16)
            dq_ref[pl.ds(r0, CHUNK), :] = (_dot(dsb, kb) * scale).astype(BF16)
            dkp[band, :] += _dot(dsb, qc, TN) * scale
            dvp[band, :] += _dot(p.astype(BF16), doc, TN)
            return 0

        lax.fori_loop(0, t // CHUNK, chunk, 0)
        for r0 in range(0, t, rb):
            dk_ref[pl.ds(r0, rb), :] = dkp[pl.ds(pad + r0, rb), :].astype(BF16)
            dv_ref[pl.ds(r0, rb), :] = dvp[pl.ds(pad + r0, rb), :].astype(BF16)

    col = lambda c0: pl.BlockSpec((t, hd), functools.partial(lambda h, c0: (0, c0 + h), c0=c0))
    tb = pl.BlockSpec((None, CHUNK, ATT_BAND), lambda h: (h, 0, 0))
    o = jax.ShapeDtypeStruct((t, w), BF16)
    return pl.pallas_call(
        body, name=name, grid=(ATT_HEADS,),
        in_specs=[col(qo), col(ko), col(vo), tb, col(0)],
        out_specs=[col(0), col(0), col(0), tb],
        out_shape=[o, o, o, jax.ShapeDtypeStruct((ATT_HEADS, CHUNK, ATT_BAND), F32)],
        scratch_shapes=[pltpu.VMEM((t + pad, hd), BF16), pltpu.VMEM((t + pad, hd), BF16),
                        pltpu.VMEM((t + pad, hd), F32), pltpu.VMEM((t + pad, hd), F32)],
        compiler_params=_cp(("parallel",)))(proj, proj, proj, bias, do)


def _rel_index():
    q_pos = ATT_LEFT * CHUNK + np.arange(CHUNK)
    k_pos = np.arange(ATT_BAND)
    return (np.clip(q_pos[:, None] - k_pos[None, :], -MAX_REL, MAX_REL) + MAX_REL).astype(np.int32).reshape(-1)


def _bias_expand(rel_bias, name):
    nl, nh, _ = rel_bias.shape
    n = CHUNK * ATT_BAND
    kb = n // 8
    idx = jnp.asarray(_rel_index().reshape(1, n))
    tab = jnp.pad(rel_bias, ((0, 0), (0, 0), (0, N_REL_PAD - N_REL)))

    def body(t_ref, i_ref, o_ref):
        onehot = (lax.broadcasted_iota(jnp.int32, (N_REL_PAD, kb), 0) == i_ref[...]).astype(F32)
        o_ref[...] = lax.dot_general(t_ref[...], onehot, NN, precision=lax.Precision.HIGHEST,
                                     preferred_element_type=F32)

    out = pl.pallas_call(
        body, name=name, grid=(nl, n // kb),
        in_specs=[pl.BlockSpec((None, nh, N_REL_PAD), lambda l, j: (l, 0, 0)), pl.BlockSpec((1, kb), lambda l, j: (0, j))],
        out_specs=pl.BlockSpec((None, nh, kb), lambda l, j: (l, 0, j)),
        out_shape=jax.ShapeDtypeStruct((nl, nh, n), F32), compiler_params=_cp(("parallel", "parallel")))(tab, idx)
    return out.reshape(nl, nh, CHUNK, ATT_BAND)


def _bias_reduce(dbias, name):
    nl, nh = dbias.shape[:2]
    n = CHUNK * ATT_BAND
    kb = n // 8
    idx = jnp.asarray(_rel_index().reshape(n, 1))

    def body(d_ref, i_ref, o_ref):
        @pl.when(pl.program_id(1) == 0)
        def _():
            o_ref[...] = jnp.zeros_like(o_ref)

        onehot = (lax.broadcasted_iota(jnp.int32, (kb, N_REL_PAD), 1) == i_ref[...]).astype(F32)
        o_ref[...] += lax.dot_general(d_ref[...], onehot, NN, precision=lax.Precision.HIGHEST,
                                      preferred_element_type=F32)

    out = pl.pallas_call(
        body, name=name, grid=(nl, n // kb),
        in_specs=[pl.BlockSpec((None, nh, kb), lambda l, j: (l, 0, j)), pl.BlockSpec((kb, 1), lambda l, j: (j, 0))],
        out_specs=pl.BlockSpec((None, nh, N_REL_PAD), lambda l, j: (l, 0, 0)),
        out_shape=jax.ShapeDtypeStruct((nl, nh, N_REL_PAD), F32),
        compiler_params=_cp(("parallel", "arbitrary")))(dbias.reshape(nl, nh, n), idx)
    return out[:, :, :N_REL]


def _s5_prep_math(a_re, a_im, ls, btr, bti):
    step = jnp.exp(ls)
    mag = jnp.exp(a_re * step)
    ang = a_im * step
    lr = mag * jnp.cos(ang)
    li = mag * jnp.sin(ang)
    den = a_re * a_re + a_im * a_im
    nr = lr - 1.0
    cr = (nr * a_re + li * a_im) / den
    ci = (li * a_re - nr * a_im) / den
    bbr = cr[:, None, :] * btr - ci[:, None, :] * bti
    bbi = cr[:, None, :] * bti + ci[:, None, :] * btr
    return lr, li, bbr, bbi


def _s5_prep(a_re, a_im, ls, btr, bti, name):
    nl, g, p = a_re.shape
    h = btr.shape[2]

    def body(ar, ai, l_ref, br, bi, o1, o2, o3, o4):
        r = _s5_prep_math(ar[...], ai[...], l_ref[...], br[...], bi[...])
        o1[...], o2[...], o3[...], o4[...] = r

    m2 = pl.BlockSpec((None, g, p), lambda l: (l, 0, 0))
    m1 = pl.BlockSpec((None, g, 1), lambda l: (l, 0, 0))
    m3 = pl.BlockSpec((None, g, h, p), lambda l: (l, 0, 0, 0))
    s2 = jax.ShapeDtypeStruct((nl, g, p), F32)
    s3 = jax.ShapeDtypeStruct((nl, g, h, p), F32)
    return pl.pallas_call(body, name=name, grid=(nl,), in_specs=[m2, m2, m1, m3, m3], out_specs=[m2, m2, m3, m3],
                          out_shape=[s2, s2, s3, s3], compiler_params=_cp(("parallel",)))(a_re, a_im, ls, btr, bti)


def _s5_prep_bwd(a_re, a_im, ls, btr, bti, dlr, dli, dbbr, dbbi, name):
    nl, g, p = a_re.shape
    h = btr.shape[2]

    def body(ar, ai, l_ref, br, bi, g1, g2, g3, g4, o1, o2, o3, o4, o5):
        _, vjp = jax.vjp(_s5_prep_math, ar[...], ai[...], l_ref[...], br[...], bi[...])
        o1[...], o2[...], o3[...], o4[...], o5[...] = vjp((g1[...], g2[...], g3[...], g4[...]))

    m2 = pl.BlockSpec((None, g, p), lambda l: (l, 0, 0))
    m1 = pl.BlockSpec((None, g, 1), lambda l: (l, 0, 0))
    m3 = pl.BlockSpec((None, g, h, p), lambda l: (l, 0, 0, 0))
    s2 = jax.ShapeDtypeStruct((nl, g, p), F32)
    s1 = jax.ShapeDtypeStruct((nl, g, 1), F32)
    s3 = jax.ShapeDtypeStruct((nl, g, h, p), F32)
    return pl.pallas_call(body, name=name, grid=(nl,), in_specs=[m2, m2, m1, m3, m3, m2, m2, m3, m3],
                          out_specs=[m2, m2, m1, m3, m3], out_shape=[s2, s2, s1, s3, s3],
                          compiler_params=_cp(("parallel",)))(a_re, a_im, ls, btr, bti, dlr, dli, dbbr, dbbi)


def _s5_states(u_ref, bbt_ref, lam_ref, xs, t, off):
    seg = t // NSEG
    rb = min(512, seg)
    nj = xs.shape[0]
    nh = nj // 2
    for r0 in range(0, t, rb):
        bu = _dot(u_ref[pl.ds(r0, rb), :], bbt_ref[...])
        for j in range(nj):
            xs[j, pl.ds(off + r0, rb), :] = bu[:, j * LANE:(j + 1) * LANE]
    lamv = lam_ref[...]
    lr = [jnp.broadcast_to(lamv[j:j + 1], (NSEG, LANE)) for j in range(nh)]
    li = [jnp.broadcast_to(lamv[nh + j:nh + j + 1], (NSEG, LANE)) for j in range(nh)]
    zero = jnp.zeros((NSEG, LANE), F32)
    one = jnp.ones((NSEG, LANE), F32)

    def step(k, c):
        xr, xi, pr, pi = c
        rows = pl.ds(off + k, NSEG, stride=seg)
        nxr, nxi, npr, npi = [], [], [], []
        for j in range(nh):
            r = lr[j] * xr[j] - li[j] * xi[j] + xs.at[j][rows, :]
            i = lr[j] * xi[j] + li[j] * xr[j] + xs.at[nh + j][rows, :]
            xs.at[j][rows, :] = r
            xs.at[nh + j][rows, :] = i
            nxr.append(r)
            nxi.append(i)
            npr.append(lr[j] * pr[j] - li[j] * pi[j])
            npi.append(lr[j] * pi[j] + li[j] * pr[j])
        return tuple(nxr), tuple(nxi), tuple(npr), tuple(npi)

    init = ((zero,) * nh, (zero,) * nh, (one,) * nh, (zero,) * nh)
    _, _, pr, pi = lax.fori_loop(0, seg, step, init)
    ctr, cti = [], []
    for j in range(nh):
        plr, pli = pr[j][0:1], pi[j][0:1]
        cr = jnp.zeros((1, LANE), F32)
        ci = jnp.zeros((1, LANE), F32)
        rows_r, rows_i = [], []
        for s in range(NSEG):
            rows_r.append(cr)
            rows_i.append(ci)
            end = pl.ds(off + (s + 1) * seg - 1, 1)
            er, ei = xs.at[j][end, :], xs.at[nh + j][end, :]
            cr, ci = er + plr * cr - pli * ci, ei + plr * ci + pli * cr
        ctr.append(jnp.concatenate(rows_r, axis=0))
        cti.append(jnp.concatenate(rows_i, axis=0))

    def fix(k, c):
        rr, ri = c
        rows = pl.ds(off + k, NSEG, stride=seg)
        nr, ni = [], []
        for j in range(nh):
            r = lr[j] * rr[j] - li[j] * ri[j]
            i = lr[j] * ri[j] + li[j] * rr[j]
            xs.at[j][rows, :] += r
            xs.at[nh + j][rows, :] += i
            nr.append(r)
            ni.append(i)
        return tuple(nr), tuple(ni)

    lax.fori_loop(0, seg, fix, (tuple(ctr), tuple(cti)))
    return lr, li, pr, pi


def _s5_fwd(proj, bbt, cmat, lam, dvec, w, name):
    t = proj.shape[0]
    nt = w // LANE
    nj = bbt.shape[2] // LANE
    rb = min(512, t // NSEG)
    uo = 5 * nt

    def body(u_ref, bbt_ref, cm_ref, lam_ref, d_ref, s5o_ref, yc_ref, xs):
        _s5_states(u_ref, bbt_ref, lam_ref, xs, t, 0)
        for r0 in range(0, t, rb):
            rows = pl.ds(r0, rb)
            y = d_ref[...] * u_ref[rows, :].astype(F32)
            for j in range(nj):
                y = y + _dot(xs.at[j][rows, :].astype(BF16), cm_ref[pl.ds(j * LANE, LANE), :])
            s5o_ref[rows, :] = y
            yc_ref[rows, :] = _gelu(y).astype(BF16)

    col = lambda c0: pl.BlockSpec((t, LANE), functools.partial(lambda i, c0: (0, c0 + i), c0=c0))
    return pl.pallas_call(
        body, name=name, grid=(nt,),
        in_specs=[col(uo), pl.BlockSpec((None, LANE, nj * LANE), lambda i: (i, 0, 0)),
                  pl.BlockSpec((None, nj * LANE, LANE), lambda i: (i, 0, 0)),
                  pl.BlockSpec((None, nj, LANE), lambda i: (i, 0, 0)), pl.BlockSpec((1, LANE), lambda i: (0, i))],
        out_specs=[col(0), col(0)],
        out_shape=[jax.ShapeDtypeStruct((t, w), F32), jax.ShapeDtypeStruct((t, w), BF16)],
        scratch_shapes=[pltpu.VMEM((nj, t, LANE), F32)],
        compiler_params=_cp(("parallel",)))(proj, bbt, cmat, lam, dvec)


def _s5_bwd(proj, dy, bbt, bbm, cmt, lam, dvec, w, name):
    t = proj.shape[0]
    nt = w // LANE
    nj = bbt.shape[2] // LANE
    nh = nj // 2
    seg = t // NSEG
    rb = min(512, seg)
    uo = 5 * nt

    def body(u_ref, dy_ref, bbt_ref, bbm_ref, cmt_ref, lam_ref, d_ref, du_ref, dbbt_ref, dcm_ref, dlam_ref, dd_ref, xs, gs):
        xs[:, pl.ds(0, 8), :] = jnp.zeros((nj, 8, LANE), F32)
        lr, li, pr, pi = _s5_states(u_ref, bbt_ref, lam_ref, xs, t, 8)
        dcm = [jnp.zeros((LANE, LANE), F32) for _ in range(nj)]
        for r0 in range(0, t, rb):
            rows = pl.ds(r0, rb)
            dyb = dy_ref[rows, :]
            dx = _dot(dyb, cmt_ref[...])
            for j in range(nj):
                gs.at[j][rows, :] = dx[:, j * LANE:(j + 1) * LANE]
                dcm[j] = dcm[j] + _dot(xs[j, pl.ds(8 + r0, rb), :].astype(BF16), dyb, TN)
        for j in range(nj):
            dcm_ref[pl.ds(j * LANE, LANE), :] = dcm[j]
        zero = jnp.zeros((NSEG, LANE), F32)

        def step(kk, c):
            gr, gi = c
            rows = pl.ds(seg - 1 - kk, NSEG, stride=seg)
            ngr, ngi = [], []
            for j in range(nh):
                r = lr[j] * gr[j] + li[j] * gi[j] + gs.at[j][rows, :]
                i = lr[j] * gi[j] - li[j] * gr[j] + gs.at[nh + j][rows, :]
                gs.at[j][rows, :] = r
                gs.at[nh + j][rows, :] = i
                ngr.append(r)
                ngi.append(i)
            return tuple(ngr), tuple(ngi)

        lax.fori_loop(0, seg, step, ((zero,) * nh, (zero,) * nh))
        ctr, cti = [], []
        for j in range(nh):
            plr, pli = pr[j][0:1], pi[j][0:1]
            cr = jnp.zeros((1, LANE), F32)
            ci = jnp.zeros((1, LANE), F32)
            rows_r, rows_i = [None] * NSEG, [None] * NSEG
            for s in range(NSEG - 1, -1, -1):
                rows_r[s], rows_i[s] = cr, ci
                start = pl.ds(s * seg, 1)
                sr, si = gs.at[j][start, :], gs.at[nh + j][start, :]
                cr, ci = sr + plr * cr + pli * ci, si + plr * ci - pli * cr
            ctr.append(jnp.concatenate(rows_r, axis=0))
            cti.append(jnp.concatenate(rows_i, axis=0))

        def fix(kk, c):
            rr, ri, ar, ai = c
            k = seg - 1 - kk
            rows = pl.ds(k, NSEG, stride=seg)
            prev = pl.ds(8 + k - 1, NSEG, stride=seg)
            nr, ni, nar, nai = [], [], [], []
            for j in range(nh):
                r = lr[j] * rr[j] + li[j] * ri[j]
                i = lr[j] * ri[j] - li[j] * rr[j]
                g_r = gs.at[j][rows, :] + r
                g_i = gs.at[nh + j][rows, :] + i
                gs.at[j][rows, :] = g_r
                gs.at[nh + j][rows, :] = g_i
                xr, xi = xs.at[j][prev, :], xs.at[nh + j][prev, :]
                nar.append(ar[j] + g_r * xr + g_i * xi)
                nai.append(ai[j] + g_i * xr - g_r * xi)
                nr.append(r)
                ni.append(i)
            return tuple(nr), tuple(ni), tuple(nar), tuple(nai)

        _, _, ar, ai = lax.fori_loop(0, seg, fix, (tuple(ctr), tuple(cti), (zero,) * nh, (zero,) * nh))
        dlam_ref[...] = jnp.concatenate([jnp.sum(v, axis=0, keepdims=True) for v in list(ar) + list(ai)], axis=0)
        dbb = [jnp.zeros((LANE, LANE), F32) for _ in range(nj)]
        dd = jnp.zeros((1, LANE), F32)
        for r0 in range(0, t, rb):
            rows = pl.ds(r0, rb)
            ub = u_ref[rows, :]
            dyf = dy_ref[rows, :].astype(F32)
            du = d_ref[...] * dyf
            dd = dd + jnp.sum(dyf * ub.astype(F32), axis=0, keepdims=True)
            for j in range(nj):
                gb = gs.at[j][rows, :].astype(BF16)
                du = du + _dot(gb, bbm_ref[pl.ds(j * LANE, LANE), :])
                dbb[j] = dbb[j] + _dot(ub, gb, TN)
            du_ref[rows, :] = du.astype(BF16)
        for j in range(nj):
            dbbt_ref[:, pl.ds(j * LANE, LANE)] = dbb[j]
        dd_ref[...] = dd

    col = lambda c0: pl.BlockSpec((t, LANE), functools.partial(lambda i, c0: (0, c0 + i), c0=c0))
    wide = pl.BlockSpec((None, LANE, nj * LANE), lambda i: (i, 0, 0))
    tall = pl.BlockSpec((None, nj * LANE, LANE), lambda i: (i, 0, 0))
    lam_s = pl.BlockSpec((None, nj, LANE), lambda i: (i, 0, 0))
    vec = pl.BlockSpec((1, LANE), lambda i: (0, i))
    return pl.pallas_call(
        body, name=name, grid=(nt,),
        in_specs=[col(uo), col(0), wide, tall, wide, lam_s, vec],
        out_specs=[col(0), wide, tall, lam_s, vec],
        out_shape=[jax.ShapeDtypeStruct((t, w), BF16), jax.ShapeDtypeStruct((nt, LANE, nj * LANE), F32),
                   jax.ShapeDtypeStruct((nt, nj * LANE, LANE), F32), jax.ShapeDtypeStruct((nt, nj, LANE), F32),
                   jax.ShapeDtypeStruct((1, w), F32)],
        scratch_shapes=[pltpu.VMEM((nj, t + 8, LANE), F32), pltpu.VMEM((nj, t, LANE), F32)],
        compiler_params=_cp(("parallel",)))(proj, dy, bbt, bbm, cmt, lam, dvec)


def _block_diag(x, nt):
    nb, r, c = x.shape
    b = nb // nt
    eye = jnp.eye(b, dtype=x.dtype)
    return jnp.einsum("ibrc,bk->ibrkc", x.reshape(nt, b, r, c), eye).reshape(nt, b * r, b * c)


def _block_diag_extract(x, b):
    nt, br, bc = x.shape
    r, c = br // b, bc // b
    eye = jnp.eye(b, dtype=x.dtype)
    return jnp.einsum("ibrkc,bk->ibrc", x.reshape(nt, b, r, b, c), eye).reshape(nt * b, r, c)


def _adamw_math(w, g, m, v):
    m = ADAM_B1 * m + (1.0 - ADAM_B1) * g
    v = ADAM_B2 * v + (1.0 - ADAM_B2) * (g * g)
    m_hat = m / (1.0 - ADAM_B1 ** ADAM_STEP)
    v_hat = v / (1.0 - ADAM_B2 ** ADAM_STEP)
    delta = -ADAM_LR * (m_hat / (jnp.sqrt(v_hat) + ADAM_EPS) + ADAM_WD * w)
    return delta, m, v


def _adamw_reduce(parts, w, m, v, name):
    n, r, c = parts.shape
    tr = r
    for cand in (512, 256, 128, 64, 32, 16, 8):
        if r % cand == 0 and cand * c * 4 <= 2 * 1024 * 1024:
            tr = cand
            break

    def body(p_ref, w_ref, m_ref, v_ref, g_ref, d_ref, nm_ref, nv_ref):
        g = p_ref[0].astype(F32)
        for q in range(1, n):
            g = g + p_ref[q].astype(F32)
        g_ref[...] = g
        d_ref[...], nm_ref[...], nv_ref[...] = _adamw_math(w_ref[...], g, m_ref[...], v_ref[...])

    row = pl.BlockSpec((tr, c), lambda i: (i, 0))
    o = jax.ShapeDtypeStruct((r, c), F32)
    return pl.pallas_call(
        body, name=name, grid=(r // tr,),
        in_specs=[pl.BlockSpec((n, tr, c), lambda i: (0, i, 0)), row, row, row],
        out_specs=[row, row, row, row], out_shape=[o, o, o, o],
        compiler_params=_cp(("parallel",)))(parts, w, m, v)


def _adamw_flat(w, g, m, v, name):
    r, c = w.shape

    def body(w_ref, g_ref, m_ref, v_ref, d_ref, nm_ref, nv_ref):
        d_ref[...], nm_ref[...], nv_ref[...] = _adamw_math(w_ref[...], g_ref[...], m_ref[...], v_ref[...])

    o = jax.ShapeDtypeStruct((r, c), F32)
    row = pl.BlockSpec((FLAT_ROWS, c), lambda i: (i, 0))
    return pl.pallas_call(body, name=name, grid=(r // FLAT_ROWS,), in_specs=[row] * 4, out_specs=[row] * 3,
                          out_shape=[o, o, o], compiler_params=_cp(("parallel",)))(w, g, m, v)


def _sum_rows8(x, name):
    n, r, c = x.shape

    def body(x_ref, o_ref):
        acc = x_ref[0]
        for q in range(1, n):
            acc = acc + x_ref[q]
        o_ref[...] = acc

    return pl.pallas_call(
        body, name=name, grid=(r // FLAT_ROWS,), in_specs=[pl.BlockSpec((n, FLAT_ROWS, c), lambda i: (0, i, 0))],
        out_specs=pl.BlockSpec((FLAT_ROWS, c), lambda i: (i, 0)), out_shape=jax.ShapeDtypeStruct((r, c), F32),
        compiler_params=_cp(("parallel",)))(x)


def _pair_sum(own, sib, core, name):
    _, _, r, c = own.shape
    tr = r
    for cand in (512, 256, 128, 64, 32, 16):
        if r % cand == 0 and cand * c * 2 <= 2 * 1024 * 1024:
            tr = cand
            break

    def body(c_ref, a_ref, b_ref, o_ref):
        o_ref[...] = (a_ref[...].astype(F32) + b_ref[...].astype(F32)).astype(BF16)

    grid_spec = pltpu.PrefetchScalarGridSpec(
        num_scalar_prefetch=1, grid=(NCHIP, r // tr),
        in_specs=[pl.BlockSpec((None, None, tr, c), lambda p, i, cr: (p, cr[0], i, 0)),
                  pl.BlockSpec((None, tr, c), lambda p, i, cr: (p, i, 0))],
        out_specs=pl.BlockSpec((None, tr, c), lambda p, i, cr: (p, i, 0)))
    return pl.pallas_call(body, name=name, grid_spec=grid_spec, out_shape=jax.ShapeDtypeStruct((NCHIP, r, c), BF16),
                          compiler_params=_cp(("parallel", "parallel")))(core, own, sib)


def _all_gather(xs, name):
    n = len(xs)

    def body(*refs):
        x_refs, o_refs = refs[:n], refs[n:2 * n]
        send_sems, recv_sems, local_sems = refs[2 * n:]
        x, y, c = lax.axis_index("x"), lax.axis_index("y"), lax.axis_index("c")
        me, sibling = (x, y, c), (x, y, 1 - c)
        chips = [(1 - x, y), (x, 1 - y), (1 - x, 1 - y)]

        def copy(t, k, block, to, src=None):
            dst = o_refs[t].at[4 * block[0] + 2 * block[1] + block[2]]
            return pltpu.make_async_remote_copy(
                src_ref=dst if src is None else src, dst_ref=dst,
                send_sem=send_sems.at[7 * t + k], recv_sem=recv_sems.at[7 * t + k],
                device_id=to, device_id_type=MESH)

        mine, first, passed = [], [], []
        for t in range(n):
            cp = pltpu.make_async_copy(x_refs[t], o_refs[t].at[4 * x + 2 * y + c], local_sems.at[t])
            cp.start()
            mine.append(cp)
            first.append(copy(t, 0, me, sibling, src=x_refs[t]))
            first += [copy(t, 1 + j, me, (*chip, c), src=x_refs[t]) for j, chip in enumerate(chips)]
        for cp in first:
            cp.start()
        for t in range(n):
            for j, chip in enumerate(chips):
                copy(t, 1 + j, (*chip, c), me).wait_recv()
                cp = copy(t, 4 + j, (*chip, c), sibling)
                cp.start()
                passed.append(cp)
        for t in range(n):
            copy(t, 0, sibling, me).wait_recv()
            for j, chip in enumerate(chips):
                copy(t, 4 + j, (*chip, 1 - c), me).wait_recv()
        for cp in first + passed:
            cp.wait_send()
        for cp in mine:
            cp.wait()

    any_spec = pl.BlockSpec(memory_space=pl.ANY)
    return pl.pallas_call(
        body, name=name, in_specs=[any_spec] * n, out_specs=[any_spec] * n,
        out_shape=[jax.ShapeDtypeStruct((NDEV,) + a.shape, a.dtype) for a in xs],
        scratch_shapes=[pltpu.SemaphoreType.DMA((7 * n,)), pltpu.SemaphoreType.DMA((7 * n,)),
                        pltpu.SemaphoreType.DMA((n,))],
        )(*xs)


def _sibling_exchange(xs, name):
    n = len(xs)

    def body(*refs):
        x_refs, o_refs = refs[:n], refs[n:2 * n]
        send_sems, recv_sems = refs[2 * n:]
        x, y, c = lax.axis_index("x"), lax.axis_index("y"), lax.axis_index("c")
        cps = []
        for t in range(n):
            cp = pltpu.make_async_remote_copy(
                src_ref=x_refs[t].at[:, 1 - c], dst_ref=o_refs[t], send_sem=send_sems.at[t], recv_sem=recv_sems.at[t],
                device_id=(x, y, 1 - c), device_id_type=MESH)
            cp.start()
            cps.append(cp)
        for cp in cps:
            cp.wait()

    any_spec = pl.BlockSpec(memory_space=pl.ANY)
    return pl.pallas_call(
        body, name=name, in_specs=[any_spec] * n, out_specs=[any_spec] * n,
        out_shape=[jax.ShapeDtypeStruct((a.shape[0],) + a.shape[2:], a.dtype) for a in xs],
        scratch_shapes=[pltpu.SemaphoreType.DMA((n,)), pltpu.SemaphoreType.DMA((n,))],
        )(*xs)


def _chip_exchange(xs, name):
    n = len(xs)

    def body(*refs):
        x_refs, o_refs = refs[:n], refs[n:2 * n]
        send_sems, recv_sems, local_sems = refs[2 * n:]
        x, y, c = lax.axis_index("x"), lax.axis_index("y"), lax.axis_index("c")
        my_chip = 2 * x + y
        chips = [(1 - x, y), (x, 1 - y), (1 - x, 1 - y)]
        cps = []
        for t in range(n):
            cp = pltpu.make_async_copy(x_refs[t].at[my_chip], o_refs[t].at[my_chip], local_sems.at[t])
            cp.start()
            cps.append(cp)
            for j, (px, py) in enumerate(chips):
                cp = pltpu.make_async_remote_copy(
                    src_ref=x_refs[t].at[2 * px + py], dst_ref=o_refs[t].at[my_chip],
                    send_sem=send_sems.at[3 * t + j], recv_sem=recv_sems.at[3 * t + j],
                    device_id=(px, py, c), device_id_type=MESH)
                cp.start()
                cps.append(cp)
        for cp in cps:
            cp.wait()

    any_spec = pl.BlockSpec(memory_space=pl.ANY)
    return pl.pallas_call(
        body, name=name, in_specs=[any_spec] * n, out_specs=[any_spec] * n,
        out_shape=[jax.ShapeDtypeStruct(a.shape, a.dtype) for a in xs],
        scratch_shapes=[pltpu.SemaphoreType.DMA((3 * n,)), pltpu.SemaphoreType.DMA((3 * n,)),
                        pltpu.SemaphoreType.DMA((n,))],
        )(*xs)


_SMALL = ["norm_mix_g", "lru_conv_b", "lru_wa", "lru_ba", "lru_wx", "lru_bx", "lru_lambda", "attn_rel_bias",
          "ssm_a_re", "ssm_a_im", "ssm_b_re", "ssm_b_im", "ssm_c_re", "ssm_c_im", "ssm_d", "ssm_log_step",
          "norm_ffn_g", "norm_final_g"]
_SMALL_SHARDED = ["gate_bias", "lru_conv_w"]
_BIG = ["w_in", "ssm_w_glu", "w_branch", "w_out", "w_ffn_gate", "w_ffn_up", "w_ffn_down"]
_WEIGHTS = ["norm_mix_g", "w_in", "gate_bias", "lru_conv_w", "lru_conv_b", "lru_wa", "lru_ba", "lru_wx", "lru_bx",
            "lru_lambda", "attn_rel_bias", "ssm_a_re", "ssm_a_im", "ssm_b_re", "ssm_b_im", "ssm_c_re", "ssm_c_im",
            "ssm_d", "ssm_log_step", "ssm_w_glu", "w_branch", "w_out", "norm_ffn_g", "w_ffn_gate", "w_ffn_up",
            "w_ffn_down", "norm_final_g"]


def _device_step(x, target, wts, gath, core):
    t, d = x.shape
    w = d // 2
    nl = wts["norm_mix_g"].shape[0]
    nt = w // LANE
    f = gath["w_ffn_gate"].shape[-1]
    ncol = gath["w_in"].shape[-1]
    tm = _tile(t, 1024)
    n_g = w // SSM_GROUP
    gpt = LANE // SSM_GROUP
    bw = w // LRU_BLOCKS
    bpt = LANE // bw

    gbias = jnp.transpose(gath["gate_bias"], (1, 2, 0, 3)).reshape(nl, N_BRANCH, d)
    convw = jnp.transpose(gath["lru_conv_w"], (1, 2, 0, 3)).reshape(nl, CONV_W, w)
    bias_tab = _bias_expand(wts["attn_rel_bias"], "bias_expand")
    btr = jnp.swapaxes(wts["ssm_b_re"], 2, 3)
    bti = jnp.swapaxes(wts["ssm_b_im"], 2, 3)
    ls3 = wts["ssm_log_step"][..., None]
    lam_r, lam_i, bbr, bbi = _s5_prep(wts["ssm_a_re"], wts["ssm_a_im"], ls3, btr, bti, "s5_prep")

    def layer_consts(l):
        c = {}
        c["wa_t"] = _block_diag(wts["lru_wa"][l], nt).astype(BF16)
        c["wx_t"] = _block_diag(wts["lru_wx"][l], nt).astype(BF16)
        c["wat_t"] = jnp.swapaxes(c["wa_t"], 1, 2)
        c["wxt_t"] = jnp.swapaxes(c["wx_t"], 1, 2)
        bd_r, bd_i = _block_diag(bbr[l], nt), _block_diag(bbi[l], nt)
        c["bbt"] = jnp.concatenate([bd_r, bd_i], axis=-1).astype(BF16)
        c["bbm"] = jnp.swapaxes(c["bbt"], 1, 2)
        cd_r, cd_i = _block_diag(wts["ssm_c_re"][l], nt), _block_diag(wts["ssm_c_im"][l], nt)
        c["cmt"] = jnp.concatenate([cd_r, -cd_i], axis=-1).astype(BF16)
        c["cmat"] = jnp.swapaxes(c["cmt"], 1, 2)
        nrow = gpt * SSM_P // LANE
        c["lam"] = jnp.concatenate([lam_r[l].reshape(nt, nrow, LANE), lam_i[l].reshape(nt, nrow, LANE)], axis=1)
        return c

    row1 = lambda a: a.reshape(1, -1)
    saved = []
    for l in range(nl):
        c = layer_consts(l)
        h1 = _rmsnorm_fwd(x, row1(wts["norm_mix_g"][l]), f"norm_mix_{l}")
        tn = _tile(ncol, 768)
        nn_ = ncol // tn
        proj = _mm(f"in_proj_{l}", NN, (t // tm, NDEV, nn_), 0, [h1, gath["w_in"]],
                   [pl.BlockSpec((tm, d), lambda i, j, n: (i, 0)),
                    pl.BlockSpec((None, None, d, tn), functools.partial(lambda i, j, n, l: (j, l, 0, n), l=l))],
                   jax.ShapeDtypeStruct((t, NDEV * ncol), BF16),
                   pl.BlockSpec((tm, tn), lambda i, j, n: (i, j * nn_ + n)), (8, LANE))
        ya, hs = _lru_fwd(proj, convw[l], row1(wts["lru_conv_b"][l]), c["wa_t"], row1(wts["lru_ba"][l]), c["wx_t"],
                          row1(wts["lru_bx"][l]), row1(wts["lru_lambda"][l]), w, f"lru_fwd_{l}")
        yb = _attn_fwd(proj, bias_tab[l], w, f"attn_fwd_{l}")
        s5o, yc = _s5_fwd(proj, c["bbt"], c["cmat"], c["lam"], row1(wts["ssm_d"][l]), w, f"s5_fwd_{l}")
        cb = d // NDEV

        def branch(y, wg, idx, nm):
            return _mm(nm, NN, (t // tm, NDEV), 0, [y, wg],
                       [pl.BlockSpec((tm, w), lambda i, j: (i, 0)),
                        pl.BlockSpec((None,) * len(idx(0)[:-2]) + (w, cb), lambda i, j: idx(j))],
                       jax.ShapeDtypeStruct((t, d), BF16), pl.BlockSpec((tm, cb), lambda i, j: (i, j)), (8, LANE))

        wb_idx = lambda b: (lambda j: (j, l, b, 0, 0))
        br_a = branch(ya, gath["w_branch"], wb_idx(0), f"branch_a_{l}")
        br_b = branch(yb, gath["w_branch"], wb_idx(1), f"branch_b_{l}")
        br_c0 = branch(yc, gath["w_branch"], wb_idx(2), f"branch_c_{l}")
        gl = branch(yc, gath["ssm_w_glu"], lambda j: (j, l, 0, 0), f"branch_glu_{l}")
        merged = _merge_fwd(proj, gbias[l], br_a, br_b, br_c0, gl, d, f"merge_fwd_{l}")
        tno = _tile(d, 1024)
        rb_ = d // NDEV
        x_mid = _mm(f"out_proj_{l}", NN, (t // tm, d // tno, NDEV), 1, [merged, gath["w_out"], x],
                    [pl.BlockSpec((tm, rb_), lambda i, n, k: (i, k)),
                     pl.BlockSpec((None, None, rb_, tno), functools.partial(lambda i, n, k, l: (k, l, 0, n), l=l)),
                     pl.BlockSpec((tm, tno), lambda i, n, k: (i, n))],
                    jax.ShapeDtypeStruct((t, d), F32), pl.BlockSpec((tm, tno), lambda i, n, k: (i, n)), (tm, tno),
                    epi=lambda acc, res: acc + res)
        h2 = _rmsnorm_fwd(x_mid, row1(wts["norm_ffn_g"][l]), f"norm_ffn_{l}")

        def ffn_in(wg, nm):
            return _mm(nm, NN, (t // tm, NDEV), 0, [h2, wg],
                       [pl.BlockSpec((tm, d), lambda i, j: (i, 0)),
                        pl.BlockSpec((None, None, d, f), functools.partial(lambda i, j, l: (j, l, 0, 0), l=l))],
                       jax.ShapeDtypeStruct((NDEV, t, f), BF16), pl.BlockSpec((None, tm, f), lambda i, j: (j, i, 0)),
                       (8, LANE))

        gp = ffn_in(gath["w_ffn_gate"], f"ffn_gate_{l}")
        up = ffn_in(gath["w_ffn_up"], f"ffn_up_{l}")
        act = _swiglu_fwd(gp, up, f"swiglu_fwd_{l}")
        x_next = _mm(f"ffn_down_{l}", NN, (t // tm, d // tno, NDEV), 1, [act, gath["w_ffn_down"], x_mid],
                     [pl.BlockSpec((None, tm, f), lambda i, n, k: (k, i, 0)),
                      pl.BlockSpec((None, None, f, tno), functools.partial(lambda i, n, k, l: (k, l, 0, n), l=l)),
                      pl.BlockSpec((tm, tno), lambda i, n, k: (i, n))],
                     jax.ShapeDtypeStruct((t, d), F32), pl.BlockSpec((tm, tno), lambda i, n, k: (i, n)), (tm, tno),
                     epi=lambda acc, res: acc + res)
        saved.append(dict(x=x, h1=h1, proj=proj, ya=ya, hs=hs, yb=yb, s5o=s5o, yc=yc, br_a=br_a, br_b=br_b,
                          br_c0=br_c0, gl=gl, merged=merged, x_mid=x_mid, h2=h2, gp=gp, up=up, act=act, c=c))
        x = x_next

    loss, dx, dxb, dgf = _loss_head(x, row1(wts["norm_final_g"]), target, "loss_head")

    gbuf = {k: lax.empty((NDEV,) + gath[k].shape[1:], BF16) for k in _BIG}
    sg = {k: [None] * nl for k in _SMALL + _SMALL_SHARDED if k != "norm_final_g"}
    dbias_tabs = [None] * nl
    dlr_l, dli_l, dbbr_l, dbbi_l = [None] * nl, [None] * nl, [None] * nl, [None] * nl
    tk = _tile(d, 1024)
    for l in range(nl - 1, -1, -1):
        s = saved[l]
        c = s["c"]
        lfix = lambda fn: functools.partial(fn, l=l)
        dact = _mm(f"d_act_{l}", NT, (t // tm, NDEV), 0, [dxb, gath["w_ffn_down"]],
                   [pl.BlockSpec((tm, d), lambda i, j: (i, 0)),
                    pl.BlockSpec((None, None, f, d), lfix(lambda i, j, l: (j, l, 0, 0)))],
                   jax.ShapeDtypeStruct((NDEV, t, f), BF16), pl.BlockSpec((None, tm, f), lambda i, j: (j, i, 0)), (8, LANE))
        tno = _tile(d, 1024)
        gbuf["w_ffn_down"] = _mm(
            f"dw_ffn_down_{l}", TN, (NDEV, d // tno, t // tm), 1, [s["act"], dxb, gbuf["w_ffn_down"]],
            [pl.BlockSpec((None, tm, f), lambda j, n, r: (j, r, 0)), pl.BlockSpec((tm, tno), lambda j, n, r: (r, n))],
            jax.ShapeDtypeStruct(gbuf["w_ffn_down"].shape, BF16),
            pl.BlockSpec((None, None, f, tno), lfix(lambda j, n, r, l: (j, l, 0, n))), (f, tno), alias=True)
        dgp, dup = _swiglu_bwd(s["gp"], s["up"], dact, f"swiglu_bwd_{l}")

        def ffn_dh(dz, wg, nm):
            return _mm(nm, NT, (t // tm, d // tk, NDEV), 1, [dz, wg],
                       [pl.BlockSpec((None, tm, f), lambda i, k, j: (j, i, 0)),
                        pl.BlockSpec((None, None, tk, f), lfix(lambda i, k, j, l: (j, l, k, 0)))],
                       jax.ShapeDtypeStruct((t, d), F32), pl.BlockSpec((tm, tk), lambda i, k, j: (i, k)), (tm, tk))

        dh2a = ffn_dh(dgp, gath["w_ffn_gate"], f"d_h2_gate_{l}")
        dh2b = ffn_dh(dup, gath["w_ffn_up"], f"d_h2_up_{l}")

        def ffn_dw(dz, key, nm):
            return _mm(nm, TN, (NDEV, d // tk, t // tm), 1, [s["h2"], dz, gbuf[key]],
                       [pl.BlockSpec((tm, tk), lambda j, k, r: (r, k)), pl.BlockSpec((None, tm, f), lambda j, k, r: (j, r, 0))],
                       jax.ShapeDtypeStruct(gbuf[key].shape, BF16),
                       pl.BlockSpec((None, None, tk, f), lfix(lambda j, k, r, l: (j, l, k, 0))), (tk, f), alias=True)

        gbuf["w_ffn_gate"] = ffn_dw(dgp, "w_ffn_gate", f"dw_ffn_gate_{l}")
        gbuf["w_ffn_up"] = ffn_dw(dup, "w_ffn_up", f"dw_ffn_up_{l}")
        dx_mid, dxmb, dg2 = _rmsnorm_bwd(s["x_mid"], row1(wts["norm_ffn_g"][l]), [dh2a, dh2b], dx, f"norm_ffn_bwd_{l}")
        sg["norm_ffn_g"][l] = dg2.reshape(-1)
        rb_ = d // NDEV
        dmerged = _mm(f"d_merged_{l}", NT, (t // tm, NDEV), 0, [dxmb, gath["w_out"]],
                      [pl.BlockSpec((tm, d), lambda i, j: (i, 0)),
                       pl.BlockSpec((None, None, rb_, d), lfix(lambda i, j, l: (j, l, 0, 0)))],
                      jax.ShapeDtypeStruct((t, d), BF16), pl.BlockSpec((tm, rb_), lambda i, j: (i, j)), (8, LANE))
        gbuf["w_out"] = _mm(
            f"dw_out_{l}", TN, (NDEV, d // tno, t // tm), 1, [s["merged"], dxmb, gbuf["w_out"]],
            [pl.BlockSpec((tm, rb_), lambda j, n, r: (r, j)), pl.BlockSpec((tm, tno), lambda j, n, r: (r, n))],
            jax.ShapeDtypeStruct(gbuf["w_out"].shape, BF16),
            pl.BlockSpec((None, None, rb_, tno), lfix(lambda j, n, r, l: (j, l, 0, n))), (rb_, tno), alias=True)
        dbr_a, dbr_b, dbr_c0, dgl, dgates, dgb = _merge_bwd(s["proj"], gbias[l], s["br_a"], s["br_b"], s["br_c0"], s["gl"],
                                                           dmerged, d, f"merge_bwd_{l}")
        sg["gate_bias"][l] = dgb.reshape(-1)
        cb = d // NDEV

        def branch_dy(dbr, wg, idx, nm, dt):
            nlead = len(idx(0, 0)) - 2
            return _mm(nm, NT, (t // tm, NDEV), 1, [dbr, wg],
                       [pl.BlockSpec((tm, cb), lambda i, j: (i, j)),
                        pl.BlockSpec((None,) * nlead + (w, cb), lambda i, j: idx(j, 0))],
                       jax.ShapeDtypeStruct((t, w), dt), pl.BlockSpec((tm, w), lambda i, j: (i, 0)), (tm, w))

        def branch_dw(y, dbr, key, idx, nm):
            nlead = len(idx(0)) - 2
            return _mm(nm, TN, (NDEV, t // tm), 1, [y, dbr, gbuf[key]],
                       [pl.BlockSpec((tm, w), lambda j, r: (r, 0)), pl.BlockSpec((tm, cb), lambda j, r: (r, j))],
                       jax.ShapeDtypeStruct(gbuf[key].shape, BF16),
                       pl.BlockSpec((None,) * nlead + (w, cb), lambda j, r: idx(j)), (w, cb), alias=True)

        wb_i = lambda b: (lambda j, z=0: (j, l, b, 0, 0))
        glu_i = lambda j, z=0: (j, l, 0, 0)
        dya = branch_dy(dbr_a, gath["w_branch"], wb_i(0), f"d_ya_{l}", BF16)
        dyb = branch_dy(dbr_b, gath["w_branch"], wb_i(1), f"d_yb_{l}", BF16)
        dyc1 = branch_dy(dbr_c0, gath["w_branch"], wb_i(2), f"d_yc_{l}", F32)
        dyc2 = branch_dy(dgl, gath["ssm_w_glu"], glu_i, f"d_yc_glu_{l}", F32)
        gbuf["w_branch"] = branch_dw(s["ya"], dbr_a, "w_branch", wb_i(0), f"dw_branch_a_{l}")
        gbuf["w_branch"] = branch_dw(s["yb"], dbr_b, "w_branch", wb_i(1), f"dw_branch_b_{l}")
        gbuf["w_branch"] = branch_dw(s["yc"], dbr_c0, "w_branch", wb_i(2), f"dw_branch_c_{l}")
        gbuf["ssm_w_glu"] = branch_dw(s["yc"], dgl, "ssm_w_glu", glu_i, f"dw_glu_{l}")
        ds5o = _gelu_bwd_sum(s["s5o"], dyc1, dyc2, f"gelu_bwd_{l}")
        du, dbbt, dcm, dlam, dd = _s5_bwd(s["proj"], ds5o, c["bbt"], c["bbm"], c["cmt"], c["lam"], row1(wts["ssm_d"][l]),
                                          w, f"s5_bwd_{l}")
        sp = gpt * SSM_P
        dbbr_l[l] = _block_diag_extract(dbbt[:, :, :sp], gpt)
        dbbi_l[l] = _block_diag_extract(dbbt[:, :, sp:], gpt)
        dcmt = jnp.swapaxes(dcm, 1, 2)
        sg["ssm_c_re"][l] = _block_diag_extract(dcmt[:, :, :sp], gpt).reshape(-1)
        sg["ssm_c_im"][l] = (-_block_diag_extract(dcmt[:, :, sp:], gpt)).reshape(-1)
        nrow = sp // LANE
        dlr_l[l] = dlam[:, :nrow].reshape(n_g, SSM_P)
        dli_l[l] = dlam[:, nrow:].reshape(n_g, SSM_P)
        sg["ssm_d"][l] = dd.reshape(-1)
        dq, dk, dv, dbt = _attn_bwd(s["proj"], bias_tab[l], dyb, w, f"attn_bwd_{l}")
        dbias_tabs[l] = dbt
        dlx, dlg, dcw, dcb, dwa, dba, dwx, dbx, dlm = _lru_bwd(
            s["proj"], s["hs"], dya, convw[l], row1(wts["lru_conv_b"][l]), c["wa_t"], row1(wts["lru_ba"][l]), c["wx_t"],
            row1(wts["lru_bx"][l]), row1(wts["lru_lambda"][l]), c["wat_t"], c["wxt_t"], w, f"lru_bwd_{l}")
        sg["lru_conv_w"][l] = dcw.reshape(-1)
        sg["lru_conv_b"][l] = dcb.reshape(-1)
        sg["lru_wa"][l] = _block_diag_extract(dwa, bpt).reshape(-1)
        sg["lru_wx"][l] = _block_diag_extract(dwx, bpt).reshape(-1)
        sg["lru_ba"][l] = dba.reshape(-1)
        sg["lru_bx"][l] = dbx.reshape(-1)
        sg["lru_lambda"][l] = dlm.reshape(-1)
        dproj = jnp.concatenate([dlx, dlg, dq, dk, dv, du, dgates], axis=1)
        tn = _tile(ncol, 768)
        nn_ = ncol // tn
        dh1 = _mm(f"d_h1_{l}", NT, (t // tm, d // tk, NDEV, nn_), 2, [dproj, gath["w_in"]],
                  [pl.BlockSpec((tm, tn), lambda i, k, j, n: (i, j * nn_ + n)),
                   pl.BlockSpec((None, None, tk, tn), lfix(lambda i, k, j, n, l: (j, l, k, n)))],
                  jax.ShapeDtypeStruct((t, d), F32), pl.BlockSpec((tm, tk), lambda i, k, j, n: (i, k)), (tm, tk))
        gbuf["w_in"] = _mm(
            f"dw_in_{l}", TN, (NDEV, d // tk, nn_, t // tm), 1, [s["h1"], dproj, gbuf["w_in"]],
            [pl.BlockSpec((tm, tk), lambda j, k, n, r: (r, k)), pl.BlockSpec((tm, tn), lambda j, k, n, r: (r, j * nn_ + n))],
            jax.ShapeDtypeStruct(gbuf["w_in"].shape, BF16),
            pl.BlockSpec((None, None, tk, tn), lfix(lambda j, k, n, r, l: (j, l, k, n))), (tk, tn), alias=True)
        dx, dxb, dg1 = _rmsnorm_bwd(s["x"], row1(wts["norm_mix_g"][l]), [dh1], dx_mid, f"norm_mix_bwd_{l}")
        sg["norm_mix_g"][l] = dg1.reshape(-1)

    da_re, da_im, dls, dbtr, dbti = _s5_prep_bwd(wts["ssm_a_re"], wts["ssm_a_im"], ls3, btr, bti, jnp.stack(dlr_l),
                                                 jnp.stack(dli_l), jnp.stack(dbbr_l), jnp.stack(dbbi_l), "s5_prep_bwd")
    drel = _bias_reduce(jnp.stack(dbias_tabs), "bias_reduce")
    small = {k: jnp.stack(v) for k, v in sg.items() if v[0] is not None}
    small["ssm_a_re"] = da_re.reshape(nl, -1)
    small["ssm_a_im"] = da_im.reshape(nl, -1)
    small["ssm_log_step"] = dls.reshape(nl, -1)
    small["ssm_b_re"] = jnp.swapaxes(dbtr, 2, 3).reshape(nl, -1)
    small["ssm_b_im"] = jnp.swapaxes(dbti, 2, 3).reshape(nl, -1)
    small["attn_rel_bias"] = drel.reshape(nl, -1)
    small["norm_final_g"] = dgf.reshape(-1)
    return loss, dx, gbuf, small


def kernel(x, norm_mix_g, w_in, gate_bias, lru_conv_w, lru_conv_b, lru_wa, lru_ba, lru_wx, lru_bx, lru_lambda, attn_rel_bias, ssm_a_re, ssm_a_im, ssm_b_re, ssm_b_im, ssm_c_re, ssm_c_im, ssm_d, ssm_log_step, ssm_w_glu, w_branch, w_out, norm_ffn_g, w_ffn_gate, w_ffn_up, w_ffn_down, norm_final_g, loss_target, m_norm_mix_g, m_w_in, m_gate_bias, m_lru_conv_w, m_lru_conv_b, m_lru_wa, m_lru_ba, m_lru_wx, m_lru_bx, m_lru_lambda, m_attn_rel_bias, m_ssm_a_re, m_ssm_a_im, m_ssm_b_re, m_ssm_b_im, m_ssm_c_re, m_ssm_c_im, m_ssm_d, m_ssm_log_step, m_ssm_w_glu, m_w_branch, m_w_out, m_norm_ffn_g, m_w_ffn_gate, m_w_ffn_up, m_w_ffn_down, m_norm_final_g, v_norm_mix_g, v_w_in, v_gate_bias, v_lru_conv_w, v_lru_conv_b, v_lru_wa, v_lru_ba, v_lru_wx, v_lru_bx, v_lru_lambda, v_attn_rel_bias, v_ssm_a_re, v_ssm_a_im, v_ssm_b_re, v_ssm_b_im, v_ssm_c_re, v_ssm_c_im, v_ssm_d, v_ssm_log_step, v_ssm_w_glu, v_w_branch, v_w_out, v_norm_ffn_g, v_w_ffn_gate, v_w_ffn_up, v_w_ffn_down, v_norm_final_g):
    args = locals()
    wts = {k: args[k] for k in _WEIGHTS}
    mom = {k: args["m_" + k] for k in _WEIGHTS}
    vel = {k: args["v_" + k] for k in _WEIGHTS}
    cx, cy, cc = lax.axis_index("x"), lax.axis_index("y"), lax.axis_index("c")
    me = 4 * cx + 2 * cy + cc
    nl = norm_mix_g.shape[0]

    names = _BIG + _SMALL_SHARDED
    gathered = _all_gather([wts[k].astype(BF16) for k in _BIG] + [wts[k] for k in _SMALL_SHARDED], "gather_weights")
    gath = dict(zip(names, gathered))

    loss, dx, gbuf, small = _device_step(x[0], loss_target[0], wts, gath, cc)

    own = [gbuf[k].reshape((NCHIP, 2) + gbuf[k].shape[1:]) for k in _BIG]
    sib = _sibling_exchange(own, "grad_sibling_exchange")
    core = cc.astype(jnp.int32).reshape(1)
    pair = []
    for k, a, b in zip(_BIG, own, sib):
        c_ = a.shape[-1]
        pair.append(_pair_sum(a.reshape(NCHIP, 2, -1, c_), b.reshape(NCHIP, -1, c_), core, f"pair_sum_{k}").reshape(b.shape))
    parts = _chip_exchange(pair, "grad_chip_exchange")
    out = {}
    for k, p in zip(_BIG, parts):
        shp = wts[k].shape
        c_ = shp[-1]
        g, dl, nm, nv = _adamw_reduce(p.reshape(NCHIP, -1, c_), wts[k].reshape(-1, c_), mom[k].reshape(-1, c_),
                                      vel[k].reshape(-1, c_), f"adamw_{k}")
        out[k] = tuple(a.reshape(shp) for a in (g, dl, nm, nv))

    order = _SMALL + _SMALL_SHARDED
    flat = jnp.concatenate([small[k].reshape(-1) for k in order])
    n_flat = flat.shape[0]
    flat = jnp.pad(flat, (0, (-n_flat) % (FLAT_ROWS * LANE))).reshape(-1, LANE)
    (allflat,) = _all_gather([flat], "gather_small_grads")
    gsum = _sum_rows8(allflat, "sum_small_grads").reshape(-1)
    gs, off = {}, 0
    for k in order:
        n = small[k].size
        gs[k] = gsum[off:off + n]
        off += n
    gs_loc = {}
    for k in _SMALL:
        gs_loc[k] = gs[k].reshape(wts[k].shape)
    ncb = gate_bias.shape[-1]
    gs_loc["gate_bias"] = lax.dynamic_slice_in_dim(gs["gate_bias"].reshape(nl, N_BRANCH, -1), me * ncb, ncb, axis=2)
    ncw = lru_conv_w.shape[-1]
    gs_loc["lru_conv_w"] = lax.dynamic_slice_in_dim(gs["lru_conv_w"].reshape(nl, CONV_W, -1), me * ncw, ncw, axis=2)

    def pack(dct):
        v = jnp.concatenate([dct[k].reshape(-1) for k in order])
        return jnp.pad(v, (0, (-v.shape[0]) % (FLAT_ROWS * LANE))).reshape(-1, LANE)

    dl_f, nm_f, nv_f = _adamw_flat(pack(wts), pack(gs_loc), pack(mom), pack(vel), "adamw_small")
    off = 0
    for k in order:
        n = wts[k].size
        sl = lambda a: a.reshape(-1)[off:off + n].reshape(wts[k].shape)
        out[k] = (gs_loc[k], sl(dl_f), sl(nm_f), sl(nv_f))
        off += n

    loss_total = lax.psum(loss[0, 0], ("x", "y", "c"))
    return (loss_total, dx[None], *[out[k][0] for k in _WEIGHTS], *[out[k][1] for k in _WEIGHTS],
            *[out[k][2] for k in _WEIGHTS], *[out[k][3] for k in _WEIGHTS])
```

```python
import functools
import math

import numpy as np
import jax
import jax.numpy as jnp
from jax import lax
from jax.experimental import pallas as pl
from jax.experimental.pallas import tpu as pltpu

F32 = jnp.float32
BF16 = jnp.bfloat16
LANE = 128
NSEG = 8
NDEV = 8
NCHIP = 4
MESH = pl.DeviceIdType.MESH
VMEM_LIMIT = 56 * 1024 * 1024
FLAT_ROWS = 512

NORM_EPS = 1e-6
CHUNK = 64
ATT_LEFT = 8
ATT_BAND = (ATT_LEFT + 1) * CHUNK
MAX_REL = 128
N_REL = 2 * MAX_REL + 1
N_REL_PAD = 384
ATT_HEADS = 8
MASK_VALUE = -1e30
LRU_C = 8.0
LRU_BLOCKS = 16
SSM_GROUP = 16
SSM_P = 64
CONV_W = 4
N_BRANCH = 3

ADAM_LR = 0.001
ADAM_B1 = 0.9
ADAM_B2 = 0.999
ADAM_EPS = 1e-08
ADAM_WD = 0.01
ADAM_STEP = 10

NN = (((1,), (0,)), ((), ()))
NT = (((1,), (1,)), ((), ()))
TN = (((0,), (0,)), ((), ()))


def _cp(sem=None, vmem=VMEM_LIMIT, **kw):
    return pltpu.CompilerParams(dimension_semantics=sem, vmem_limit_bytes=vmem, **kw)


def _dot(a, b, dn=NN):
    return lax.dot_general(a, b, dn, preferred_element_type=F32)


def _gelu(x):
    c = math.sqrt(2.0 / math.pi)
    return 0.5 * x * (1.0 + jnp.tanh(c * (x + 0.044715 * x * x * x)))


def _gelu_grad(x):
    c = math.sqrt(2.0 / math.pi)
    t = jnp.tanh(c * (x + 0.044715 * x * x * x))
    return 0.5 * (1.0 + t) + 0.5 * x * (1.0 - t * t) * c * (1.0 + 3.0 * 0.044715 * x * x)


def _sigmoid(x):
    return 1.0 / (1.0 + jnp.exp(-x))


def _one_minus_exp(z):
    series = -(z * (1.0 + z * (0.5 + z * (1.0 / 6.0 + z * (1.0 / 24.0)))))
    return jnp.where(z > -0.02, series, 1.0 - jnp.exp(z))


def _softplus_neg(lam):
    e = jnp.exp(-jnp.abs(lam))
    series = e * (1.0 - e * (0.5 - e * (1.0 / 3.0 - e * 0.25)))
    log1p_e = jnp.where(e < 0.02, series, jnp.log(1.0 + e))
    return jnp.maximum(-lam, 0.0) + log1p_e


def _mm(name, dn, grid, n_red, ins, in_specs, out_shape, out_spec, acc_shape, epi=None, alias=None):
    n_extra = len(ins) - 2 - (1 if alias is not None else 0)
    red_axes = tuple(range(len(grid) - n_red, len(grid)))
    red_sizes = tuple(grid[r] for r in red_axes)

    def body(*refs):
        a_ref, b_ref = refs[0], refs[1]
        extra = refs[2:2 + n_extra]
        o_ref, acc = refs[-2], refs[-1]
        if n_red:
            first = functools.reduce(jnp.logical_and, [pl.program_id(r) == 0 for r in red_axes])
            last = functools.reduce(jnp.logical_and,
                                    [pl.program_id(r) == n - 1 for r, n in zip(red_axes, red_sizes)])

            @pl.when(first)
            def _():
                acc[...] = jnp.zeros_like(acc)

            acc[...] += _dot(a_ref[...], b_ref[...], dn)

            @pl.when(last)
            def _():
                r = acc[...]
                if epi is not None:
                    r = epi(r, *[e[...] for e in extra])
                o_ref[...] = r.astype(o_ref.dtype)
        else:
            r = _dot(a_ref[...], b_ref[...], dn)
            if epi is not None:
                r = epi(r, *[e[...] for e in extra])
            o_ref[...] = r.astype(o_ref.dtype)

    specs = list(in_specs)
    kw = {}
    if alias is not None:
        specs.append(pl.BlockSpec(memory_space=pl.ANY))
        kw["input_output_aliases"] = {len(ins) - 1: 0}
    sem = ("parallel",) * (len(grid) - n_red) + ("arbitrary",) * n_red
    return pl.pallas_call(
        body, name=name, grid=grid, in_specs=specs, out_specs=out_spec, out_shape=out_shape,
        scratch_shapes=[pltpu.VMEM(acc_shape, F32)], compiler_params=_cp(sem), **kw)(*ins)


def _tile(n, pref):
    if n <= pref:
        return n
    t = pref
    while t >= LANE:
        if n % t == 0 and t % LANE == 0:
            return t
        t -= LANE
    return n


def _rmsnorm_fwd(x, g, name):
    t, d = x.shape
    tm = min(512, t)

    def body(x_ref, g_ref, h_ref):
        xv = x_ref[...]
        r = lax.rsqrt(jnp.mean(xv * xv, axis=-1, keepdims=True) + NORM_EPS)
        h_ref[...] = (xv * r * g_ref[...]).astype(h_ref.dtype)

    return pl.pallas_call(
        body, name=name, grid=(t // tm,),
        in_specs=[pl.BlockSpec((tm, d), lambda i: (i, 0)), pl.BlockSpec((1, d), lambda i: (0, 0))],
        out_specs=pl.BlockSpec((tm, d), lambda i: (i, 0)),
        out_shape=jax.ShapeDtypeStruct((t, d), BF16), compiler_params=_cp(("parallel",)))(x, g)


def _rmsnorm_bwd(x, g, dhs, dres, name):
    t, d = x.shape
    tm = min(256, t)
    n_dh = len(dhs)

    def body(*refs):
        x_ref, g_ref = refs[0], refs[1]
        dh_refs = refs[2:2 + n_dh]
        dres_ref = refs[2 + n_dh]
        dx_ref, dxb_ref, dg_ref = refs[3 + n_dh:]
        xv = x_ref[...]
        r = lax.rsqrt(jnp.mean(xv * xv, axis=-1, keepdims=True) + NORM_EPS)
        xhat = xv * r
        dh = dh_refs[0][...].astype(F32)
        for e in dh_refs[1:]:
            dh = dh + e[...].astype(F32)
        dxh = dh * g_ref[...]
        dx = r * (dxh - xhat * jnp.mean(dxh * xhat, axis=-1, keepdims=True)) + dres_ref[...]
        dx_ref[...] = dx
        dxb_ref[...] = dx.astype(BF16)

        @pl.when(pl.program_id(0) == 0)
        def _():
            dg_ref[...] = jnp.zeros_like(dg_ref)

        dg_ref[...] += jnp.sum(dh * xhat, axis=0, keepdims=True)

    row = pl.BlockSpec((tm, d), lambda i: (i, 0))
    par = pl.BlockSpec((1, d), lambda i: (0, 0))
    return pl.pallas_call(
        body, name=name, grid=(t // tm,),
        in_specs=[row, par] + [row] * n_dh + [row],
        out_specs=[row, row, par],
        out_shape=[jax.ShapeDtypeStruct((t, d), F32), jax.ShapeDtypeStruct((t, d), BF16),
                   jax.ShapeDtypeStruct((1, d), F32)],
        compiler_params=_cp(("arbitrary",)))(x, g, *dhs, dres)


def _loss_head(x, g, target, name):
    t, d = x.shape
    tm = min(256, t)

    def body(x_ref, g_ref, t_ref, loss_ref, dx_ref, dxb_ref, dg_ref):
        xv = x_ref[...]
        r = lax.rsqrt(jnp.mean(xv * xv, axis=-1, keepdims=True) + NORM_EPS)
        xhat = xv * r
        y = xhat * g_ref[...]
        err = y - t_ref[...]
        dy = err * (1.0 / d)
        dxh = dy * g_ref[...]
        dx = r * (dxh - xhat * jnp.mean(dxh * xhat, axis=-1, keepdims=True))
        dx_ref[...] = dx
        dxb_ref[...] = dx.astype(BF16)

        @pl.when(pl.program_id(0) == 0)
        def _():
            dg_ref[...] = jnp.zeros_like(dg_ref)
            loss_ref[...] = jnp.zeros_like(loss_ref)

        dg_ref[...] += jnp.sum(dy * xhat, axis=0, keepdims=True)
        per_tok = jnp.sum(err * err, axis=-1, keepdims=True) * (0.5 / d)
        loss_ref[...] += jnp.sum(per_tok, axis=0, keepdims=True)

    row = pl.BlockSpec((tm, d), lambda i: (i, 0))
    par = pl.BlockSpec((1, d), lambda i: (0, 0))
    return pl.pallas_call(
        body, name=name, grid=(t // tm,),
        in_specs=[row, par, row],
        out_specs=[pl.BlockSpec((1, 1), lambda i: (0, 0)), row, row, par],
        out_shape=[jax.ShapeDtypeStruct((1, 1), F32), jax.ShapeDtypeStruct((t, d), F32),
                   jax.ShapeDtypeStruct((t, d), BF16), jax.ShapeDtypeStruct((1, d), F32)],
        compiler_params=_cp(("arbitrary",)))(x, g, target)


def _merge_fwd(proj, gbias, br_a, br_b, br_c0, gl, d, name):
    t = proj.shape[0]
    tm = min(128, t)
    off = proj.shape[1] // d - N_BRANCH

    def body(g0, g1, g2, gb, a_ref, b_ref, c_ref, gl_ref, o_ref):
        gbv = gb[...]
        s0 = _sigmoid(g0[...].astype(F32) + gbv[0:1])
        s1 = _sigmoid(g1[...].astype(F32) + gbv[1:2])
        s2 = _sigmoid(g2[...].astype(F32) + gbv[2:3])
        brc = c_ref[...].astype(F32) * _sigmoid(gl_ref[...].astype(F32))
        o_ref[...] = (s0 * a_ref[...].astype(F32) + s1 * b_ref[...].astype(F32) + s2 * brc).astype(BF16)

    row = pl.BlockSpec((tm, d), lambda i: (i, 0))
    gs = [pl.BlockSpec((tm, d), functools.partial(lambda i, b: (i, off + b), b=b)) for b in range(N_BRANCH)]
    return pl.pallas_call(
        body, name=name, grid=(t // tm,),
        in_specs=gs + [pl.BlockSpec((N_BRANCH, d), lambda i: (0, 0)), row, row, row, row],
        out_specs=row, out_shape=jax.ShapeDtypeStruct((t, d), BF16),
        compiler_params=_cp(("parallel",)))(proj, proj, proj, gbias, br_a, br_b, br_c0, gl)


def _merge_bwd(proj, gbias, br_a, br_b, br_c0, gl, dmerged, d, name):
    t = proj.shape[0]
    tm = min(128, t)
    off = proj.shape[1] // d - N_BRANCH

    def body(g0, g1, g2, gb, a_ref, b_ref, c_ref, gl_ref, dm_ref, da_ref, db_ref, dc_ref, dgl_ref, dg_ref, dgb_ref):
        gbv = gb[...]
        dm = dm_ref[...].astype(F32)
        s0 = _sigmoid(g0[...].astype(F32) + gbv[0:1])
        s1 = _sigmoid(g1[...].astype(F32) + gbv[1:2])
        s2 = _sigmoid(g2[...].astype(F32) + gbv[2:3])
        sg = _sigmoid(gl_ref[...].astype(F32))
        c0 = c_ref[...].astype(F32)
        brc = c0 * sg
        da_ref[...] = (dm * s0).astype(BF16)
        db_ref[...] = (dm * s1).astype(BF16)
        dbrc = dm * s2
        dc_ref[...] = (dbrc * sg).astype(BF16)
        dgl_ref[...] = (dbrc * c0 * sg * (1.0 - sg)).astype(BF16)
        dp0 = dm * a_ref[...].astype(F32) * s0 * (1.0 - s0)
        dp1 = dm * b_ref[...].astype(F32) * s1 * (1.0 - s1)
        dp2 = dm * brc * s2 * (1.0 - s2)
        dg_ref[:, 0:d] = dp0.astype(BF16)
        dg_ref[:, d:2 * d] = dp1.astype(BF16)
        dg_ref[:, 2 * d:3 * d] = dp2.astype(BF16)

        @pl.when(pl.program_id(0) == 0)
        def _():
            dgb_ref[...] = jnp.zeros_like(dgb_ref)

        dgb_ref[0:1, :] += jnp.sum(dp0, axis=0, keepdims=True)
        dgb_ref[1:2, :] += jnp.sum(dp1, axis=0, keepdims=True)
        dgb_ref[2:3, :] += jnp.sum(dp2, axis=0, keepdims=True)

    row = pl.BlockSpec((tm, d), lambda i: (i, 0))
    gs = [pl.BlockSpec((tm, d), functools.partial(lambda i, b: (i, off + b), b=b)) for b in range(N_BRANCH)]
    par = pl.BlockSpec((N_BRANCH, d), lambda i: (0, 0))
    bf = jax.ShapeDtypeStruct((t, d), BF16)
    return pl.pallas_call(
        body, name=name, grid=(t // tm,),
        in_specs=gs + [par, row, row, row, row, row],
        out_specs=[row, row, row, row, pl.BlockSpec((tm, N_BRANCH * d), lambda i: (i, 0)), par],
        out_shape=[bf, bf, bf, bf, jax.ShapeDtypeStruct((t, N_BRANCH * d), BF16),
                   jax.ShapeDtypeStruct((N_BRANCH, d), F32)],
        compiler_params=_cp(("arbitrary",)))(proj, proj, proj, gbias, br_a, br_b, br_c0, gl, dmerged)


def _swiglu_fwd(gp, up, name):
    s, t, f = gp.shape
    tm = min(512, t)

    def body(g_ref, u_ref, o_ref):
        g = g_ref[...].astype(F32)
        o_ref[...] = (g * _sigmoid(g) * u_ref[...].astype(F32)).astype(BF16)

    blk = pl.BlockSpec((None, tm, f), lambda j, i: (j, i, 0))
    return pl.pallas_call(
        body, name=name, grid=(s, t // tm), in_specs=[blk, blk], out_specs=blk,
        out_shape=jax.ShapeDtypeStruct((s, t, f), BF16), compiler_params=_cp(("parallel", "parallel")))(gp, up)


def _swiglu_bwd(gp, up, dact, name):
    s, t, f = gp.shape
    tm = min(512, t)

    def body(g_ref, u_ref, d_ref, dg_ref, du_ref):
        g = g_ref[...].astype(F32)
        u = u_ref[...].astype(F32)
        da = d_ref[...].astype(F32)
        sg = _sigmoid(g)
        silu = g * sg
        du_ref[...] = (da * silu).astype(BF16)
        dg_ref[...] = (da * u * (sg + silu * (1.0 - sg))).astype(BF16)

    blk = pl.BlockSpec((None, tm, f), lambda j, i: (j, i, 0))
    o = jax.ShapeDtypeStruct((s, t, f), BF16)
    return pl.pallas_call(
        body, name=name, grid=(s, t // tm), in_specs=[blk, blk, blk], out_specs=[blk, blk],
        out_shape=[o, o], compiler_params=_cp(("parallel", "parallel")))(gp, up, dact)


def _gelu_bwd_sum(s5o, dy1, dy2, name):
    t, w = s5o.shape
    tm = min(512, t)

    def body(s_ref, a_ref, b_ref, o_ref):
        o_ref[...] = ((a_ref[...] + b_ref[...]) * _gelu_grad(s_ref[...])).astype(BF16)

    row = pl.BlockSpec((tm, w), lambda i: (i, 0))
    return pl.pallas_call(
        body, name=name, grid=(t // tm,), in_specs=[row, row, row], out_specs=row,
        out_shape=jax.ShapeDtypeStruct((t, w), BF16), compiler_params=_cp(("parallel",)))(s5o, dy1, dy2)


def _lru_gates(xc, wa, ba, wx, bx, sp):
    xcb = xc.astype(BF16)
    r = _sigmoid(_dot(xcb, wa) + ba)
    i = _sigmoid(_dot(xcb, wx) + bx)
    la = -LRU_C * r * sp
    return xcb, r, i, la


def _lru_fwd(proj, cw, cb, wa_t, ba, wx_t, bx, lam, w, name):
    t = proj.shape[0]
    nt = w // LANE
    seg = t // NSEG
    rb = min(512, seg)

    def body(lx_ref, lg_ref, cw_ref, cb_ref, wa_ref, ba_ref, wx_ref, bx_ref, lam_ref, ya_ref, hs_ref, xp, a_s, b_s):
        xp[pl.ds(0, 8), :] = jnp.zeros((8, LANE), F32)
        for r0 in range(0, t, rb):
            xp[pl.ds(8 + r0, rb), :] = lx_ref[pl.ds(r0, rb), :].astype(F32)
        sp = _softplus_neg(lam_ref[...])
        cwv = cw_ref[...]
        for r0 in range(0, t, rb):
            xc = cb_ref[...] + sum(cwv[k:k + 1] * xp[pl.ds(8 + r0 - (CONV_W - 1) + k, rb), :] for k in range(CONV_W))
            _, _, i, la = _lru_gates(xc, wa_ref[...], ba_ref[...], wx_ref[...], bx_ref[...], sp)
            a_s[pl.ds(r0, rb), :] = jnp.exp(la)
            b_s[pl.ds(r0, rb), :] = jnp.sqrt(_one_minus_exp(2.0 * la)) * (i * xc)

        def load(k):
            rows = pl.ds(k, NSEG, stride=seg)
            return a_s[rows, :], b_s[rows, :]

        def step(k, c):
            h, p, a, b = c
            na, nb = load(jnp.minimum(k + 1, seg - 1))
            rows = pl.ds(k, NSEG, stride=seg)
            h = a * h + b
            p = a * p
            b_s[rows, :] = h
            a_s[rows, :] = p
            return h, p, na, nb

        lax.fori_loop(0, seg, step, (jnp.zeros((NSEG, LANE), F32), jnp.ones((NSEG, LANE), F32)) + load(0), unroll=4)
        carry = jnp.zeros((1, LANE), F32)
        for s in range(NSEG):
            for r0 in range(s * seg, (s + 1) * seg, rb):
                rows = pl.ds(r0, rb)
                h = b_s[rows, :] + a_s[rows, :] * carry
                hs_ref[rows, :] = h
                ya_ref[rows, :] = (h * _gelu(lg_ref[rows, :].astype(F32))).astype(BF16)
            end = pl.ds((s + 1) * seg - 1, 1)
            carry = b_s[end, :] + a_s[end, :] * carry

    col = lambda c0: pl.BlockSpec((t, LANE), functools.partial(lambda i, c0: (0, c0 + i), c0=c0))
    par = lambda k: pl.BlockSpec((k, LANE), lambda i: (0, i))
    mat = pl.BlockSpec((None, LANE, LANE), lambda i: (i, 0, 0))
    return pl.pallas_call(
        body, name=name, grid=(nt,),
        in_specs=[col(0), col(nt), par(CONV_W), par(1), mat, par(1), mat, par(1), par(1)],
        out_specs=[col(0), col(0)],
        out_shape=[jax.ShapeDtypeStruct((t, w), BF16), jax.ShapeDtypeStruct((t, w), F32)],
        scratch_shapes=[pltpu.VMEM((t + 8, LANE), F32), pltpu.VMEM((t, LANE), F32), pltpu.VMEM((t, LANE), F32)],
        compiler_params=_cp(("parallel",)))(proj, proj, cw, cb, wa_t, ba, wx_t, bx, lam)


def _lru_bwd(proj, hs, dya, cw, cb, wa_t, ba, wx_t, bx, lam, wat_t, wxt_t, w, name):
    t = proj.shape[0]
    nt = w // LANE
    seg = t // NSEG
    rb = min(512, seg)

    def body(lx_ref, lg_ref, hs_ref, dy_ref, cw_ref, cb_ref, wa_ref, ba_ref, wx_ref, bx_ref, lam_ref, wat_ref, wxt_ref,
             dlx_ref, dlg_ref, dcw_ref, dcb_ref, dwa_ref, dba_ref, dwx_ref, dbx_ref, dlam_ref,
             xp, hp, a_s, g_s, q_s, dxc_s):
        z8 = jnp.zeros((8, LANE), F32)
        xp[pl.ds(0, 8), :] = z8
        hp[pl.ds(0, 8), :] = z8
        a_s[pl.ds(t, 8), :] = z8
        dxc_s[pl.ds(t, 8), :] = z8
        lamv = lam_ref[...]
        sp = _softplus_neg(lamv)
        cwv = cw_ref[...]

        def conv(r0):
            return cb_ref[...] + sum(cwv[k:k + 1] * xp[pl.ds(8 + r0 - (CONV_W - 1) + k, rb), :] for k in range(CONV_W))

        for r0 in range(0, t, rb):
            rows = pl.ds(r0, rb)
            xp[pl.ds(8 + r0, rb), :] = lx_ref[rows, :].astype(F32)
            hp[pl.ds(8 + r0, rb), :] = hs_ref[rows, :]
        for r0 in range(0, t, rb):
            rows = pl.ds(r0, rb)
            _, _, _, la = _lru_gates(conv(r0), wa_ref[...], ba_ref[...], wx_ref[...], bx_ref[...], sp)
            a_s[rows, :] = jnp.exp(la)
            g_s[rows, :] = dy_ref[rows, :].astype(F32) * _gelu(lg_ref[rows, :].astype(F32))

        def load(k):
            return a_s[pl.ds(k + 1, NSEG, stride=seg), :], g_s[pl.ds(k, NSEG, stride=seg), :]

        def step(kk, c):
            g, q, an, dh = c
            k = seg - 1 - kk
            nan, ndh = load(jnp.maximum(k - 1, 0))
            rows = pl.ds(k, NSEG, stride=seg)
            g = dh + an * g
            q = an * q
            g_s[rows, :] = g
            q_s[rows, :] = q
            return g, q, nan, ndh

        lax.fori_loop(0, seg, step, (jnp.zeros((NSEG, LANE), F32), jnp.ones((NSEG, LANE), F32)) + load(seg - 1), unroll=4)
        nxt = [None] * NSEG
        carry = jnp.zeros((1, LANE), F32)
        for s in range(NSEG - 1, -1, -1):
            nxt[s] = carry
            start = pl.ds(s * seg, 1)
            carry = g_s[start, :] + q_s[start, :] * carry

        zrow = jnp.zeros((1, LANE), F32)
        dsp = zrow
        dba = zrow
        dbx = zrow
        dcb = zrow
        dwa = jnp.zeros((LANE, LANE), F32)
        dwx = jnp.zeros((LANE, LANE), F32)
        for s in range(NSEG):
            for r0 in range(s * seg, (s + 1) * seg, rb):
                rows = pl.ds(r0, rb)
                g = g_s[rows, :] + q_s[rows, :] * nxt[s]
                xc = conv(r0)
                xcb, r, i, la = _lru_gates(xc, wa_ref[...], ba_ref[...], wx_ref[...], bx_ref[...], sp)
                a = jnp.exp(la)
                om = _one_minus_exp(2.0 * la)
                mult = jnp.sqrt(om)
                hprev = hp[pl.ds(8 + r0 - 1, rb), :]
                da = g * hprev
                dmult = g * i * xc
                di = g * mult * xc
                dxc = g * mult * i
                dla = da * a - dmult * (1.0 - om) / mult
                dr = dla * (-LRU_C) * sp
                dsp = dsp + jnp.sum(dla * (-LRU_C) * r, axis=0, keepdims=True)
                dpr = dr * r * (1.0 - r)
                dpi = di * i * (1.0 - i)
                dba = dba + jnp.sum(dpr, axis=0, keepdims=True)
                dbx = dbx + jnp.sum(dpi, axis=0, keepdims=True)
                dprb = dpr.astype(BF16)
                dpib = dpi.astype(BF16)
                dxc = dxc + _dot(dprb, wat_ref[...]) + _dot(dpib, wxt_ref[...])
                dwa = dwa + _dot(xcb, dprb, TN)
                dwx = dwx + _dot(xcb, dpib, TN)
                dxc_s[rows, :] = dxc
                dcb = dcb + jnp.sum(dxc, axis=0, keepdims=True)
                lg = lg_ref[rows, :].astype(F32)
                dlg_ref[rows, :] = (dy_ref[rows, :].astype(F32) * hs_ref[rows, :] * _gelu_grad(lg)).astype(BF16)
        dcw = [zrow] * CONV_W
        for r0 in range(0, t, rb):
            rows = pl.ds(r0, rb)
            dlx = sum(cwv[k:k + 1] * dxc_s[pl.ds(r0 + (CONV_W - 1) - k, rb), :] for k in range(CONV_W))
            dlx_ref[rows, :] = dlx.astype(BF16)
            dxc = dxc_s[rows, :]
            for k in range(CONV_W):
                dcw[k] = dcw[k] + jnp.sum(dxc * xp[pl.ds(8 + r0 - (CONV_W - 1) + k, rb), :], axis=0, keepdims=True)
        dcw_ref[...] = jnp.concatenate(dcw, axis=0)
        dcb_ref[...] = dcb
        dwa_ref[...] = dwa
        dwx_ref[...] = dwx
        dba_ref[...] = dba
        dbx_ref[...] = dbx
        dlam_ref[...] = dsp * (-_sigmoid(-lamv))

    col = lambda c0: pl.BlockSpec((t, LANE), functools.partial(lambda i, c0: (0, c0 + i), c0=c0))
    par = lambda k: pl.BlockSpec((k, LANE), lambda i: (0, i))
    mat = pl.BlockSpec((None, LANE, LANE), lambda i: (i, 0, 0))
    vec = jax.ShapeDtypeStruct((1, w), F32)
    big = lambda: pltpu.VMEM((t + 8, LANE), F32)
    return pl.pallas_call(
        body, name=name, grid=(nt,),
        in_specs=[col(0), col(nt), col(0), col(0), par(CONV_W), par(1), mat, par(1), mat, par(1), par(1), mat, mat],
        out_specs=[col(0), col(0), par(CONV_W), par(1), mat, par(1), mat, par(1), par(1)],
        out_shape=[jax.ShapeDtypeStruct((t, w), BF16), jax.ShapeDtypeStruct((t, w), BF16),
                   jax.ShapeDtypeStruct((CONV_W, w), F32), vec, jax.ShapeDtypeStruct((nt, LANE, LANE), F32), vec,
                   jax.ShapeDtypeStruct((nt, LANE, LANE), F32), vec, vec],
        scratch_shapes=[big(), big(), big(), pltpu.VMEM((t, LANE), F32), pltpu.VMEM((t, LANE), F32), big()],
        compiler_params=_cp(("parallel",)))(proj, proj, hs, dya, cw, cb, wa_t, ba, wx_t, bx, lam, wat_t, wxt_t)


def _attn_scores(q_ref, kp, bias_ref, c, hd):
    r0 = pl.multiple_of(c * CHUNK, CHUNK)
    qc = q_ref[pl.ds(r0, CHUNK), :]
    kb = kp[pl.ds(r0, ATT_BAND), :]
    s = _dot(qc, kb, NT) * (hd ** -0.5) + bias_ref[...]
    kpos = lax.broadcasted_iota(jnp.int32, (CHUNK, ATT_BAND), 1)
    s = jnp.where(kpos + (c - ATT_LEFT) * CHUNK >= 0, s, MASK_VALUE)
    m = jnp.max(s, axis=-1, keepdims=True)
    e = jnp.exp(s - m)
    p = e / jnp.sum(e, axis=-1, keepdims=True)
    return r0, qc, kb, p


def _attn_pad_copy(src_ref, dst, t, pad):
    dst[pl.ds(0, pad), :] = jnp.zeros((pad, dst.shape[1]), dst.dtype)
    rb = min(512, t)
    for r0 in range(0, t, rb):
        dst[pl.ds(pad + r0, rb), :] = src_ref[pl.ds(r0, rb), :]


def _attn_fwd(proj, bias, w, name):
    t = proj.shape[0]
    hd = w // ATT_HEADS
    pad = ATT_LEFT * CHUNK
    qo, ko, vo = 2 * w // hd, 3 * w // hd, 4 * w // hd

    def body(q_ref, k_ref, v_ref, bias_ref, o_ref, kp, vp):
        _attn_pad_copy(k_ref, kp, t, pad)
        _attn_pad_copy(v_ref, vp, t, pad)

        def chunk(c, _):
            r0, _, _, p = _attn_scores(q_ref, kp, bias_ref, c, hd)
            vb = vp[pl.ds(r0, ATT_BAND), :]
            o_ref[pl.ds(r0, CHUNK), :] = _dot(p.astype(BF16), vb).astype(BF16)
            return 0

        lax.fori_loop(0, t // CHUNK, chunk, 0, unroll=2)

    col = lambda c0: pl.BlockSpec((t, hd), functools.partial(lambda h, c0: (0, c0 + h), c0=c0))
    return pl.pallas_call(
        body, name=name, grid=(ATT_HEADS,),
        in_specs=[col(qo), col(ko), col(vo), pl.BlockSpec((None, CHUNK, ATT_BAND), lambda h: (h, 0, 0))],
        out_specs=col(0), out_shape=jax.ShapeDtypeStruct((t, w), BF16),
        scratch_shapes=[pltpu.VMEM((t + pad, hd), BF16), pltpu.VMEM((t + pad, hd), BF16)],
        compiler_params=_cp(("parallel",)))(proj, proj, proj, bias)


def _attn_bwd(proj, bias, do, w, name):
    t = proj.shape[0]
    hd = w // ATT_HEADS
    pad = ATT_LEFT * CHUNK
    qo, ko, vo = 2 * w // hd, 3 * w // hd, 4 * w // hd

    def body(q_ref, k_ref, v_ref, bias_ref, do_ref, dq_ref, dk_ref, dv_ref, db_ref, kp, vp, dkp, dvp):
        _attn_pad_copy(k_ref, kp, t, pad)
        _attn_pad_copy(v_ref, vp, t, pad)
        rb = min(512, t)
        for r0 in range(0, t + pad, rb):
            n = min(rb, t + pad - r0)
            dkp[pl.ds(r0, n), :] = jnp.zeros((n, hd), F32)
            dvp[pl.ds(r0, n), :] = jnp.zeros((n, hd), F32)
        db_ref[...] = jnp.zeros_like(db_ref)
        scale = hd ** -0.5

        def chunk(c, _):
            r0, qc, kb, p = _attn_scores(q_ref, kp, bias_ref, c, hd)
            band = pl.ds(r0, ATT_BAND)
            vb = vp[band, :]
            doc = do_ref[pl.ds(r0, CHUNK), :]
            dp = _dot(doc, vb, NT)
            ds = p * (dp - jnp.sum(dp * p, axis=-1, keepdims=True))
            db_ref[...] += ds
            dsb = ds.astype(BF16)
            dq_ref[pl.ds(r0, CHUNK), :] = (_dot(dsb, kb) * scale).astype(BF16)
            dkp[band, :] += _dot(dsb, qc, TN) * scale
            dvp[band, :] += _dot(p.astype(BF16), doc, TN)
            return 0

        lax.fori_loop(0, t // CHUNK, chunk, 0, unroll=2)
        for r0 in range(0, t, rb):
            dk_ref[pl.ds(r0, rb), :] = dkp[pl.ds(pad + r0, rb), :].astype(BF16)
            dv_ref[pl.ds(r0, rb), :] = dvp[pl.ds(pad + r0, rb), :].astype(BF16)

    col = lambda c0: pl.BlockSpec((t, hd), functools.partial(lambda h, c0: (0, c0 + h), c0=c0))
    tb = pl.BlockSpec((None, CHUNK, ATT_BAND), lambda h: (h, 0, 0))
    o = jax.ShapeDtypeStruct((t, w), BF16)
    return pl.pallas_call(
        body, name=name, grid=(ATT_HEADS,),
        in_specs=[col(qo), col(ko), col(vo), tb, col(0)],
        out_specs=[col(0), col(0), col(0), tb],
        out_shape=[o, o, o, jax.ShapeDtypeStruct((ATT_HEADS, CHUNK, ATT_BAND), F32)],
        scratch_shapes=[pltpu.VMEM((t + pad, hd), BF16), pltpu.VMEM((t + pad, hd), BF16),
                        pltpu.VMEM((t + pad, hd), F32), pltpu.VMEM((t + pad, hd), F32)],
        compiler_params=_cp(("parallel",)))(proj, proj, proj, bias, do)


def _rel_index():
    q_pos = ATT_LEFT * CHUNK + np.arange(CHUNK)
    k_pos = np.arange(ATT_BAND)
    return (np.clip(q_pos[:, None] - k_pos[None, :], -MAX_REL, MAX_REL) + MAX_REL).astype(np.int32).reshape(-1)


def _bias_expand(rel_bias, name):
    nl, nh, _ = rel_bias.shape
    n = CHUNK * ATT_BAND
    kb = n // 8
    idx = jnp.asarray(_rel_index().reshape(1, n))
    tab = jnp.pad(rel_bias, ((0, 0), (0, 0), (0, N_REL_PAD - N_REL)))

    def body(t_ref, i_ref, o_ref):
        onehot = (lax.broadcasted_iota(jnp.int32, (N_REL_PAD, kb), 0) == i_ref[...]).astype(F32)
        o_ref[...] = lax.dot_general(t_ref[...], onehot, NN, precision=lax.Precision.HIGHEST,
                                     preferred_element_type=F32)

    out = pl.pallas_call(
        body, name=name, grid=(nl, n // kb),
        in_specs=[pl.BlockSpec((None, nh, N_REL_PAD), lambda l, j: (l, 0, 0)), pl.BlockSpec((1, kb), lambda l, j: (0, j))],
        out_specs=pl.BlockSpec((None, nh, kb), lambda l, j: (l, 0, j)),
        out_shape=jax.ShapeDtypeStruct((nl, nh, n), F32), compiler_params=_cp(("parallel", "parallel")))(tab, idx)
    return out.reshape(nl, nh, CHUNK, ATT_BAND)


def _bias_reduce(dbias, name):
    nl, nh = dbias.shape[:2]
    n = CHUNK * ATT_BAND
    kb = n // 8
    idx = jnp.asarray(_rel_index().reshape(n, 1))

    def body(d_ref, i_ref, o_ref):
        @pl.when(pl.program_id(1) == 0)
        def _():
            o_ref[...] = jnp.zeros_like(o_ref)

        onehot = (lax.broadcasted_iota(jnp.int32, (kb, N_REL_PAD), 1) == i_ref[...]).astype(F32)
        o_ref[...] += lax.dot_general(d_ref[...], onehot, NN, precision=lax.Precision.HIGHEST,
                                      preferred_element_type=F32)

    out = pl.pallas_call(
        body, name=name, grid=(nl, n // kb),
        in_specs=[pl.BlockSpec((None, nh, kb), lambda l, j: (l, 0, j)), pl.BlockSpec((kb, 1), lambda l, j: (j, 0))],
        out_specs=pl.BlockSpec((None, nh, N_REL_PAD), lambda l, j: (l, 0, 0)),
        out_shape=jax.ShapeDtypeStruct((nl, nh, N_REL_PAD), F32),
        compiler_params=_cp(("parallel", "arbitrary")))(dbias.reshape(nl, nh, n), idx)
    return out[:, :, :N_REL]


def _s5_prep_math(a_re, a_im, ls, btr, bti):
    step = jnp.exp(ls)
    mag = jnp.exp(a_re * step)
    ang = a_im * step
    lr = mag * jnp.cos(ang)
    li = mag * jnp.sin(ang)
    den = a_re * a_re + a_im * a_im
    nr = lr - 1.0
    cr = (nr * a_re + li * a_im) / den
    ci = (li * a_re - nr * a_im) / den
    bbr = cr[:, None, :] * btr - ci[:, None, :] * bti
    bbi = cr[:, None, :] * bti + ci[:, None, :] * btr
    return lr, li, bbr, bbi


def _s5_prep(a_re, a_im, ls, btr, bti, name):
    nl, g, p = a_re.shape
    h = btr.shape[2]

    def body(ar, ai, l_ref, br, bi, o1, o2, o3, o4):
        r = _s5_prep_math(ar[...], ai[...], l_ref[...], br[...], bi[...])
        o1[...], o2[...], o3[...], o4[...] = r

    m2 = pl.BlockSpec((None, g, p), lambda l: (l, 0, 0))
    m1 = pl.BlockSpec((None, g, 1), lambda l: (l, 0, 0))
    m3 = pl.BlockSpec((None, g, h, p), lambda l: (l, 0, 0, 0))
    s2 = jax.ShapeDtypeStruct((nl, g, p), F32)
    s3 = jax.ShapeDtypeStruct((nl, g, h, p), F32)
    return pl.pallas_call(body, name=name, grid=(nl,), in_specs=[m2, m2, m1, m3, m3], out_specs=[m2, m2, m3, m3],
                          out_shape=[s2, s2, s3, s3], compiler_params=_cp(("parallel",)))(a_re, a_im, ls, btr, bti)


def _s5_prep_bwd(a_re, a_im, ls, btr, bti, dlr, dli, dbbr, dbbi, name):
    nl, g, p = a_re.shape
    h = btr.shape[2]

    def body(ar, ai, l_ref, br, bi, g1, g2, g3, g4, o1, o2, o3, o4, o5):
        _, vjp = jax.vjp(_s5_prep_math, ar[...], ai[...], l_ref[...], br[...], bi[...])
        o1[...], o2[...], o3[...], o4[...], o5[...] = vjp((g1[...], g2[...], g3[...], g4[...]))

    m2 = pl.BlockSpec((None, g, p), lambda l: (l, 0, 0))
    m1 = pl.BlockSpec((None, g, 1), lambda l: (l, 0, 0))
    m3 = pl.BlockSpec((None, g, h, p), lambda l: (l, 0, 0, 0))
    s2 = jax.ShapeDtypeStruct((nl, g, p), F32)
    s1 = jax.ShapeDtypeStruct((nl, g, 1), F32)
    s3 = jax.ShapeDtypeStruct((nl, g, h, p), F32)
    return pl.pallas_call(body, name=name, grid=(nl,), in_specs=[m2, m2, m1, m3, m3, m2, m2, m3, m3],
                          out_specs=[m2, m2, m1, m3, m3], out_shape=[s2, s2, s1, s3, s3],
                          compiler_params=_cp(("parallel",)))(a_re, a_im, ls, btr, bti, dlr, dli, dbbr, dbbi)


def _cpow(lr, li, n):
    rr = ri = None
    br, bi = lr, li
    while n:
        if n & 1:
            rr, ri = (br, bi) if rr is None else (rr * br - ri * bi, rr * bi + ri * br)
        n >>= 1
        if n:
            br, bi = br * br - bi * bi, 2.0 * br * bi
    return rr, ri


def _permute_rows(src, dst, seg, inverse):
    def body(k, _):
        tile = pl.ds(pl.multiple_of(k * NSEG, NSEG), NSEG)
        spread = pl.ds(k, NSEG, stride=seg)
        if inverse:
            dst[spread, :] = src[tile, :]
        else:
            dst[tile, :] = src[spread, :]
        return 0

    lax.fori_loop(0, seg, body, 0, unroll=8)


def _s5_states(up, bbt_ref, lam_ref, xs, t):
    seg = t // NSEG
    rb = min(512, t)
    nj = xs.shape[0]
    nh = nj // 2
    for r0 in range(0, t, rb):
        bu = _dot(up[pl.ds(r0, rb), :].astype(BF16), bbt_ref[...])
        for j in range(nj):
            xs[j, pl.ds(r0, rb), :] = bu[:, j * LANE:(j + 1) * LANE]
    lamv = lam_ref[...]
    lr = [jnp.broadcast_to(lamv[j:j + 1], (NSEG, LANE)) for j in range(nh)]
    li = [jnp.broadcast_to(lamv[nh + j:nh + j + 1], (NSEG, LANE)) for j in range(nh)]
    zero = jnp.zeros((NSEG, LANE), F32)

    def load(k):
        rows = pl.ds(pl.multiple_of(k * NSEG, NSEG), NSEG)
        return (tuple(xs.at[j][rows, :] for j in range(nh)), tuple(xs.at[nh + j][rows, :] for j in range(nh)))

    def step(k, c):
        xr, xi, br, bi = c
        nbr, nbi = load(jnp.minimum(k + 1, seg - 1))
        rows = pl.ds(pl.multiple_of(k * NSEG, NSEG), NSEG)
        nxr, nxi = [], []
        for j in range(nh):
            r = lr[j] * xr[j] - li[j] * xi[j] + br[j]
            i = lr[j] * xi[j] + li[j] * xr[j] + bi[j]
            xs.at[j][rows, :] = r
            xs.at[nh + j][rows, :] = i
            nxr.append(r)
            nxi.append(i)
        return tuple(nxr), tuple(nxi), nbr, nbi

    xr, xi, _, _ = lax.fori_loop(0, seg, step, ((zero,) * nh, (zero,) * nh) + load(0), unroll=4)
    pw = [_cpow(lr[j], li[j], seg) for j in range(nh)]
    sub = lax.broadcasted_iota(jnp.int32, (NSEG, LANE), 0)
    ctr, cti = [], []
    for j in range(nh):
        plr, pli = pw[j]
        cr, ci = zero, zero
        for s in range(1, NSEG):
            er = pltpu.roll(xr[j], 1, 0)
            ei = pltpu.roll(xi[j], 1, 0)
            pr_ = pltpu.roll(cr, 1, 0)
            pi_ = pltpu.roll(ci, 1, 0)
            nr = er + plr * pr_ - pli * pi_
            ni = ei + plr * pi_ + pli * pr_
            cr = jnp.where(sub == s, nr, cr)
            ci = jnp.where(sub == s, ni, ci)
        ctr.append(cr)
        cti.append(ci)

    def fix(k, c):
        rr, ri, xr_, xi_ = c
        nxr, nxi = load(jnp.minimum(k + 1, seg - 1))
        rows = pl.ds(pl.multiple_of(k * NSEG, NSEG), NSEG)
        nr, ni = [], []
        for j in range(nh):
            r = lr[j] * rr[j] - li[j] * ri[j]
            i = lr[j] * ri[j] + li[j] * rr[j]
            xs.at[j][rows, :] = xr_[j] + r
            xs.at[nh + j][rows, :] = xi_[j] + i
            nr.append(r)
            ni.append(i)
        return tuple(nr), tuple(ni), nxr, nxi

    lax.fori_loop(0, seg, fix, (tuple(ctr), tuple(cti)) + load(0), unroll=4)
    return lr, li, pw


def _s5_fwd(proj, bbt, cmat, lam, dvec, w, name):
    t = proj.shape[0]
    nt = w // LANE
    nj = bbt.shape[2] // LANE
    seg = t // NSEG
    rb = min(512, t)
    uo = 5 * nt

    def body(u_ref, bbt_ref, cm_ref, lam_ref, d_ref, s5o_ref, yc_ref, xs, uf, up):
        for r0 in range(0, t, rb):
            uf[pl.ds(r0, rb), :] = u_ref[pl.ds(r0, rb), :].astype(F32)
        _permute_rows(uf, up, seg, False)
        _s5_states(up, bbt_ref, lam_ref, xs, t)
        for r0 in range(0, t, rb):
            rows = pl.ds(r0, rb)
            y = d_ref[...] * up[rows, :]
            for j in range(nj):
                y = y + _dot(xs.at[j][rows, :].astype(BF16), cm_ref[pl.ds(j * LANE, LANE), :])
            uf[rows, :] = y
        _permute_rows(uf, s5o_ref, seg, True)
        for r0 in range(0, t, rb):
            rows = pl.ds(r0, rb)
            yc_ref[rows, :] = _gelu(s5o_ref[rows, :]).astype(BF16)

    col = lambda c0: pl.BlockSpec((t, LANE), functools.partial(lambda i, c0: (0, c0 + i), c0=c0))
    return pl.pallas_call(
        body, name=name, grid=(nt,),
        in_specs=[col(uo), pl.BlockSpec((None, LANE, nj * LANE), lambda i: (i, 0, 0)),
                  pl.BlockSpec((None, nj * LANE, LANE), lambda i: (i, 0, 0)),
                  pl.BlockSpec((None, nj, LANE), lambda i: (i, 0, 0)), pl.BlockSpec((1, LANE), lambda i: (0, i))],
        out_specs=[col(0), col(0)],
        out_shape=[jax.ShapeDtypeStruct((t, w), F32), jax.ShapeDtypeStruct((t, w), BF16)],
        scratch_shapes=[pltpu.VMEM((nj, t, LANE), F32), pltpu.VMEM((t, LANE), F32), pltpu.VMEM((t, LANE), F32)],
        compiler_params=_cp(("parallel",)))(proj, bbt, cmat, lam, dvec)


def _s5_bwd(proj, dy, bbt, bbm, cmt, lam, dvec, w, name):
    t = proj.shape[0]
    nt = w // LANE
    nj = bbt.shape[2] // LANE
    nh = nj // 2
    seg = t // NSEG
    rb = min(512, t)
    uo = 5 * nt

    def body(u_ref, dy_ref, bbt_ref, bbm_ref, cmt_ref, lam_ref, d_ref, du_ref, dbbt_ref, dcm_ref, dlam_ref, dd_ref,
             xs, gs, uf, up, dyp):
        for r0 in range(0, t, rb):
            uf[pl.ds(r0, rb), :] = u_ref[pl.ds(r0, rb), :].astype(F32)
        _permute_rows(uf, up, seg, False)
        for r0 in range(0, t, rb):
            uf[pl.ds(r0, rb), :] = dy_ref[pl.ds(r0, rb), :].astype(F32)
        _permute_rows(uf, dyp, seg, False)
        lr, li, pw = _s5_states(up, bbt_ref, lam_ref, xs, t)
        dcm = [jnp.zeros((LANE, LANE), F32) for _ in range(nj)]
        for r0 in range(0, t, rb):
            rows = pl.ds(r0, rb)
            dyb = dyp[rows, :].astype(BF16)
            dx = _dot(dyb, cmt_ref[...])
            for j in range(nj):
                gs.at[j][rows, :] = dx[:, j * LANE:(j + 1) * LANE]
                dcm[j] = dcm[j] + _dot(xs.at[j][rows, :].astype(BF16), dyb, TN)
        for j in range(nj):
            dcm_ref[pl.ds(j * LANE, LANE), :] = dcm[j]
        zero = jnp.zeros((NSEG, LANE), F32)
        sub = lax.broadcasted_iota(jnp.int32, (NSEG, LANE), 0)
        tile = lambda k: pl.ds(pl.multiple_of(k * NSEG, NSEG), NSEG)

        def gload(k):
            return (tuple(gs.at[j][tile(k), :] for j in range(nh)), tuple(gs.at[nh + j][tile(k), :] for j in range(nh)))

        def xload(k):
            return (tuple(xs.at[j][tile(k), :] for j in range(nh)), tuple(xs.at[nh + j][tile(k), :] for j in range(nh)))

        def step(kk, c):
            gr, gi, dr, di = c
            k = seg - 1 - kk
            ndr, ndi = gload(jnp.maximum(k - 1, 0))
            ngr, ngi = [], []
            for j in range(nh):
                r = lr[j] * gr[j] + li[j] * gi[j] + dr[j]
                i = lr[j] * gi[j] - li[j] * gr[j] + di[j]
                gs.at[j][tile(k), :] = r
                gs.at[nh + j][tile(k), :] = i
                ngr.append(r)
                ngi.append(i)
            return tuple(ngr), tuple(ngi), ndr, ndi

        g0r, g0i, _, _ = lax.fori_loop(0, seg, step, ((zero,) * nh, (zero,) * nh) + gload(seg - 1), unroll=4)
        ctr, cti = [], []
        for j in range(nh):
            plr, pli = pw[j]
            cr, ci = zero, zero
            for s in range(NSEG - 2, -1, -1):
                sr = pltpu.roll(g0r[j], NSEG - 1, 0)
                si = pltpu.roll(g0i[j], NSEG - 1, 0)
                pr_ = pltpu.roll(cr, NSEG - 1, 0)
                pi_ = pltpu.roll(ci, NSEG - 1, 0)
                nr = sr + plr * pr_ + pli * pi_
                ni = si + plr * pi_ - pli * pr_
                cr = jnp.where(sub == s, nr, cr)
                ci = jnp.where(sub == s, ni, ci)
            ctr.append(cr)
            cti.append(ci)

        def fix_step(k, c, xr, xi, gr, gi):
            rr, ri, ar, ai = c
            nr, ni, nar, nai = [], [], [], []
            for j in range(nh):
                r = lr[j] * rr[j] + li[j] * ri[j]
                i = lr[j] * ri[j] - li[j] * rr[j]
                g_r = gr[j] + r
                g_i = gi[j] + i
                gs.at[j][tile(k), :] = g_r
                gs.at[nh + j][tile(k), :] = g_i
                nar.append(ar[j] + g_r * xr[j] + g_i * xi[j])
                nai.append(ai[j] + g_i * xr[j] - g_r * xi[j])
                nr.append(r)
                ni.append(i)
            return tuple(nr), tuple(ni), tuple(nar), tuple(nai)

        def fix(kk, c):
            k = seg - 1 - kk
            nk = jnp.maximum(k - 1, 0)
            ngr, ngi = gload(nk)
            nxr, nxi = xload(jnp.maximum(nk - 1, 0))
            return fix_step(k, c[:4], c[6], c[7], c[4], c[5]) + (ngr, ngi, nxr, nxi)

        init = (tuple(ctr), tuple(cti), (zero,) * nh, (zero,) * nh) + gload(seg - 1) + xload(seg - 2)
        c = lax.fori_loop(0, seg - 1, fix, init, unroll=4)
        lxr, lxi = xload(seg - 1)
        shift = lambda v: jnp.where(sub == 0, 0.0, pltpu.roll(v, 1, 0))
        _, _, ar, ai = fix_step(0, c[:4], tuple(shift(v) for v in lxr), tuple(shift(v) for v in lxi), c[4], c[5])
        for j in range(nh):
            dlam_ref[pl.ds(j, 1), :] = jnp.sum(ar[j], axis=0, keepdims=True)
            dlam_ref[pl.ds(nh + j, 1), :] = jnp.sum(ai[j], axis=0, keepdims=True)
        dbb = [jnp.zeros((LANE, LANE), F32) for _ in range(nj)]
        dd = jnp.zeros((1, LANE), F32)
        for r0 in range(0, t, rb):
            rows = pl.ds(r0, rb)
            uv = up[rows, :]
            ub = uv.astype(BF16)
            dyf = dyp[rows, :]
            du = d_ref[...] * dyf
            dd = dd + jnp.sum(dyf * uv, axis=0, keepdims=True)
            for j in range(nj):
                gb = gs.at[j][rows, :].astype(BF16)
                du = du + _dot(gb, bbm_ref[pl.ds(j * LANE, LANE), :])
                dbb[j] = dbb[j] + _dot(ub, gb, TN)
            uf[rows, :] = du
        for j in range(nj):
            dbbt_ref[:, pl.ds(j * LANE, LANE)] = dbb[j]
        dd_ref[...] = dd
        _permute_rows(uf, up, seg, True)
        for r0 in range(0, t, rb):
            rows = pl.ds(r0, rb)
            du_ref[rows, :] = up[rows, :].astype(BF16)

    col = lambda c0: pl.BlockSpec((t, LANE), functools.partial(lambda i, c0: (0, c0 + i), c0=c0))
    wide = pl.BlockSpec((None, LANE, nj * LANE), lambda i: (i, 0, 0))
    tall = pl.BlockSpec((None, nj * LANE, LANE), lambda i: (i, 0, 0))
    lam_s = pl.BlockSpec((None, nj, LANE), lambda i: (i, 0, 0))
    vec = pl.BlockSpec((1, LANE), lambda i: (0, i))
    flat = lambda: pltpu.VMEM((t, LANE), F32)
    return pl.pallas_call(
        body, name=name, grid=(nt,),
        in_specs=[col(uo), col(0), wide, tall, wide, lam_s, vec],
        out_specs=[col(0), wide, tall, lam_s, vec],
        out_shape=[jax.ShapeDtypeStruct((t, w), BF16), jax.ShapeDtypeStruct((nt, LANE, nj * LANE), F32),
                   jax.ShapeDtypeStruct((nt, nj * LANE, LANE), F32), jax.ShapeDtypeStruct((nt, nj, LANE), F32),
                   jax.ShapeDtypeStruct((1, w), F32)],
        scratch_shapes=[pltpu.VMEM((nj, t, LANE), F32), pltpu.VMEM((nj, t, LANE), F32), flat(), flat(), flat()],
        compiler_params=_cp(("parallel",)))(proj, dy, bbt, bbm, cmt, lam, dvec)


def _block_diag(x, nt):
    nb, r, c = x.shape
    b = nb // nt
    eye = jnp.eye(b, dtype=x.dtype)
    return jnp.einsum("ibrc,bk->ibrkc", x.reshape(nt, b, r, c), eye).reshape(nt, b * r, b * c)


def _block_diag_extract(x, b):
    nt, br, bc = x.shape
    r, c = br // b, bc // b
    eye = jnp.eye(b, dtype=x.dtype)
    return jnp.einsum("ibrkc,bk->ibrc", x.reshape(nt, b, r, b, c), eye).reshape(nt * b, r, c)


def _adamw_math(w, g, m, v):
    m = ADAM_B1 * m + (1.0 - ADAM_B1) * g
    v = ADAM_B2 * v + (1.0 - ADAM_B2) * (g * g)
    m_hat = m / (1.0 - ADAM_B1 ** ADAM_STEP)
    v_hat = v / (1.0 - ADAM_B2 ** ADAM_STEP)
    delta = -ADAM_LR * (m_hat / (jnp.sqrt(v_hat) + ADAM_EPS) + ADAM_WD * w)
    return delta, m, v


def _adamw_reduce(parts, w, m, v, name):
    n, r, c = parts.shape
    tr = r
    for cand in (512, 256, 128, 64, 32, 16, 8):
        if r % cand == 0 and cand * c * 4 <= 2 * 1024 * 1024:
            tr = cand
            break

    def body(p_ref, w_ref, m_ref, v_ref, g_ref, d_ref, nm_ref, nv_ref):
        g = p_ref[0].astype(F32)
        for q in range(1, n):
            g = g + p_ref[q].astype(F32)
        g_ref[...] = g
        d_ref[...], nm_ref[...], nv_ref[...] = _adamw_math(w_ref[...], g, m_ref[...], v_ref[...])

    row = pl.BlockSpec((tr, c), lambda i: (i, 0))
    o = jax.ShapeDtypeStruct((r, c), F32)
    return pl.pallas_call(
        body, name=name, grid=(r // tr,),
        in_specs=[pl.BlockSpec((n, tr, c), lambda i: (0, i, 0)), row, row, row],
        out_specs=[row, row, row, row], out_shape=[o, o, o, o],
        compiler_params=_cp(("parallel",)))(parts, w, m, v)


def _adamw_flat(w, g, m, v, name):
    r, c = w.shape

    def body(w_ref, g_ref, m_ref, v_ref, d_ref, nm_ref, nv_ref):
        d_ref[...], nm_ref[...], nv_ref[...] = _adamw_math(w_ref[...], g_ref[...], m_ref[...], v_ref[...])

    o = jax.ShapeDtypeStruct((r, c), F32)
    row = pl.BlockSpec((FLAT_ROWS, c), lambda i: (i, 0))
    return pl.pallas_call(body, name=name, grid=(r // FLAT_ROWS,), in_specs=[row] * 4, out_specs=[row] * 3,
                          out_shape=[o, o, o], compiler_params=_cp(("parallel",)))(w, g, m, v)


def _sum_rows8(x, name):
    n, r, c = x.shape

    def body(x_ref, o_ref):
        acc = x_ref[0]
        for q in range(1, n):
            acc = acc + x_ref[q]
        o_ref[...] = acc

    return pl.pallas_call(
        body, name=name, grid=(r // FLAT_ROWS,), in_specs=[pl.BlockSpec((n, FLAT_ROWS, c), lambda i: (0, i, 0))],
        out_specs=pl.BlockSpec((FLAT_ROWS, c), lambda i: (i, 0)), out_shape=jax.ShapeDtypeStruct((r, c), F32),
        compiler_params=_cp(("parallel",)))(x)


def _pair_sum(own, sib, core, name):
    _, _, r, c = own.shape
    tr = r
    for cand in (512, 256, 128, 64, 32, 16):
        if r % cand == 0 and cand * c * 2 <= 2 * 1024 * 1024:
            tr = cand
            break

    def body(c_ref, a_ref, b_ref, o_ref):
        o_ref[...] = (a_ref[...].astype(F32) + b_ref[...].astype(F32)).astype(BF16)

    grid_spec = pltpu.PrefetchScalarGridSpec(
        num_scalar_prefetch=1, grid=(NCHIP, r // tr),
        in_specs=[pl.BlockSpec((None, None, tr, c), lambda p, i, cr: (p, cr[0], i, 0)),
                  pl.BlockSpec((None, tr, c), lambda p, i, cr: (p, i, 0))],
        out_specs=pl.BlockSpec((None, tr, c), lambda p, i, cr: (p, i, 0)))
    return pl.pallas_call(body, name=name, grid_spec=grid_spec, out_shape=jax.ShapeDtypeStruct((NCHIP, r, c), BF16),
                          compiler_params=_cp(("parallel", "parallel")))(core, own, sib)


def _all_gather(xs, name):
    n = len(xs)

    def body(*refs):
        x_refs, o_refs = refs[:n], refs[n:2 * n]
        send_sems, recv_sems, local_sems = refs[2 * n:]
        x, y, c = lax.axis_index("x"), lax.axis_index("y"), lax.axis_index("c")
        me, sibling = (x, y, c), (x, y, 1 - c)
        chips = [(1 - x, y), (x, 1 - y), (1 - x, 1 - y)]

        def copy(t, k, block, to, src=None):
            dst = o_refs[t].at[4 * block[0] + 2 * block[1] + block[2]]
            return pltpu.make_async_remote_copy(
                src_ref=dst if src is None else src, dst_ref=dst,
                send_sem=send_sems.at[7 * t + k], recv_sem=recv_sems.at[7 * t + k],
                device_id=to, device_id_type=MESH)

        mine, first, passed = [], [], []
        for t in range(n):
            cp = pltpu.make_async_copy(x_refs[t], o_refs[t].at[4 * x + 2 * y + c], local_sems.at[t])
            cp.start()
            mine.append(cp)
            first.append(copy(t, 0, me, sibling, src=x_refs[t]))
            first += [copy(t, 1 + j, me, (*chip, c), src=x_refs[t]) for j, chip in enumerate(chips)]
        for cp in first:
            cp.start()
        for t in range(n):
            for j, chip in enumerate(chips):
                copy(t, 1 + j, (*chip, c), me).wait_recv()
                cp = copy(t, 4 + j, (*chip, c), sibling)
                cp.start()
                passed.append(cp)
        for t in range(n):
            copy(t, 0, sibling, me).wait_recv()
            for j, chip in enumerate(chips):
                copy(t, 4 + j, (*chip, 1 - c), me).wait_recv()
        for cp in first + passed:
            cp.wait_send()
        for cp in mine:
            cp.wait()

    any_spec = pl.BlockSpec(memory_space=pl.ANY)
    return pl.pallas_call(
        body, name=name, in_specs=[any_spec] * n, out_specs=[any_spec] * n,
        out_shape=[jax.ShapeDtypeStruct((NDEV,) + a.shape, a.dtype) for a in xs],
        scratch_shapes=[pltpu.SemaphoreType.DMA((7 * n,)), pltpu.SemaphoreType.DMA((7 * n,)),
                        pltpu.SemaphoreType.DMA((n,))],
        )(*xs)


def _sibling_exchange(xs, name):
    n = len(xs)

    def body(*refs):
        x_refs, o_refs = refs[:n], refs[n:2 * n]
        send_sems, recv_sems = refs[2 * n:]
        x, y, c = lax.axis_index("x"), lax.axis_index("y"), lax.axis_index("c")
        cps = []
        for t in range(n):
            cp = pltpu.make_async_remote_copy(
                src_ref=x_refs[t].at[:, 1 - c], dst_ref=o_refs[t], send_sem=send_sems.at[t], recv_sem=recv_sems.at[t],
                device_id=(x, y, 1 - c), device_id_type=MESH)
            cp.start()
            cps.append(cp)
        for cp in cps:
            cp.wait()

    any_spec = pl.BlockSpec(memory_space=pl.ANY)
    return pl.pallas_call(
        body, name=name, in_specs=[any_spec] * n, out_specs=[any_spec] * n,
        out_shape=[jax.ShapeDtypeStruct((a.shape[0],) + a.shape[2:], a.dtype) for a in xs],
        scratch_shapes=[pltpu.SemaphoreType.DMA((n,)), pltpu.SemaphoreType.DMA((n,))],
        )(*xs)


def _chip_exchange(xs, name):
    n = len(xs)

    def body(*refs):
        x_refs, o_refs = refs[:n], refs[n:2 * n]
        send_sems, recv_sems, local_sems = refs[2 * n:]
        x, y, c = lax.axis_index("x"), lax.axis_index("y"), lax.axis_index("c")
        my_chip = 2 * x + y
        chips = [(1 - x, y), (x, 1 - y), (1 - x, 1 - y)]
        cps = []
        for t in range(n):
            cp = pltpu.make_async_copy(x_refs[t].at[my_chip], o_refs[t].at[my_chip], local_sems.at[t])
            cp.start()
            cps.append(cp)
            for j, (px, py) in enumerate(chips):
                cp = pltpu.make_async_remote_copy(
                    src_ref=x_refs[t].at[2 * px + py], dst_ref=o_refs[t].at[my_chip],
                    send_sem=send_sems.at[3 * t + j], recv_sem=recv_sems.at[3 * t + j],
                    device_id=(px, py, c), device_id_type=MESH)
                cp.start()
                cps.append(cp)
        for cp in cps:
            cp.wait()

    any_spec = pl.BlockSpec(memory_space=pl.ANY)
    return pl.pallas_call(
        body, name=name, in_specs=[any_spec] * n, out_specs=[any_spec] * n,
        out_shape=[jax.ShapeDtypeStruct(a.shape, a.dtype) for a in xs],
        scratch_shapes=[pltpu.SemaphoreType.DMA((3 * n,)), pltpu.SemaphoreType.DMA((3 * n,)),
                        pltpu.SemaphoreType.DMA((n,))],
        )(*xs)


_SMALL = ["norm_mix_g", "lru_conv_b", "lru_wa", "lru_ba", "lru_wx", "lru_bx", "lru_lambda", "attn_rel_bias",
          "ssm_a_re", "ssm_a_im", "ssm_b_re", "ssm_b_im", "ssm_c_re", "ssm_c_im", "ssm_d", "ssm_log_step",
          "norm_ffn_g", "norm_final_g"]
_SMALL_SHARDED = ["gate_bias", "lru_conv_w"]
_BIG = ["w_in", "ssm_w_glu", "w_branch", "w_out", "w_ffn_gate", "w_ffn_up", "w_ffn_down"]
_WEIGHTS = ["norm_mix_g", "w_in", "gate_bias", "lru_conv_w", "lru_conv_b", "lru_wa", "lru_ba", "lru_wx", "lru_bx",
            "lru_lambda", "attn_rel_bias", "ssm_a_re", "ssm_a_im", "ssm_b_re", "ssm_b_im", "ssm_c_re", "ssm_c_im",
            "ssm_d", "ssm_log_step", "ssm_w_glu", "w_branch", "w_out", "norm_ffn_g", "w_ffn_gate", "w_ffn_up",
            "w_ffn_down", "norm_final_g"]


def _device_step(x, target, wts, gath, core):
    t, d = x.shape
    w = d // 2
    nl = wts["norm_mix_g"].shape[0]
    nt = w // LANE
    f = gath["w_ffn_gate"].shape[-1]
    ncol = gath["w_in"].shape[-1]
    tm = _tile(t, 1024)
    n_g = w // SSM_GROUP
    gpt = LANE // SSM_GROUP
    bw = w // LRU_BLOCKS
    bpt = LANE // bw

    gbias = jnp.transpose(gath["gate_bias"], (1, 2, 0, 3)).reshape(nl, N_BRANCH, d)
    convw = jnp.transpose(gath["lru_conv_w"], (1, 2, 0, 3)).reshape(nl, CONV_W, w)
    bias_tab = _bias_expand(wts["attn_rel_bias"], "bias_expand")
    btr = jnp.swapaxes(wts["ssm_b_re"], 2, 3)
    bti = jnp.swapaxes(wts["ssm_b_im"], 2, 3)
    ls3 = wts["ssm_log_step"][..., None]
    lam_r, lam_i, bbr, bbi = _s5_prep(wts["ssm_a_re"], wts["ssm_a_im"], ls3, btr, bti, "s5_prep")

    def layer_consts(l):
        c = {}
        c["wa_t"] = _block_diag(wts["lru_wa"][l], nt).astype(BF16)
        c["wx_t"] = _block_diag(wts["lru_wx"][l], nt).astype(BF16)
        c["wat_t"] = jnp.swapaxes(c["wa_t"], 1, 2)
        c["wxt_t"] = jnp.swapaxes(c["wx_t"], 1, 2)
        bd_r, bd_i = _block_diag(bbr[l], nt), _block_diag(bbi[l], nt)
        c["bbt"] = jnp.concatenate([bd_r, bd_i], axis=-1).astype(BF16)
        c["bbm"] = jnp.swapaxes(c["bbt"], 1, 2)
        cd_r, cd_i = _block_diag(wts["ssm_c_re"][l], nt), _block_diag(wts["ssm_c_im"][l], nt)
        c["cmt"] = jnp.concatenate([cd_r, -cd_i], axis=-1).astype(BF16)
        c["cmat"] = jnp.swapaxes(c["cmt"], 1, 2)
        nrow = gpt * SSM_P // LANE
        c["lam"] = jnp.concatenate([lam_r[l].reshape(nt, nrow, LANE), lam_i[l].reshape(nt, nrow, LANE)], axis=1)
        return c

    row1 = lambda a: a.reshape(1, -1)
    saved = []
    for l in range(nl):
        c = layer_consts(l)
        h1 = _rmsnorm_fwd(x, row1(wts["norm_mix_g"][l]), f"norm_mix_{l}")
        tn = _tile(ncol, 768)
        nn_ = ncol // tn
        proj = _mm(f"in_proj_{l}", NN, (t // tm, NDEV, nn_), 0, [h1, gath["w_in"]],
                   [pl.BlockSpec((tm, d), lambda i, j, n: (i, 0)),
                    pl.BlockSpec((None, None, d, tn), functools.partial(lambda i, j, n, l: (j, l, 0, n), l=l))],
                   jax.ShapeDtypeStruct((t, NDEV * ncol), BF16),
                   pl.BlockSpec((tm, tn), lambda i, j, n: (i, j * nn_ + n)), (8, LANE))
        ya, hs = _lru_fwd(proj, convw[l], row1(wts["lru_conv_b"][l]), c["wa_t"], row1(wts["lru_ba"][l]), c["wx_t"],
                          row1(wts["lru_bx"][l]), row1(wts["lru_lambda"][l]), w, f"lru_fwd_{l}")
        yb = _attn_fwd(proj, bias_tab[l], w, f"attn_fwd_{l}")
        s5o, yc = _s5_fwd(proj, c["bbt"], c["cmat"], c["lam"], row1(wts["ssm_d"][l]), w, f"s5_fwd_{l}")
        cb = d // NDEV

        def branch(y, wg, idx, nm):
            return _mm(nm, NN, (t // tm, NDEV), 0, [y, wg],
                       [pl.BlockSpec((tm, w), lambda i, j: (i, 0)),
                        pl.BlockSpec((None,) * len(idx(0)[:-2]) + (w, cb), lambda i, j: idx(j))],
                       jax.ShapeDtypeStruct((t, d), BF16), pl.BlockSpec((tm, cb), lambda i, j: (i, j)), (8, LANE))

        wb_idx = lambda b: (lambda j: (j, l, b, 0, 0))
        br_a = branch(ya, gath["w_branch"], wb_idx(0), f"branch_a_{l}")
        br_b = branch(yb, gath["w_branch"], wb_idx(1), f"branch_b_{l}")
        br_c0 = branch(yc, gath["w_branch"], wb_idx(2), f"branch_c_{l}")
        gl = branch(yc, gath["ssm_w_glu"], lambda j: (j, l, 0, 0), f"branch_glu_{l}")
        merged = _merge_fwd(proj, gbias[l], br_a, br_b, br_c0, gl, d, f"merge_fwd_{l}")
        tno = _tile(d, 1024)
        rb_ = d // NDEV
        x_mid = _mm(f"out_proj_{l}", NN, (t // tm, d // tno, NDEV), 1, [merged, gath["w_out"], x],
                    [pl.BlockSpec((tm, rb_), lambda i, n, k: (i, k)),
                     pl.BlockSpec((None, None, rb_, tno), functools.partial(lambda i, n, k, l: (k, l, 0, n), l=l)),
                     pl.BlockSpec((tm, tno), lambda i, n, k: (i, n))],
                    jax.ShapeDtypeStruct((t, d), F32), pl.BlockSpec((tm, tno), lambda i, n, k: (i, n)), (tm, tno),
                    epi=lambda acc, res: acc + res)
        h2 = _rmsnorm_fwd(x_mid, row1(wts["norm_ffn_g"][l]), f"norm_ffn_{l}")

        def ffn_in(wg, nm):
            return _mm(nm, NN, (t // tm, NDEV), 0, [h2, wg],
                       [pl.BlockSpec((tm, d), lambda i, j: (i, 0)),
                        pl.BlockSpec((None, None, d, f), functools.partial(lambda i, j, l: (j, l, 0, 0), l=l))],
                       jax.ShapeDtypeStruct((NDEV, t, f), BF16), pl.BlockSpec((None, tm, f), lambda i, j: (j, i, 0)),
                       (8, LANE))

        gp = ffn_in(gath["w_ffn_gate"], f"ffn_gate_{l}")
        up = ffn_in(gath["w_ffn_up"], f"ffn_up_{l}")
        act = _swiglu_fwd(gp, up, f"swiglu_fwd_{l}")
        x_next = _mm(f"ffn_down_{l}", NN, (t // tm, d // tno, NDEV), 1, [act, gath["w_ffn_down"], x_mid],
                     [pl.BlockSpec((None, tm, f), lambda i, n, k: (k, i, 0)),
                      pl.BlockSpec((None, None, f, tno), functools.partial(lambda i, n, k, l: (k, l, 0, n), l=l)),
                      pl.BlockSpec((tm, tno), lambda i, n, k: (i, n))],
                     jax.ShapeDtypeStruct((t, d), F32), pl.BlockSpec((tm, tno), lambda i, n, k: (i, n)), (tm, tno),
                     epi=lambda acc, res: acc + res)
        saved.append(dict(x=x, h1=h1, proj=proj, ya=ya, hs=hs, yb=yb, s5o=s5o, yc=yc, br_a=br_a, br_b=br_b,
                          br_c0=br_c0, gl=gl, merged=merged, x_mid=x_mid, h2=h2, gp=gp, up=up, act=act, c=c))
        x = x_next

    loss, dx, dxb, dgf = _loss_head(x, row1(wts["norm_final_g"]), target, "loss_head")

    gbuf = {k: lax.empty((NDEV,) + gath[k].shape[1:], BF16) for k in _BIG}
    sg = {k: [None] * nl for k in _SMALL + _SMALL_SHARDED if k != "norm_final_g"}
    dbias_tabs = [None] * nl
    dlr_l, dli_l, dbbr_l, dbbi_l = [None] * nl, [None] * nl, [None] * nl, [None] * nl
    tk = _tile(d, 1024)
    for l in range(nl - 1, -1, -1):
        s = saved[l]
        c = s["c"]
        lfix = lambda fn: functools.partial(fn, l=l)
        dact = _mm(f"d_act_{l}", NT, (t // tm, NDEV), 0, [dxb, gath["w_ffn_down"]],
                   [pl.BlockSpec((tm, d), lambda i, j: (i, 0)),
                    pl.BlockSpec((None, None, f, d), lfix(lambda i, j, l: (j, l, 0, 0)))],
                   jax.ShapeDtypeStruct((NDEV, t, f), BF16), pl.BlockSpec((None, tm, f), lambda i, j: (j, i, 0)), (8, LANE))
        tno = _tile(d, 1024)
        gbuf["w_ffn_down"] = _mm(
            f"dw_ffn_down_{l}", TN, (NDEV, d // tno, t // tm), 1, [s["act"], dxb, gbuf["w_ffn_down"]],
            [pl.BlockSpec((None, tm, f), lambda j, n, r: (j, r, 0)), pl.BlockSpec((tm, tno), lambda j, n, r: (r, n))],
            jax.ShapeDtypeStruct(gbuf["w_ffn_down"].shape, BF16),
            pl.BlockSpec((None, None, f, tno), lfix(lambda j, n, r, l: (j, l, 0, n))), (f, tno), alias=True)
        dgp, dup = _swiglu_bwd(s["gp"], s["up"], dact, f"swiglu_bwd_{l}")

        def ffn_dh(dz, wg, nm):
            return _mm(nm, NT, (t // tm, d // tk, NDEV), 1, [dz, wg],
                       [pl.BlockSpec((None, tm, f), lambda i, k, j: (j, i, 0)),
                        pl.BlockSpec((None, None, tk, f), lfix(lambda i, k, j, l: (j, l, k, 0)))],
                       jax.ShapeDtypeStruct((t, d), F32), pl.BlockSpec((tm, tk), lambda i, k, j: (i, k)), (tm, tk))

        dh2a = ffn_dh(dgp, gath["w_ffn_gate"], f"d_h2_gate_{l}")
        dh2b = ffn_dh(dup, gath["w_ffn_up"], f"d_h2_up_{l}")

        def ffn_dw(dz, key, nm):
            return _mm(nm, TN, (NDEV, d // tk, t // tm), 1, [s["h2"], dz, gbuf[key]],
                       [pl.BlockSpec((tm, tk), lambda j, k, r: (r, k)), pl.BlockSpec((None, tm, f), lambda j, k, r: (j, r, 0))],
                       jax.ShapeDtypeStruct(gbuf[key].shape, BF16),
                       pl.BlockSpec((None, None, tk, f), lfix(lambda j, k, r, l: (j, l, k, 0))), (tk, f), alias=True)

        gbuf["w_ffn_gate"] = ffn_dw(dgp, "w_ffn_gate", f"dw_ffn_gate_{l}")
        gbuf["w_ffn_up"] = ffn_dw(dup, "w_ffn_up", f"dw_ffn_up_{l}")
        dx_mid, dxmb, dg2 = _rmsnorm_bwd(s["x_mid"], row1(wts["norm_ffn_g"][l]), [dh2a, dh2b], dx, f"norm_ffn_bwd_{l}")
        sg["norm_ffn_g"][l] = dg2.reshape(-1)
        rb_ = d // NDEV
        dmerged = _mm(f"d_merged_{l}", NT, (t // tm, NDEV), 0, [dxmb, gath["w_out"]],
                      [pl.BlockSpec((tm, d), lambda i, j: (i, 0)),
                       pl.BlockSpec((None, None, rb_, d), lfix(lambda i, j, l: (j, l, 0, 0)))],
                      jax.ShapeDtypeStruct((t, d), BF16), pl.BlockSpec((tm, rb_), lambda i, j: (i, j)), (8, LANE))
        gbuf["w_out"] = _mm(
            f"dw_out_{l}", TN, (NDEV, d // tno, t // tm), 1, [s["merged"], dxmb, gbuf["w_out"]],
            [pl.BlockSpec((tm, rb_), lambda j, n, r: (r, j)), pl.BlockSpec((tm, tno), lambda j, n, r: (r, n))],
            jax.ShapeDtypeStruct(gbuf["w_out"].shape, BF16),
            pl.BlockSpec((None, None, rb_, tno), lfix(lambda j, n, r, l: (j, l, 0, n))), (rb_, tno), alias=True)
        dbr_a, dbr_b, dbr_c0, dgl, dgates, dgb = _merge_bwd(s["proj"], gbias[l], s["br_a"], s["br_b"], s["br_c0"], s["gl"],
                                                           dmerged, d, f"merge_bwd_{l}")
        sg["gate_bias"][l] = dgb.reshape(-1)
        cb = d // NDEV

        def branch_dy(dbr, wg, idx, nm, dt):
            nlead = len(idx(0, 0)) - 2
            return _mm(nm, NT, (t // tm, NDEV), 1, [dbr, wg],
                       [pl.BlockSpec((tm, cb), lambda i, j: (i, j)),
                        pl.BlockSpec((None,) * nlead + (w, cb), lambda i, j: idx(j, 0))],
                       jax.ShapeDtypeStruct((t, w), dt), pl.BlockSpec((tm, w), lambda i, j: (i, 0)), (tm, w))

        def branch_dw(y, dbr, key, idx, nm):
            nlead = len(idx(0)) - 2
            return _mm(nm, TN, (NDEV, t // tm), 1, [y, dbr, gbuf[key]],
                       [pl.BlockSpec((tm, w), lambda j, r: (r, 0)), pl.BlockSpec((tm, cb), lambda j, r: (r, j))],
                       jax.ShapeDtypeStruct(gbuf[key].shape, BF16),
                       pl.BlockSpec((None,) * nlead + (w, cb), lambda j, r: idx(j)), (w, cb), alias=True)

        wb_i = lambda b: (lambda j, z=0: (j, l, b, 0, 0))
        glu_i = lambda j, z=0: (j, l, 0, 0)
        dya = branch_dy(dbr_a, gath["w_branch"], wb_i(0), f"d_ya_{l}", BF16)
        dyb = branch_dy(dbr_b, gath["w_branch"], wb_i(1), f"d_yb_{l}", BF16)
        dyc1 = branch_dy(dbr_c0, gath["w_branch"], wb_i(2), f"d_yc_{l}", F32)
        dyc2 = branch_dy(dgl, gath["ssm_w_glu"], glu_i, f"d_yc_glu_{l}", F32)
        gbuf["w_branch"] = branch_dw(s["ya"], dbr_a, "w_branch", wb_i(0), f"dw_branch_a_{l}")
        gbuf["w_branch"] = branch_dw(s["yb"], dbr_b, "w_branch", wb_i(1), f"dw_branch_b_{l}")
        gbuf["w_branch"] = branch_dw(s["yc"], dbr_c0, "w_branch", wb_i(2), f"dw_branch_c_{l}")
        gbuf["ssm_w_glu"] = branch_dw(s["yc"], dgl, "ssm_w_glu", glu_i, f"dw_glu_{l}")
        ds5o = _gelu_bwd_sum(s["s5o"], dyc1, dyc2, f"gelu_bwd_{l}")
        du, dbbt, dcm, dlam, dd = _s5_bwd(s["proj"], ds5o, c["bbt"], c["bbm"], c["cmt"], c["lam"], row1(wts["ssm_d"][l]),
                                          w, f"s5_bwd_{l}")
        sp = gpt * SSM_P
        dbbr_l[l] = _block_diag_extract(dbbt[:, :, :sp], gpt)
        dbbi_l[l] = _block_diag_extract(dbbt[:, :, sp:], gpt)
        dcmt = jnp.swapaxes(dcm, 1, 2)
        sg["ssm_c_re"][l] = _block_diag_extract(dcmt[:, :, :sp], gpt).reshape(-1)
        sg["ssm_c_im"][l] = (-_block_diag_extract(dcmt[:, :, sp:], gpt)).reshape(-1)
        nrow = sp // LANE
        dlr_l[l] = dlam[:, :nrow].reshape(n_g, SSM_P)
        dli_l[l] = dlam[:, nrow:].reshape(n_g, SSM_P)
        sg["ssm_d"][l] = dd.reshape(-1)
        dq, dk, dv, dbt = _attn_bwd(s["proj"], bias_tab[l], dyb, w, f"attn_bwd_{l}")
        dbias_tabs[l] = dbt
        dlx, dlg, dcw, dcb, dwa, dba, dwx, dbx, dlm = _lru_bwd(
            s["proj"], s["hs"], dya, convw[l], row1(wts["lru_conv_b"][l]), c["wa_t"], row1(wts["lru_ba"][l]), c["wx_t"],
            row1(wts["lru_bx"][l]), row1(wts["lru_lambda"][l]), c["wat_t"], c["wxt_t"], w, f"lru_bwd_{l}")
        sg["lru_conv_w"][l] = dcw.reshape(-1)
        sg["lru_conv_b"][l] = dcb.reshape(-1)
        sg["lru_wa"][l] = _block_diag_extract(dwa, bpt).reshape(-1)
        sg["lru_wx"][l] = _block_diag_extract(dwx, bpt).reshape(-1)
        sg["lru_ba"][l] = dba.reshape(-1)
        sg["lru_bx"][l] = dbx.reshape(-1)
        sg["lru_lambda"][l] = dlm.reshape(-1)
        dproj = jnp.concatenate([dlx, dlg, dq, dk, dv, du, dgates], axis=1)
        tn = _tile(ncol, 768)
        nn_ = ncol // tn
        dh1 = _mm(f"d_h1_{l}", NT, (t // tm, d // tk, NDEV, nn_), 2, [dproj, gath["w_in"]],
                  [pl.BlockSpec((tm, tn), lambda i, k, j, n: (i, j * nn_ + n)),
                   pl.BlockSpec((None, None, tk, tn), lfix(lambda i, k, j, n, l: (j, l, k, n)))],
                  jax.ShapeDtypeStruct((t, d), F32), pl.BlockSpec((tm, tk), lambda i, k, j, n: (i, k)), (tm, tk))
        gbuf["w_in"] = _mm(
            f"dw_in_{l}", TN, (NDEV, d // tk, nn_, t // tm), 1, [s["h1"], dproj, gbuf["w_in"]],
            [pl.BlockSpec((tm, tk), lambda j, k, n, r: (r, k)), pl.BlockSpec((tm, tn), lambda j, k, n, r: (r, j * nn_ + n))],
            jax.ShapeDtypeStruct(gbuf["w_in"].shape, BF16),
            pl.BlockSpec((None, None, tk, tn), lfix(lambda j, k, n, r, l: (j, l, k, n))), (tk, tn), alias=True)
        dx, dxb, dg1 = _rmsnorm_bwd(s["x"], row1(wts["norm_mix_g"][l]), [dh1], dx_mid, f"norm_mix_bwd_{l}")
        sg["norm_mix_g"][l] = dg1.reshape(-1)

    da_re, da_im, dls, dbtr, dbti = _s5_prep_bwd(wts["ssm_a_re"], wts["ssm_a_im"], ls3, btr, bti, jnp.stack(dlr_l),
                                                 jnp.stack(dli_l), jnp.stack(dbbr_l), jnp.stack(dbbi_l), "s5_prep_bwd")
    drel = _bias_reduce(jnp.stack(dbias_tabs), "bias_reduce")
    small = {k: jnp.stack(v) for k, v in sg.items() if v[0] is not None}
    small["ssm_a_re"] = da_re.reshape(nl, -1)
    small["ssm_a_im"] = da_im.reshape(nl, -1)
    small["ssm_log_step"] = dls.reshape(nl, -1)
    small["ssm_b_re"] = jnp.swapaxes(dbtr, 2, 3).reshape(nl, -1)
    small["ssm_b_im"] = jnp.swapaxes(dbti, 2, 3).reshape(nl, -1)
    small["attn_rel_bias"] = drel.reshape(nl, -1)
    small["norm_final_g"] = dgf.reshape(-1)
    return loss, dx, gbuf, small


def kernel(x, norm_mix_g, w_in, gate_bias, lru_conv_w, lru_conv_b, lru_wa, lru_ba, lru_wx, lru_bx, lru_lambda, attn_rel_bias, ssm_a_re, ssm_a_im, ssm_b_re, ssm_b_im, ssm_c_re, ssm_c_im, ssm_d, ssm_log_step, ssm_w_glu, w_branch, w_out, norm_ffn_g, w_ffn_gate, w_ffn_up, w_ffn_down, norm_final_g, loss_target, m_norm_mix_g, m_w_in, m_gate_bias, m_lru_conv_w, m_lru_conv_b, m_lru_wa, m_lru_ba, m_lru_wx, m_lru_bx, m_lru_lambda, m_attn_rel_bias, m_ssm_a_re, m_ssm_a_im, m_ssm_b_re, m_ssm_b_im, m_ssm_c_re, m_ssm_c_im, m_ssm_d, m_ssm_log_step, m_ssm_w_glu, m_w_branch, m_w_out, m_norm_ffn_g, m_w_ffn_gate, m_w_ffn_up, m_w_ffn_down, m_norm_final_g, v_norm_mix_g, v_w_in, v_gate_bias, v_lru_conv_w, v_lru_conv_b, v_lru_wa, v_lru_ba, v_lru_wx, v_lru_bx, v_lru_lambda, v_attn_rel_bias, v_ssm_a_re, v_ssm_a_im, v_ssm_b_re, v_ssm_b_im, v_ssm_c_re, v_ssm_c_im, v_ssm_d, v_ssm_log_step, v_ssm_w_glu, v_w_branch, v_w_out, v_norm_ffn_g, v_w_ffn_gate, v_w_ffn_up, v_w_ffn_down, v_norm_final_g):
    args = locals()
    wts = {k: args[k] for k in _WEIGHTS}
    mom = {k: args["m_" + k] for k in _WEIGHTS}
    vel = {k: args["v_" + k] for k in _WEIGHTS}
    cx, cy, cc = lax.axis_index("x"), lax.axis_index("y"), lax.axis_index("c")
    me = 4 * cx + 2 * cy + cc
    nl = norm_mix_g.shape[0]

    names = _BIG + _SMALL_SHARDED
    gathered = _all_gather([wts[k].astype(BF16) for k in _BIG] + [wts[k] for k in _SMALL_SHARDED], "gather_weights")
    gath = dict(zip(names, gathered))

    loss, dx, gbuf, small = _device_step(x[0], loss_target[0], wts, gath, cc)

    own = [gbuf[k].reshape((NCHIP, 2) + gbuf[k].shape[1:]) for k in _BIG]
    sib = _sibling_exchange(own, "grad_sibling_exchange")
    core = cc.astype(jnp.int32).reshape(1)
    pair = []
    for k, a, b in zip(_BIG, own, sib):
        c_ = a.shape[-1]
        pair.append(_pair_sum(a.reshape(NCHIP, 2, -1, c_), b.reshape(NCHIP, -1, c_), core, f"pair_sum_{k}").reshape(b.shape))
    parts = _chip_exchange(pair, "grad_chip_exchange")
    out = {}
    for k, p in zip(_BIG, parts):
        shp = wts[k].shape
        c_ = shp[-1]
        g, dl, nm, nv = _adamw_reduce(p.reshape(NCHIP, -1, c_), wts[k].reshape(-1, c_), mom[k].reshape(-1, c_),
                                      vel[k].reshape(-1, c_), f"adamw_{k}")
        out[k] = tuple(a.reshape(shp) for a in (g, dl, nm, nv))

    order = _SMALL + _SMALL_SHARDED
    flat = jnp.concatenate([small[k].reshape(-1) for k in order])
    n_flat = flat.shape[0]
    flat = jnp.pad(flat, (0, (-n_flat) % (FLAT_ROWS * LANE))).reshape(-1, LANE)
    (allflat,) = _all_gather([flat], "gather_small_grads")
    gsum = _sum_rows8(allflat, "sum_small_grads").reshape(-1)
    gs, off = {}, 0
    for k in order:
        n = small[k].size
        gs[k] = gsum[off:off + n]
        off += n
    gs_loc = {}
    for k in _SMALL:
        gs_loc[k] = gs[k].reshape(wts[k].shape)
    ncb = gate_bias.shape[-1]
    gs_loc["gate_bias"] = lax.dynamic_slice_in_dim(gs["gate_bias"].reshape(nl, N_BRANCH, -1), me * ncb, ncb, axis=2)
    ncw = lru_conv_w.shape[-1]
    gs_loc["lru_conv_w"] = lax.dynamic_slice_in_dim(gs["lru_conv_w"].reshape(nl, CONV_W, -1), me * ncw, ncw, axis=2)

    def pack(dct):
        v = jnp.concatenate([dct[k].reshape(-1) for k in order])
        return jnp.pad(v, (0, (-v.shape[0]) % (FLAT_ROWS * LANE))).reshape(-1, LANE)

    dl_f, nm_f, nv_f = _adamw_flat(pack(wts), pack(gs_loc), pack(mom), pack(vel), "adamw_small")
    off = 0
    for k in order:
        n = wts[k].size
        sl = lambda a: a.reshape(-1)[off:off + n].reshape(wts[k].shape)
        out[k] = (gs_loc[k], sl(dl_f), sl(nm_f), sl(nv_f))
        off += n

    loss_total = lax.psum(loss[0, 0], ("x", "y", "c"))
    return (loss_total, dx[None], *[out[k][0] for k in _WEIGHTS], *[out[k][1] for k in _WEIGHTS],
            *[out[k][2] for k in _WEIGHTS], *[out[k][3] for k in _WEIGHTS])
```

```python
import functools
import math

import numpy as np
import jax
import jax.numpy as jnp
from jax import lax
from jax.experimental import pallas as pl
from jax.experimental.pallas import tpu as pltpu

F32 = jnp.float32
BF16 = jnp.bfloat16
LANE = 128
NSEG = 8
NDEV = 8
NCHIP = 4
MESH = pl.DeviceIdType.MESH
VMEM_LIMIT = 56 * 1024 * 1024
FLAT_ROWS = 512

NORM_EPS = 1e-6
CHUNK = 64
ATT_LEFT = 8
ATT_BAND = (ATT_LEFT + 1) * CHUNK
MAX_REL = 128
N_REL = 2 * MAX_REL + 1
N_REL_PAD = 384
ATT_HEADS = 8
MASK_VALUE = -1e30
LRU_C = 8.0
LRU_BLOCKS = 16
SSM_GROUP = 16
SSM_P = 64
CONV_W = 4
N_BRANCH = 3

ADAM_LR = 0.001
ADAM_B1 = 0.9
ADAM_B2 = 0.999
ADAM_EPS = 1e-08
ADAM_WD = 0.01
ADAM_STEP = 10

NN = (((1,), (0,)), ((), ()))
NT = (((1,), (1,)), ((), ()))
TN = (((0,), (0,)), ((), ()))


def _cp(sem=None, vmem=VMEM_LIMIT, **kw):
    return pltpu.CompilerParams(dimension_semantics=sem, vmem_limit_bytes=vmem, **kw)


def _dot(a, b, dn=NN):
    return lax.dot_general(a, b, dn, preferred_element_type=F32)


def _gelu(x):
    c = math.sqrt(2.0 / math.pi)
    return 0.5 * x * (1.0 + jnp.tanh(c * (x + 0.044715 * x * x * x)))


def _gelu_grad(x):
    c = math.sqrt(2.0 / math.pi)
    t = jnp.tanh(c * (x + 0.044715 * x * x * x))
    return 0.5 * (1.0 + t) + 0.5 * x * (1.0 - t * t) * c * (1.0 + 3.0 * 0.044715 * x * x)


def _sigmoid(x):
    return 1.0 / (1.0 + jnp.exp(-x))


def _one_minus_exp(z):
    series = -(z * (1.0 + z * (0.5 + z * (1.0 / 6.0 + z * (1.0 / 24.0)))))
    return jnp.where(z > -0.02, series, 1.0 - jnp.exp(z))


def _softplus_neg(lam):
    e = jnp.exp(-jnp.abs(lam))
    series = e * (1.0 - e * (0.5 - e * (1.0 / 3.0 - e * 0.25)))
    log1p_e = jnp.where(e < 0.02, series, jnp.log(1.0 + e))
    return jnp.maximum(-lam, 0.0) + log1p_e


def _mm(name, dn, grid, n_red, ins, in_specs, out_shape, out_spec, acc_shape, epi=None, alias=None):
    n_extra = len(ins) - 2 - (1 if alias is not None else 0)
    red_axes = tuple(range(len(grid) - n_red, len(grid)))
    red_sizes = tuple(grid[r] for r in red_axes)

    def body(*refs):
        a_ref, b_ref = refs[0], refs[1]
        extra = refs[2:2 + n_extra]
        o_ref, acc = refs[-2], refs[-1]
        if n_red:
            first = functools.reduce(jnp.logical_and, [pl.program_id(r) == 0 for r in red_axes])
            last = functools.reduce(jnp.logical_and,
                                    [pl.program_id(r) == n - 1 for r, n in zip(red_axes, red_sizes)])

            @pl.when(first)
            def _():
                acc[...] = jnp.zeros_like(acc)

            acc[...] += _dot(a_ref[...], b_ref[...], dn)

            @pl.when(last)
            def _():
                r = acc[...]
                if epi is not None:
                    r = epi(r, *[e[...] for e in extra])
                o_ref[...] = r.astype(o_ref.dtype)
        else:
            r = _dot(a_ref[...], b_ref[...], dn)
            if epi is not None:
                r = epi(r, *[e[...] for e in extra])
            o_ref[...] = r.astype(o_ref.dtype)

    specs = list(in_specs)
    kw = {}
    if alias is not None:
        specs.append(pl.BlockSpec(memory_space=pl.ANY))
        kw["input_output_aliases"] = {len(ins) - 1: 0}
    sem = ("parallel",) * (len(grid) - n_red) + ("arbitrary",) * n_red
    return pl.pallas_call(
        body, name=name, grid=grid, in_specs=specs, out_specs=out_spec, out_shape=out_shape,
        scratch_shapes=[pltpu.VMEM(acc_shape, F32)], compiler_params=_cp(sem), **kw)(*ins)


def _tile(n, pref):
    if n <= pref:
        return n
    t = pref
    while t >= LANE:
        if n % t == 0 and t % LANE == 0:
            return t
        t -= LANE
    return n


def _rmsnorm_fwd(x, g, name):
    t, d = x.shape
    tm = min(512, t)

    def body(x_ref, g_ref, h_ref):
        xv = x_ref[...]
        r = lax.rsqrt(jnp.mean(xv * xv, axis=-1, keepdims=True) + NORM_EPS)
        h_ref[...] = (xv * r * g_ref[...]).astype(h_ref.dtype)

    return pl.pallas_call(
        body, name=name, grid=(t // tm,),
        in_specs=[pl.BlockSpec((tm, d), lambda i: (i, 0)), pl.BlockSpec((1, d), lambda i: (0, 0))],
        out_specs=pl.BlockSpec((tm, d), lambda i: (i, 0)),
        out_shape=jax.ShapeDtypeStruct((t, d), BF16), compiler_params=_cp(("parallel",)))(x, g)


def _rmsnorm_bwd(x, g, dhs, dres, name):
    t, d = x.shape
    tm = min(256, t)
    n_dh = len(dhs)

    def body(*refs):
        x_ref, g_ref = refs[0], refs[1]
        dh_refs = refs[2:2 + n_dh]
        dres_ref = refs[2 + n_dh]
        dx_ref, dxb_ref, dg_ref = refs[3 + n_dh:]
        xv = x_ref[...]
        r = lax.rsqrt(jnp.mean(xv * xv, axis=-1, keepdims=True) + NORM_EPS)
        xhat = xv * r
        dh = dh_refs[0][...].astype(F32)
        for e in dh_refs[1:]:
            dh = dh + e[...].astype(F32)
        dxh = dh * g_ref[...]
        dx = r * (dxh - xhat * jnp.mean(dxh * xhat, axis=-1, keepdims=True)) + dres_ref[...]
        dx_ref[...] = dx
        dxb_ref[...] = dx.astype(BF16)

        @pl.when(pl.program_id(0) == 0)
        def _():
            dg_ref[...] = jnp.zeros_like(dg_ref)

        dg_ref[...] += jnp.sum(dh * xhat, axis=0, keepdims=True)

    row = pl.BlockSpec((tm, d), lambda i: (i, 0))
    par = pl.BlockSpec((1, d), lambda i: (0, 0))
    return pl.pallas_call(
        body, name=name, grid=(t // tm,),
        in_specs=[row, par] + [row] * n_dh + [row],
        out_specs=[row, row, par],
        out_shape=[jax.ShapeDtypeStruct((t, d), F32), jax.ShapeDtypeStruct((t, d), BF16),
                   jax.ShapeDtypeStruct((1, d), F32)],
        compiler_params=_cp(("arbitrary",)))(x, g, *dhs, dres)


def _loss_head(x, g, target, name):
    t, d = x.shape
    tm = min(256, t)

    def body(x_ref, g_ref, t_ref, loss_ref, dx_ref, dxb_ref, dg_ref):
        xv = x_ref[...]
        r = lax.rsqrt(jnp.mean(xv * xv, axis=-1, keepdims=True) + NORM_EPS)
        xhat = xv * r
        y = xhat * g_ref[...]
        err = y - t_ref[...]
        dy = err * (1.0 / d)
        dxh = dy * g_ref[...]
        dx = r * (dxh - xhat * jnp.mean(dxh * xhat, axis=-1, keepdims=True))
        dx_ref[...] = dx
        dxb_ref[...] = dx.astype(BF16)

        @pl.when(pl.program_id(0) == 0)
        def _():
            dg_ref[...] = jnp.zeros_like(dg_ref)
            loss_ref[...] = jnp.zeros_like(loss_ref)

        dg_ref[...] += jnp.sum(dy * xhat, axis=0, keepdims=True)
        per_tok = jnp.sum(err * err, axis=-1, keepdims=True) * (0.5 / d)
        loss_ref[...] += jnp.sum(per_tok, axis=0, keepdims=True)

    row = pl.BlockSpec((tm, d), lambda i: (i, 0))
    par = pl.BlockSpec((1, d), lambda i: (0, 0))
    return pl.pallas_call(
        body, name=name, grid=(t // tm,),
        in_specs=[row, par, row],
        out_specs=[pl.BlockSpec((1, 1), lambda i: (0, 0)), row, row, par],
        out_shape=[jax.ShapeDtypeStruct((1, 1), F32), jax.ShapeDtypeStruct((t, d), F32),
                   jax.ShapeDtypeStruct((t, d), BF16), jax.ShapeDtypeStruct((1, d), F32)],
        compiler_params=_cp(("arbitrary",)))(x, g, target)


def _merge_fwd(proj, gbias, br_a, br_b, br_c0, gl, d, name):
    t = proj.shape[0]
    tm = min(128, t)
    off = proj.shape[1] // d - N_BRANCH

    def body(g0, g1, g2, gb, a_ref, b_ref, c_ref, gl_ref, o_ref):
        gbv = gb[...]
        s0 = _sigmoid(g0[...].astype(F32) + gbv[0:1])
        s1 = _sigmoid(g1[...].astype(F32) + gbv[1:2])
        s2 = _sigmoid(g2[...].astype(F32) + gbv[2:3])
        brc = c_ref[...].astype(F32) * _sigmoid(gl_ref[...].astype(F32))
        o_ref[...] = (s0 * a_ref[...].astype(F32) + s1 * b_ref[...].astype(F32) + s2 * brc).astype(BF16)

    row = pl.BlockSpec((tm, d), lambda i: (i, 0))
    gs = [pl.BlockSpec((tm, d), functools.partial(lambda i, b: (i, off + b), b=b)) for b in range(N_BRANCH)]
    return pl.pallas_call(
        body, name=name, grid=(t // tm,),
        in_specs=gs + [pl.BlockSpec((N_BRANCH, d), lambda i: (0, 0)), row, row, row, row],
        out_specs=row, out_shape=jax.ShapeDtypeStruct((t, d), BF16),
        compiler_params=_cp(("parallel",)))(proj, proj, proj, gbias, br_a, br_b, br_c0, gl)


def _merge_bwd(proj, gbias, br_a, br_b, br_c0, gl, dmerged, d, name):
    t = proj.shape[0]
    tm = min(128, t)
    off = proj.shape[1] // d - N_BRANCH

    def body(g0, g1, g2, gb, a_ref, b_ref, c_ref, gl_ref, dm_ref, da_ref, db_ref, dc_ref, dgl_ref, dg_ref, dgb_ref):
        gbv = gb[...]
        dm = dm_ref[...].astype(F32)
        s0 = _sigmoid(g0[...].astype(F32) + gbv[0:1])
        s1 = _sigmoid(g1[...].astype(F32) + gbv[1:2])
        s2 = _sigmoid(g2[...].astype(F32) + gbv[2:3])
        sg = _sigmoid(gl_ref[...].astype(F32))
        c0 = c_ref[...].astype(F32)
        brc = c0 * sg
        da_ref[...] = (dm * s0).astype(BF16)
        db_ref[...] = (dm * s1).astype(BF16)
        dbrc = dm * s2
        dc_ref[...] = (dbrc * sg).astype(BF16)
        dgl_ref[...] = (dbrc * c0 * sg * (1.0 - sg)).astype(BF16)
        dp0 = dm * a_ref[...].astype(F32) * s0 * (1.0 - s0)
        dp1 = dm * b_ref[...].astype(F32) * s1 * (1.0 - s1)
        dp2 = dm * brc * s2 * (1.0 - s2)
        dg_ref[:, 0:d] = dp0.astype(BF16)
        dg_ref[:, d:2 * d] = dp1.astype(BF16)
        dg_ref[:, 2 * d:3 * d] = dp2.astype(BF16)

        @pl.when(pl.program_id(0) == 0)
        def _():
            dgb_ref[...] = jnp.zeros_like(dgb_ref)

        dgb_ref[0:1, :] += jnp.sum(dp0, axis=0, keepdims=True)
        dgb_ref[1:2, :] += jnp.sum(dp1, axis=0, keepdims=True)
        dgb_ref[2:3, :] += jnp.sum(dp2, axis=0, keepdims=True)

    row = pl.BlockSpec((tm, d), lambda i: (i, 0))
    gs = [pl.BlockSpec((tm, d), functools.partial(lambda i, b: (i, off + b), b=b)) for b in range(N_BRANCH)]
    par = pl.BlockSpec((N_BRANCH, d), lambda i: (0, 0))
    bf = jax.ShapeDtypeStruct((t, d), BF16)
    return pl.pallas_call(
        body, name=name, grid=(t // tm,),
        in_specs=gs + [par, row, row, row, row, row],
        out_specs=[row, row, row, row, pl.BlockSpec((tm, N_BRANCH * d), lambda i: (i, 0)), par],
        out_shape=[bf, bf, bf, bf, jax.ShapeDtypeStruct((t, N_BRANCH * d), BF16),
                   jax.ShapeDtypeStruct((N_BRANCH, d), F32)],
        compiler_params=_cp(("arbitrary",)))(proj, proj, proj, gbias, br_a, br_b, br_c0, gl, dmerged)


def _swiglu_fwd(gp, up, name):
    s, t, f = gp.shape
    tm = min(512, t)

    def body(g_ref, u_ref, o_ref):
        g = g_ref[...].astype(F32)
        o_ref[...] = (g * _sigmoid(g) * u_ref[...].astype(F32)).astype(BF16)

    blk = pl.BlockSpec((None, tm, f), lambda j, i: (j, i, 0))
    return pl.pallas_call(
        body, name=name, grid=(s, t // tm), in_specs=[blk, blk], out_specs=blk,
        out_shape=jax.ShapeDtypeStruct((s, t, f), BF16), compiler_params=_cp(("parallel", "parallel")))(gp, up)


def _swiglu_bwd(gp, up, dact, name):
    s, t, f = gp.shape
    tm = min(512, t)

    def body(g_ref, u_ref, d_ref, dg_ref, du_ref):
        g = g_ref[...].astype(F32)
        u = u_ref[...].astype(F32)
        da = d_ref[...].astype(F32)
        sg = _sigmoid(g)
        silu = g * sg
        du_ref[...] = (da * silu).astype(BF16)
        dg_ref[...] = (da * u * (sg + silu * (1.0 - sg))).astype(BF16)

    blk = pl.BlockSpec((None, tm, f), lambda j, i: (j, i, 0))
    o = jax.ShapeDtypeStruct((s, t, f), BF16)
    return pl.pallas_call(
        body, name=name, grid=(s, t // tm), in_specs=[blk, blk, blk], out_specs=[blk, blk],
        out_shape=[o, o], compiler_params=_cp(("parallel", "parallel")))(gp, up, dact)


def _gelu_bwd_sum(s5o, dy1, dy2, name):
    t, w = s5o.shape
    tm = min(512, t)

    def body(s_ref, a_ref, b_ref, o_ref):
        o_ref[...] = ((a_ref[...] + b_ref[...]) * _gelu_grad(s_ref[...])).astype(BF16)

    row = pl.BlockSpec((tm, w), lambda i: (i, 0))
    return pl.pallas_call(
        body, name=name, grid=(t // tm,), in_specs=[row, row, row], out_specs=row,
        out_shape=jax.ShapeDtypeStruct((t, w), BF16), compiler_params=_cp(("parallel",)))(s5o, dy1, dy2)


def _lru_gates(xc, wa, ba, wx, bx, sp):
    xcb = xc.astype(BF16)
    r = _sigmoid(_dot(xcb, wa) + ba)
    i = _sigmoid(_dot(xcb, wx) + bx)
    la = -LRU_C * r * sp
    return xcb, r, i, la


def _lru_fwd(proj, cw, cb, wa_t, ba, wx_t, bx, lam, w, name):
    t = proj.shape[0]
    nt = w // LANE
    seg = t // NSEG
    rb = min(512, seg)

    def body(lx_ref, lg_ref, cw_ref, cb_ref, wa_ref, ba_ref, wx_ref, bx_ref, lam_ref, ya_ref, hs_ref, xp, a_s, b_s):
        xp[pl.ds(0, 8), :] = jnp.zeros((8, LANE), F32)
        for r0 in range(0, t, rb):
            xp[pl.ds(8 + r0, rb), :] = lx_ref[pl.ds(r0, rb), :].astype(F32)
        sp = _softplus_neg(lam_ref[...])
        cwv = cw_ref[...]
        for r0 in range(0, t, rb):
            xc = cb_ref[...] + sum(cwv[k:k + 1] * xp[pl.ds(8 + r0 - (CONV_W - 1) + k, rb), :] for k in range(CONV_W))
            _, _, i, la = _lru_gates(xc, wa_ref[...], ba_ref[...], wx_ref[...], bx_ref[...], sp)
            a_s[pl.ds(r0, rb), :] = jnp.exp(la)
            b_s[pl.ds(r0, rb), :] = jnp.sqrt(_one_minus_exp(2.0 * la)) * (i * xc)

        def load(k):
            rows = pl.ds(k, NSEG, stride=seg)
            return a_s[rows, :], b_s[rows, :]

        def step(k, c):
            h, p, a, b = c
            na, nb = load(jnp.minimum(k + 1, seg - 1))
            rows = pl.ds(k, NSEG, stride=seg)
            h = a * h + b
            p = a * p
            b_s[rows, :] = h
            a_s[rows, :] = p
            return h, p, na, nb

        lax.fori_loop(0, seg, step, (jnp.zeros((NSEG, LANE), F32), jnp.ones((NSEG, LANE), F32)) + load(0), unroll=4)
        carry = jnp.zeros((1, LANE), F32)
        for s in range(NSEG):
            for r0 in range(s * seg, (s + 1) * seg, rb):
                rows = pl.ds(r0, rb)
                h = b_s[rows, :] + a_s[rows, :] * carry
                hs_ref[rows, :] = h
                ya_ref[rows, :] = (h * _gelu(lg_ref[rows, :].astype(F32))).astype(BF16)
            end = pl.ds((s + 1) * seg - 1, 1)
            carry = b_s[end, :] + a_s[end, :] * carry

    col = lambda c0: pl.BlockSpec((t, LANE), functools.partial(lambda i, c0: (0, c0 + i), c0=c0))
    par = lambda k: pl.BlockSpec((k, LANE), lambda i: (0, i))
    mat = pl.BlockSpec((None, LANE, LANE), lambda i: (i, 0, 0))
    return pl.pallas_call(
        body, name=name, grid=(nt,),
        in_specs=[col(0), col(nt), par(CONV_W), par(1), mat, par(1), mat, par(1), par(1)],
        out_specs=[col(0), col(0)],
        out_shape=[jax.ShapeDtypeStruct((t, w), BF16), jax.ShapeDtypeStruct((t, w), F32)],
        scratch_shapes=[pltpu.VMEM((t + 8, LANE), F32), pltpu.VMEM((t, LANE), F32), pltpu.VMEM((t, LANE), F32)],
        compiler_params=_cp(("parallel",)))(proj, proj, cw, cb, wa_t, ba, wx_t, bx, lam)


def _lru_bwd(proj, hs, dya, cw, cb, wa_t, ba, wx_t, bx, lam, wat_t, wxt_t, w, name):
    t = proj.shape[0]
    nt = w // LANE
    seg = t // NSEG
    rb = min(512, seg)

    def body(lx_ref, lg_ref, hs_ref, dy_ref, cw_ref, cb_ref, wa_ref, ba_ref, wx_ref, bx_ref, lam_ref, wat_ref, wxt_ref,
             dlx_ref, dlg_ref, dcw_ref, dcb_ref, dwa_ref, dba_ref, dwx_ref, dbx_ref, dlam_ref,
             xp, hp, a_s, g_s, q_s, dxc_s):
        z8 = jnp.zeros((8, LANE), F32)
        xp[pl.ds(0, 8), :] = z8
        hp[pl.ds(0, 8), :] = z8
        a_s[pl.ds(t, 8), :] = z8
        dxc_s[pl.ds(t, 8), :] = z8
        lamv = lam_ref[...]
        sp = _softplus_neg(lamv)
        cwv = cw_ref[...]

        def conv(r0):
            return cb_ref[...] + sum(cwv[k:k + 1] * xp[pl.ds(8 + r0 - (CONV_W - 1) + k, rb), :] for k in range(CONV_W))

        for r0 in range(0, t, rb):
            rows = pl.ds(r0, rb)
            xp[pl.ds(8 + r0, rb), :] = lx_ref[rows, :].astype(F32)
            hp[pl.ds(8 + r0, rb), :] = hs_ref[rows, :]
        for r0 in range(0, t, rb):
            rows = pl.ds(r0, rb)
            _, _, _, la = _lru_gates(conv(r0), wa_ref[...], ba_ref[...], wx_ref[...], bx_ref[...], sp)
            a_s[rows, :] = jnp.exp(la)
            g_s[rows, :] = dy_ref[rows, :].astype(F32) * _gelu(lg_ref[rows, :].astype(F32))

        def load(k):
            return a_s[pl.ds(k + 1, NSEG, stride=seg), :], g_s[pl.ds(k, NSEG, stride=seg), :]

        def step(kk, c):
            g, q, an, dh = c
            k = seg - 1 - kk
            nan, ndh = load(jnp.maximum(k - 1, 0))
            rows = pl.ds(k, NSEG, stride=seg)
            g = dh + an * g
            q = an * q
            g_s[rows, :] = g
            q_s[rows, :] = q
            return g, q, nan, ndh

        lax.fori_loop(0, seg, step, (jnp.zeros((NSEG, LANE), F32), jnp.ones((NSEG, LANE), F32)) + load(seg - 1), unroll=4)
        nxt = [None] * NSEG
        carry = jnp.zeros((1, LANE), F32)
        for s in range(NSEG - 1, -1, -1):
            nxt[s] = carry
            start = pl.ds(s * seg, 1)
            carry = g_s[start, :] + q_s[start, :] * carry

        zrow = jnp.zeros((1, LANE), F32)
        dsp = zrow
        dba = zrow
        dbx = zrow
        dcb = zrow
        dwa = jnp.zeros((LANE, LANE), F32)
        dwx = jnp.zeros((LANE, LANE), F32)
        for s in range(NSEG):
            for r0 in range(s * seg, (s + 1) * seg, rb):
                rows = pl.ds(r0, rb)
                g = g_s[rows, :] + q_s[rows, :] * nxt[s]
                xc = conv(r0)
                xcb, r, i, la = _lru_gates(xc, wa_ref[...], ba_ref[...], wx_ref[...], bx_ref[...], sp)
                a = jnp.exp(la)
                om = _one_minus_exp(2.0 * la)
                mult = jnp.sqrt(om)
                hprev = hp[pl.ds(8 + r0 - 1, rb), :]
                da = g * hprev
                dmult = g * i * xc
                di = g * mult * xc
                dxc = g * mult * i
                dla = da * a - dmult * (1.0 - om) / mult
                dr = dla * (-LRU_C) * sp
                dsp = dsp + jnp.sum(dla * (-LRU_C) * r, axis=0, keepdims=True)
                dpr = dr * r * (1.0 - r)
                dpi = di * i * (1.0 - i)
                dba = dba + jnp.sum(dpr, axis=0, keepdims=True)
                dbx = dbx + jnp.sum(dpi, axis=0, keepdims=True)
                dprb = dpr.astype(BF16)
                dpib = dpi.astype(BF16)
                dxc = dxc + _dot(dprb, wat_ref[...]) + _dot(dpib, wxt_ref[...])
                dwa = dwa + _dot(xcb, dprb, TN)
                dwx = dwx + _dot(xcb, dpib, TN)
                dxc_s[rows, :] = dxc
                dcb = dcb + jnp.sum(dxc, axis=0, keepdims=True)
                lg = lg_ref[rows, :].astype(F32)
                dlg_ref[rows, :] = (dy_ref[rows, :].astype(F32) * hs_ref[rows, :] * _gelu_grad(lg)).astype(BF16)
        dcw = [zrow] * CONV_W
        for r0 in range(0, t, rb):
            rows = pl.ds(r0, rb)
            dlx = sum(cwv[k:k + 1] * dxc_s[pl.ds(r0 + (CONV_W - 1) - k, rb), :] for k in range(CONV_W))
            dlx_ref[rows, :] = dlx.astype(BF16)
            dxc = dxc_s[rows, :]
            for k in range(CONV_W):
                dcw[k] = dcw[k] + jnp.sum(dxc * xp[pl.ds(8 + r0 - (CONV_W - 1) + k, rb), :], axis=0, keepdims=True)
        dcw_ref[...] = jnp.concatenate(dcw, axis=0)
        dcb_ref[...] = dcb
        dwa_ref[...] = dwa
        dwx_ref[...] = dwx
        dba_ref[...] = dba
        dbx_ref[...] = dbx
        dlam_ref[...] = dsp * (-_sigmoid(-lamv))

    col = lambda c0: pl.BlockSpec((t, LANE), functools.partial(lambda i, c0: (0, c0 + i), c0=c0))
    par = lambda k: pl.BlockSpec((k, LANE), lambda i: (0, i))
    mat = pl.BlockSpec((None, LANE, LANE), lambda i: (i, 0, 0))
    vec = jax.ShapeDtypeStruct((1, w), F32)
    big = lambda: pltpu.VMEM((t + 8, LANE), F32)
    return pl.pallas_call(
        body, name=name, grid=(nt,),
        in_specs=[col(0), col(nt), col(0), col(0), par(CONV_W), par(1), mat, par(1), mat, par(1), par(1), mat, mat],
        out_specs=[col(0), col(0), par(CONV_W), par(1), mat, par(1), mat, par(1), par(1)],
        out_shape=[jax.ShapeDtypeStruct((t, w), BF16), jax.ShapeDtypeStruct((t, w), BF16),
                   jax.ShapeDtypeStruct((CONV_W, w), F32), vec, jax.ShapeDtypeStruct((nt, LANE, LANE), F32), vec,
                   jax.ShapeDtypeStruct((nt, LANE, LANE), F32), vec, vec],
        scratch_shapes=[big(), big(), big(), pltpu.VMEM((t, LANE), F32), pltpu.VMEM((t, LANE), F32), big()],
        compiler_params=_cp(("parallel",)))(proj, proj, hs, dya, cw, cb, wa_t, ba, wx_t, bx, lam, wat_t, wxt_t)


def _attn_scores(q_ref, kp, bias_ref, c, hd):
    r0 = pl.multiple_of(c * CHUNK, CHUNK)
    qc = q_ref[pl.ds(r0, CHUNK), :]
    kb = kp[pl.ds(r0, ATT_BAND), :]
    s = _dot(qc, kb, NT) * (hd ** -0.5) + bias_ref[...]
    kpos = lax.broadcasted_iota(jnp.int32, (CHUNK, ATT_BAND), 1)
    s = jnp.where(kpos + (c - ATT_LEFT) * CHUNK >= 0, s, MASK_VALUE)
    m = jnp.max(s, axis=-1, keepdims=True)
    e = jnp.exp(s - m)
    p = e / jnp.sum(e, axis=-1, keepdims=True)
    return r0, qc, kb, p


def _attn_pad_copy(src_ref, dst, t, pad):
    dst[pl.ds(0, pad), :] = jnp.zeros((pad, dst.shape[1]), dst.dtype)
    rb = min(512, t)
    for r0 in range(0, t, rb):
        dst[pl.ds(pad + r0, rb), :] = src_ref[pl.ds(r0, rb), :]


def _attn_fwd(proj, bias, w, name):
    t = proj.shape[0]
    hd = w // ATT_HEADS
    pad = ATT_LEFT * CHUNK
    qo, ko, vo = 2 * w // hd, 3 * w // hd, 4 * w // hd

    def body(q_ref, k_ref, v_ref, bias_ref, o_ref, kp, vp):
        _attn_pad_copy(k_ref, kp, t, pad)
        _attn_pad_copy(v_ref, vp, t, pad)

        def chunk(c, _):
            r0, _, _, p = _attn_scores(q_ref, kp, bias_ref, c, hd)
            vb = vp[pl.ds(r0, ATT_BAND), :]
            o_ref[pl.ds(r0, CHUNK), :] = _dot(p.astype(BF16), vb).astype(BF16)
            return 0

        lax.fori_loop(0, t // CHUNK, chunk, 0, unroll=2)

    col = lambda c0: pl.BlockSpec((t, hd), functools.partial(lambda h, c0: (0, c0 + h), c0=c0))
    return pl.pallas_call(
        body, name=name, grid=(ATT_HEADS,),
        in_specs=[col(qo), col(ko), col(vo), pl.BlockSpec((None, CHUNK, ATT_BAND), lambda h: (h, 0, 0))],
        out_specs=col(0), out_shape=jax.ShapeDtypeStruct((t, w), BF16),
        scratch_shapes=[pltpu.VMEM((t + pad, hd), BF16), pltpu.VMEM((t + pad, hd), BF16)],
        compiler_params=_cp(("parallel",)))(proj, proj, proj, bias)


def _attn_bwd(proj, bias, do, w, name):
    t = proj.shape[0]
    hd = w // ATT_HEADS
    pad = ATT_LEFT * CHUNK
    qo, ko, vo = 2 * w // hd, 3 * w // hd, 4 * w // hd

    def body(q_ref, k_ref, v_ref, bias_ref, do_ref, dq_ref, dk_ref, dv_ref, db_ref, kp, vp, dkp, dvp):
        _attn_pad_copy(k_ref, kp, t, pad)
        _attn_pad_copy(v_ref, vp, t, pad)
        rb = min(512, t)
        for r0 in range(0, t + pad, rb):
            n = min(rb, t + pad - r0)
            dkp[pl.ds(r0, n), :] = jnp.zeros((n, hd), F32)
            dvp[pl.ds(r0, n), :] = jnp.zeros((n, hd), F32)
        db_ref[...] = jnp.zeros_like(db_ref)
        scale = hd ** -0.5

        def chunk(c, _):
            r0, qc, kb, p = _attn_scores(q_ref, kp, bias_ref, c, hd)
            band = pl.ds(r0, ATT_BAND)
            vb = vp[band, :]
            doc = do_ref[pl.ds(r0, CHUNK), :]
            dp = _dot(doc, vb, NT)
            ds = p * (dp - jnp.sum(dp * p, axis=-1, keepdims=True))
            db_ref[...] += ds
            dsb = ds.astype(BF16)
            dq_ref[pl.ds(r0, CHUNK), :] = (_dot(dsb, kb) * scale).astype(BF16)
            dkp[band, :] += _dot(dsb, qc, TN) * scale
            dvp[band, :] += _dot(p.astype(BF16), doc, TN)
            return 0

        lax.fori_loop(0, t // CHUNK, chunk, 0, unroll=2)
        for r0 in range(0, t, rb):
            dk_ref[pl.ds(r0, rb), :] = dkp[pl.ds(pad + r0, rb), :].astype(BF16)
            dv_ref[pl.ds(r0, rb), :] = dvp[pl.ds(pad + r0, rb), :].astype(BF16)

    col = lambda c0: pl.BlockSpec((t, hd), functools.partial(lambda h, c0: (0, c0 + h), c0=c0))
    tb = pl.BlockSpec((None, CHUNK, ATT_BAND), lambda h: (h, 0, 0))
    o = jax.ShapeDtypeStruct((t, w), BF16)
    return pl.pallas_call(
        body, name=name, grid=(ATT_HEADS,),
        in_specs=[col(qo), col(ko), col(vo), tb, col(0)],
        out_specs=[col(0), col(0), col(0), tb],
        out_shape=[o, o, o, jax.ShapeDtypeStruct((ATT_HEADS, CHUNK, ATT_BAND), F32)],
        scratch_shapes=[pltpu.VMEM((t + pad, hd), BF16), pltpu.VMEM((t + pad, hd), BF16),
                        pltpu.VMEM((t + pad, hd), F32), pltpu.VMEM((t + pad, hd), F32)],
        compiler_params=_cp(("parallel",)))(proj, proj, proj, bias, do)


def _rel_index():
    q_pos = ATT_LEFT * CHUNK + np.arange(CHUNK)
    k_pos = np.arange(ATT_BAND)
    return (np.clip(q_pos[:, None] - k_pos[None, :], -MAX_REL, MAX_REL) + MAX_REL).astype(np.int32).reshape(-1)


def _bias_expand(rel_bias, name):
    nl, nh, _ = rel_bias.shape
    n = CHUNK * ATT_BAND
    kb = n // 8
    idx = jnp.asarray(_rel_index().reshape(1, n))
    tab = jnp.pad(rel_bias, ((0, 0), (0, 0), (0, N_REL_PAD - N_REL)))

    def body(t_ref, i_ref, o_ref):
        onehot = (lax.broadcasted_iota(jnp.int32, (N_REL_PAD, kb), 0) == i_ref[...]).astype(F32)
        o_ref[...] = lax.dot_general(t_ref[...], onehot, NN, precision=lax.Precision.HIGHEST,
                                     preferred_element_type=F32)

    out = pl.pallas_call(
        body, name=name, grid=(nl, n // kb),
        in_specs=[pl.BlockSpec((None, nh, N_REL_PAD), lambda l, j: (l, 0, 0)), pl.BlockSpec((1, kb), lambda l, j: (0, j))],
        out_specs=pl.BlockSpec((None, nh, kb), lambda l, j: (l, 0, j)),
        out_shape=jax.ShapeDtypeStruct((nl, nh, n), F32), compiler_params=_cp(("parallel", "parallel")))(tab, idx)
    return out.reshape(nl, nh, CHUNK, ATT_BAND)


def _bias_reduce(dbias, name):
    nl, nh = dbias.shape[:2]
    n = CHUNK * ATT_BAND
    kb = n // 8
    idx = jnp.asarray(_rel_index().reshape(n, 1))

    def body(d_ref, i_ref, o_ref):
        @pl.when(pl.program_id(1) == 0)
        def _():
            o_ref[...] = jnp.zeros_like(o_ref)

        onehot = (lax.broadcasted_iota(jnp.int32, (kb, N_REL_PAD), 1) == i_ref[...]).astype(F32)
        o_ref[...] += lax.dot_general(d_ref[...], onehot, NN, precision=lax.Precision.HIGHEST,
                                      preferred_element_type=F32)

    out = pl.pallas_call(
        body, name=name, grid=(nl, n // kb),
        in_specs=[pl.BlockSpec((None, nh, kb), lambda l, j: (l, 0, j)), pl.BlockSpec((kb, 1), lambda l, j: (j, 0))],
        out_specs=pl.BlockSpec((None, nh, N_REL_PAD), lambda l, j: (l, 0, 0)),
        out_shape=jax.ShapeDtypeStruct((nl, nh, N_REL_PAD), F32),
        compiler_params=_cp(("parallel", "arbitrary")))(dbias.reshape(nl, nh, n), idx)
    return out[:, :, :N_REL]


def _s5_prep_math(a_re, a_im, ls, btr, bti):
    step = jnp.exp(ls)
    mag = jnp.exp(a_re * step)
    ang = a_im * step
    lr = mag * jnp.cos(ang)
    li = mag * jnp.sin(ang)
    den = a_re * a_re + a_im * a_im
    nr = lr - 1.0
    cr = (nr * a_re + li * a_im) / den
    ci = (li * a_re - nr * a_im) / den
    bbr = cr[:, None, :] * btr - ci[:, None, :] * bti
    bbi = cr[:, None, :] * bti + ci[:, None, :] * btr
    return lr, li, bbr, bbi


def _s5_prep(a_re, a_im, ls, btr, bti, name):
    nl, g, p = a_re.shape
    h = btr.shape[2]

    def body(ar, ai, l_ref, br, bi, o1, o2, o3, o4):
        r = _s5_prep_math(ar[...], ai[...], l_ref[...], br[...], bi[...])
        o1[...], o2[...], o3[...], o4[...] = r

    m2 = pl.BlockSpec((None, g, p), lambda l: (l, 0, 0))
    m1 = pl.BlockSpec((None, g, 1), lambda l: (l, 0, 0))
    m3 = pl.BlockSpec((None, g, h, p), lambda l: (l, 0, 0, 0))
    s2 = jax.ShapeDtypeStruct((nl, g, p), F32)
    s3 = jax.ShapeDtypeStruct((nl, g, h, p), F32)
    return pl.pallas_call(body, name=name, grid=(nl,), in_specs=[m2, m2, m1, m3, m3], out_specs=[m2, m2, m3, m3],
                          out_shape=[s2, s2, s3, s3], compiler_params=_cp(("parallel",)))(a_re, a_im, ls, btr, bti)


def _s5_prep_bwd(a_re, a_im, ls, btr, bti, dlr, dli, dbbr, dbbi, name):
    nl, g, p = a_re.shape
    h = btr.shape[2]

    def body(ar, ai, l_ref, br, bi, g1, g2, g3, g4, o1, o2, o3, o4, o5):
        _, vjp = jax.vjp(_s5_prep_math, ar[...], ai[...], l_ref[...], br[...], bi[...])
        o1[...], o2[...], o3[...], o4[...], o5[...] = vjp((g1[...], g2[...], g3[...], g4[...]))

    m2 = pl.BlockSpec((None, g, p), lambda l: (l, 0, 0))
    m1 = pl.BlockSpec((None, g, 1), lambda l: (l, 0, 0))
    m3 = pl.BlockSpec((None, g, h, p), lambda l: (l, 0, 0, 0))
    s2 = jax.ShapeDtypeStruct((nl, g, p), F32)
    s1 = jax.ShapeDtypeStruct((nl, g, 1), F32)
    s3 = jax.ShapeDtypeStruct((nl, g, h, p), F32)
    return pl.pallas_call(body, name=name, grid=(nl,), in_specs=[m2, m2, m1, m3, m3, m2, m2, m3, m3],
                          out_specs=[m2, m2, m1, m3, m3], out_shape=[s2, s2, s1, s3, s3],
                          compiler_params=_cp(("parallel",)))(a_re, a_im, ls, btr, bti, dlr, dli, dbbr, dbbi)


def _cpow(lr, li, n):
    rr = ri = None
    br, bi = lr, li
    while n:
        if n & 1:
            rr, ri = (br, bi) if rr is None else (rr * br - ri * bi, rr * bi + ri * br)
        n >>= 1
        if n:
            br, bi = br * br - bi * bi, 2.0 * br * bi
    return rr, ri


def _permute_rows(src, dst, seg, inverse):
    def body(k, _):
        tile = pl.ds(pl.multiple_of(k * NSEG, NSEG), NSEG)
        spread = pl.ds(k, NSEG, stride=seg)
        if inverse:
            dst[spread, :] = src[tile, :]
        else:
            dst[tile, :] = src[spread, :]
        return 0

    lax.fori_loop(0, seg, body, 0, unroll=8)


def _s5_states(up, bbt_ref, lam_ref, xs, t):
    seg = t // NSEG
    rb = min(512, t)
    nj = xs.shape[0]
    nh = nj // 2
    for r0 in range(0, t, rb):
        bu = _dot(up[pl.ds(r0, rb), :].astype(BF16), bbt_ref[...])
        for j in range(nj):
            xs[j, pl.ds(r0, rb), :] = bu[:, j * LANE:(j + 1) * LANE]
    lamv = lam_ref[...]
    lr = [jnp.broadcast_to(lamv[j:j + 1], (NSEG, LANE)) for j in range(nh)]
    li = [jnp.broadcast_to(lamv[nh + j:nh + j + 1], (NSEG, LANE)) for j in range(nh)]
    zero = jnp.zeros((NSEG, LANE), F32)

    def load(k):
        rows = pl.ds(pl.multiple_of(k * NSEG, NSEG), NSEG)
        return (tuple(xs.at[j][rows, :] for j in range(nh)), tuple(xs.at[nh + j][rows, :] for j in range(nh)))

    def step(k, c):
        xr, xi, br, bi = c
        nbr, nbi = load(jnp.minimum(k + 1, seg - 1))
        rows = pl.ds(pl.multiple_of(k * NSEG, NSEG), NSEG)
        nxr, nxi = [], []
        for j in range(nh):
            r = lr[j] * xr[j] - li[j] * xi[j] + br[j]
            i = lr[j] * xi[j] + li[j] * xr[j] + bi[j]
            xs.at[j][rows, :] = r
            xs.at[nh + j][rows, :] = i
            nxr.append(r)
            nxi.append(i)
        return tuple(nxr), tuple(nxi), nbr, nbi

    xr, xi, _, _ = lax.fori_loop(0, seg, step, ((zero,) * nh, (zero,) * nh) + load(0), unroll=4)
    pw = [_cpow(lr[j], li[j], seg) for j in range(nh)]
    sub = lax.broadcasted_iota(jnp.int32, (NSEG, LANE), 0)
    ctr, cti = [], []
    for j in range(nh):
        plr, pli = pw[j]
        cr, ci = zero, zero
        for s in range(1, NSEG):
            er = pltpu.roll(xr[j], 1, 0)
            ei = pltpu.roll(xi[j], 1, 0)
            pr_ = pltpu.roll(cr, 1, 0)
            pi_ = pltpu.roll(ci, 1, 0)
            nr = er + plr * pr_ - pli * pi_
            ni = ei + plr * pi_ + pli * pr_
            cr = jnp.where(sub == s, nr, cr)
            ci = jnp.where(sub == s, ni, ci)
        ctr.append(cr)
        cti.append(ci)

    def fix(k, c):
        rr, ri, xr_, xi_ = c
        nxr, nxi = load(jnp.minimum(k + 1, seg - 1))
        rows = pl.ds(pl.multiple_of(k * NSEG, NSEG), NSEG)
        nr, ni = [], []
        for j in range(nh):
            r = lr[j] * rr[j] - li[j] * ri[j]
            i = lr[j] * ri[j] + li[j] * rr[j]
            xs.at[j][rows, :] = xr_[j] + r
            xs.at[nh + j][rows, :] = xi_[j] + i
            nr.append(r)
            ni.append(i)
        return tuple(nr), tuple(ni), nxr, nxi

    lax.fori_loop(0, seg, fix, (tuple(ctr), tuple(cti)) + load(0), unroll=4)
    return lr, li, pw


def _s5_fwd(proj, bbt, cmat, lam, dvec, w, name):
    t = proj.shape[0]
    nt = w // LANE
    nj = bbt.shape[2] // LANE
    seg = t // NSEG
    rb = min(512, t)
    uo = 5 * nt

    def body(u_ref, bbt_ref, cm_ref, lam_ref, d_ref, s5o_ref, yc_ref, xs, uf, up):
        for r0 in range(0, t, rb):
            uf[pl.ds(r0, rb), :] = u_ref[pl.ds(r0, rb), :].astype(F32)
        _permute_rows(uf, up, seg, False)
        _s5_states(up, bbt_ref, lam_ref, xs, t)
        for r0 in range(0, t, rb):
            rows = pl.ds(r0, rb)
            y = d_ref[...] * up[rows, :]
            for j in range(nj):
                y = y + _dot(xs.at[j][rows, :].astype(BF16), cm_ref[pl.ds(j * LANE, LANE), :])
            uf[rows, :] = y
        _permute_rows(uf, s5o_ref, seg, True)
        for r0 in range(0, t, rb):
            rows = pl.ds(r0, rb)
            yc_ref[rows, :] = _gelu(s5o_ref[rows, :]).astype(BF16)

    col = lambda c0: pl.BlockSpec((t, LANE), functools.partial(lambda i, c0: (0, c0 + i), c0=c0))
    return pl.pallas_call(
        body, name=name, grid=(nt,),
        in_specs=[col(uo), pl.BlockSpec((None, LANE, nj * LANE), lambda i: (i, 0, 0)),
                  pl.BlockSpec((None, nj * LANE, LANE), lambda i: (i, 0, 0)),
                  pl.BlockSpec((None, nj, LANE), lambda i: (i, 0, 0)), pl.BlockSpec((1, LANE), lambda i: (0, i))],
        out_specs=[col(0), col(0)],
        out_shape=[jax.ShapeDtypeStruct((t, w), F32), jax.ShapeDtypeStruct((t, w), BF16)],
        scratch_shapes=[pltpu.VMEM((nj, t, LANE), F32), pltpu.VMEM((t, LANE), F32), pltpu.VMEM((t, LANE), F32)],
        compiler_params=_cp(("parallel",)))(proj, bbt, cmat, lam, dvec)


def _s5_bwd(proj, dy, bbt, bbm, cmt, lam, dvec, w, name):
    t = proj.shape[0]
    nt = w // LANE
    nj = bbt.shape[2] // LANE
    nh = nj // 2
    seg = t // NSEG
    rb = min(512, t)
    uo = 5 * nt

    def body(u_ref, dy_ref, bbt_ref, bbm_ref, cmt_ref, lam_ref, d_ref, du_ref, dbbt_ref, dcm_ref, dlam_ref, dd_ref,
             xs, gs, uf, up, dyp):
        for r0 in range(0, t, rb):
            uf[pl.ds(r0, rb), :] = u_ref[pl.ds(r0, rb), :].astype(F32)
        _permute_rows(uf, up, seg, False)
        for r0 in range(0, t, rb):
            uf[pl.ds(r0, rb), :] = dy_ref[pl.ds(r0, rb), :].astype(F32)
        _permute_rows(uf, dyp, seg, False)
        lr, li, pw = _s5_states(up, bbt_ref, lam_ref, xs, t)
        dcm = [jnp.zeros((LANE, LANE), F32) for _ in range(nj)]
        for r0 in range(0, t, rb):
            rows = pl.ds(r0, rb)
            dyb = dyp[rows, :].astype(BF16)
            dx = _dot(dyb, cmt_ref[...])
            for j in range(nj):
                gs.at[j][rows, :] = dx[:, j * LANE:(j + 1) * LANE]
                dcm[j] = dcm[j] + _dot(xs.at[j][rows, :].astype(BF16), dyb, TN)
        for j in range(nj):
            dcm_ref[pl.ds(j * LANE, LANE), :] = dcm[j]
        zero = jnp.zeros((NSEG, LANE), F32)
        sub = lax.broadcasted_iota(jnp.int32, (NSEG, LANE), 0)
        tile = lambda k: pl.ds(pl.multiple_of(k * NSEG, NSEG), NSEG)

        def gload(k):
            return (tuple(gs.at[j][tile(k), :] for j in range(nh)), tuple(gs.at[nh + j][tile(k), :] for j in range(nh)))

        def xload(k):
            return (tuple(xs.at[j][tile(k), :] for j in range(nh)), tuple(xs.at[nh + j][tile(k), :] for j in range(nh)))

        def step(kk, c):
            gr, gi, dr, di = c
            k = seg - 1 - kk
            ndr, ndi = gload(jnp.maximum(k - 1, 0))
            ngr, ngi = [], []
            for j in range(nh):
                r = lr[j] * gr[j] + li[j] * gi[j] + dr[j]
                i = lr[j] * gi[j] - li[j] * gr[j] + di[j]
                gs.at[j][tile(k), :] = r
                gs.at[nh + j][tile(k), :] = i
                ngr.append(r)
                ngi.append(i)
            return tuple(ngr), tuple(ngi), ndr, ndi

        g0r, g0i, _, _ = lax.fori_loop(0, seg, step, ((zero,) * nh, (zero,) * nh) + gload(seg - 1), unroll=4)
        ctr, cti = [], []
        for j in range(nh):
            plr, pli = pw[j]
            cr, ci = zero, zero
            for s in range(NSEG - 2, -1, -1):
                sr = pltpu.roll(g0r[j], NSEG - 1, 0)
                si = pltpu.roll(g0i[j], NSEG - 1, 0)
                pr_ = pltpu.roll(cr, NSEG - 1, 0)
                pi_ = pltpu.roll(ci, NSEG - 1, 0)
                nr = sr + plr * pr_ + pli * pi_
                ni = si + plr * pi_ - pli * pr_
                cr = jnp.where(sub == s, nr, cr)
                ci = jnp.where(sub == s, ni, ci)
            ctr.append(cr)
            cti.append(ci)

        def fix_step(k, c, xr, xi, gr, gi):
            rr, ri, ar, ai = c
            nr, ni, nar, nai = [], [], [], []
            for j in range(nh):
                r = lr[j] * rr[j] + li[j] * ri[j]
                i = lr[j] * ri[j] - li[j] * rr[j]
                g_r = gr[j] + r
                g_i = gi[j] + i
                gs.at[j][tile(k), :] = g_r
                gs.at[nh + j][tile(k), :] = g_i
                nar.append(ar[j] + g_r * xr[j] + g_i * xi[j])
                nai.append(ai[j] + g_i * xr[j] - g_r * xi[j])
                nr.append(r)
                ni.append(i)
            return tuple(nr), tuple(ni), tuple(nar), tuple(nai)

        def fix(kk, c):
            k = seg - 1 - kk
            nk = jnp.maximum(k - 1, 0)
            ngr, ngi = gload(nk)
            nxr, nxi = xload(jnp.maximum(nk - 1, 0))
            return fix_step(k, c[:4], c[6], c[7], c[4], c[5]) + (ngr, ngi, nxr, nxi)

        init = (tuple(ctr), tuple(cti), (zero,) * nh, (zero,) * nh) + gload(seg - 1) + xload(seg - 2)
        c = lax.fori_loop(0, seg - 1, fix, init, unroll=4)
        lxr, lxi = xload(seg - 1)
        shift = lambda v: jnp.where(sub == 0, 0.0, pltpu.roll(v, 1, 0))
        _, _, ar, ai = fix_step(0, c[:4], tuple(shift(v) for v in lxr), tuple(shift(v) for v in lxi), c[4], c[5])
        for j in range(nh):
            dlam_ref[pl.ds(j, 1), :] = jnp.sum(ar[j], axis=0, keepdims=True)
            dlam_ref[pl.ds(nh + j, 1), :] = jnp.sum(ai[j], axis=0, keepdims=True)
        dbb = [jnp.zeros((LANE, LANE), F32) for _ in range(nj)]
        dd = jnp.zeros((1, LANE), F32)
        for r0 in range(0, t, rb):
            rows = pl.ds(r0, rb)
            uv = up[rows, :]
            ub = uv.astype(BF16)
            dyf = dyp[rows, :]
            du = d_ref[...] * dyf
            dd = dd + jnp.sum(dyf * uv, axis=0, keepdims=True)
            for j in range(nj):
                gb = gs.at[j][rows, :].astype(BF16)
                du = du + _dot(gb, bbm_ref[pl.ds(j * LANE, LANE), :])
                dbb[j] = dbb[j] + _dot(ub, gb, TN)
            uf[rows, :] = du
        for j in range(nj):
            dbbt_ref[:, pl.ds(j * LANE, LANE)] = dbb[j]
        dd_ref[...] = dd
        _permute_rows(uf, up, seg, True)
        for r0 in range(0, t, rb):
            rows = pl.ds(r0, rb)
            du_ref[rows, :] = up[rows, :].astype(BF16)

    col = lambda c0: pl.BlockSpec((t, LANE), functools.partial(lambda i, c0: (0, c0 + i), c0=c0))
    wide = pl.BlockSpec((None, LANE, nj * LANE), lambda i: (i, 0, 0))
    tall = pl.BlockSpec((None, nj * LANE, LANE), lambda i: (i, 0, 0))
    lam_s = pl.BlockSpec((None, nj, LANE), lambda i: (i, 0, 0))
    vec = pl.BlockSpec((1, LANE), lambda i: (0, i))
    flat = lambda: pltpu.VMEM((t, LANE), F32)
    return pl.pallas_call(
        body, name=name, grid=(nt,),
        in_specs=[col(uo), col(0), wide, tall, wide, lam_s, vec],
        out_specs=[col(0), wide, tall, lam_s, vec],
        out_shape=[jax.ShapeDtypeStruct((t, w), BF16), jax.ShapeDtypeStruct((nt, LANE, nj * LANE), F32),
                   jax.ShapeDtypeStruct((nt, nj * LANE, LANE), F32), jax.ShapeDtypeStruct((nt, nj, LANE), F32),
                   jax.ShapeDtypeStruct((1, w), F32)],
        scratch_shapes=[pltpu.VMEM((nj, t, LANE), F32), pltpu.VMEM((nj, t, LANE), F32), flat(), flat(), flat()],
        compiler_params=_cp(("parallel",)))(proj, dy, bbt, bbm, cmt, lam, dvec)


def _block_diag(x, nt):
    nb, r, c = x.shape
    b = nb // nt
    eye = jnp.eye(b, dtype=x.dtype)
    return jnp.einsum("ibrc,bk->ibrkc", x.reshape(nt, b, r, c), eye).reshape(nt, b * r, b * c)


def _block_diag_extract(x, b):
    nt, br, bc = x.shape
    r, c = br // b, bc // b
    eye = jnp.eye(b, dtype=x.dtype)
    return jnp.einsum("ibrkc,bk->ibrc", x.reshape(nt, b, r, b, c), eye).reshape(nt * b, r, c)


def _adamw_math(w, g, m, v):
    m = ADAM_B1 * m + (1.0 - ADAM_B1) * g
    v = ADAM_B2 * v + (1.0 - ADAM_B2) * (g * g)
    m_hat = m / (1.0 - ADAM_B1 ** ADAM_STEP)
    v_hat = v / (1.0 - ADAM_B2 ** ADAM_STEP)
    delta = -ADAM_LR * (m_hat / (jnp.sqrt(v_hat) + ADAM_EPS) + ADAM_WD * w)
    return delta, m, v


def _adamw_reduce(parts, w, m, v, prev, l, name):
    n, r, c = parts.shape
    tr = r
    for cand in (512, 256, 128, 64, 32, 16, 8):
        if r % cand == 0 and cand * c * 4 <= 2 * 1024 * 1024:
            tr = cand
            break
    nb = r // tr
    if prev is None:
        prev = [lax.empty(w.shape, F32) for _ in range(4)]

    def body(p_ref, w_ref, m_ref, v_ref, *rest):
        g_ref, d_ref, nm_ref, nv_ref = rest[4:]
        g = p_ref[0].astype(F32)
        for q in range(1, n):
            g = g + p_ref[q].astype(F32)
        g_ref[...] = g
        d_ref[...], nm_ref[...], nv_ref[...] = _adamw_math(w_ref[...], g, m_ref[...], v_ref[...])

    row = pl.BlockSpec((tr, c), lambda i: (l * nb + i, 0))
    any_spec = pl.BlockSpec(memory_space=pl.ANY)
    o = jax.ShapeDtypeStruct(w.shape, F32)
    return pl.pallas_call(
        body, name=name, grid=(nb,),
        in_specs=[pl.BlockSpec((n, tr, c), lambda i: (0, i, 0)), row, row, row] + [any_spec] * 4,
        out_specs=[row, row, row, row], out_shape=[o, o, o, o],
        input_output_aliases={4: 0, 5: 1, 6: 2, 7: 3},
        compiler_params=_cp(("parallel",)))(parts, w, m, v, *prev)


def _adamw_flat(w, g, m, v, name):
    r, c = w.shape

    def body(w_ref, g_ref, m_ref, v_ref, d_ref, nm_ref, nv_ref):
        d_ref[...], nm_ref[...], nv_ref[...] = _adamw_math(w_ref[...], g_ref[...], m_ref[...], v_ref[...])

    o = jax.ShapeDtypeStruct((r, c), F32)
    row = pl.BlockSpec((FLAT_ROWS, c), lambda i: (i, 0))
    return pl.pallas_call(body, name=name, grid=(r // FLAT_ROWS,), in_specs=[row] * 4, out_specs=[row] * 3,
                          out_shape=[o, o, o], compiler_params=_cp(("parallel",)))(w, g, m, v)


def _sum_rows8(x, name):
    n, r, c = x.shape

    def body(x_ref, o_ref):
        acc = x_ref[0]
        for q in range(1, n):
            acc = acc + x_ref[q]
        o_ref[...] = acc

    return pl.pallas_call(
        body, name=name, grid=(r // FLAT_ROWS,), in_specs=[pl.BlockSpec((n, FLAT_ROWS, c), lambda i: (0, i, 0))],
        out_specs=pl.BlockSpec((FLAT_ROWS, c), lambda i: (i, 0)), out_shape=jax.ShapeDtypeStruct((r, c), F32),
        compiler_params=_cp(("parallel",)))(x)


def _pair_sum(own, sib, core, name):
    _, _, r, c = own.shape
    tr = r
    for cand in (512, 256, 128, 64, 32, 16):
        if r % cand == 0 and cand * c * 2 <= 2 * 1024 * 1024:
            tr = cand
            break

    def body(c_ref, a_ref, b_ref, o_ref):
        o_ref[...] = (a_ref[...].astype(F32) + b_ref[...].astype(F32)).astype(BF16)

    grid_spec = pltpu.PrefetchScalarGridSpec(
        num_scalar_prefetch=1, grid=(NCHIP, r // tr),
        in_specs=[pl.BlockSpec((None, None, tr, c), lambda p, i, cr: (p, cr[0], i, 0)),
                  pl.BlockSpec((None, tr, c), lambda p, i, cr: (p, i, 0))],
        out_specs=pl.BlockSpec((None, tr, c), lambda p, i, cr: (p, i, 0)))
    return pl.pallas_call(body, name=name, grid_spec=grid_spec, out_shape=jax.ShapeDtypeStruct((NCHIP, r, c), BF16),
                          compiler_params=_cp(("parallel", "parallel")))(core, own, sib)


def _all_gather(xs, name):
    n = len(xs)

    def body(*refs):
        x_refs, o_refs = refs[:n], refs[n:2 * n]
        send_sems, recv_sems, local_sems = refs[2 * n:]
        x, y, c = lax.axis_index("x"), lax.axis_index("y"), lax.axis_index("c")
        me, sibling = (x, y, c), (x, y, 1 - c)
        chips = [(1 - x, y), (x, 1 - y), (1 - x, 1 - y)]

        def copy(t, k, block, to, src=None):
            dst = o_refs[t].at[4 * block[0] + 2 * block[1] + block[2]]
            return pltpu.make_async_remote_copy(
                src_ref=dst if src is None else src, dst_ref=dst,
                send_sem=send_sems.at[7 * t + k], recv_sem=recv_sems.at[7 * t + k],
                device_id=to, device_id_type=MESH)

        mine, first, passed = [], [], []
        for t in range(n):
            cp = pltpu.make_async_copy(x_refs[t], o_refs[t].at[4 * x + 2 * y + c], local_sems.at[t])
            cp.start()
            mine.append(cp)
            first.append(copy(t, 0, me, sibling, src=x_refs[t]))
            first += [copy(t, 1 + j, me, (*chip, c), src=x_refs[t]) for j, chip in enumerate(chips)]
        for cp in first:
            cp.start()
        for t in range(n):
            for j, chip in enumerate(chips):
                copy(t, 1 + j, (*chip, c), me).wait_recv()
                cp = copy(t, 4 + j, (*chip, c), sibling)
                cp.start()
                passed.append(cp)
        for t in range(n):
            copy(t, 0, sibling, me).wait_recv()
            for j, chip in enumerate(chips):
                copy(t, 4 + j, (*chip, 1 - c), me).wait_recv()
        for cp in first + passed:
            cp.wait_send()
        for cp in mine:
            cp.wait()

    any_spec = pl.BlockSpec(memory_space=pl.ANY)
    return pl.pallas_call(
        body, name=name, in_specs=[any_spec] * n, out_specs=[any_spec] * n,
        out_shape=[jax.ShapeDtypeStruct((NDEV,) + a.shape, a.dtype) for a in xs],
        scratch_shapes=[pltpu.SemaphoreType.DMA((7 * n,)), pltpu.SemaphoreType.DMA((7 * n,)),
                        pltpu.SemaphoreType.DMA((n,))],
        )(*xs)


def _sibling_exchange(xs, name):
    n = len(xs)

    def body(*refs):
        x_refs, o_refs = refs[:n], refs[n:2 * n]
        send_sems, recv_sems = refs[2 * n:]
        x, y, c = lax.axis_index("x"), lax.axis_index("y"), lax.axis_index("c")
        cps = []
        for t in range(n):
            cp = pltpu.make_async_remote_copy(
                src_ref=x_refs[t].at[:, 1 - c], dst_ref=o_refs[t], send_sem=send_sems.at[t], recv_sem=recv_sems.at[t],
                device_id=(x, y, 1 - c), device_id_type=MESH)
            cp.start()
            cps.append(cp)
        for cp in cps:
            cp.wait()

    any_spec = pl.BlockSpec(memory_space=pl.ANY)
    return pl.pallas_call(
        body, name=name, in_specs=[any_spec] * n, out_specs=[any_spec] * n,
        out_shape=[jax.ShapeDtypeStruct((a.shape[0],) + a.shape[2:], a.dtype) for a in xs],
        scratch_shapes=[pltpu.SemaphoreType.DMA((n,)), pltpu.SemaphoreType.DMA((n,))],
        )(*xs)


def _place_own(kind, srcs, name):
    n = len(srcs)

    def body(*refs):
        s_refs, o_refs, sems = refs[:n], refs[n:2 * n], refs[2 * n]
        x, y, c = lax.axis_index("x"), lax.axis_index("y"), lax.axis_index("c")
        cps = []
        for t in range(n):
            if kind == "all":
                cp = pltpu.make_async_copy(s_refs[t], o_refs[t].at[4 * x + 2 * y + c], sems.at[t])
            else:
                cp = pltpu.make_async_copy(s_refs[t].at[2 * x + y], o_refs[t].at[2 * x + y], sems.at[t])
            cp.start()
            cps.append(cp)
        for cp in cps:
            cp.wait()

    any_spec = pl.BlockSpec(memory_space=pl.ANY)
    shape = lambda a: ((NDEV,) + a.shape) if kind == "all" else a.shape
    return pl.pallas_call(
        body, name=name, in_specs=[any_spec] * n, out_specs=[any_spec] * n,
        out_shape=[jax.ShapeDtypeStruct(shape(a), a.dtype) for a in srcs],
        scratch_shapes=[pltpu.SemaphoreType.DMA((n,))])(*srcs)


def _split_copies(kind, srcs, lands, send_sems, recv_sems):
    x, y, c = lax.axis_index("x"), lax.axis_index("y"), lax.axis_index("c")
    flip = lambda v, b: 1 - v if b else v
    cps = []
    for t in range(len(srcs)):
        if kind == "all":
            for mask in range(1, NDEV):
                peer = (flip(x, mask & 4), flip(y, mask & 2), flip(c, mask & 1))
                cps.append(pltpu.make_async_remote_copy(
                    src_ref=srcs[t], dst_ref=lands[t].at[4 * x + 2 * y + c],
                    send_sem=send_sems.at[(NDEV - 1) * t + mask - 1], recv_sem=recv_sems.at[(NDEV - 1) * t + mask - 1],
                    device_id=peer, device_id_type=MESH))
        else:
            for j, (px, py) in enumerate([(1 - x, y), (x, 1 - y), (1 - x, 1 - y)]):
                cps.append(pltpu.make_async_remote_copy(
                    src_ref=srcs[t].at[2 * px + py], dst_ref=lands[t].at[2 * x + y],
                    send_sem=send_sems.at[3 * t + j], recv_sem=recv_sems.at[3 * t + j],
                    device_id=(px, py, c), device_id_type=MESH))
    return cps


def _split_start(kind, srcs, lands, name):
    n = len(srcs)
    ncp = n * (NDEV - 1 if kind == "all" else NCHIP - 1)
    arrs = list(srcs) + list(lands)

    def body(*refs):
        for cp in _split_copies(kind, refs[:n], refs[n:2 * n], refs[2 * n], refs[2 * n + 1]):
            cp.start()
        refs[-1][...] = jnp.zeros_like(refs[-1])

    hbm = pl.BlockSpec(memory_space=pltpu.HBM)
    sem = pl.BlockSpec(memory_space=pltpu.SEMAPHORE)
    return pl.pallas_call(
        body, name=name,
        out_shape=(pltpu.SemaphoreType.DMA((ncp,)), pltpu.SemaphoreType.DMA((ncp,)),
                   *[pltpu.HBM(a.shape, a.dtype) for a in arrs], jax.ShapeDtypeStruct((8, LANE), F32)),
        in_specs=[hbm] * (2 * n), out_specs=(sem, sem, *[hbm] * (2 * n), pl.BlockSpec(memory_space=pltpu.VMEM)),
        input_output_aliases={i: 2 + i for i in range(2 * n)},
        compiler_params=pltpu.CompilerParams(has_side_effects=pltpu.SideEffectType.DATAFLOW_SIDE_EFFECTING),
    )(*[pltpu.with_memory_space_constraint(a, pltpu.HBM) for a in arrs])


def _split_wait(kind, started, after, name):
    send_sems, recv_sems, thru = started[0], started[1], list(started[2:])
    n = len(thru) // 2

    def body(*refs):
        for cp in _split_copies(kind, refs[:n], refs[n:2 * n], refs[2 * n], refs[2 * n + 1]):
            cp.wait_send()
            cp.wait_recv()

    hbm = pl.BlockSpec(memory_space=pltpu.HBM)
    sem = pl.BlockSpec(memory_space=pltpu.SEMAPHORE)
    outs = pl.pallas_call(
        body, name=name, out_shape=tuple(pltpu.HBM(a.shape, a.dtype) for a in thru),
        in_specs=[hbm] * (2 * n) + [sem, sem, pl.BlockSpec(memory_space=pl.ANY)], out_specs=tuple([hbm] * (2 * n)),
        input_output_aliases={i: i for i in range(2 * n)},
        compiler_params=pltpu.CompilerParams(has_side_effects=pltpu.SideEffectType.DATAFLOW_SIDE_EFFECTING),
    )(*thru, send_sems, recv_sems, after)
    return list(outs[n:])


_SMALL = ["norm_mix_g", "lru_conv_b", "lru_wa", "lru_ba", "lru_wx", "lru_bx", "lru_lambda", "attn_rel_bias",
          "ssm_a_re", "ssm_a_im", "ssm_b_re", "ssm_b_im", "ssm_c_re", "ssm_c_im", "ssm_d", "ssm_log_step",
          "norm_ffn_g", "norm_final_g"]
_SMALL_SHARDED = ["gate_bias", "lru_conv_w"]
_BIG = ["w_in", "ssm_w_glu", "w_branch", "w_out", "w_ffn_gate", "w_ffn_up", "w_ffn_down"]
_WEIGHTS = ["norm_mix_g", "w_in", "gate_bias", "lru_conv_w", "lru_conv_b", "lru_wa", "lru_ba", "lru_wx", "lru_bx",
            "lru_lambda", "attn_rel_bias", "ssm_a_re", "ssm_a_im", "ssm_b_re", "ssm_b_im", "ssm_c_re", "ssm_c_im",
            "ssm_d", "ssm_log_step", "ssm_w_glu", "w_branch", "w_out", "norm_ffn_g", "w_ffn_gate", "w_ffn_up",
            "w_ffn_down", "norm_final_g"]


def _device_step(x, target, wts, small_gath, get_gath, on_grads):
    t, d = x.shape
    w = d // 2
    nl = wts["norm_mix_g"].shape[0]
    nt = w // LANE
    f = wts["w_ffn_gate"].shape[-1]
    ncol = wts["w_in"].shape[-1]
    tm = _tile(t, 1024)
    n_g = w // SSM_GROUP
    gpt = LANE // SSM_GROUP
    bw = w // LRU_BLOCKS
    bpt = LANE // bw

    gbias = jnp.transpose(small_gath["gate_bias"], (1, 2, 0, 3)).reshape(nl, N_BRANCH, d)
    convw = jnp.transpose(small_gath["lru_conv_w"], (1, 2, 0, 3)).reshape(nl, CONV_W, w)
    bias_tab = _bias_expand(wts["attn_rel_bias"], "bias_expand")
    btr = jnp.swapaxes(wts["ssm_b_re"], 2, 3)
    bti = jnp.swapaxes(wts["ssm_b_im"], 2, 3)
    ls3 = wts["ssm_log_step"][..., None]
    lam_r, lam_i, bbr, bbi = _s5_prep(wts["ssm_a_re"], wts["ssm_a_im"], ls3, btr, bti, "s5_prep")

    def layer_consts(l):
        c = {}
        c["wa_t"] = _block_diag(wts["lru_wa"][l], nt).astype(BF16)
        c["wx_t"] = _block_diag(wts["lru_wx"][l], nt).astype(BF16)
        c["wat_t"] = jnp.swapaxes(c["wa_t"], 1, 2)
        c["wxt_t"] = jnp.swapaxes(c["wx_t"], 1, 2)
        bd_r, bd_i = _block_diag(bbr[l], nt), _block_diag(bbi[l], nt)
        c["bbt"] = jnp.concatenate([bd_r, bd_i], axis=-1).astype(BF16)
        c["bbm"] = jnp.swapaxes(c["bbt"], 1, 2)
        cd_r, cd_i = _block_diag(wts["ssm_c_re"][l], nt), _block_diag(wts["ssm_c_im"][l], nt)
        c["cmt"] = jnp.concatenate([cd_r, -cd_i], axis=-1).astype(BF16)
        c["cmat"] = jnp.swapaxes(c["cmt"], 1, 2)
        nrow = gpt * SSM_P // LANE
        c["lam"] = jnp.concatenate([lam_r[l].reshape(nt, nrow, LANE), lam_i[l].reshape(nt, nrow, LANE)], axis=1)
        return c

    row1 = lambda a: a.reshape(1, -1)
    tied = lambda g, tok: g if tok is None else g + tok[0, 0]
    saved = []
    for l in range(nl):
        c = layer_consts(l)
        gath, tok = get_gath(l, x)
        h1 = _rmsnorm_fwd(x, tied(row1(wts["norm_mix_g"][l]), tok), f"norm_mix_{l}")
        tn = _tile(ncol, 768)
        nn_ = ncol // tn
        proj = _mm(f"in_proj_{l}", NN, (t // tm, NDEV, nn_), 0, [h1, gath["w_in"]],
                   [pl.BlockSpec((tm, d), lambda i, j, n: (i, 0)),
                    pl.BlockSpec((None, d, tn), lambda i, j, n: (j, 0, n))],
                   jax.ShapeDtypeStruct((t, NDEV * ncol), BF16),
                   pl.BlockSpec((tm, tn), lambda i, j, n: (i, j * nn_ + n)), (8, LANE))
        ya, hs = _lru_fwd(proj, convw[l], row1(wts["lru_conv_b"][l]), c["wa_t"], row1(wts["lru_ba"][l]), c["wx_t"],
                          row1(wts["lru_bx"][l]), row1(wts["lru_lambda"][l]), w, f"lru_fwd_{l}")
        yb = _attn_fwd(proj, bias_tab[l], w, f"attn_fwd_{l}")
        s5o, yc = _s5_fwd(proj, c["bbt"], c["cmat"], c["lam"], row1(wts["ssm_d"][l]), w, f"s5_fwd_{l}")
        cb = d // NDEV

        def branch(y, wg, idx, nm):
            return _mm(nm, NN, (t // tm, NDEV), 0, [y, wg],
                       [pl.BlockSpec((tm, w), lambda i, j: (i, 0)),
                        pl.BlockSpec((None,) * len(idx(0)[:-2]) + (w, cb), lambda i, j: idx(j))],
                       jax.ShapeDtypeStruct((t, d), BF16), pl.BlockSpec((tm, cb), lambda i, j: (i, j)), (8, LANE))

        wb_idx = lambda b: (lambda j: (j, b, 0, 0))
        br_a = branch(ya, gath["w_branch"], wb_idx(0), f"branch_a_{l}")
        br_b = branch(yb, gath["w_branch"], wb_idx(1), f"branch_b_{l}")
        br_c0 = branch(yc, gath["w_branch"], wb_idx(2), f"branch_c_{l}")
        gl = branch(yc, gath["ssm_w_glu"], lambda j: (j, 0, 0), f"branch_glu_{l}")
        merged = _merge_fwd(proj, gbias[l], br_a, br_b, br_c0, gl, d, f"merge_fwd_{l}")
        tno = _tile(d, 1024)
        rb_ = d // NDEV
        x_mid = _mm(f"out_proj_{l}", NN, (t // tm, d // tno, NDEV), 1, [merged, gath["w_out"], x],
                    [pl.BlockSpec((tm, rb_), lambda i, n, k: (i, k)),
                     pl.BlockSpec((None, rb_, tno), lambda i, n, k: (k, 0, n)),
                     pl.BlockSpec((tm, tno), lambda i, n, k: (i, n))],
                    jax.ShapeDtypeStruct((t, d), F32), pl.BlockSpec((tm, tno), lambda i, n, k: (i, n)), (tm, tno),
                    epi=lambda acc, res: acc + res)
        h2 = _rmsnorm_fwd(x_mid, row1(wts["norm_ffn_g"][l]), f"norm_ffn_{l}")

        def ffn_in(wg, nm):
            return _mm(nm, NN, (t // tm, NDEV), 0, [h2, wg],
                       [pl.BlockSpec((tm, d), lambda i, j: (i, 0)),
                        pl.BlockSpec((None, d, f), lambda i, j: (j, 0, 0))],
                       jax.ShapeDtypeStruct((NDEV, t, f), BF16), pl.BlockSpec((None, tm, f), lambda i, j: (j, i, 0)),
                       (8, LANE))

        gp = ffn_in(gath["w_ffn_gate"], f"ffn_gate_{l}")
        up = ffn_in(gath["w_ffn_up"], f"ffn_up_{l}")
        act = _swiglu_fwd(gp, up, f"swiglu_fwd_{l}")
        x_next = _mm(f"ffn_down_{l}", NN, (t // tm, d // tno, NDEV), 1, [act, gath["w_ffn_down"], x_mid],
                     [pl.BlockSpec((None, tm, f), lambda i, n, k: (k, i, 0)),
                      pl.BlockSpec((None, f, tno), lambda i, n, k: (k, 0, n)),
                      pl.BlockSpec((tm, tno), lambda i, n, k: (i, n))],
                     jax.ShapeDtypeStruct((t, d), F32), pl.BlockSpec((tm, tno), lambda i, n, k: (i, n)), (tm, tno),
                     epi=lambda acc, res: acc + res)
        saved.append(dict(x=x, h1=h1, proj=proj, ya=ya, hs=hs, yb=yb, s5o=s5o, yc=yc, br_a=br_a, br_b=br_b,
                          br_c0=br_c0, gl=gl, merged=merged, x_mid=x_mid, h2=h2, gp=gp, up=up, act=act, c=c, gath=gath))
        x = x_next

    loss, dx, dxb, dgf = _loss_head(x, row1(wts["norm_final_g"]), target, "loss_head")

    sg = {k: [None] * nl for k in _SMALL + _SMALL_SHARDED if k != "norm_final_g"}
    dbias_tabs = [None] * nl
    dlr_l, dli_l, dbbr_l, dbbi_l = [None] * nl, [None] * nl, [None] * nl, [None] * nl
    tk = _tile(d, 1024)
    for l in range(nl - 1, -1, -1):
        s = saved[l]
        c = s["c"]
        gath = s["gath"]
        gbuf = {}
        dact = _mm(f"d_act_{l}", NT, (t // tm, NDEV), 0, [dxb, gath["w_ffn_down"]],
                   [pl.BlockSpec((tm, d), lambda i, j: (i, 0)),
                    pl.BlockSpec((None, f, d), lambda i, j: (j, 0, 0))],
                   jax.ShapeDtypeStruct((NDEV, t, f), BF16), pl.BlockSpec((None, tm, f), lambda i, j: (j, i, 0)), (8, LANE))
        tno = _tile(d, 1024)
        gbuf["w_ffn_down"] = _mm(
            f"dw_ffn_down_{l}", TN, (NDEV, d // tno, t // tm), 1, [s["act"], dxb],
            [pl.BlockSpec((None, tm, f), lambda j, n, r: (j, r, 0)), pl.BlockSpec((tm, tno), lambda j, n, r: (r, n))],
            jax.ShapeDtypeStruct((NDEV, f, d), BF16),
            pl.BlockSpec((None, f, tno), lambda j, n, r: (j, 0, n)), (f, tno))
        dgp, dup = _swiglu_bwd(s["gp"], s["up"], dact, f"swiglu_bwd_{l}")

        def ffn_dh(dz, wg, nm):
            return _mm(nm, NT, (t // tm, d // tk, NDEV), 1, [dz, wg],
                       [pl.BlockSpec((None, tm, f), lambda i, k, j: (j, i, 0)),
                        pl.BlockSpec((None, tk, f), lambda i, k, j: (j, k, 0))],
                       jax.ShapeDtypeStruct((t, d), F32), pl.BlockSpec((tm, tk), lambda i, k, j: (i, k)), (tm, tk))

        dh2a = ffn_dh(dgp, gath["w_ffn_gate"], f"d_h2_gate_{l}")
        dh2b = ffn_dh(dup, gath["w_ffn_up"], f"d_h2_up_{l}")

        def ffn_dw(dz, key, nm):
            return _mm(nm, TN, (NDEV, d // tk, t // tm), 1, [s["h2"], dz],
                       [pl.BlockSpec((tm, tk), lambda j, k, r: (r, k)), pl.BlockSpec((None, tm, f), lambda j, k, r: (j, r, 0))],
                       jax.ShapeDtypeStruct((NDEV, d, f), BF16),
                       pl.BlockSpec((None, tk, f), lambda j, k, r: (j, k, 0)), (tk, f))

        gbuf["w_ffn_gate"] = ffn_dw(dgp, "w_ffn_gate", f"dw_ffn_gate_{l}")
        gbuf["w_ffn_up"] = ffn_dw(dup, "w_ffn_up", f"dw_ffn_up_{l}")
        dx_mid, dxmb, dg2 = _rmsnorm_bwd(s["x_mid"], row1(wts["norm_ffn_g"][l]), [dh2a, dh2b], dx, f"norm_ffn_bwd_{l}")
        sg["norm_ffn_g"][l] = dg2.reshape(-1)
        rb_ = d // NDEV
        dmerged = _mm(f"d_merged_{l}", NT, (t // tm, NDEV), 0, [dxmb, gath["w_out"]],
                      [pl.BlockSpec((tm, d), lambda i, j: (i, 0)),
                       pl.BlockSpec((None, rb_, d), lambda i, j: (j, 0, 0))],
                      jax.ShapeDtypeStruct((t, d), BF16), pl.BlockSpec((tm, rb_), lambda i, j: (i, j)), (8, LANE))
        gbuf["w_out"] = _mm(
            f"dw_out_{l}", TN, (NDEV, d // tno, t // tm), 1, [s["merged"], dxmb],
            [pl.BlockSpec((tm, rb_), lambda j, n, r: (r, j)), pl.BlockSpec((tm, tno), lambda j, n, r: (r, n))],
            jax.ShapeDtypeStruct((NDEV, rb_, d), BF16),
            pl.BlockSpec((None, rb_, tno), lambda j, n, r: (j, 0, n)), (rb_, tno))
        dbr_a, dbr_b, dbr_c0, dgl, dgates, dgb = _merge_bwd(s["proj"], gbias[l], s["br_a"], s["br_b"], s["br_c0"], s["gl"],
                                                           dmerged, d, f"merge_bwd_{l}")
        sg["gate_bias"][l] = dgb.reshape(-1)
        cb = d // NDEV

        def branch_dy(dbr, wg, idx, nm, dt):
            nlead = len(idx(0, 0)) - 2
            return _mm(nm, NT, (t // tm, NDEV), 1, [dbr, wg],
                       [pl.BlockSpec((tm, cb), lambda i, j: (i, j)),
                        pl.BlockSpec((None,) * nlead + (w, cb), lambda i, j: idx(j, 0))],
                       jax.ShapeDtypeStruct((t, w), dt), pl.BlockSpec((tm, w), lambda i, j: (i, 0)), (tm, w))

        def branch_dw(y, dbr, buf, shape, idx, nm):
            nlead = len(idx(0)) - 2
            return _mm(nm, TN, (NDEV, t // tm), 1, [y, dbr] + ([] if buf is None else [buf]),
                       [pl.BlockSpec((tm, w), lambda j, r: (r, 0)), pl.BlockSpec((tm, cb), lambda j, r: (r, j))],
                       jax.ShapeDtypeStruct(shape, BF16),
                       pl.BlockSpec((None,) * nlead + (w, cb), lambda j, r: idx(j)), (w, cb),
                       alias=None if buf is None else True)

        wb_i = lambda b: (lambda j, z=0: (j, b, 0, 0))
        glu_i = lambda j, z=0: (j, 0, 0)
        dya = branch_dy(dbr_a, gath["w_branch"], wb_i(0), f"d_ya_{l}", BF16)
        dyb = branch_dy(dbr_b, gath["w_branch"], wb_i(1), f"d_yb_{l}", BF16)
        dyc1 = branch_dy(dbr_c0, gath["w_branch"], wb_i(2), f"d_yc_{l}", F32)
        dyc2 = branch_dy(dgl, gath["ssm_w_glu"], glu_i, f"d_yc_glu_{l}", F32)
        wb_shape = (NDEV, N_BRANCH, w, cb)
        gwb = branch_dw(s["ya"], dbr_a, lax.empty(wb_shape, BF16), wb_shape, wb_i(0), f"dw_branch_a_{l}")
        gwb = branch_dw(s["yb"], dbr_b, gwb, wb_shape, wb_i(1), f"dw_branch_b_{l}")
        gbuf["w_branch"] = branch_dw(s["yc"], dbr_c0, gwb, wb_shape, wb_i(2), f"dw_branch_c_{l}")
        gbuf["ssm_w_glu"] = branch_dw(s["yc"], dgl, None, (NDEV, w, cb), glu_i, f"dw_glu_{l}")
        ds5o = _gelu_bwd_sum(s["s5o"], dyc1, dyc2, f"gelu_bwd_{l}")
        du, dbbt, dcm, dlam, dd = _s5_bwd(s["proj"], ds5o, c["bbt"], c["bbm"], c["cmt"], c["lam"], row1(wts["ssm_d"][l]),
                                          w, f"s5_bwd_{l}")
        sp = gpt * SSM_P
        dbbr_l[l] = _block_diag_extract(dbbt[:, :, :sp], gpt)
        dbbi_l[l] = _block_diag_extract(dbbt[:, :, sp:], gpt)
        dcmt = jnp.swapaxes(dcm, 1, 2)
        sg["ssm_c_re"][l] = _block_diag_extract(dcmt[:, :, :sp], gpt).reshape(-1)
        sg["ssm_c_im"][l] = (-_block_diag_extract(dcmt[:, :, sp:], gpt)).reshape(-1)
        nrow = sp // LANE
        dlr_l[l] = dlam[:, :nrow].reshape(n_g, SSM_P)
        dli_l[l] = dlam[:, nrow:].reshape(n_g, SSM_P)
        sg["ssm_d"][l] = dd.reshape(-1)
        dq, dk, dv, dbt = _attn_bwd(s["proj"], bias_tab[l], dyb, w, f"attn_bwd_{l}")
        dbias_tabs[l] = dbt
        dlx, dlg, dcw, dcb, dwa, dba, dwx, dbx, dlm = _lru_bwd(
            s["proj"], s["hs"], dya, convw[l], row1(wts["lru_conv_b"][l]), c["wa_t"], row1(wts["lru_ba"][l]), c["wx_t"],
            row1(wts["lru_bx"][l]), row1(wts["lru_lambda"][l]), c["wat_t"], c["wxt_t"], w, f"lru_bwd_{l}")
        sg["lru_conv_w"][l] = dcw.reshape(-1)
        sg["lru_conv_b"][l] = dcb.reshape(-1)
        sg["lru_wa"][l] = _block_diag_extract(dwa, bpt).reshape(-1)
        sg["lru_wx"][l] = _block_diag_extract(dwx, bpt).reshape(-1)
        sg["lru_ba"][l] = dba.reshape(-1)
        sg["lru_bx"][l] = dbx.reshape(-1)
        sg["lru_lambda"][l] = dlm.reshape(-1)
        dproj = jnp.concatenate([dlx, dlg, dq, dk, dv, du, dgates], axis=1)
        tn = _tile(ncol, 768)
        nn_ = ncol // tn
        dh1 = _mm(f"d_h1_{l}", NT, (t // tm, d // tk, NDEV, nn_), 2, [dproj, gath["w_in"]],
                  [pl.BlockSpec((tm, tn), lambda i, k, j, n: (i, j * nn_ + n)),
                   pl.BlockSpec((None, tk, tn), lambda i, k, j, n: (j, k, n))],
                  jax.ShapeDtypeStruct((t, d), F32), pl.BlockSpec((tm, tk), lambda i, k, j, n: (i, k)), (tm, tk))
        gbuf["w_in"] = _mm(
            f"dw_in_{l}", TN, (NDEV, d // tk, nn_, t // tm), 1, [s["h1"], dproj],
            [pl.BlockSpec((tm, tk), lambda j, k, n, r: (r, k)), pl.BlockSpec((tm, tn), lambda j, k, n, r: (r, j * nn_ + n))],
            jax.ShapeDtypeStruct((NDEV, d, ncol), BF16),
            pl.BlockSpec((None, tk, tn), lambda j, k, n, r: (j, k, n)), (tk, tn))
        tok = on_grads(l, gbuf)
        dx, dxb, dg1 = _rmsnorm_bwd(s["x"], tied(row1(wts["norm_mix_g"][l]), tok), [dh1], dx_mid, f"norm_mix_bwd_{l}")
        sg["norm_mix_g"][l] = dg1.reshape(-1)

    da_re, da_im, dls, dbtr, dbti = _s5_prep_bwd(wts["ssm_a_re"], wts["ssm_a_im"], ls3, btr, bti, jnp.stack(dlr_l),
                                                 jnp.stack(dli_l), jnp.stack(dbbr_l), jnp.stack(dbbi_l), "s5_prep_bwd")
    drel = _bias_reduce(jnp.stack(dbias_tabs), "bias_reduce")
    small = {k: jnp.stack(v) for k, v in sg.items() if v[0] is not None}
    small["ssm_a_re"] = da_re.reshape(nl, -1)
    small["ssm_a_im"] = da_im.reshape(nl, -1)
    small["ssm_log_step"] = dls.reshape(nl, -1)
    small["ssm_b_re"] = jnp.swapaxes(dbtr, 2, 3).reshape(nl, -1)
    small["ssm_b_im"] = jnp.swapaxes(dbti, 2, 3).reshape(nl, -1)
    small["attn_rel_bias"] = drel.reshape(nl, -1)
    small["norm_final_g"] = dgf.reshape(-1)
    return loss, dx, small


def kernel(x, norm_mix_g, w_in, gate_bias, lru_conv_w, lru_conv_b, lru_wa, lru_ba, lru_wx, lru_bx, lru_lambda, attn_rel_bias, ssm_a_re, ssm_a_im, ssm_b_re, ssm_b_im, ssm_c_re, ssm_c_im, ssm_d, ssm_log_step, ssm_w_glu, w_branch, w_out, norm_ffn_g, w_ffn_gate, w_ffn_up, w_ffn_down, norm_final_g, loss_target, m_norm_mix_g, m_w_in, m_gate_bias, m_lru_conv_w, m_lru_conv_b, m_lru_wa, m_lru_ba, m_lru_wx, m_lru_bx, m_lru_lambda, m_attn_rel_bias, m_ssm_a_re, m_ssm_a_im, m_ssm_b_re, m_ssm_b_im, m_ssm_c_re, m_ssm_c_im, m_ssm_d, m_ssm_log_step, m_ssm_w_glu, m_w_branch, m_w_out, m_norm_ffn_g, m_w_ffn_gate, m_w_ffn_up, m_w_ffn_down, m_norm_final_g, v_norm_mix_g, v_w_in, v_gate_bias, v_lru_conv_w, v_lru_conv_b, v_lru_wa, v_lru_ba, v_lru_wx, v_lru_bx, v_lru_lambda, v_attn_rel_bias, v_ssm_a_re, v_ssm_a_im, v_ssm_b_re, v_ssm_b_im, v_ssm_c_re, v_ssm_c_im, v_ssm_d, v_ssm_log_step, v_ssm_w_glu, v_w_branch, v_w_out, v_norm_ffn_g, v_w_ffn_gate, v_w_ffn_up, v_w_ffn_down, v_norm_final_g):
    args = locals()
    wts = {k: args[k] for k in _WEIGHTS}
    mom = {k: args["m_" + k] for k in _WEIGHTS}
    vel = {k: args["v_" + k] for k in _WEIGHTS}
    cx, cy, cc = lax.axis_index("x"), lax.axis_index("y"), lax.axis_index("c")
    me = 4 * cx + 2 * cy + cc
    nl = norm_mix_g.shape[0]

    big16 = {k: wts[k].astype(BF16) for k in _BIG}
    first = _all_gather([big16[k][0] for k in _BIG] + [wts[k] for k in _SMALL_SHARDED], "gather_weights_0")
    small_gath = dict(zip(_SMALL_SHARDED, first[len(_BIG):]))
    in_flight = {}

    def get_gath(l, x_in):
        if l == 0:
            gath = dict(zip(_BIG, first[:len(_BIG)]))
        else:
            gath = dict(zip(_BIG, _split_wait("all", in_flight.pop(("w", l)), x_in, f"gather_wait_{l}")))
        if l + 1 == nl:
            return gath, None
        srcs = [big16[k][l + 1] for k in _BIG]
        started = _split_start("all", srcs, _place_own("all", srcs, f"gather_place_{l + 1}"), f"gather_start_{l + 1}")
        in_flight[("w", l + 1)] = started[:-1]
        return gath, started[-1]

    core = cc.astype(jnp.int32).reshape(1)

    def on_grads(l, gbuf):
        own = [gbuf[k].reshape((NCHIP, 2) + gbuf[k].shape[1:]) for k in _BIG]
        sib = _sibling_exchange(own, f"grad_sibling_exchange_{l}")
        pair = []
        for k, a, b in zip(_BIG, own, sib):
            c_ = a.shape[-1]
            pair.append(_pair_sum(a.reshape(NCHIP, 2, -1, c_), b.reshape(NCHIP, -1, c_), core,
                                  f"pair_sum_{k}_{l}").reshape(b.shape))
        started = _split_start("chips", pair, _place_own("chips", pair, f"grad_place_{l}"), f"grad_start_{l}")
        in_flight[("g", l)] = started[:-1]
        return started[-1]

    loss, dx, small = _device_step(x[0], loss_target[0], wts, small_gath, get_gath, on_grads)

    res = {k: None for k in _BIG}
    for l in range(nl - 1, -1, -1):
        parts = _split_wait("chips", in_flight.pop(("g", l)), dx, f"grad_wait_{l}")
        for k, p in zip(_BIG, parts):
            c_ = wts[k].shape[-1]
            res[k] = _adamw_reduce(p.reshape(NCHIP, -1, c_), wts[k].reshape(-1, c_), mom[k].reshape(-1, c_),
                                   vel[k].reshape(-1, c_), res[k], l, f"adamw_{k}_{l}")
    out = {k: tuple(a.reshape(wts[k].shape) for a in res[k]) for k in _BIG}

    order = _SMALL + _SMALL_SHARDED
    flat = jnp.concatenate([small[k].reshape(-1) for k in order])
    n_flat = flat.shape[0]
    flat = jnp.pad(flat, (0, (-n_flat) % (FLAT_ROWS * LANE))).reshape(-1, LANE)
    (allflat,) = _all_gather([flat], "gather_small_grads")
    gsum = _sum_rows8(allflat, "sum_small_grads").reshape(-1)
    gs, off = {}, 0
    for k in order:
        n = small[k].size
        gs[k] = gsum[off:off + n]
        off += n
    gs_loc = {}
    for k in _SMALL:
        gs_loc[k] = gs[k].reshape(wts[k].shape)
    ncb = gate_bias.shape[-1]
    gs_loc["gate_bias"] = lax.dynamic_slice_in_dim(gs["gate_bias"].reshape(nl, N_BRANCH, -1), me * ncb, ncb, axis=2)
    ncw = lru_conv_w.shape[-1]
    gs_loc["lru_conv_w"] = lax.dynamic_slice_in_dim(gs["lru_conv_w"].reshape(nl, CONV_W, -1), me * ncw, ncw, axis=2)

    def pack(dct):
        v = jnp.concatenate([dct[k].reshape(-1) for k in order])
        return jnp.pad(v, (0, (-v.shape[0]) % (FLAT_ROWS * LANE))).reshape(-1, LANE)

    dl_f, nm_f, nv_f = _adamw_flat(pack(wts), pack(gs_loc), pack(mom), pack(vel), "adamw_small")
    off = 0
    for k in order:
        n = wts[k].size
        sl = lambda a: a.reshape(-1)[off:off + n].reshape(wts[k].shape)
        out[k] = (gs_loc[k], sl(dl_f), sl(nm_f), sl(nv_f))
        off += n

    loss_total = lax.psum(loss[0, 0], ("x", "y", "c"))
    return (loss_total, dx[None], *[out[k][0] for k in _WEIGHTS], *[out[k][1] for k in _WEIGHTS],
            *[out[k][2] for k in _WEIGHTS], *[out[k][3] for k in _WEIGHTS])
```

```python
import functools
import math

import numpy as np
import jax
import jax.numpy as jnp
from jax import lax
from jax.experimental import pallas as pl
from jax.experimental.pallas import tpu as pltpu

F32 = jnp.float32
BF16 = jnp.bfloat16
LANE = 128
NSEG = 8
NDEV = 8
NCHIP = 4
MESH = pl.DeviceIdType.MESH
VMEM_LIMIT = 56 * 1024 * 1024
FLAT_ROWS = 512

NORM_EPS = 1e-6
CHUNK = 64
ATT_LEFT = 8
ATT_BAND = (ATT_LEFT + 1) * CHUNK
MAX_REL = 128
N_REL = 2 * MAX_REL + 1
N_REL_PAD = 384
ATT_HEADS = 8
MASK_VALUE = -1e30
LRU_C = 8.0
LRU_BLOCKS = 16
SSM_GROUP = 16
SSM_P = 64
CONV_W = 4
N_BRANCH = 3

ADAM_LR = 0.001
ADAM_B1 = 0.9
ADAM_B2 = 0.999
ADAM_EPS = 1e-08
ADAM_WD = 0.01
ADAM_STEP = 10

NN = (((1,), (0,)), ((), ()))
NT = (((1,), (1,)), ((), ()))
TN = (((0,), (0,)), ((), ()))


def _cp(sem=None, vmem=VMEM_LIMIT, **kw):
    return pltpu.CompilerParams(dimension_semantics=sem, vmem_limit_bytes=vmem, **kw)


def _dot(a, b, dn=NN):
    return lax.dot_general(a, b, dn, preferred_element_type=F32)


def _gelu(x):
    c = math.sqrt(2.0 / math.pi)
    return 0.5 * x * (1.0 + jnp.tanh(c * (x + 0.044715 * x * x * x)))


def _gelu_grad(x):
    c = math.sqrt(2.0 / math.pi)
    t = jnp.tanh(c * (x + 0.044715 * x * x * x))
    return 0.5 * (1.0 + t) + 0.5 * x * (1.0 - t * t) * c * (1.0 + 3.0 * 0.044715 * x * x)


def _sigmoid(x):
    return 1.0 / (1.0 + jnp.exp(-x))


def _one_minus_exp(z):
    series = -(z * (1.0 + z * (0.5 + z * (1.0 / 6.0 + z * (1.0 / 24.0)))))
    return jnp.where(z > -0.02, series, 1.0 - jnp.exp(z))


def _softplus_neg(lam):
    e = jnp.exp(-jnp.abs(lam))
    series = e * (1.0 - e * (0.5 - e * (1.0 / 3.0 - e * 0.25)))
    log1p_e = jnp.where(e < 0.02, series, jnp.log(1.0 + e))
    return jnp.maximum(-lam, 0.0) + log1p_e


def _mm(name, dn, grid, n_red, ins, in_specs, out_shape, out_spec, acc_shape, epi=None, alias=None):
    n_extra = len(ins) - 2 - (1 if alias is not None else 0)
    red_axes = tuple(range(len(grid) - n_red, len(grid)))
    red_sizes = tuple(grid[r] for r in red_axes)

    def body(*refs):
        a_ref, b_ref = refs[0], refs[1]
        extra = refs[2:2 + n_extra]
        o_ref, acc = refs[-2], refs[-1]
        if n_red:
            first = functools.reduce(jnp.logical_and, [pl.program_id(r) == 0 for r in red_axes])
            last = functools.reduce(jnp.logical_and,
                                    [pl.program_id(r) == n - 1 for r, n in zip(red_axes, red_sizes)])

            @pl.when(first)
            def _():
                acc[...] = jnp.zeros_like(acc)

            acc[...] += _dot(a_ref[...], b_ref[...], dn)

            @pl.when(last)
            def _():
                r = acc[...]
                if epi is not None:
                    r = epi(r, *[e[...] for e in extra])
                o_ref[...] = r.astype(o_ref.dtype)
        else:
            r = _dot(a_ref[...], b_ref[...], dn)
            if epi is not None:
                r = epi(r, *[e[...] for e in extra])
            o_ref[...] = r.astype(o_ref.dtype)

    specs = list(in_specs)
    kw = {}
    if alias is not None:
        specs.append(pl.BlockSpec(memory_space=pl.ANY))
        kw["input_output_aliases"] = {len(ins) - 1: 0}
    sem = ("parallel",) * (len(grid) - n_red) + ("arbitrary",) * n_red
    return pl.pallas_call(
        body, name=name, grid=grid, in_specs=specs, out_specs=out_spec, out_shape=out_shape,
        scratch_shapes=[pltpu.VMEM(acc_shape, F32)], compiler_params=_cp(sem), **kw)(*ins)


def _tile(n, pref):
    if n <= pref:
        return n
    t = pref
    while t >= LANE:
        if n % t == 0 and t % LANE == 0:
            return t
        t -= LANE
    return n


def _rmsnorm_fwd(x, g, name):
    t, d = x.shape
    tm = min(512, t)

    def body(x_ref, g_ref, h_ref):
        xv = x_ref[...]
        r = lax.rsqrt(jnp.mean(xv * xv, axis=-1, keepdims=True) + NORM_EPS)
        h_ref[...] = (xv * r * g_ref[...]).astype(h_ref.dtype)

    return pl.pallas_call(
        body, name=name, grid=(t // tm,),
        in_specs=[pl.BlockSpec((tm, d), lambda i: (i, 0)), pl.BlockSpec((1, d), lambda i: (0, 0))],
        out_specs=pl.BlockSpec((tm, d), lambda i: (i, 0)),
        out_shape=jax.ShapeDtypeStruct((t, d), BF16), compiler_params=_cp(("parallel",)))(x, g)


def _rmsnorm_bwd(x, g, dhs, dres, name):
    t, d = x.shape
    tm = min(256, t)
    n_dh = len(dhs)

    def body(*refs):
        x_ref, g_ref = refs[0], refs[1]
        dh_refs = refs[2:2 + n_dh]
        dres_ref = refs[2 + n_dh]
        dx_ref, dxb_ref, dg_ref = refs[3 + n_dh:]
        xv = x_ref[...]
        r = lax.rsqrt(jnp.mean(xv * xv, axis=-1, keepdims=True) + NORM_EPS)
        xhat = xv * r
        dh = dh_refs[0][...].astype(F32)
        for e in dh_refs[1:]:
            dh = dh + e[...].astype(F32)
        dxh = dh * g_ref[...]
        dx = r * (dxh - xhat * jnp.mean(dxh * xhat, axis=-1, keepdims=True)) + dres_ref[...]
        dx_ref[...] = dx
        dxb_ref[...] = dx.astype(BF16)

        @pl.when(pl.program_id(0) == 0)
        def _():
            dg_ref[...] = jnp.zeros_like(dg_ref)

        dg_ref[...] += jnp.sum(dh * xhat, axis=0, keepdims=True)

    row = pl.BlockSpec((tm, d), lambda i: (i, 0))
    par = pl.BlockSpec((1, d), lambda i: (0, 0))
    return pl.pallas_call(
        body, name=name, grid=(t // tm,),
        in_specs=[row, par] + [row] * n_dh + [row],
        out_specs=[row, row, par],
        out_shape=[jax.ShapeDtypeStruct((t, d), F32), jax.ShapeDtypeStruct((t, d), BF16),
                   jax.ShapeDtypeStruct((1, d), F32)],
        compiler_params=_cp(("arbitrary",)))(x, g, *dhs, dres)


def _loss_head(x, g, target, name):
    t, d = x.shape
    tm = min(256, t)

    def body(x_ref, g_ref, t_ref, loss_ref, dx_ref, dxb_ref, dg_ref):
        xv = x_ref[...]
        r = lax.rsqrt(jnp.mean(xv * xv, axis=-1, keepdims=True) + NORM_EPS)
        xhat = xv * r
        y = xhat * g_ref[...]
        err = y - t_ref[...]
        dy = err * (1.0 / d)
        dxh = dy * g_ref[...]
        dx = r * (dxh - xhat * jnp.mean(dxh * xhat, axis=-1, keepdims=True))
        dx_ref[...] = dx
        dxb_ref[...] = dx.astype(BF16)

        @pl.when(pl.program_id(0) == 0)
        def _():
            dg_ref[...] = jnp.zeros_like(dg_ref)
            loss_ref[...] = jnp.zeros_like(loss_ref)

        dg_ref[...] += jnp.sum(dy * xhat, axis=0, keepdims=True)
        per_tok = jnp.sum(err * err, axis=-1, keepdims=True) * (0.5 / d)
        loss_ref[...] += jnp.sum(per_tok, axis=0, keepdims=True)

    row = pl.BlockSpec((tm, d), lambda i: (i, 0))
    par = pl.BlockSpec((1, d), lambda i: (0, 0))
    return pl.pallas_call(
        body, name=name, grid=(t // tm,),
        in_specs=[row, par, row],
        out_specs=[pl.BlockSpec((1, 1), lambda i: (0, 0)), row, row, par],
        out_shape=[jax.ShapeDtypeStruct((1, 1), F32), jax.ShapeDtypeStruct((t, d), F32),
                   jax.ShapeDtypeStruct((t, d), BF16), jax.ShapeDtypeStruct((1, d), F32)],
        compiler_params=_cp(("arbitrary",)))(x, g, target)


def _merge_fwd(proj, gbias, br_a, br_b, br_c0, gl, d, name):
    t = proj.shape[0]
    tm = min(128, t)
    off = proj.shape[1] // d - N_BRANCH

    def body(g0, g1, g2, gb, a_ref, b_ref, c_ref, gl_ref, o_ref):
        gbv = gb[...]
        s0 = _sigmoid(g0[...].astype(F32) + gbv[0:1])
        s1 = _sigmoid(g1[...].astype(F32) + gbv[1:2])
        s2 = _sigmoid(g2[...].astype(F32) + gbv[2:3])
        brc = c_ref[...].astype(F32) * _sigmoid(gl_ref[...].astype(F32))
        o_ref[...] = (s0 * a_ref[...].astype(F32) + s1 * b_ref[...].astype(F32) + s2 * brc).astype(BF16)

    row = pl.BlockSpec((tm, d), lambda i: (i, 0))
    gs = [pl.BlockSpec((tm, d), functools.partial(lambda i, b: (i, off + b), b=b)) for b in range(N_BRANCH)]
    return pl.pallas_call(
        body, name=name, grid=(t // tm,),
        in_specs=gs + [pl.BlockSpec((N_BRANCH, d), lambda i: (0, 0)), row, row, row, row],
        out_specs=row, out_shape=jax.ShapeDtypeStruct((t, d), BF16),
        compiler_params=_cp(("parallel",)))(proj, proj, proj, gbias, br_a, br_b, br_c0, gl)


def _merge_bwd(proj, gbias, br_a, br_b, br_c0, gl, dmerged, d, name):
    t = proj.shape[0]
    tm = min(128, t)
    off = proj.shape[1] // d - N_BRANCH

    def body(g0, g1, g2, gb, a_ref, b_ref, c_ref, gl_ref, dm_ref, da_ref, db_ref, dc_ref, dgl_ref, dg_ref, dgb_ref):
        gbv = gb[...]
        dm = dm_ref[...].astype(F32)
        s0 = _sigmoid(g0[...].astype(F32) + gbv[0:1])
        s1 = _sigmoid(g1[...].astype(F32) + gbv[1:2])
        s2 = _sigmoid(g2[...].astype(F32) + gbv[2:3])
        sg = _sigmoid(gl_ref[...].astype(F32))
        c0 = c_ref[...].astype(F32)
        brc = c0 * sg
        da_ref[...] = (dm * s0).astype(BF16)
        db_ref[...] = (dm * s1).astype(BF16)
        dbrc = dm * s2
        dc_ref[...] = (dbrc * sg).astype(BF16)
        dgl_ref[...] = (dbrc * c0 * sg * (1.0 - sg)).astype(BF16)
        dp0 = dm * a_ref[...].astype(F32) * s0 * (1.0 - s0)
        dp1 = dm * b_ref[...].astype(F32) * s1 * (1.0 - s1)
        dp2 = dm * brc * s2 * (1.0 - s2)
        dg_ref[:, 0:d] = dp0.astype(BF16)
        dg_ref[:, d:2 * d] = dp1.astype(BF16)
        dg_ref[:, 2 * d:3 * d] = dp2.astype(BF16)

        @pl.when(pl.program_id(0) == 0)
        def _():
            dgb_ref[...] = jnp.zeros_like(dgb_ref)

        dgb_ref[0:1, :] += jnp.sum(dp0, axis=0, keepdims=True)
        dgb_ref[1:2, :] += jnp.sum(dp1, axis=0, keepdims=True)
        dgb_ref[2:3, :] += jnp.sum(dp2, axis=0, keepdims=True)

    row = pl.BlockSpec((tm, d), lambda i: (i, 0))
    gs = [pl.BlockSpec((tm, d), functools.partial(lambda i, b: (i, off + b), b=b)) for b in range(N_BRANCH)]
    par = pl.BlockSpec((N_BRANCH, d), lambda i: (0, 0))
    bf = jax.ShapeDtypeStruct((t, d), BF16)
    return pl.pallas_call(
        body, name=name, grid=(t // tm,),
        in_specs=gs + [par, row, row, row, row, row],
        out_specs=[row, row, row, row, pl.BlockSpec((tm, N_BRANCH * d), lambda i: (i, 0)), par],
        out_shape=[bf, bf, bf, bf, jax.ShapeDtypeStruct((t, N_BRANCH * d), BF16),
                   jax.ShapeDtypeStruct((N_BRANCH, d), F32)],
        compiler_params=_cp(("arbitrary",)))(proj, proj, proj, gbias, br_a, br_b, br_c0, gl, dmerged)


def _swiglu_fwd(gp, up, name):
    s, t, f = gp.shape
    tm = min(512, t)

    def body(g_ref, u_ref, o_ref):
        g = g_ref[...].astype(F32)
        o_ref[...] = (g * _sigmoid(g) * u_ref[...].astype(F32)).astype(BF16)

    blk = pl.BlockSpec((None, tm, f), lambda j, i: (j, i, 0))
    return pl.pallas_call(
        body, name=name, grid=(s, t // tm), in_specs=[blk, blk], out_specs=blk,
        out_shape=jax.ShapeDtypeStruct((s, t, f), BF16), compiler_params=_cp(("parallel", "parallel")))(gp, up)


def _swiglu_bwd(gp, up, dact, name):
    s, t, f = gp.shape
    tm = min(512, t)

    def body(g_ref, u_ref, d_ref, dg_ref, du_ref):
        g = g_ref[...].astype(F32)
        u = u_ref[...].astype(F32)
        da = d_ref[...].astype(F32)
        sg = _sigmoid(g)
        silu = g * sg
        du_ref[...] = (da * silu).astype(BF16)
        dg_ref[...] = (da * u * (sg + silu * (1.0 - sg))).astype(BF16)

    blk = pl.BlockSpec((None, tm, f), lambda j, i: (j, i, 0))
    o = jax.ShapeDtypeStruct((s, t, f), BF16)
    return pl.pallas_call(
        body, name=name, grid=(s, t // tm), in_specs=[blk, blk, blk], out_specs=[blk, blk],
        out_shape=[o, o], compiler_params=_cp(("parallel", "parallel")))(gp, up, dact)


def _gelu_bwd_sum(s5o, dy1, dy2, name):
    t, w = s5o.shape
    tm = min(512, t)

    def body(s_ref, a_ref, b_ref, o_ref):
        o_ref[...] = ((a_ref[...] + b_ref[...]) * _gelu_grad(s_ref[...])).astype(BF16)

    row = pl.BlockSpec((tm, w), lambda i: (i, 0))
    return pl.pallas_call(
        body, name=name, grid=(t // tm,), in_specs=[row, row, row], out_specs=row,
        out_shape=jax.ShapeDtypeStruct((t, w), BF16), compiler_params=_cp(("parallel",)))(s5o, dy1, dy2)


def _lru_gates(xc, wa, ba, wx, bx, sp):
    xcb = xc.astype(BF16)
    r = _sigmoid(_dot(xcb, wa) + ba)
    i = _sigmoid(_dot(xcb, wx) + bx)
    la = -LRU_C * r * sp
    return xcb, r, i, la


def _lru_fwd(proj, cw, cb, wa_t, ba, wx_t, bx, lam, w, name):
    t = proj.shape[0]
    nt = w // LANE
    seg = t // NSEG
    rb = min(512, seg)

    def body(lx_ref, lg_ref, cw_ref, cb_ref, wa_ref, ba_ref, wx_ref, bx_ref, lam_ref, ya_ref, hs_ref, xp, a_s, b_s):
        xp[pl.ds(0, 8), :] = jnp.zeros((8, LANE), F32)
        for r0 in range(0, t, rb):
            xp[pl.ds(8 + r0, rb), :] = lx_ref[pl.ds(r0, rb), :].astype(F32)
        sp = _softplus_neg(lam_ref[...])
        cwv = cw_ref[...]
        for r0 in range(0, t, rb):
            xc = cb_ref[...] + sum(cwv[k:k + 1] * xp[pl.ds(8 + r0 - (CONV_W - 1) + k, rb), :] for k in range(CONV_W))
            _, _, i, la = _lru_gates(xc, wa_ref[...], ba_ref[...], wx_ref[...], bx_ref[...], sp)
            a_s[pl.ds(r0, rb), :] = jnp.exp(la)
            b_s[pl.ds(r0, rb), :] = jnp.sqrt(_one_minus_exp(2.0 * la)) * (i * xc)

        def load(k):
            rows = pl.ds(k, NSEG, stride=seg)
            return a_s[rows, :], b_s[rows, :]

        def step(k, c):
            h, p, a, b = c
            na, nb = load(jnp.minimum(k + 1, seg - 1))
            rows = pl.ds(k, NSEG, stride=seg)
            h = a * h + b
            p = a * p
            b_s[rows, :] = h
            a_s[rows, :] = p
            return h, p, na, nb

        lax.fori_loop(0, seg, step, (jnp.zeros((NSEG, LANE), F32), jnp.ones((NSEG, LANE), F32)) + load(0), unroll=4)
        carry = jnp.zeros((1, LANE), F32)
        for s in range(NSEG):
            for r0 in range(s * seg, (s + 1) * seg, rb):
                rows = pl.ds(r0, rb)
                h = b_s[rows, :] + a_s[rows, :] * carry
                hs_ref[rows, :] = h
                ya_ref[rows, :] = (h * _gelu(lg_ref[rows, :].astype(F32))).astype(BF16)
            end = pl.ds((s + 1) * seg - 1, 1)
            carry = b_s[end, :] + a_s[end, :] * carry

    col = lambda c0: pl.BlockSpec((t, LANE), functools.partial(lambda i, c0: (0, c0 + i), c0=c0))
    par = lambda k: pl.BlockSpec((k, LANE), lambda i: (0, i))
    mat = pl.BlockSpec((None, LANE, LANE), lambda i: (i, 0, 0))
    return pl.pallas_call(
        body, name=name, grid=(nt,),
        in_specs=[col(0), col(nt), par(CONV_W), par(1), mat, par(1), mat, par(1), par(1)],
        out_specs=[col(0), col(0)],
        out_shape=[jax.ShapeDtypeStruct((t, w), BF16), jax.ShapeDtypeStruct((t, w), F32)],
        scratch_shapes=[pltpu.VMEM((t + 8, LANE), F32), pltpu.VMEM((t, LANE), F32), pltpu.VMEM((t, LANE), F32)],
        compiler_params=_cp(("parallel",)))(proj, proj, cw, cb, wa_t, ba, wx_t, bx, lam)


def _lru_bwd(proj, hs, dya, cw, cb, wa_t, ba, wx_t, bx, lam, wat_t, wxt_t, w, name):
    t = proj.shape[0]
    nt = w // LANE
    seg = t // NSEG
    rb = min(512, seg)

    def body(lx_ref, lg_ref, hs_ref, dy_ref, cw_ref, cb_ref, wa_ref, ba_ref, wx_ref, bx_ref, lam_ref, wat_ref, wxt_ref,
             dlx_ref, dlg_ref, dcw_ref, dcb_ref, dwa_ref, dba_ref, dwx_ref, dbx_ref, dlam_ref,
             xp, hp, a_s, g_s, q_s, dxc_s):
        z8 = jnp.zeros((8, LANE), F32)
        xp[pl.ds(0, 8), :] = z8
        hp[pl.ds(0, 8), :] = z8
        a_s[pl.ds(t, 8), :] = z8
        dxc_s[pl.ds(t, 8), :] = z8
        lamv = lam_ref[...]
        sp = _softplus_neg(lamv)
        cwv = cw_ref[...]

        def conv(r0):
            return cb_ref[...] + sum(cwv[k:k + 1] * xp[pl.ds(8 + r0 - (CONV_W - 1) + k, rb), :] for k in range(CONV_W))

        for r0 in range(0, t, rb):
            rows = pl.ds(r0, rb)
            xp[pl.ds(8 + r0, rb), :] = lx_ref[rows, :].astype(F32)
            hp[pl.ds(8 + r0, rb), :] = hs_ref[rows, :]
        for r0 in range(0, t, rb):
            rows = pl.ds(r0, rb)
            _, _, _, la = _lru_gates(conv(r0), wa_ref[...], ba_ref[...], wx_ref[...], bx_ref[...], sp)
            a_s[rows, :] = jnp.exp(la)
            g_s[rows, :] = dy_ref[rows, :].astype(F32) * _gelu(lg_ref[rows, :].astype(F32))

        def load(k):
            return a_s[pl.ds(k + 1, NSEG, stride=seg), :], g_s[pl.ds(k, NSEG, stride=seg), :]

        def step(kk, c):
            g, q, an, dh = c
            k = seg - 1 - kk
            nan, ndh = load(jnp.maximum(k - 1, 0))
            rows = pl.ds(k, NSEG, stride=seg)
            g = dh + an * g
            q = an * q
            g_s[rows, :] = g
            q_s[rows, :] = q
            return g, q, nan, ndh

        lax.fori_loop(0, seg, step, (jnp.zeros((NSEG, LANE), F32), jnp.ones((NSEG, LANE), F32)) + load(seg - 1), unroll=4)
        nxt = [None] * NSEG
        carry = jnp.zeros((1, LANE), F32)
        for s in range(NSEG - 1, -1, -1):
            nxt[s] = carry
            start = pl.ds(s * seg, 1)
            carry = g_s[start, :] + q_s[start, :] * carry

        zrow = jnp.zeros((1, LANE), F32)
        dsp = zrow
        dba = zrow
        dbx = zrow
        dcb = zrow
        dwa = jnp.zeros((LANE, LANE), F32)
        dwx = jnp.zeros((LANE, LANE), F32)
        for s in range(NSEG):
            for r0 in range(s * seg, (s + 1) * seg, rb):
                rows = pl.ds(r0, rb)
                g = g_s[rows, :] + q_s[rows, :] * nxt[s]
                xc = conv(r0)
                xcb, r, i, la = _lru_gates(xc, wa_ref[...], ba_ref[...], wx_ref[...], bx_ref[...], sp)
                a = jnp.exp(la)
                om = _one_minus_exp(2.0 * la)
                mult = jnp.sqrt(om)
                hprev = hp[pl.ds(8 + r0 - 1, rb), :]
                da = g * hprev
                dmult = g * i * xc
                di = g * mult * xc
                dxc = g * mult * i
                dla = da * a - dmult * (1.0 - om) / mult
                dr = dla * (-LRU_C) * sp
                dsp = dsp + jnp.sum(dla * (-LRU_C) * r, axis=0, keepdims=True)
                dpr = dr * r * (1.0 - r)
                dpi = di * i * (1.0 - i)
                dba = dba + jnp.sum(dpr, axis=0, keepdims=True)
                dbx = dbx + jnp.sum(dpi, axis=0, keepdims=True)
                dprb = dpr.astype(BF16)
                dpib = dpi.astype(BF16)
                dxc = dxc + _dot(dprb, wat_ref[...]) + _dot(dpib, wxt_ref[...])
                dwa = dwa + _dot(xcb, dprb, TN)
                dwx = dwx + _dot(xcb, dpib, TN)
                dxc_s[rows, :] = dxc
                dcb = dcb + jnp.sum(dxc, axis=0, keepdims=True)
                lg = lg_ref[rows, :].astype(F32)
                dlg_ref[rows, :] = (dy_ref[rows, :].astype(F32) * hs_ref[rows, :] * _gelu_grad(lg)).astype(BF16)
        dcw = [zrow] * CONV_W
        for r0 in range(0, t, rb):
            rows = pl.ds(r0, rb)
            dlx = sum(cwv[k:k + 1] * dxc_s[pl.ds(r0 + (CONV_W - 1) - k, rb), :] for k in range(CONV_W))
            dlx_ref[rows, :] = dlx.astype(BF16)
            dxc = dxc_s[rows, :]
            for k in range(CONV_W):
                dcw[k] = dcw[k] + jnp.sum(dxc * xp[pl.ds(8 + r0 - (CONV_W - 1) + k, rb), :], axis=0, keepdims=True)
        dcw_ref[...] = jnp.concatenate(dcw, axis=0)
        dcb_ref[...] = dcb
        dwa_ref[...] = dwa
        dwx_ref[...] = dwx
        dba_ref[...] = dba
        dbx_ref[...] = dbx
        dlam_ref[...] = dsp * (-_sigmoid(-lamv))

    col = lambda c0: pl.BlockSpec((t, LANE), functools.partial(lambda i, c0: (0, c0 + i), c0=c0))
    par = lambda k: pl.BlockSpec((k, LANE), lambda i: (0, i))
    mat = pl.BlockSpec((None, LANE, LANE), lambda i: (i, 0, 0))
    vec = jax.ShapeDtypeStruct((1, w), F32)
    big = lambda: pltpu.VMEM((t + 8, LANE), F32)
    return pl.pallas_call(
        body, name=name, grid=(nt,),
        in_specs=[col(0), col(nt), col(0), col(0), par(CONV_W), par(1), mat, par(1), mat, par(1), par(1), mat, mat],
        out_specs=[col(0), col(0), par(CONV_W), par(1), mat, par(1), mat, par(1), par(1)],
        out_shape=[jax.ShapeDtypeStruct((t, w), BF16), jax.ShapeDtypeStruct((t, w), BF16),
                   jax.ShapeDtypeStruct((CONV_W, w), F32), vec, jax.ShapeDtypeStruct((nt, LANE, LANE), F32), vec,
                   jax.ShapeDtypeStruct((nt, LANE, LANE), F32), vec, vec],
        scratch_shapes=[big(), big(), big(), pltpu.VMEM((t, LANE), F32), pltpu.VMEM((t, LANE), F32), big()],
        compiler_params=_cp(("parallel",)))(proj, proj, hs, dya, cw, cb, wa_t, ba, wx_t, bx, lam, wat_t, wxt_t)


def _attn_scores(q_ref, kp, bias_ref, c, hd):
    r0 = pl.multiple_of(c * CHUNK, CHUNK)
    qc = q_ref[pl.ds(r0, CHUNK), :]
    kb = kp[pl.ds(r0, ATT_BAND), :]
    s = _dot(qc, kb, NT) * (hd ** -0.5) + bias_ref[...]
    kpos = lax.broadcasted_iota(jnp.int32, (CHUNK, ATT_BAND), 1)
    s = jnp.where(kpos + (c - ATT_LEFT) * CHUNK >= 0, s, MASK_VALUE)
    m = jnp.max(s, axis=-1, keepdims=True)
    e = jnp.exp(s - m)
    p = e / jnp.sum(e, axis=-1, keepdims=True)
    return r0, qc, kb, p


def _attn_pad_copy(src_ref, dst, t, pad):
    dst[pl.ds(0, pad), :] = jnp.zeros((pad, dst.shape[1]), dst.dtype)
    rb = min(512, t)
    for r0 in range(0, t, rb):
        dst[pl.ds(pad + r0, rb), :] = src_ref[pl.ds(r0, rb), :]


def _attn_fwd(proj, bias, w, name):
    t = proj.shape[0]
    hd = w // ATT_HEADS
    pad = ATT_LEFT * CHUNK
    qo, ko, vo = 2 * w // hd, 3 * w // hd, 4 * w // hd

    def body(q_ref, k_ref, v_ref, bias_ref, o_ref, kp, vp):
        _attn_pad_copy(k_ref, kp, t, pad)
        _attn_pad_copy(v_ref, vp, t, pad)

        def chunk(c, _):
            r0, _, _, p = _attn_scores(q_ref, kp, bias_ref, c, hd)
            vb = vp[pl.ds(r0, ATT_BAND), :]
            o_ref[pl.ds(r0, CHUNK), :] = _dot(p.astype(BF16), vb).astype(BF16)
            return 0

        lax.fori_loop(0, t // CHUNK, chunk, 0, unroll=2)

    col = lambda c0: pl.BlockSpec((t, hd), functools.partial(lambda h, c0: (0, c0 + h), c0=c0))
    return pl.pallas_call(
        body, name=name, grid=(ATT_HEADS,),
        in_specs=[col(qo), col(ko), col(vo), pl.BlockSpec((None, CHUNK, ATT_BAND), lambda h: (h, 0, 0))],
        out_specs=col(0), out_shape=jax.ShapeDtypeStruct((t, w), BF16),
        scratch_shapes=[pltpu.VMEM((t + pad, hd), BF16), pltpu.VMEM((t + pad, hd), BF16)],
        compiler_params=_cp(("parallel",)))(proj, proj, proj, bias)


def _attn_bwd(proj, bias, do, w, name):
    t = proj.shape[0]
    hd = w // ATT_HEADS
    pad = ATT_LEFT * CHUNK
    qo, ko, vo = 2 * w // hd, 3 * w // hd, 4 * w // hd

    def body(q_ref, k_ref, v_ref, bias_ref, do_ref, dq_ref, dk_ref, dv_ref, db_ref, kp, vp, dkp, dvp):
        _attn_pad_copy(k_ref, kp, t, pad)
        _attn_pad_copy(v_ref, vp, t, pad)
        rb = min(512, t)
        for r0 in range(0, t + pad, rb):
            n = min(rb, t + pad - r0)
            dkp[pl.ds(r0, n), :] = jnp.zeros((n, hd), F32)
            dvp[pl.ds(r0, n), :] = jnp.zeros((n, hd), F32)
        db_ref[...] = jnp.zeros_like(db_ref)
        scale = hd ** -0.5

        def chunk(c, _):
            r0, qc, kb, p = _attn_scores(q_ref, kp, bias_ref, c, hd)
            band = pl.ds(r0, ATT_BAND)
            vb = vp[band, :]
            doc = do_ref[pl.ds(r0, CHUNK), :]
            dp = _dot(doc, vb, NT)
            ds = p * (dp - jnp.sum(dp * p, axis=-1, keepdims=True))
            db_ref[...] += ds
            dsb = ds.astype(BF16)
            dq_ref[pl.ds(r0, CHUNK), :] = (_dot(dsb, kb) * scale).astype(BF16)
            dkp[band, :] += _dot(dsb, qc, TN) * scale
            dvp[band, :] += _dot(p.astype(BF16), doc, TN)
            return 0

        lax.fori_loop(0, t // CHUNK, chunk, 0, unroll=2)
        for r0 in range(0, t, rb):
            dk_ref[pl.ds(r0, rb), :] = dkp[pl.ds(pad + r0, rb), :].astype(BF16)
            dv_ref[pl.ds(r0, rb), :] = dvp[pl.ds(pad + r0, rb), :].astype(BF16)

    col = lambda c0: pl.BlockSpec((t, hd), functools.partial(lambda h, c0: (0, c0 + h), c0=c0))
    tb = pl.BlockSpec((None, CHUNK, ATT_BAND), lambda h: (h, 0, 0))
    o = jax.ShapeDtypeStruct((t, w), BF16)
    return pl.pallas_call(
        body, name=name, grid=(ATT_HEADS,),
        in_specs=[col(qo), col(ko), col(vo), tb, col(0)],
        out_specs=[col(0), col(0), col(0), tb],
        out_shape=[o, o, o, jax.ShapeDtypeStruct((ATT_HEADS, CHUNK, ATT_BAND), F32)],
        scratch_shapes=[pltpu.VMEM((t + pad, hd), BF16), pltpu.VMEM((t + pad, hd), BF16),
                        pltpu.VMEM((t + pad, hd), F32), pltpu.VMEM((t + pad, hd), F32)],
        compiler_params=_cp(("parallel",)))(proj, proj, proj, bias, do)


def _rel_index():
    q_pos = ATT_LEFT * CHUNK + np.arange(CHUNK)
    k_pos = np.arange(ATT_BAND)
    return (np.clip(q_pos[:, None] - k_pos[None, :], -MAX_REL, MAX_REL) + MAX_REL).astype(np.int32).reshape(-1)


def _bias_expand(rel_bias, name):
    nl, nh, _ = rel_bias.shape
    n = CHUNK * ATT_BAND
    kb = n // 8
    idx = jnp.asarray(_rel_index().reshape(1, n))
    tab = jnp.pad(rel_bias, ((0, 0), (0, 0), (0, N_REL_PAD - N_REL)))

    def body(t_ref, i_ref, o_ref):
        onehot = (lax.broadcasted_iota(jnp.int32, (N_REL_PAD, kb), 0) == i_ref[...]).astype(F32)
        o_ref[...] = lax.dot_general(t_ref[...], onehot, NN, precision=lax.Precision.HIGHEST,
                                     preferred_element_type=F32)

    out = pl.pallas_call(
        body, name=name, grid=(nl, n // kb),
        in_specs=[pl.BlockSpec((None, nh, N_REL_PAD), lambda l, j: (l, 0, 0)), pl.BlockSpec((1, kb), lambda l, j: (0, j))],
        out_specs=pl.BlockSpec((None, nh, kb), lambda l, j: (l, 0, j)),
        out_shape=jax.ShapeDtypeStruct((nl, nh, n), F32), compiler_params=_cp(("parallel", "parallel")))(tab, idx)
    return out.reshape(nl, nh, CHUNK, ATT_BAND)


def _bias_reduce(dbias, name):
    nl, nh = dbias.shape[:2]
    n = CHUNK * ATT_BAND
    kb = n // 8
    idx = jnp.asarray(_rel_index().reshape(n, 1))

    def body(d_ref, i_ref, o_ref):
        @pl.when(pl.program_id(1) == 0)
        def _():
            o_ref[...] = jnp.zeros_like(o_ref)

        onehot = (lax.broadcasted_iota(jnp.int32, (kb, N_REL_PAD), 1) == i_ref[...]).astype(F32)
        o_ref[...] += lax.dot_general(d_ref[...], onehot, NN, precision=lax.Precision.HIGHEST,
                                      preferred_element_type=F32)

    out = pl.pallas_call(
        body, name=name, grid=(nl, n // kb),
        in_specs=[pl.BlockSpec((None, nh, kb), lambda l, j: (l, 0, j)), pl.BlockSpec((kb, 1), lambda l, j: (j, 0))],
        out_specs=pl.BlockSpec((None, nh, N_REL_PAD), lambda l, j: (l, 0, 0)),
        out_shape=jax.ShapeDtypeStruct((nl, nh, N_REL_PAD), F32),
        compiler_params=_cp(("parallel", "arbitrary")))(dbias.reshape(nl, nh, n), idx)
    return out[:, :, :N_REL]


def _s5_prep_math(a_re, a_im, ls, btr, bti):
    step = jnp.exp(ls)
    mag = jnp.exp(a_re * step)
    ang = a_im * step
    lr = mag * jnp.cos(ang)
    li = mag * jnp.sin(ang)
    den = a_re * a_re + a_im * a_im
    nr = lr - 1.0
    cr = (nr * a_re + li * a_im) / den
    ci = (li * a_re - nr * a_im) / den
    bbr = cr[:, None, :] * btr - ci[:, None, :] * bti
    bbi = cr[:, None, :] * bti + ci[:, None, :] * btr
    return lr, li, bbr, bbi


def _s5_prep(a_re, a_im, ls, btr, bti, name):
    nl, g, p = a_re.shape
    h = btr.shape[2]

    def body(ar, ai, l_ref, br, bi, o1, o2, o3, o4):
        r = _s5_prep_math(ar[...], ai[...], l_ref[...], br[...], bi[...])
        o1[...], o2[...], o3[...], o4[...] = r

    m2 = pl.BlockSpec((None, g, p), lambda l: (l, 0, 0))
    m1 = pl.BlockSpec((None, g, 1), lambda l: (l, 0, 0))
    m3 = pl.BlockSpec((None, g, h, p), lambda l: (l, 0, 0, 0))
    s2 = jax.ShapeDtypeStruct((nl, g, p), F32)
    s3 = jax.ShapeDtypeStruct((nl, g, h, p), F32)
    return pl.pallas_call(body, name=name, grid=(nl,), in_specs=[m2, m2, m1, m3, m3], out_specs=[m2, m2, m3, m3],
                          out_shape=[s2, s2, s3, s3], compiler_params=_cp(("parallel",)))(a_re, a_im, ls, btr, bti)


def _s5_prep_bwd(a_re, a_im, ls, btr, bti, dlr, dli, dbbr, dbbi, name):
    nl, g, p = a_re.shape
    h = btr.shape[2]

    def body(ar, ai, l_ref, br, bi, g1, g2, g3, g4, o1, o2, o3, o4, o5):
        _, vjp = jax.vjp(_s5_prep_math, ar[...], ai[...], l_ref[...], br[...], bi[...])
        o1[...], o2[...], o3[...], o4[...], o5[...] = vjp((g1[...], g2[...], g3[...], g4[...]))

    m2 = pl.BlockSpec((None, g, p), lambda l: (l, 0, 0))
    m1 = pl.BlockSpec((None, g, 1), lambda l: (l, 0, 0))
    m3 = pl.BlockSpec((None, g, h, p), lambda l: (l, 0, 0, 0))
    s2 = jax.ShapeDtypeStruct((nl, g, p), F32)
    s1 = jax.ShapeDtypeStruct((nl, g, 1), F32)
    s3 = jax.ShapeDtypeStruct((nl, g, h, p), F32)
    return pl.pallas_call(body, name=name, grid=(nl,), in_specs=[m2, m2, m1, m3, m3, m2, m2, m3, m3],
                          out_specs=[m2, m2, m1, m3, m3], out_shape=[s2, s2, s1, s3, s3],
                          compiler_params=_cp(("parallel",)))(a_re, a_im, ls, btr, bti, dlr, dli, dbbr, dbbi)


def _cpow(lr, li, n):
    rr = ri = None
    br, bi = lr, li
    while n:
        if n & 1:
            rr, ri = (br, bi) if rr is None else (rr * br - ri * bi, rr * bi + ri * br)
        n >>= 1
        if n:
            br, bi = br * br - bi * bi, 2.0 * br * bi
    return rr, ri


def _permute_rows(src, dst, seg, inverse):
    def body(k, _):
        tile = pl.ds(pl.multiple_of(k * NSEG, NSEG), NSEG)
        spread = pl.ds(k, NSEG, stride=seg)
        if inverse:
            dst[spread, :] = src[tile, :]
        else:
            dst[tile, :] = src[spread, :]
        return 0

    lax.fori_loop(0, seg, body, 0, unroll=8)


def _s5_states(up, bbt_ref, lam_ref, xs, t):
    seg = t // NSEG
    rb = min(512, t)
    nj = xs.shape[0]
    nh = nj // 2
    for r0 in range(0, t, rb):
        bu = _dot(up[pl.ds(r0, rb), :].astype(BF16), bbt_ref[...])
        for j in range(nj):
            xs[j, pl.ds(r0, rb), :] = bu[:, j * LANE:(j + 1) * LANE]
    lamv = lam_ref[...]
    lr = [jnp.broadcast_to(lamv[j:j + 1], (NSEG, LANE)) for j in range(nh)]
    li = [jnp.broadcast_to(lamv[nh + j:nh + j + 1], (NSEG, LANE)) for j in range(nh)]
    zero = jnp.zeros((NSEG, LANE), F32)

    def load(k):
        rows = pl.ds(pl.multiple_of(k * NSEG, NSEG), NSEG)
        return (tuple(xs.at[j][rows, :] for j in range(nh)), tuple(xs.at[nh + j][rows, :] for j in range(nh)))

    def step(k, c):
        xr, xi, br, bi = c
        nbr, nbi = load(jnp.minimum(k + 1, seg - 1))
        rows = pl.ds(pl.multiple_of(k * NSEG, NSEG), NSEG)
        nxr, nxi = [], []
        for j in range(nh):
            r = lr[j] * xr[j] - li[j] * xi[j] + br[j]
            i = lr[j] * xi[j] + li[j] * xr[j] + bi[j]
            xs.at[j][rows, :] = r
            xs.at[nh + j][rows, :] = i
            nxr.append(r)
            nxi.append(i)
        return tuple(nxr), tuple(nxi), nbr, nbi

    xr, xi, _, _ = lax.fori_loop(0, seg, step, ((zero,) * nh, (zero,) * nh) + load(0), unroll=4)
    pw = [_cpow(lr[j], li[j], seg) for j in range(nh)]
    sub = lax.broadcasted_iota(jnp.int32, (NSEG, LANE), 0)
    ctr, cti = [], []
    for j in range(nh):
        plr, pli = pw[j]
        cr, ci = zero, zero
        for s in range(1, NSEG):
            er = pltpu.roll(xr[j], 1, 0)
            ei = pltpu.roll(xi[j], 1, 0)
            pr_ = pltpu.roll(cr, 1, 0)
            pi_ = pltpu.roll(ci, 1, 0)
            nr = er + plr * pr_ - pli * pi_
            ni = ei + plr * pi_ + pli * pr_
            cr = jnp.where(sub == s, nr, cr)
            ci = jnp.where(sub == s, ni, ci)
        ctr.append(cr)
        cti.append(ci)

    def fix(k, c):
        rr, ri, xr_, xi_ = c
        nxr, nxi = load(jnp.minimum(k + 1, seg - 1))
        rows = pl.ds(pl.multiple_of(k * NSEG, NSEG), NSEG)
        nr, ni = [], []
        for j in range(nh):
            r = lr[j] * rr[j] - li[j] * ri[j]
            i = lr[j] * ri[j] + li[j] * rr[j]
            xs.at[j][rows, :] = xr_[j] + r
            xs.at[nh + j][rows, :] = xi_[j] + i
            nr.append(r)
            ni.append(i)
        return tuple(nr), tuple(ni), nxr, nxi

    lax.fori_loop(0, seg, fix, (tuple(ctr), tuple(cti)) + load(0), unroll=4)
    return lr, li, pw


def _s5_fwd(proj, bbt, cmat, lam, dvec, w, name):
    t = proj.shape[0]
    nt = w // LANE
    nj = bbt.shape[2] // LANE
    seg = t // NSEG
    rb = min(512, t)
    uo = 5 * nt

    def body(u_ref, bbt_ref, cm_ref, lam_ref, d_ref, s5o_ref, yc_ref, xs, uf, up):
        for r0 in range(0, t, rb):
            uf[pl.ds(r0, rb), :] = u_ref[pl.ds(r0, rb), :].astype(F32)
        _permute_rows(uf, up, seg, False)
        _s5_states(up, bbt_ref, lam_ref, xs, t)
        for r0 in range(0, t, rb):
            rows = pl.ds(r0, rb)
            y = d_ref[...] * up[rows, :]
            for j in range(nj):
                y = y + _dot(xs.at[j][rows, :].astype(BF16), cm_ref[pl.ds(j * LANE, LANE), :])
            uf[rows, :] = y
        _permute_rows(uf, s5o_ref, seg, True)
        for r0 in range(0, t, rb):
            rows = pl.ds(r0, rb)
            yc_ref[rows, :] = _gelu(s5o_ref[rows, :]).astype(BF16)

    col = lambda c0: pl.BlockSpec((t, LANE), functools.partial(lambda i, c0: (0, c0 + i), c0=c0))
    return pl.pallas_call(
        body, name=name, grid=(nt,),
        in_specs=[col(uo), pl.BlockSpec((None, LANE, nj * LANE), lambda i: (i, 0, 0)),
                  pl.BlockSpec((None, nj * LANE, LANE), lambda i: (i, 0, 0)),
                  pl.BlockSpec((None, nj, LANE), lambda i: (i, 0, 0)), pl.BlockSpec((1, LANE), lambda i: (0, i))],
        out_specs=[col(0), col(0)],
        out_shape=[jax.ShapeDtypeStruct((t, w), F32), jax.ShapeDtypeStruct((t, w), BF16)],
        scratch_shapes=[pltpu.VMEM((nj, t, LANE), F32), pltpu.VMEM((t, LANE), F32), pltpu.VMEM((t, LANE), F32)],
        compiler_params=_cp(("parallel",)))(proj, bbt, cmat, lam, dvec)


def _s5_bwd(proj, dy, bbt, bbm, cmt, lam, dvec, w, name):
    t = proj.shape[0]
    nt = w // LANE
    nj = bbt.shape[2] // LANE
    nh = nj // 2
    seg = t // NSEG
    rb = min(512, t)
    uo = 5 * nt

    def body(u_ref, dy_ref, bbt_ref, bbm_ref, cmt_ref, lam_ref, d_ref, du_ref, dbbt_ref, dcm_ref, dlam_ref, dd_ref,
             xs, gs, uf, up, dyp):
        for r0 in range(0, t, rb):
            uf[pl.ds(r0, rb), :] = u_ref[pl.ds(r0, rb), :].astype(F32)
        _permute_rows(uf, up, seg, False)
        for r0 in range(0, t, rb):
            uf[pl.ds(r0, rb), :] = dy_ref[pl.ds(r0, rb), :].astype(F32)
        _permute_rows(uf, dyp, seg, False)
        lr, li, pw = _s5_states(up, bbt_ref, lam_ref, xs, t)
        dcm = [jnp.zeros((LANE, LANE), F32) for _ in range(nj)]
        for r0 in range(0, t, rb):
            rows = pl.ds(r0, rb)
            dyb = dyp[rows, :].astype(BF16)
            dx = _dot(dyb, cmt_ref[...])
            for j in range(nj):
                gs.at[j][rows, :] = dx[:, j * LANE:(j + 1) * LANE]
                dcm[j] = dcm[j] + _dot(xs.at[j][rows, :].astype(BF16), dyb, TN)
        for j in range(nj):
            dcm_ref[pl.ds(j * LANE, LANE), :] = dcm[j]
        zero = jnp.zeros((NSEG, LANE), F32)
        sub = lax.broadcasted_iota(jnp.int32, (NSEG, LANE), 0)
        tile = lambda k: pl.ds(pl.multiple_of(k * NSEG, NSEG), NSEG)

        def gload(k):
            return (tuple(gs.at[j][tile(k), :] for j in range(nh)), tuple(gs.at[nh + j][tile(k), :] for j in range(nh)))

        def xload(k):
            return (tuple(xs.at[j][tile(k), :] for j in range(nh)), tuple(xs.at[nh + j][tile(k), :] for j in range(nh)))

        def step(kk, c):
            gr, gi, dr, di = c
            k = seg - 1 - kk
            ndr, ndi = gload(jnp.maximum(k - 1, 0))
            ngr, ngi = [], []
            for j in range(nh):
                r = lr[j] * gr[j] + li[j] * gi[j] + dr[j]
                i = lr[j] * gi[j] - li[j] * gr[j] + di[j]
                gs.at[j][tile(k), :] = r
                gs.at[nh + j][tile(k), :] = i
                ngr.append(r)
                ngi.append(i)
            return tuple(ngr), tuple(ngi), ndr, ndi

        g0r, g0i, _, _ = lax.fori_loop(0, seg, step, ((zero,) * nh, (zero,) * nh) + gload(seg - 1), unroll=4)
        ctr, cti = [], []
        for j in range(nh):
            plr, pli = pw[j]
            cr, ci = zero, zero
            for s in range(NSEG - 2, -1, -1):
                sr = pltpu.roll(g0r[j], NSEG - 1, 0)
                si = pltpu.roll(g0i[j], NSEG - 1, 0)
                pr_ = pltpu.roll(cr, NSEG - 1, 0)
                pi_ = pltpu.roll(ci, NSEG - 1, 0)
                nr = sr + plr * pr_ + pli * pi_
                ni = si + plr * pi_ - pli * pr_
                cr = jnp.where(sub == s, nr, cr)
                ci = jnp.where(sub == s, ni, ci)
            ctr.append(cr)
            cti.append(ci)

        def fix_step(k, c, xr, xi, gr, gi):
            rr, ri, ar, ai = c
            nr, ni, nar, nai = [], [], [], []
            for j in range(nh):
                r = lr[j] * rr[j] + li[j] * ri[j]
                i = lr[j] * ri[j] - li[j] * rr[j]
                g_r = gr[j] + r
                g_i = gi[j] + i
                gs.at[j][tile(k), :] = g_r
                gs.at[nh + j][tile(k), :] = g_i
                nar.append(ar[j] + g_r * xr[j] + g_i * xi[j])
                nai.append(ai[j] + g_i * xr[j] - g_r * xi[j])
                nr.append(r)
                ni.append(i)
            return tuple(nr), tuple(ni), tuple(nar), tuple(nai)

        def fix(kk, c):
            k = seg - 1 - kk
            nk = jnp.maximum(k - 1, 0)
            ngr, ngi = gload(nk)
            nxr, nxi = xload(jnp.maximum(nk - 1, 0))
            return fix_step(k, c[:4], c[6], c[7], c[4], c[5]) + (ngr, ngi, nxr, nxi)

        init = (tuple(ctr), tuple(cti), (zero,) * nh, (zero,) * nh) + gload(seg - 1) + xload(seg - 2)
        c = lax.fori_loop(0, seg - 1, fix, init, unroll=4)
        lxr, lxi = xload(seg - 1)
        shift = lambda v: jnp.where(sub == 0, 0.0, pltpu.roll(v, 1, 0))
        _, _, ar, ai = fix_step(0, c[:4], tuple(shift(v) for v in lxr), tuple(shift(v) for v in lxi), c[4], c[5])
        for j in range(nh):
            dlam_ref[pl.ds(j, 1), :] = jnp.sum(ar[j], axis=0, keepdims=True)
            dlam_ref[pl.ds(nh + j, 1), :] = jnp.sum(ai[j], axis=0, keepdims=True)
        dbb = [jnp.zeros((LANE, LANE), F32) for _ in range(nj)]
        dd = jnp.zeros((1, LANE), F32)
        for r0 in range(0, t, rb):
            rows = pl.ds(r0, rb)
            uv = up[rows, :]
            ub = uv.astype(BF16)
            dyf = dyp[rows, :]
            du = d_ref[...] * dyf
            dd = dd + jnp.sum(dyf * uv, axis=0, keepdims=True)
            for j in range(nj):
                gb = gs.at[j][rows, :].astype(BF16)
                du = du + _dot(gb, bbm_ref[pl.ds(j * LANE, LANE), :])
                dbb[j] = dbb[j] + _dot(ub, gb, TN)
            uf[rows, :] = du
        for j in range(nj):
            dbbt_ref[:, pl.ds(j * LANE, LANE)] = dbb[j]
        dd_ref[...] = dd
        _permute_rows(uf, up, seg, True)
        for r0 in range(0, t, rb):
            rows = pl.ds(r0, rb)
            du_ref[rows, :] = up[rows, :].astype(BF16)

    col = lambda c0: pl.BlockSpec((t, LANE), functools.partial(lambda i, c0: (0, c0 + i), c0=c0))
    wide = pl.BlockSpec((None, LANE, nj * LANE), lambda i: (i, 0, 0))
    tall = pl.BlockSpec((None, nj * LANE, LANE), lambda i: (i, 0, 0))
    lam_s = pl.BlockSpec((None, nj, LANE), lambda i: (i, 0, 0))
    vec = pl.BlockSpec((1, LANE), lambda i: (0, i))
    flat = lambda: pltpu.VMEM((t, LANE), F32)
    return pl.pallas_call(
        body, name=name, grid=(nt,),
        in_specs=[col(uo), col(0), wide, tall, wide, lam_s, vec],
        out_specs=[col(0), wide, tall, lam_s, vec],
        out_shape=[jax.ShapeDtypeStruct((t, w), BF16), jax.ShapeDtypeStruct((nt, LANE, nj * LANE), F32),
                   jax.ShapeDtypeStruct((nt, nj * LANE, LANE), F32), jax.ShapeDtypeStruct((nt, nj, LANE), F32),
                   jax.ShapeDtypeStruct((1, w), F32)],
        scratch_shapes=[pltpu.VMEM((nj, t, LANE), F32), pltpu.VMEM((nj, t, LANE), F32), flat(), flat(), flat()],
        compiler_params=_cp(("parallel",)))(proj, dy, bbt, bbm, cmt, lam, dvec)


def _block_diag(x, nt):
    nb, r, c = x.shape
    b = nb // nt
    eye = jnp.eye(b, dtype=x.dtype)
    return jnp.einsum("ibrc,bk->ibrkc", x.reshape(nt, b, r, c), eye).reshape(nt, b * r, b * c)


def _block_diag_extract(x, b):
    nt, br, bc = x.shape
    r, c = br // b, bc // b
    eye = jnp.eye(b, dtype=x.dtype)
    return jnp.einsum("ibrkc,bk->ibrc", x.reshape(nt, b, r, b, c), eye).reshape(nt * b, r, c)


def _adamw_math(w, g, m, v):
    m = ADAM_B1 * m + (1.0 - ADAM_B1) * g
    v = ADAM_B2 * v + (1.0 - ADAM_B2) * (g * g)
    m_hat = m / (1.0 - ADAM_B1 ** ADAM_STEP)
    v_hat = v / (1.0 - ADAM_B2 ** ADAM_STEP)
    delta = -ADAM_LR * (m_hat / (jnp.sqrt(v_hat) + ADAM_EPS) + ADAM_WD * w)
    return delta, m, v


def _adamw_reduce(parts, w, m, v, prev, l, name):
    n, r, c = parts.shape
    tr = r
    for cand in (512, 256, 128, 64, 32, 16, 8):
        if r % cand == 0 and cand * c * 4 <= 2 * 1024 * 1024:
            tr = cand
            break
    nb = r // tr
    if prev is None:
        prev = [lax.empty(w.shape, F32) for _ in range(4)]

    def body(p_ref, w_ref, m_ref, v_ref, *rest):
        g_ref, d_ref, nm_ref, nv_ref = rest[4:]
        g = p_ref[0].astype(F32)
        for q in range(1, n):
            g = g + p_ref[q].astype(F32)
        g_ref[...] = g
        d_ref[...], nm_ref[...], nv_ref[...] = _adamw_math(w_ref[...], g, m_ref[...], v_ref[...])

    row = pl.BlockSpec((tr, c), lambda i: (l * nb + i, 0))
    any_spec = pl.BlockSpec(memory_space=pl.ANY)
    o = jax.ShapeDtypeStruct(w.shape, F32)
    return pl.pallas_call(
        body, name=name, grid=(nb,),
        in_specs=[pl.BlockSpec((n, tr, c), lambda i: (0, i, 0)), row, row, row] + [any_spec] * 4,
        out_specs=[row, row, row, row], out_shape=[o, o, o, o],
        input_output_aliases={4: 0, 5: 1, 6: 2, 7: 3},
        compiler_params=_cp(("parallel",)))(parts, w, m, v, *prev)


def _adamw_flat(w, g, m, v, name):
    r, c = w.shape

    def body(w_ref, g_ref, m_ref, v_ref, d_ref, nm_ref, nv_ref):
        d_ref[...], nm_ref[...], nv_ref[...] = _adamw_math(w_ref[...], g_ref[...], m_ref[...], v_ref[...])

    o = jax.ShapeDtypeStruct((r, c), F32)
    row = pl.BlockSpec((FLAT_ROWS, c), lambda i: (i, 0))
    return pl.pallas_call(body, name=name, grid=(r // FLAT_ROWS,), in_specs=[row] * 4, out_specs=[row] * 3,
                          out_shape=[o, o, o], compiler_params=_cp(("parallel",)))(w, g, m, v)


def _sum_rows8(x, name):
    n, r, c = x.shape

    def body(x_ref, o_ref):
        acc = x_ref[0]
        for q in range(1, n):
            acc = acc + x_ref[q]
        o_ref[...] = acc

    return pl.pallas_call(
        body, name=name, grid=(r // FLAT_ROWS,), in_specs=[pl.BlockSpec((n, FLAT_ROWS, c), lambda i: (0, i, 0))],
        out_specs=pl.BlockSpec((FLAT_ROWS, c), lambda i: (i, 0)), out_shape=jax.ShapeDtypeStruct((r, c), F32),
        compiler_params=_cp(("parallel",)))(x)


def _row_tile(r, row_bytes, limit=2 * 1024 * 1024):
    for cand in (512, 256, 128, 64, 32, 16):
        if r % cand == 0 and cand * row_bytes <= limit:
            return cand
    return r


def _pair_sum(own, sib, core, chip, name):
    _, _, r, c = own.shape
    tr = _row_tile(r, 2 * c)

    def body(core_ref, chip_ref, a_ref, b_ref, o_ref, land_ref):
        s = (a_ref[...].astype(F32) + b_ref[...].astype(F32)).astype(BF16)
        o_ref[...] = s

        @pl.when(pl.program_id(1) == chip_ref[0])
        def _():
            land_ref[...] = s

    grid_spec = pltpu.PrefetchScalarGridSpec(
        num_scalar_prefetch=2, grid=(r // tr, NCHIP),
        in_specs=[pl.BlockSpec((None, None, tr, c), lambda i, p, cr, hr: (p, cr[0], i, 0)),
                  pl.BlockSpec((None, tr, c), lambda i, p, cr, hr: (p, i, 0))],
        out_specs=[pl.BlockSpec((None, tr, c), lambda i, p, cr, hr: (p, i, 0)),
                   pl.BlockSpec((None, tr, c), lambda i, p, cr, hr: (hr[0], i, 0))])
    o = jax.ShapeDtypeStruct((NCHIP, r, c), BF16)
    return pl.pallas_call(body, name=name, grid_spec=grid_spec, out_shape=[o, o],
                          compiler_params=_cp(("parallel", "arbitrary")))(core, chip, own, sib)


def _cast_place(w, l, me, after, name):
    _, r, c = w.shape
    tr = _row_tile(r, 4 * c)

    def body(m_ref, w_ref, a_ref, o_ref, land_ref):
        v = w_ref[...].astype(BF16)
        o_ref[...] = v
        land_ref[...] = v

    grid_spec = pltpu.PrefetchScalarGridSpec(
        num_scalar_prefetch=1, grid=(r // tr,),
        in_specs=[pl.BlockSpec((None, tr, c), lambda i, mr: (l, i, 0)), pl.BlockSpec(memory_space=pl.ANY)],
        out_specs=[pl.BlockSpec((tr, c), lambda i, mr: (i, 0)), pl.BlockSpec((None, tr, c), lambda i, mr: (mr[0], i, 0))])
    return pl.pallas_call(body, name=name, grid_spec=grid_spec,
                          out_shape=[jax.ShapeDtypeStruct((r, c), BF16), jax.ShapeDtypeStruct((NDEV, r, c), BF16)],
                          compiler_params=_cp(("parallel",)))(me, w, after)


def _all_gather(xs, name):
    n = len(xs)

    def body(*refs):
        x_refs, o_refs = refs[:n], refs[n:2 * n]
        send_sems, recv_sems, local_sems = refs[2 * n:]
        x, y, c = lax.axis_index("x"), lax.axis_index("y"), lax.axis_index("c")
        me, sibling = (x, y, c), (x, y, 1 - c)
        chips = [(1 - x, y), (x, 1 - y), (1 - x, 1 - y)]

        def copy(t, k, block, to, src=None):
            dst = o_refs[t].at[4 * block[0] + 2 * block[1] + block[2]]
            return pltpu.make_async_remote_copy(
                src_ref=dst if src is None else src, dst_ref=dst,
                send_sem=send_sems.at[7 * t + k], recv_sem=recv_sems.at[7 * t + k],
                device_id=to, device_id_type=MESH)

        mine, first, passed = [], [], []
        for t in range(n):
            cp = pltpu.make_async_copy(x_refs[t], o_refs[t].at[4 * x + 2 * y + c], local_sems.at[t])
            cp.start()
            mine.append(cp)
            first.append(copy(t, 0, me, sibling, src=x_refs[t]))
            first += [copy(t, 1 + j, me, (*chip, c), src=x_refs[t]) for j, chip in enumerate(chips)]
        for cp in first:
            cp.start()
        for t in range(n):
            for j, chip in enumerate(chips):
                copy(t, 1 + j, (*chip, c), me).wait_recv()
                cp = copy(t, 4 + j, (*chip, c), sibling)
                cp.start()
                passed.append(cp)
        for t in range(n):
            copy(t, 0, sibling, me).wait_recv()
            for j, chip in enumerate(chips):
                copy(t, 4 + j, (*chip, 1 - c), me).wait_recv()
        for cp in first + passed:
            cp.wait_send()
        for cp in mine:
            cp.wait()

    any_spec = pl.BlockSpec(memory_space=pl.ANY)
    return pl.pallas_call(
        body, name=name, in_specs=[any_spec] * n, out_specs=[any_spec] * n,
        out_shape=[jax.ShapeDtypeStruct((NDEV,) + a.shape, a.dtype) for a in xs],
        scratch_shapes=[pltpu.SemaphoreType.DMA((7 * n,)), pltpu.SemaphoreType.DMA((7 * n,)),
                        pltpu.SemaphoreType.DMA((n,))],
        )(*xs)


def _sibling_exchange(xs, name):
    n = len(xs)

    def body(*refs):
        x_refs, o_refs = refs[:n], refs[n:2 * n]
        send_sems, recv_sems = refs[2 * n:]
        x, y, c = lax.axis_index("x"), lax.axis_index("y"), lax.axis_index("c")
        cps = []
        for t in range(n):
            cp = pltpu.make_async_remote_copy(
                src_ref=x_refs[t].at[:, 1 - c], dst_ref=o_refs[t], send_sem=send_sems.at[t], recv_sem=recv_sems.at[t],
                device_id=(x, y, 1 - c), device_id_type=MESH)
            cp.start()
            cps.append(cp)
        for cp in cps:
            cp.wait()

    any_spec = pl.BlockSpec(memory_space=pl.ANY)
    return pl.pallas_call(
        body, name=name, in_specs=[any_spec] * n, out_specs=[any_spec] * n,
        out_shape=[jax.ShapeDtypeStruct((a.shape[0],) + a.shape[2:], a.dtype) for a in xs],
        scratch_shapes=[pltpu.SemaphoreType.DMA((n,)), pltpu.SemaphoreType.DMA((n,))],
        )(*xs)


def _split_copies(kind, srcs, lands, send_sems, recv_sems):
    x, y, c = lax.axis_index("x"), lax.axis_index("y"), lax.axis_index("c")
    flip = lambda v, b: 1 - v if b else v
    cps = []
    for t in range(len(srcs)):
        if kind == "all":
            for mask in range(1, NDEV):
                peer = (flip(x, mask & 4), flip(y, mask & 2), flip(c, mask & 1))
                cps.append(pltpu.make_async_remote_copy(
                    src_ref=srcs[t], dst_ref=lands[t].at[4 * x + 2 * y + c],
                    send_sem=send_sems.at[(NDEV - 1) * t + mask - 1], recv_sem=recv_sems.at[(NDEV - 1) * t + mask - 1],
                    device_id=peer, device_id_type=MESH))
        else:
            for j, (px, py) in enumerate([(1 - x, y), (x, 1 - y), (1 - x, 1 - y)]):
                cps.append(pltpu.make_async_remote_copy(
                    src_ref=srcs[t].at[2 * px + py], dst_ref=lands[t].at[2 * x + y],
                    send_sem=send_sems.at[3 * t + j], recv_sem=recv_sems.at[3 * t + j],
                    device_id=(px, py, c), device_id_type=MESH))
    return cps


def _split_start(kind, srcs, lands, name):
    n = len(srcs)
    ncp = n * (NDEV - 1 if kind == "all" else NCHIP - 1)
    arrs = list(srcs) + list(lands)

    def body(*refs):
        for cp in _split_copies(kind, refs[:n], refs[n:2 * n], refs[2 * n], refs[2 * n + 1]):
            cp.start()
        refs[-1][...] = jnp.zeros_like(refs[-1])

    hbm = pl.BlockSpec(memory_space=pltpu.HBM)
    sem = pl.BlockSpec(memory_space=pltpu.SEMAPHORE)
    return pl.pallas_call(
        body, name=name,
        out_shape=(pltpu.SemaphoreType.DMA((ncp,)), pltpu.SemaphoreType.DMA((ncp,)),
                   *[pltpu.HBM(a.shape, a.dtype) for a in arrs], jax.ShapeDtypeStruct((8, LANE), F32)),
        in_specs=[hbm] * (2 * n), out_specs=(sem, sem, *[hbm] * (2 * n), pl.BlockSpec(memory_space=pltpu.VMEM)),
        input_output_aliases={i: 2 + i for i in range(2 * n)},
        compiler_params=pltpu.CompilerParams(has_side_effects=pltpu.SideEffectType.DATAFLOW_SIDE_EFFECTING),
    )(*[pltpu.with_memory_space_constraint(a, pltpu.HBM) for a in arrs])


def _split_wait(kind, started, after, name):
    send_sems, recv_sems, thru = started[0], started[1], list(started[2:])
    n = len(thru) // 2

    def body(*refs):
        for cp in _split_copies(kind, refs[:n], refs[n:2 * n], refs[2 * n], refs[2 * n + 1]):
            cp.wait_send()
            cp.wait_recv()

    hbm = pl.BlockSpec(memory_space=pltpu.HBM)
    sem = pl.BlockSpec(memory_space=pltpu.SEMAPHORE)
    outs = pl.pallas_call(
        body, name=name, out_shape=tuple(pltpu.HBM(a.shape, a.dtype) for a in thru),
        in_specs=[hbm] * (2 * n) + [sem, sem, pl.BlockSpec(memory_space=pl.ANY)], out_specs=tuple([hbm] * (2 * n)),
        input_output_aliases={i: i for i in range(2 * n)},
        compiler_params=pltpu.CompilerParams(has_side_effects=pltpu.SideEffectType.DATAFLOW_SIDE_EFFECTING),
    )(*thru, send_sems, recv_sems, after)
    return list(outs[n:])


_SMALL = ["norm_mix_g", "lru_conv_b", "lru_wa", "lru_ba", "lru_wx", "lru_bx", "lru_lambda", "attn_rel_bias",
          "ssm_a_re", "ssm_a_im", "ssm_b_re", "ssm_b_im", "ssm_c_re", "ssm_c_im", "ssm_d", "ssm_log_step",
          "norm_ffn_g", "norm_final_g"]
_SMALL_SHARDED = ["gate_bias", "lru_conv_w"]
_BIG = ["w_in", "ssm_w_glu", "w_branch", "w_out", "w_ffn_gate", "w_ffn_up", "w_ffn_down"]
_WEIGHTS = ["norm_mix_g", "w_in", "gate_bias", "lru_conv_w", "lru_conv_b", "lru_wa", "lru_ba", "lru_wx", "lru_bx",
            "lru_lambda", "attn_rel_bias", "ssm_a_re", "ssm_a_im", "ssm_b_re", "ssm_b_im", "ssm_c_re", "ssm_c_im",
            "ssm_d", "ssm_log_step", "ssm_w_glu", "w_branch", "w_out", "norm_ffn_g", "w_ffn_gate", "w_ffn_up",
            "w_ffn_down", "norm_final_g"]


def _device_step(x, target, wts, small_gath, get_gath, on_grads):
    t, d = x.shape
    w = d // 2
    nl = wts["norm_mix_g"].shape[0]
    nt = w // LANE
    f = wts["w_ffn_gate"].shape[-1]
    ncol = wts["w_in"].shape[-1]
    tm = _tile(t, 1024)
    n_g = w // SSM_GROUP
    gpt = LANE // SSM_GROUP
    bw = w // LRU_BLOCKS
    bpt = LANE // bw

    gbias = jnp.transpose(small_gath["gate_bias"], (1, 2, 0, 3)).reshape(nl, N_BRANCH, d)
    convw = jnp.transpose(small_gath["lru_conv_w"], (1, 2, 0, 3)).reshape(nl, CONV_W, w)
    bias_tab = _bias_expand(wts["attn_rel_bias"], "bias_expand")
    btr = jnp.swapaxes(wts["ssm_b_re"], 2, 3)
    bti = jnp.swapaxes(wts["ssm_b_im"], 2, 3)
    ls3 = wts["ssm_log_step"][..., None]
    lam_r, lam_i, bbr, bbi = _s5_prep(wts["ssm_a_re"], wts["ssm_a_im"], ls3, btr, bti, "s5_prep")

    def layer_consts(l):
        c = {}
        c["wa_t"] = _block_diag(wts["lru_wa"][l], nt).astype(BF16)
        c["wx_t"] = _block_diag(wts["lru_wx"][l], nt).astype(BF16)
        c["wat_t"] = jnp.swapaxes(c["wa_t"], 1, 2)
        c["wxt_t"] = jnp.swapaxes(c["wx_t"], 1, 2)
        bd_r, bd_i = _block_diag(bbr[l], nt), _block_diag(bbi[l], nt)
        c["bbt"] = jnp.concatenate([bd_r, bd_i], axis=-1).astype(BF16)
        c["bbm"] = jnp.swapaxes(c["bbt"], 1, 2)
        cd_r, cd_i = _block_diag(wts["ssm_c_re"][l], nt), _block_diag(wts["ssm_c_im"][l], nt)
        c["cmt"] = jnp.concatenate([cd_r, -cd_i], axis=-1).astype(BF16)
        c["cmat"] = jnp.swapaxes(c["cmt"], 1, 2)
        nrow = gpt * SSM_P // LANE
        c["lam"] = jnp.concatenate([lam_r[l].reshape(nt, nrow, LANE), lam_i[l].reshape(nt, nrow, LANE)], axis=1)
        return c

    row1 = lambda a: a.reshape(1, -1)
    tied = lambda g, tok: g if tok is None else g + tok[0, 0]
    saved = []
    for l in range(nl):
        c = layer_consts(l)
        gath, tok = get_gath(l, x)
        h1 = _rmsnorm_fwd(x, tied(row1(wts["norm_mix_g"][l]), tok), f"norm_mix_{l}")
        tn = _tile(ncol, 768)
        nn_ = ncol // tn
        proj = _mm(f"in_proj_{l}", NN, (t // tm, NDEV, nn_), 0, [h1, gath["w_in"]],
                   [pl.BlockSpec((tm, d), lambda i, j, n: (i, 0)),
                    pl.BlockSpec((None, d, tn), lambda i, j, n: (j, 0, n))],
                   jax.ShapeDtypeStruct((t, NDEV * ncol), BF16),
                   pl.BlockSpec((tm, tn), lambda i, j, n: (i, j * nn_ + n)), (8, LANE))
        ya, hs = _lru_fwd(proj, convw[l], row1(wts["lru_conv_b"][l]), c["wa_t"], row1(wts["lru_ba"][l]), c["wx_t"],
                          row1(wts["lru_bx"][l]), row1(wts["lru_lambda"][l]), w, f"lru_fwd_{l}")
        yb = _attn_fwd(proj, bias_tab[l], w, f"attn_fwd_{l}")
        s5o, yc = _s5_fwd(proj, c["bbt"], c["cmat"], c["lam"], row1(wts["ssm_d"][l]), w, f"s5_fwd_{l}")
        cb = d // NDEV

        def branch(y, wg, idx, nm):
            return _mm(nm, NN, (t // tm, NDEV), 0, [y, wg],
                       [pl.BlockSpec((tm, w), lambda i, j: (i, 0)),
                        pl.BlockSpec((None,) * len(idx(0)[:-2]) + (w, cb), lambda i, j: idx(j))],
                       jax.ShapeDtypeStruct((t, d), BF16), pl.BlockSpec((tm, cb), lambda i, j: (i, j)), (8, LANE))

        wb_idx = lambda b: (lambda j: (j, b, 0, 0))
        br_a = branch(ya, gath["w_branch"], wb_idx(0), f"branch_a_{l}")
        br_b = branch(yb, gath["w_branch"], wb_idx(1), f"branch_b_{l}")
        br_c0 = branch(yc, gath["w_branch"], wb_idx(2), f"branch_c_{l}")
        gl = branch(yc, gath["ssm_w_glu"], lambda j: (j, 0, 0), f"branch_glu_{l}")
        merged = _merge_fwd(proj, gbias[l], br_a, br_b, br_c0, gl, d, f"merge_fwd_{l}")
        tno = _tile(d, 1024)
        rb_ = d // NDEV
        x_mid = _mm(f"out_proj_{l}", NN, (t // tm, d // tno, NDEV), 1, [merged, gath["w_out"], x],
                    [pl.BlockSpec((tm, rb_), lambda i, n, k: (i, k)),
                     pl.BlockSpec((None, rb_, tno), lambda i, n, k: (k, 0, n)),
                     pl.BlockSpec((tm, tno), lambda i, n, k: (i, n))],
                    jax.ShapeDtypeStruct((t, d), F32), pl.BlockSpec((tm, tno), lambda i, n, k: (i, n)), (tm, tno),
                    epi=lambda acc, res: acc + res)
        h2 = _rmsnorm_fwd(x_mid, row1(wts["norm_ffn_g"][l]), f"norm_ffn_{l}")

        def ffn_in(wg, nm):
            return _mm(nm, NN, (t // tm, NDEV), 0, [h2, wg],
                       [pl.BlockSpec((tm, d), lambda i, j: (i, 0)),
                        pl.BlockSpec((None, d, f), lambda i, j: (j, 0, 0))],
                       jax.ShapeDtypeStruct((NDEV, t, f), BF16), pl.BlockSpec((None, tm, f), lambda i, j: (j, i, 0)),
                       (8, LANE))

        gp = ffn_in(gath["w_ffn_gate"], f"ffn_gate_{l}")
        up = ffn_in(gath["w_ffn_up"], f"ffn_up_{l}")
        act = _swiglu_fwd(gp, up, f"swiglu_fwd_{l}")
        x_next = _mm(f"ffn_down_{l}", NN, (t // tm, d // tno, NDEV), 1, [act, gath["w_ffn_down"], x_mid],
                     [pl.BlockSpec((None, tm, f), lambda i, n, k: (k, i, 0)),
                      pl.BlockSpec((None, f, tno), lambda i, n, k: (k, 0, n)),
                      pl.BlockSpec((tm, tno), lambda i, n, k: (i, n))],
                     jax.ShapeDtypeStruct((t, d), F32), pl.BlockSpec((tm, tno), lambda i, n, k: (i, n)), (tm, tno),
                     epi=lambda acc, res: acc + res)
        saved.append(dict(x=x, h1=h1, proj=proj, ya=ya, hs=hs, yb=yb, s5o=s5o, yc=yc, br_a=br_a, br_b=br_b,
                          br_c0=br_c0, gl=gl, merged=merged, x_mid=x_mid, h2=h2, gp=gp, up=up, act=act, c=c, gath=gath))
        x = x_next

    loss, dx, dxb, dgf = _loss_head(x, row1(wts["norm_final_g"]), target, "loss_head")

    sg = {k: [None] * nl for k in _SMALL + _SMALL_SHARDED if k != "norm_final_g"}
    dbias_tabs = [None] * nl
    dlr_l, dli_l, dbbr_l, dbbi_l = [None] * nl, [None] * nl, [None] * nl, [None] * nl
    tk = _tile(d, 1024)
    for l in range(nl - 1, -1, -1):
        s = saved[l]
        c = s["c"]
        gath = s["gath"]
        gbuf = {}
        dact = _mm(f"d_act_{l}", NT, (t // tm, NDEV), 0, [dxb, gath["w_ffn_down"]],
                   [pl.BlockSpec((tm, d), lambda i, j: (i, 0)),
                    pl.BlockSpec((None, f, d), lambda i, j: (j, 0, 0))],
                   jax.ShapeDtypeStruct((NDEV, t, f), BF16), pl.BlockSpec((None, tm, f), lambda i, j: (j, i, 0)), (8, LANE))
        tno = _tile(d, 1024)
        gbuf["w_ffn_down"] = _mm(
            f"dw_ffn_down_{l}", TN, (NDEV, d // tno, t // tm), 1, [s["act"], dxb],
            [pl.BlockSpec((None, tm, f), lambda j, n, r: (j, r, 0)), pl.BlockSpec((tm, tno), lambda j, n, r: (r, n))],
            jax.ShapeDtypeStruct((NDEV, f, d), BF16),
            pl.BlockSpec((None, f, tno), lambda j, n, r: (j, 0, n)), (f, tno))
        dgp, dup = _swiglu_bwd(s["gp"], s["up"], dact, f"swiglu_bwd_{l}")

        def ffn_dh(dz, wg, nm):
            return _mm(nm, NT, (t // tm, d // tk, NDEV), 1, [dz, wg],
                       [pl.BlockSpec((None, tm, f), lambda i, k, j: (j, i, 0)),
                        pl.BlockSpec((None, tk, f), lambda i, k, j: (j, k, 0))],
                       jax.ShapeDtypeStruct((t, d), F32), pl.BlockSpec((tm, tk), lambda i, k, j: (i, k)), (tm, tk))

        dh2a = ffn_dh(dgp, gath["w_ffn_gate"], f"d_h2_gate_{l}")
        dh2b = ffn_dh(dup, gath["w_ffn_up"], f"d_h2_up_{l}")

        def ffn_dw(dz, key, nm):
            return _mm(nm, TN, (NDEV, d // tk, t // tm), 1, [s["h2"], dz],
                       [pl.BlockSpec((tm, tk), lambda j, k, r: (r, k)), pl.BlockSpec((None, tm, f), lambda j, k, r: (j, r, 0))],
                       jax.ShapeDtypeStruct((NDEV, d, f), BF16),
                       pl.BlockSpec((None, tk, f), lambda j, k, r: (j, k, 0)), (tk, f))

        gbuf["w_ffn_gate"] = ffn_dw(dgp, "w_ffn_gate", f"dw_ffn_gate_{l}")
        gbuf["w_ffn_up"] = ffn_dw(dup, "w_ffn_up", f"dw_ffn_up_{l}")
        dx_mid, dxmb, dg2 = _rmsnorm_bwd(s["x_mid"], row1(wts["norm_ffn_g"][l]), [dh2a, dh2b], dx, f"norm_ffn_bwd_{l}")
        sg["norm_ffn_g"][l] = dg2.reshape(-1)
        rb_ = d // NDEV
        dmerged = _mm(f"d_merged_{l}", NT, (t // tm, NDEV), 0, [dxmb, gath["w_out"]],
                      [pl.BlockSpec((tm, d), lambda i, j: (i, 0)),
                       pl.BlockSpec((None, rb_, d), lambda i, j: (j, 0, 0))],
                      jax.ShapeDtypeStruct((t, d), BF16), pl.BlockSpec((tm, rb_), lambda i, j: (i, j)), (8, LANE))
        gbuf["w_out"] = _mm(
            f"dw_out_{l}", TN, (NDEV, d // tno, t // tm), 1, [s["merged"], dxmb],
            [pl.BlockSpec((tm, rb_), lambda j, n, r: (r, j)), pl.BlockSpec((tm, tno), lambda j, n, r: (r, n))],
            jax.ShapeDtypeStruct((NDEV, rb_, d), BF16),
            pl.BlockSpec((None, rb_, tno), lambda j, n, r: (j, 0, n)), (rb_, tno))
        dbr_a, dbr_b, dbr_c0, dgl, dgates, dgb = _merge_bwd(s["proj"], gbias[l], s["br_a"], s["br_b"], s["br_c0"], s["gl"],
                                                           dmerged, d, f"merge_bwd_{l}")
        sg["gate_bias"][l] = dgb.reshape(-1)
        cb = d // NDEV

        def branch_dy(dbr, wg, idx, nm, dt):
            nlead = len(idx(0, 0)) - 2
            return _mm(nm, NT, (t // tm, NDEV), 1, [dbr, wg],
                       [pl.BlockSpec((tm, cb), lambda i, j: (i, j)),
                        pl.BlockSpec((None,) * nlead + (w, cb), lambda i, j: idx(j, 0))],
                       jax.ShapeDtypeStruct((t, w), dt), pl.BlockSpec((tm, w), lambda i, j: (i, 0)), (tm, w))

        def branch_dw(y, dbr, buf, shape, idx, nm):
            nlead = len(idx(0)) - 2
            return _mm(nm, TN, (NDEV, t // tm), 1, [y, dbr] + ([] if buf is None else [buf]),
                       [pl.BlockSpec((tm, w), lambda j, r: (r, 0)), pl.BlockSpec((tm, cb), lambda j, r: (r, j))],
                       jax.ShapeDtypeStruct(shape, BF16),
                       pl.BlockSpec((None,) * nlead + (w, cb), lambda j, r: idx(j)), (w, cb),
                       alias=None if buf is None else True)

        wb_i = lambda b: (lambda j, z=0: (j, b, 0, 0))
        glu_i = lambda j, z=0: (j, 0, 0)
        dya = branch_dy(dbr_a, gath["w_branch"], wb_i(0), f"d_ya_{l}", BF16)
        dyb = branch_dy(dbr_b, gath["w_branch"], wb_i(1), f"d_yb_{l}", BF16)
        dyc1 = branch_dy(dbr_c0, gath["w_branch"], wb_i(2), f"d_yc_{l}", F32)
        dyc2 = branch_dy(dgl, gath["ssm_w_glu"], glu_i, f"d_yc_glu_{l}", F32)
        wb_shape = (NDEV, N_BRANCH, w, cb)
        gwb = branch_dw(s["ya"], dbr_a, lax.empty(wb_shape, BF16), wb_shape, wb_i(0), f"dw_branch_a_{l}")
        gwb = branch_dw(s["yb"], dbr_b, gwb, wb_shape, wb_i(1), f"dw_branch_b_{l}")
        gbuf["w_branch"] = branch_dw(s["yc"], dbr_c0, gwb, wb_shape, wb_i(2), f"dw_branch_c_{l}")
        gbuf["ssm_w_glu"] = branch_dw(s["yc"], dgl, None, (NDEV, w, cb), glu_i, f"dw_glu_{l}")
        ds5o = _gelu_bwd_sum(s["s5o"], dyc1, dyc2, f"gelu_bwd_{l}")
        du, dbbt, dcm, dlam, dd = _s5_bwd(s["proj"], ds5o, c["bbt"], c["bbm"], c["cmt"], c["lam"], row1(wts["ssm_d"][l]),
                                          w, f"s5_bwd_{l}")
        sp = gpt * SSM_P
        dbbr_l[l] = _block_diag_extract(dbbt[:, :, :sp], gpt)
        dbbi_l[l] = _block_diag_extract(dbbt[:, :, sp:], gpt)
        dcmt = jnp.swapaxes(dcm, 1, 2)
        sg["ssm_c_re"][l] = _block_diag_extract(dcmt[:, :, :sp], gpt).reshape(-1)
        sg["ssm_c_im"][l] = (-_block_diag_extract(dcmt[:, :, sp:], gpt)).reshape(-1)
        nrow = sp // LANE
        dlr_l[l] = dlam[:, :nrow].reshape(n_g, SSM_P)
        dli_l[l] = dlam[:, nrow:].reshape(n_g, SSM_P)
        sg["ssm_d"][l] = dd.reshape(-1)
        dq, dk, dv, dbt = _attn_bwd(s["proj"], bias_tab[l], dyb, w, f"attn_bwd_{l}")
        dbias_tabs[l] = dbt
        dlx, dlg, dcw, dcb, dwa, dba, dwx, dbx, dlm = _lru_bwd(
            s["proj"], s["hs"], dya, convw[l], row1(wts["lru_conv_b"][l]), c["wa_t"], row1(wts["lru_ba"][l]), c["wx_t"],
            row1(wts["lru_bx"][l]), row1(wts["lru_lambda"][l]), c["wat_t"], c["wxt_t"], w, f"lru_bwd_{l}")
        sg["lru_conv_w"][l] = dcw.reshape(-1)
        sg["lru_conv_b"][l] = dcb.reshape(-1)
        sg["lru_wa"][l] = _block_diag_extract(dwa, bpt).reshape(-1)
        sg["lru_wx"][l] = _block_diag_extract(dwx, bpt).reshape(-1)
        sg["lru_ba"][l] = dba.reshape(-1)
        sg["lru_bx"][l] = dbx.reshape(-1)
        sg["lru_lambda"][l] = dlm.reshape(-1)
        dproj = jnp.concatenate([dlx, dlg, dq, dk, dv, du, dgates], axis=1)
        tn = _tile(ncol, 768)
        nn_ = ncol // tn
        dh1 = _mm(f"d_h1_{l}", NT, (t // tm, d // tk, NDEV, nn_), 2, [dproj, gath["w_in"]],
                  [pl.BlockSpec((tm, tn), lambda i, k, j, n: (i, j * nn_ + n)),
                   pl.BlockSpec((None, tk, tn), lambda i, k, j, n: (j, k, n))],
                  jax.ShapeDtypeStruct((t, d), F32), pl.BlockSpec((tm, tk), lambda i, k, j, n: (i, k)), (tm, tk))
        gbuf["w_in"] = _mm(
            f"dw_in_{l}", TN, (NDEV, d // tk, nn_, t // tm), 1, [s["h1"], dproj],
            [pl.BlockSpec((tm, tk), lambda j, k, n, r: (r, k)), pl.BlockSpec((tm, tn), lambda j, k, n, r: (r, j * nn_ + n))],
            jax.ShapeDtypeStruct((NDEV, d, ncol), BF16),
            pl.BlockSpec((None, tk, tn), lambda j, k, n, r: (j, k, n)), (tk, tn))
        tok = on_grads(l, gbuf)
        dx, dxb, dg1 = _rmsnorm_bwd(s["x"], tied(row1(wts["norm_mix_g"][l]), tok), [dh1], dx_mid, f"norm_mix_bwd_{l}")
        sg["norm_mix_g"][l] = dg1.reshape(-1)

    da_re, da_im, dls, dbtr, dbti = _s5_prep_bwd(wts["ssm_a_re"], wts["ssm_a_im"], ls3, btr, bti, jnp.stack(dlr_l),
                                                 jnp.stack(dli_l), jnp.stack(dbbr_l), jnp.stack(dbbi_l), "s5_prep_bwd")
    drel = _bias_reduce(jnp.stack(dbias_tabs), "bias_reduce")
    small = {k: jnp.stack(v) for k, v in sg.items() if v[0] is not None}
    small["ssm_a_re"] = da_re.reshape(nl, -1)
    small["ssm_a_im"] = da_im.reshape(nl, -1)
    small["ssm_log_step"] = dls.reshape(nl, -1)
    small["ssm_b_re"] = jnp.swapaxes(dbtr, 2, 3).reshape(nl, -1)
    small["ssm_b_im"] = jnp.swapaxes(dbti, 2, 3).reshape(nl, -1)
    small["attn_rel_bias"] = drel.reshape(nl, -1)
    small["norm_final_g"] = dgf.reshape(-1)
    return loss, dx, small


def kernel(x, norm_mix_g, w_in, gate_bias, lru_conv_w, lru_conv_b, lru_wa, lru_ba, lru_wx, lru_bx, lru_lambda, attn_rel_bias, ssm_a_re, ssm_a_im, ssm_b_re, ssm_b_im, ssm_c_re, ssm_c_im, ssm_d, ssm_log_step, ssm_w_glu, w_branch, w_out, norm_ffn_g, w_ffn_gate, w_ffn_up, w_ffn_down, norm_final_g, loss_target, m_norm_mix_g, m_w_in, m_gate_bias, m_lru_conv_w, m_lru_conv_b, m_lru_wa, m_lru_ba, m_lru_wx, m_lru_bx, m_lru_lambda, m_attn_rel_bias, m_ssm_a_re, m_ssm_a_im, m_ssm_b_re, m_ssm_b_im, m_ssm_c_re, m_ssm_c_im, m_ssm_d, m_ssm_log_step, m_ssm_w_glu, m_w_branch, m_w_out, m_norm_ffn_g, m_w_ffn_gate, m_w_ffn_up, m_w_ffn_down, m_norm_final_g, v_norm_mix_g, v_w_in, v_gate_bias, v_lru_conv_w, v_lru_conv_b, v_lru_wa, v_lru_ba, v_lru_wx, v_lru_bx, v_lru_lambda, v_attn_rel_bias, v_ssm_a_re, v_ssm_a_im, v_ssm_b_re, v_ssm_b_im, v_ssm_c_re, v_ssm_c_im, v_ssm_d, v_ssm_log_step, v_ssm_w_glu, v_w_branch, v_w_out, v_norm_ffn_g, v_w_ffn_gate, v_w_ffn_up, v_w_ffn_down, v_norm_final_g):
    args = locals()
    wts = {k: args[k] for k in _WEIGHTS}
    mom = {k: args["m_" + k] for k in _WEIGHTS}
    vel = {k: args["v_" + k] for k in _WEIGHTS}
    cx, cy, cc = lax.axis_index("x"), lax.axis_index("y"), lax.axis_index("c")
    me = 4 * cx + 2 * cy + cc
    nl = norm_mix_g.shape[0]

    first = _all_gather([wts[k][0].astype(BF16) for k in _BIG] + [wts[k] for k in _SMALL_SHARDED], "gather_weights_0")
    small_gath = dict(zip(_SMALL_SHARDED, first[len(_BIG):]))
    in_flight = {}
    me_arr = me.astype(jnp.int32).reshape(1)
    layer_shape = {k: wts[k].shape[1:] for k in _BIG}

    def get_gath(l, x_in):
        if l == 0:
            gath = dict(zip(_BIG, first[:len(_BIG)]))
        else:
            lands = _split_wait("all", in_flight.pop(("w", l)), x_in, f"gather_wait_{l}")
            gath = {k: a.reshape((NDEV,) + layer_shape[k]) for k, a in zip(_BIG, lands)}
        if l + 1 == nl:
            return gath, None
        after = first[0] if l == 0 else x_in
        cast = [_cast_place(wts[k].reshape(nl, -1, wts[k].shape[-1]), l + 1, me_arr, after, f"cast_{k}_{l + 1}")
                for k in _BIG]
        started = _split_start("all", [a for a, _ in cast], [b for _, b in cast], f"gather_start_{l + 1}")
        in_flight[("w", l + 1)] = started[:-1]
        return gath, started[-1]

    core = cc.astype(jnp.int32).reshape(1)
    chip = (2 * cx + cy).astype(jnp.int32).reshape(1)

    def on_grads(l, gbuf):
        own = [gbuf[k].reshape((NCHIP, 2) + gbuf[k].shape[1:]) for k in _BIG]
        sib = _sibling_exchange(own, f"grad_sibling_exchange_{l}")
        pair, lands = [], []
        for k, a, b in zip(_BIG, own, sib):
            c_ = a.shape[-1]
            p, q = _pair_sum(a.reshape(NCHIP, 2, -1, c_), b.reshape(NCHIP, -1, c_), core, chip, f"pair_sum_{k}_{l}")
            pair.append(p)
            lands.append(q)
        started = _split_start("chips", pair, lands, f"grad_start_{l}")
        in_flight[("g", l)] = started[:-1]
        return started[-1]

    loss, dx, small = _device_step(x[0], loss_target[0], wts, small_gath, get_gath, on_grads)

    res = {k: None for k in _BIG}
    for l in range(nl - 1, -1, -1):
        parts = _split_wait("chips", in_flight.pop(("g", l)), dx, f"grad_wait_{l}")
        for k, p in zip(_BIG, parts):
            c_ = wts[k].shape[-1]
            res[k] = _adamw_reduce(p.reshape(NCHIP, -1, c_), wts[k].reshape(-1, c_), mom[k].reshape(-1, c_),
                                   vel[k].reshape(-1, c_), res[k], l, f"adamw_{k}_{l}")
    out = {k: tuple(a.reshape(wts[k].shape) for a in res[k]) for k in _BIG}

    order = _SMALL + _SMALL_SHARDED
    flat = jnp.concatenate([small[k].reshape(-1) for k in order])
    n_flat = flat.shape[0]
    flat = jnp.pad(flat, (0, (-n_flat) % (FLAT_ROWS * LANE))).reshape(-1, LANE)
    (allflat,) = _all_gather([flat], "gather_small_grads")
    gsum = _sum_rows8(allflat, "sum_small_grads").reshape(-1)
    gs, off = {}, 0
    for k in order:
        n = small[k].size
        gs[k] = gsum[off:off + n]
        off += n
    gs_loc = {}
    for k in _SMALL:
        gs_loc[k] = gs[k].reshape(wts[k].shape)
    ncb = gate_bias.shape[-1]
    gs_loc["gate_bias"] = lax.dynamic_slice_in_dim(gs["gate_bias"].reshape(nl, N_BRANCH, -1), me * ncb, ncb, axis=2)
    ncw = lru_conv_w.shape[-1]
    gs_loc["lru_conv_w"] = lax.dynamic_slice_in_dim(gs["lru_conv_w"].reshape(nl, CONV_W, -1), me * ncw, ncw, axis=2)

    def pack(dct):
        v = jnp.concatenate([dct[k].reshape(-1) for k in order])
        return jnp.pad(v, (0, (-v.shape[0]) % (FLAT_ROWS * LANE))).reshape(-1, LANE)

    dl_f, nm_f, nv_f = _adamw_flat(pack(wts), pack(gs_loc), pack(mom), pack(vel), "adamw_small")
    off = 0
    for k in order:
        n = wts[k].size
        sl = lambda a: a.reshape(-1)[off:off + n].reshape(wts[k].shape)
        out[k] = (gs_loc[k], sl(dl_f), sl(nm_f), sl(nv_f))
        off += n

    loss_total = lax.psum(loss[0, 0], ("x", "y", "c"))
    return (loss_total, dx[None], *[out[k][0] for k in _WEIGHTS], *[out[k][1] for k in _WEIGHTS],
            *[out[k][2] for k in _WEIGHTS], *[out[k][3] for k in _WEIGHTS])
```

```python
import functools
import math

import numpy as np
import jax
import jax.numpy as jnp
from jax import lax
from jax.experimental import pallas as pl
from jax.experimental.pallas import tpu as pltpu

F32 = jnp.float32
BF16 = jnp.bfloat16
LANE = 128
NSEG = 8
NDEV = 8
NCHIP = 4
MESH = pl.DeviceIdType.MESH
VMEM_LIMIT = 56 * 1024 * 1024
FLAT_ROWS = 512

NORM_EPS = 1e-6
CHUNK = 64
ATT_LEFT = 8
ATT_BAND = (ATT_LEFT + 1) * CHUNK
MAX_REL = 128
N_REL = 2 * MAX_REL + 1
N_REL_PAD = 384
ATT_HEADS = 8
MASK_VALUE = -1e30
LRU_C = 8.0
LRU_BLOCKS = 16
SSM_GROUP = 16
SSM_P = 64
CONV_W = 4
N_BRANCH = 3

ADAM_LR = 0.001
ADAM_B1 = 0.9
ADAM_B2 = 0.999
ADAM_EPS = 1e-08
ADAM_WD = 0.01
ADAM_STEP = 10

NN = (((1,), (0,)), ((), ()))
NT = (((1,), (1,)), ((), ()))
TN = (((0,), (0,)), ((), ()))


def _cp(sem=None, vmem=VMEM_LIMIT, **kw):
    return pltpu.CompilerParams(dimension_semantics=sem, vmem_limit_bytes=vmem, **kw)


def _dot(a, b, dn=NN):
    return lax.dot_general(a, b, dn, preferred_element_type=F32)


def _gelu(x):
    c = math.sqrt(2.0 / math.pi)
    return 0.5 * x * (1.0 + jnp.tanh(c * (x + 0.044715 * x * x * x)))


def _gelu_grad(x):
    c = math.sqrt(2.0 / math.pi)
    t = jnp.tanh(c * (x + 0.044715 * x * x * x))
    return 0.5 * (1.0 + t) + 0.5 * x * (1.0 - t * t) * c * (1.0 + 3.0 * 0.044715 * x * x)


def _sigmoid(x):
    return 1.0 / (1.0 + jnp.exp(-x))


def _one_minus_exp(z):
    series = -(z * (1.0 + z * (0.5 + z * (1.0 / 6.0 + z * (1.0 / 24.0)))))
    return jnp.where(z > -0.02, series, 1.0 - jnp.exp(z))


def _softplus_neg(lam):
    e = jnp.exp(-jnp.abs(lam))
    series = e * (1.0 - e * (0.5 - e * (1.0 / 3.0 - e * 0.25)))
    log1p_e = jnp.where(e < 0.02, series, jnp.log(1.0 + e))
    return jnp.maximum(-lam, 0.0) + log1p_e


def _mm(name, dn, grid, n_red, ins, in_specs, out_shape, out_spec, acc_shape, epi=None, alias=None):
    n_extra = len(ins) - 2 - (1 if alias is not None else 0)
    red_axes = tuple(range(len(grid) - n_red, len(grid)))
    red_sizes = tuple(grid[r] for r in red_axes)

    def body(*refs):
        a_ref, b_ref = refs[0], refs[1]
        extra = refs[2:2 + n_extra]
        o_ref, acc = refs[-2], refs[-1]
        if n_red:
            first = functools.reduce(jnp.logical_and, [pl.program_id(r) == 0 for r in red_axes])
            last = functools.reduce(jnp.logical_and,
                                    [pl.program_id(r) == n - 1 for r, n in zip(red_axes, red_sizes)])

            @pl.when(first)
            def _():
                acc[...] = jnp.zeros_like(acc)

            acc[...] += _dot(a_ref[...], b_ref[...], dn)

            @pl.when(last)
            def _():
                r = acc[...]
                if epi is not None:
                    r = epi(r, *[e[...] for e in extra])
                o_ref[...] = r.astype(o_ref.dtype)
        else:
            r = _dot(a_ref[...], b_ref[...], dn)
            if epi is not None:
                r = epi(r, *[e[...] for e in extra])
            o_ref[...] = r.astype(o_ref.dtype)

    specs = list(in_specs)
    kw = {}
    if alias is not None:
        specs.append(pl.BlockSpec(memory_space=pl.ANY))
        kw["input_output_aliases"] = {len(ins) - 1: 0}
    sem = ("parallel",) * (len(grid) - n_red) + ("arbitrary",) * n_red
    return pl.pallas_call(
        body, name=name, grid=grid, in_specs=specs, out_specs=out_spec, out_shape=out_shape,
        scratch_shapes=[pltpu.VMEM(acc_shape, F32)], compiler_params=_cp(sem), **kw)(*ins)


def _tile(n, pref):
    if n <= pref:
        return n
    t = pref
    while t >= LANE:
        if n % t == 0 and t % LANE == 0:
            return t
        t -= LANE
    return n


def _rmsnorm_fwd(x, g, name):
    t, d = x.shape
    tm = min(512, t)

    def body(x_ref, g_ref, h_ref):
        xv = x_ref[...]
        r = lax.rsqrt(jnp.mean(xv * xv, axis=-1, keepdims=True) + NORM_EPS)
        h_ref[...] = (xv * r * g_ref[...]).astype(h_ref.dtype)

    return pl.pallas_call(
        body, name=name, grid=(t // tm,),
        in_specs=[pl.BlockSpec((tm, d), lambda i: (i, 0)), pl.BlockSpec((1, d), lambda i: (0, 0))],
        out_specs=pl.BlockSpec((tm, d), lambda i: (i, 0)),
        out_shape=jax.ShapeDtypeStruct((t, d), BF16), compiler_params=_cp(("parallel",)))(x, g)


def _rmsnorm_bwd(x, g, dhs, dres, name):
    t, d = x.shape
    tm = min(256, t)
    n_dh = len(dhs)

    def body(*refs):
        x_ref, g_ref = refs[0], refs[1]
        dh_refs = refs[2:2 + n_dh]
        dres_ref = refs[2 + n_dh]
        dx_ref, dxb_ref, dg_ref = refs[3 + n_dh:]
        xv = x_ref[...]
        r = lax.rsqrt(jnp.mean(xv * xv, axis=-1, keepdims=True) + NORM_EPS)
        xhat = xv * r
        dh = dh_refs[0][...].astype(F32)
        for e in dh_refs[1:]:
            dh = dh + e[...].astype(F32)
        dxh = dh * g_ref[...]
        dx = r * (dxh - xhat * jnp.mean(dxh * xhat, axis=-1, keepdims=True)) + dres_ref[...]
        dx_ref[...] = dx
        dxb_ref[...] = dx.astype(BF16)

        @pl.when(pl.program_id(0) == 0)
        def _():
            dg_ref[...] = jnp.zeros_like(dg_ref)

        dg_ref[...] += jnp.sum(dh * xhat, axis=0, keepdims=True)

    row = pl.BlockSpec((tm, d), lambda i: (i, 0))
    par = pl.BlockSpec((1, d), lambda i: (0, 0))
    return pl.pallas_call(
        body, name=name, grid=(t // tm,),
        in_specs=[row, par] + [row] * n_dh + [row],
        out_specs=[row, row, par],
        out_shape=[jax.ShapeDtypeStruct((t, d), F32), jax.ShapeDtypeStruct((t, d), BF16),
                   jax.ShapeDtypeStruct((1, d), F32)],
        compiler_params=_cp(("arbitrary",)))(x, g, *dhs, dres)


def _loss_head(x, g, target, name):
    t, d = x.shape
    tm = min(256, t)

    def body(x_ref, g_ref, t_ref, loss_ref, dx_ref, dxb_ref, dg_ref):
        xv = x_ref[...]
        r = lax.rsqrt(jnp.mean(xv * xv, axis=-1, keepdims=True) + NORM_EPS)
        xhat = xv * r
        y = xhat * g_ref[...]
        err = y - t_ref[...]
        dy = err * (1.0 / d)
        dxh = dy * g_ref[...]
        dx = r * (dxh - xhat * jnp.mean(dxh * xhat, axis=-1, keepdims=True))
        dx_ref[...] = dx
        dxb_ref[...] = dx.astype(BF16)

        @pl.when(pl.program_id(0) == 0)
        def _():
            dg_ref[...] = jnp.zeros_like(dg_ref)
            loss_ref[...] = jnp.zeros_like(loss_ref)

        dg_ref[...] += jnp.sum(dy * xhat, axis=0, keepdims=True)
        per_tok = jnp.sum(err * err, axis=-1, keepdims=True) * (0.5 / d)
        loss_ref[...] += jnp.sum(per_tok, axis=0, keepdims=True)

    row = pl.BlockSpec((tm, d), lambda i: (i, 0))
    par = pl.BlockSpec((1, d), lambda i: (0, 0))
    return pl.pallas_call(
        body, name=name, grid=(t // tm,),
        in_specs=[row, par, row],
        out_specs=[pl.BlockSpec((1, 1), lambda i: (0, 0)), row, row, par],
        out_shape=[jax.ShapeDtypeStruct((1, 1), F32), jax.ShapeDtypeStruct((t, d), F32),
                   jax.ShapeDtypeStruct((t, d), BF16), jax.ShapeDtypeStruct((1, d), F32)],
        compiler_params=_cp(("arbitrary",)))(x, g, target)


def _merge_fwd(proj, gbias, br_a, br_b, br_c0, gl, d, name):
    t = proj.shape[0]
    tm = min(128, t)
    off = proj.shape[1] // d - N_BRANCH

    def body(g0, g1, g2, gb, a_ref, b_ref, c_ref, gl_ref, o_ref):
        gbv = gb[...]
        s0 = _sigmoid(g0[...].astype(F32) + gbv[0:1])
        s1 = _sigmoid(g1[...].astype(F32) + gbv[1:2])
        s2 = _sigmoid(g2[...].astype(F32) + gbv[2:3])
        brc = c_ref[...].astype(F32) * _sigmoid(gl_ref[...].astype(F32))
        o_ref[...] = (s0 * a_ref[...].astype(F32) + s1 * b_ref[...].astype(F32) + s2 * brc).astype(BF16)

    row = pl.BlockSpec((tm, d), lambda i: (i, 0))
    gs = [pl.BlockSpec((tm, d), functools.partial(lambda i, b: (i, off + b), b=b)) for b in range(N_BRANCH)]
    return pl.pallas_call(
        body, name=name, grid=(t // tm,),
        in_specs=gs + [pl.BlockSpec((N_BRANCH, d), lambda i: (0, 0)), row, row, row, row],
        out_specs=row, out_shape=jax.ShapeDtypeStruct((t, d), BF16),
        compiler_params=_cp(("parallel",)))(proj, proj, proj, gbias, br_a, br_b, br_c0, gl)


def _merge_bwd(proj, gbias, br_a, br_b, br_c0, gl, dmerged, d, name):
    t = proj.shape[0]
    tm = min(128, t)
    off = proj.shape[1] // d - N_BRANCH

    def body(g0, g1, g2, gb, a_ref, b_ref, c_ref, gl_ref, dm_ref, da_ref, db_ref, dc_ref, dgl_ref, dg_ref, dgb_ref):
        gbv = gb[...]
        dm = dm_ref[...].astype(F32)
        s0 = _sigmoid(g0[...].astype(F32) + gbv[0:1])
        s1 = _sigmoid(g1[...].astype(F32) + gbv[1:2])
        s2 = _sigmoid(g2[...].astype(F32) + gbv[2:3])
        sg = _sigmoid(gl_ref[...].astype(F32))
        c0 = c_ref[...].astype(F32)
        brc = c0 * sg
        da_ref[...] = (dm * s0).astype(BF16)
        db_ref[...] = (dm * s1).astype(BF16)
        dbrc = dm * s2
        dc_ref[...] = (dbrc * sg).astype(BF16)
        dgl_ref[...] = (dbrc * c0 * sg * (1.0 - sg)).astype(BF16)
        dp0 = dm * a_ref[...].astype(F32) * s0 * (1.0 - s0)
        dp1 = dm * b_ref[...].astype(F32) * s1 * (1.0 - s1)
        dp2 = dm * brc * s2 * (1.0 - s2)
        dg_ref[:, 0:d] = dp0.astype(BF16)
        dg_ref[:, d:2 * d] = dp1.astype(BF16)
        dg_ref[:, 2 * d:3 * d] = dp2.astype(BF16)

        @pl.when(pl.program_id(0) == 0)
        def _():
            dgb_ref[...] = jnp.zeros_like(dgb_ref)

        dgb_ref[0:1, :] += jnp.sum(dp0, axis=0, keepdims=True)
        dgb_ref[1:2, :] += jnp.sum(dp1, axis=0, keepdims=True)
        dgb_ref[2:3, :] += jnp.sum(dp2, axis=0, keepdims=True)

    row = pl.BlockSpec((tm, d), lambda i: (i, 0))
    gs = [pl.BlockSpec((tm, d), functools.partial(lambda i, b: (i, off + b), b=b)) for b in range(N_BRANCH)]
    par = pl.BlockSpec((N_BRANCH, d), lambda i: (0, 0))
    bf = jax.ShapeDtypeStruct((t, d), BF16)
    return pl.pallas_call(
        body, name=name, grid=(t // tm,),
        in_specs=gs + [par, row, row, row, row, row],
        out_specs=[row, row, row, row, pl.BlockSpec((tm, N_BRANCH * d), lambda i: (i, 0)), par],
        out_shape=[bf, bf, bf, bf, jax.ShapeDtypeStruct((t, N_BRANCH * d), BF16),
                   jax.ShapeDtypeStruct((N_BRANCH, d), F32)],
        compiler_params=_cp(("arbitrary",)))(proj, proj, proj, gbias, br_a, br_b, br_c0, gl, dmerged)


def _ffn_in(h2, wg, wu, name):
    t, d = h2.shape
    f = wg.shape[-1]
    tm = _tile(t, 1024)

    def body(h_ref, g_ref, u_ref, gp_ref, up_ref, act_ref):
        h = h_ref[...]
        g = _dot(h, g_ref[...])
        u = _dot(h, u_ref[...])
        gp_ref[...] = g.astype(BF16)
        up_ref[...] = u.astype(BF16)
        act_ref[...] = (g * _sigmoid(g) * u).astype(BF16)

    wspec = pl.BlockSpec((None, d, f), lambda i, j: (j, 0, 0))
    ospec = pl.BlockSpec((None, tm, f), lambda i, j: (j, i, 0))
    o = jax.ShapeDtypeStruct((NDEV, t, f), BF16)
    return pl.pallas_call(
        body, name=name, grid=(t // tm, NDEV), in_specs=[pl.BlockSpec((tm, d), lambda i, j: (i, 0)), wspec, wspec],
        out_specs=[ospec, ospec, ospec], out_shape=[o, o, o],
        compiler_params=_cp(("parallel", "parallel")))(h2, wg, wu)


def _ffn_dact(dxb, wd, gp, up, name):
    t, d = dxb.shape
    f = wd.shape[1]
    tm = _tile(t, 1024)

    def body(dx_ref, w_ref, g_ref, u_ref, dg_ref, du_ref):
        da = _dot(dx_ref[...], w_ref[...], NT)
        g = g_ref[...].astype(F32)
        u = u_ref[...].astype(F32)
        sg = _sigmoid(g)
        silu = g * sg
        du_ref[...] = (da * silu).astype(BF16)
        dg_ref[...] = (da * u * (sg + silu * (1.0 - sg))).astype(BF16)

    ospec = pl.BlockSpec((None, tm, f), lambda i, j: (j, i, 0))
    o = jax.ShapeDtypeStruct((NDEV, t, f), BF16)
    return pl.pallas_call(
        body, name=name, grid=(t // tm, NDEV),
        in_specs=[pl.BlockSpec((tm, d), lambda i, j: (i, 0)), pl.BlockSpec((None, f, d), lambda i, j: (j, 0, 0)), ospec, ospec],
        out_specs=[ospec, ospec], out_shape=[o, o],
        compiler_params=_cp(("parallel", "parallel")))(dxb, wd, gp, up)


def _ffn_dh(dgp, dup, wg, wu, name):
    _, t, f = dgp.shape
    d = wg.shape[1]
    tm = _tile(t, 1024)
    tk = _tile(d, 1024)

    def body(g_ref, u_ref, wg_ref, wu_ref, o_ref, acc):
        j = pl.program_id(2)

        @pl.when(j == 0)
        def _():
            acc[...] = jnp.zeros_like(acc)

        acc[...] += _dot(g_ref[...], wg_ref[...], NT) + _dot(u_ref[...], wu_ref[...], NT)

        @pl.when(j == NDEV - 1)
        def _():
            o_ref[...] = acc[...]

    zspec = pl.BlockSpec((None, tm, f), lambda i, k, j: (j, i, 0))
    wspec = pl.BlockSpec((None, tk, f), lambda i, k, j: (j, k, 0))
    return pl.pallas_call(
        body, name=name, grid=(t // tm, d // tk, NDEV), in_specs=[zspec, zspec, wspec, wspec],
        out_specs=pl.BlockSpec((tm, tk), lambda i, k, j: (i, k)), out_shape=jax.ShapeDtypeStruct((t, d), F32),
        scratch_shapes=[pltpu.VMEM((tm, tk), F32)],
        compiler_params=_cp(("parallel", "parallel", "arbitrary")))(dgp, dup, wg, wu)


def _gelu_bwd_sum(s5o, dy1, dy2, name):
    t, w = s5o.shape
    tm = min(512, t)

    def body(s_ref, a_ref, b_ref, o_ref):
        o_ref[...] = ((a_ref[...] + b_ref[...]) * _gelu_grad(s_ref[...])).astype(BF16)

    row = pl.BlockSpec((tm, w), lambda i: (i, 0))
    return pl.pallas_call(
        body, name=name, grid=(t // tm,), in_specs=[row, row, row], out_specs=row,
        out_shape=jax.ShapeDtypeStruct((t, w), BF16), compiler_params=_cp(("parallel",)))(s5o, dy1, dy2)


def _lru_gates(xc, wa, ba, wx, bx, sp):
    xcb = xc.astype(BF16)
    r = _sigmoid(_dot(xcb, wa) + ba)
    i = _sigmoid(_dot(xcb, wx) + bx)
    la = -LRU_C * r * sp
    return xcb, r, i, la


def _lru_fwd(proj, cw, cb, wa_t, ba, wx_t, bx, lam, w, name):
    t = proj.shape[0]
    nt = w // LANE
    seg = t // NSEG
    rb = min(512, seg)

    def body(lx_ref, lg_ref, cw_ref, cb_ref, wa_ref, ba_ref, wx_ref, bx_ref, lam_ref, ya_ref, hs_ref, xp, a_s, b_s):
        xp[pl.ds(0, 8), :] = jnp.zeros((8, LANE), F32)
        for r0 in range(0, t, rb):
            xp[pl.ds(8 + r0, rb), :] = lx_ref[pl.ds(r0, rb), :].astype(F32)
        sp = _softplus_neg(lam_ref[...])
        cwv = cw_ref[...]
        for r0 in range(0, t, rb):
            xc = cb_ref[...] + sum(cwv[k:k + 1] * xp[pl.ds(8 + r0 - (CONV_W - 1) + k, rb), :] for k in range(CONV_W))
            _, _, i, la = _lru_gates(xc, wa_ref[...], ba_ref[...], wx_ref[...], bx_ref[...], sp)
            a_s[pl.ds(r0, rb), :] = jnp.exp(la)
            b_s[pl.ds(r0, rb), :] = jnp.sqrt(_one_minus_exp(2.0 * la)) * (i * xc)

        def load(k):
            rows = pl.ds(k, NSEG, stride=seg)
            return a_s[rows, :], b_s[rows, :]

        def step(k, c):
            h, p, a, b = c
            na, nb = load(jnp.minimum(k + 1, seg - 1))
            rows = pl.ds(k, NSEG, stride=seg)
            h = a * h + b
            p = a * p
            b_s[rows, :] = h
            a_s[rows, :] = p
            return h, p, na, nb

        lax.fori_loop(0, seg, step, (jnp.zeros((NSEG, LANE), F32), jnp.ones((NSEG, LANE), F32)) + load(0), unroll=4)
        carry = jnp.zeros((1, LANE), F32)
        for s in range(NSEG):
            for r0 in range(s * seg, (s + 1) * seg, rb):
                rows = pl.ds(r0, rb)
                h = b_s[rows, :] + a_s[rows, :] * carry
                hs_ref[rows, :] = h
                ya_ref[rows, :] = (h * _gelu(lg_ref[rows, :].astype(F32))).astype(BF16)
            end = pl.ds((s + 1) * seg - 1, 1)
            carry = b_s[end, :] + a_s[end, :] * carry

    col = lambda c0: pl.BlockSpec((t, LANE), functools.partial(lambda i, c0: (0, c0 + i), c0=c0))
    par = lambda k: pl.BlockSpec((k, LANE), lambda i: (0, i))
    mat = pl.BlockSpec((None, LANE, LANE), lambda i: (i, 0, 0))
    return pl.pallas_call(
        body, name=name, grid=(nt,),
        in_specs=[col(0), col(nt), par(CONV_W), par(1), mat, par(1), mat, par(1), par(1)],
        out_specs=[col(0), col(0)],
        out_shape=[jax.ShapeDtypeStruct((t, w), BF16), jax.ShapeDtypeStruct((t, w), F32)],
        scratch_shapes=[pltpu.VMEM((t + 8, LANE), F32), pltpu.VMEM((t, LANE), F32), pltpu.VMEM((t, LANE), F32)],
        compiler_params=_cp(("parallel",)))(proj, proj, cw, cb, wa_t, ba, wx_t, bx, lam)


def _lru_bwd(proj, hs, dya, cw, cb, wa_t, ba, wx_t, bx, lam, wat_t, wxt_t, w, name):
    t = proj.shape[0]
    nt = w // LANE
    seg = t // NSEG
    rb = min(512, seg)

    def body(lx_ref, lg_ref, hs_ref, dy_ref, cw_ref, cb_ref, wa_ref, ba_ref, wx_ref, bx_ref, lam_ref, wat_ref, wxt_ref,
             dlx_ref, dlg_ref, dcw_ref, dcb_ref, dwa_ref, dba_ref, dwx_ref, dbx_ref, dlam_ref,
             xp, hp, a_s, g_s, q_s, dxc_s):
        z8 = jnp.zeros((8, LANE), F32)
        xp[pl.ds(0, 8), :] = z8
        hp[pl.ds(0, 8), :] = z8
        a_s[pl.ds(t, 8), :] = z8
        dxc_s[pl.ds(t, 8), :] = z8
        lamv = lam_ref[...]
        sp = _softplus_neg(lamv)
        cwv = cw_ref[...]

        def conv(r0):
            return cb_ref[...] + sum(cwv[k:k + 1] * xp[pl.ds(8 + r0 - (CONV_W - 1) + k, rb), :] for k in range(CONV_W))

        for r0 in range(0, t, rb):
            rows = pl.ds(r0, rb)
            xp[pl.ds(8 + r0, rb), :] = lx_ref[rows, :].astype(F32)
            hp[pl.ds(8 + r0, rb), :] = hs_ref[rows, :]
        for r0 in range(0, t, rb):
            rows = pl.ds(r0, rb)
            _, _, _, la = _lru_gates(conv(r0), wa_ref[...], ba_ref[...], wx_ref[...], bx_ref[...], sp)
            a_s[rows, :] = jnp.exp(la)
            g_s[rows, :] = dy_ref[rows, :].astype(F32) * _gelu(lg_ref[rows, :].astype(F32))

        def load(k):
            return a_s[pl.ds(k + 1, NSEG, stride=seg), :], g_s[pl.ds(k, NSEG, stride=seg), :]

        def step(kk, c):
            g, q, an, dh = c
            k = seg - 1 - kk
            nan, ndh = load(jnp.maximum(k - 1, 0))
            rows = pl.ds(k, NSEG, stride=seg)
            g = dh + an * g
            q = an * q
            g_s[rows, :] = g
            q_s[rows, :] = q
            return g, q, nan, ndh

        lax.fori_loop(0, seg, step, (jnp.zeros((NSEG, LANE), F32), jnp.ones((NSEG, LANE), F32)) + load(seg - 1), unroll=4)
        nxt = [None] * NSEG
        carry = jnp.zeros((1, LANE), F32)
        for s in range(NSEG - 1, -1, -1):
            nxt[s] = carry
            start = pl.ds(s * seg, 1)
            carry = g_s[start, :] + q_s[start, :] * carry

        zrow = jnp.zeros((1, LANE), F32)
        dsp = zrow
        dba = zrow
        dbx = zrow
        dcb = zrow
        dwa = jnp.zeros((LANE, LANE), F32)
        dwx = jnp.zeros((LANE, LANE), F32)
        for s in range(NSEG):
            for r0 in range(s * seg, (s + 1) * seg, rb):
                rows = pl.ds(r0, rb)
                g = g_s[rows, :] + q_s[rows, :] * nxt[s]
                xc = conv(r0)
                xcb, r, i, la = _lru_gates(xc, wa_ref[...], ba_ref[...], wx_ref[...], bx_ref[...], sp)
                a = jnp.exp(la)
                om = _one_minus_exp(2.0 * la)
                mult = jnp.sqrt(om)
                hprev = hp[pl.ds(8 + r0 - 1, rb), :]
                da = g * hprev
                dmult = g * i * xc
                di = g * mult * xc
                dxc = g * mult * i
                dla = da * a - dmult * (1.0 - om) / mult
                dr = dla * (-LRU_C) * sp
                dsp = dsp + jnp.sum(dla * (-LRU_C) * r, axis=0, keepdims=True)
                dpr = dr * r * (1.0 - r)
                dpi = di * i * (1.0 - i)
                dba = dba + jnp.sum(dpr, axis=0, keepdims=True)
                dbx = dbx + jnp.sum(dpi, axis=0, keepdims=True)
                dprb = dpr.astype(BF16)
                dpib = dpi.astype(BF16)
                dxc = dxc + _dot(dprb, wat_ref[...]) + _dot(dpib, wxt_ref[...])
                dwa = dwa + _dot(xcb, dprb, TN)
                dwx = dwx + _dot(xcb, dpib, TN)
                dxc_s[rows, :] = dxc
                dcb = dcb + jnp.sum(dxc, axis=0, keepdims=True)
                lg = lg_ref[rows, :].astype(F32)
                dlg_ref[rows, :] = (dy_ref[rows, :].astype(F32) * hs_ref[rows, :] * _gelu_grad(lg)).astype(BF16)
        dcw = [zrow] * CONV_W
        for r0 in range(0, t, rb):
            rows = pl.ds(r0, rb)
            dlx = sum(cwv[k:k + 1] * dxc_s[pl.ds(r0 + (CONV_W - 1) - k, rb), :] for k in range(CONV_W))
            dlx_ref[rows, :] = dlx.astype(BF16)
            dxc = dxc_s[rows, :]
            for k in range(CONV_W):
                dcw[k] = dcw[k] + jnp.sum(dxc * xp[pl.ds(8 + r0 - (CONV_W - 1) + k, rb), :], axis=0, keepdims=True)
        dcw_ref[...] = jnp.concatenate(dcw, axis=0)
        dcb_ref[...] = dcb
        dwa_ref[...] = dwa
        dwx_ref[...] = dwx
        dba_ref[...] = dba
        dbx_ref[...] = dbx
        dlam_ref[...] = dsp * (-_sigmoid(-lamv))

    col = lambda c0: pl.BlockSpec((t, LANE), functools.partial(lambda i, c0: (0, c0 + i), c0=c0))
    par = lambda k: pl.BlockSpec((k, LANE), lambda i: (0, i))
    mat = pl.BlockSpec((None, LANE, LANE), lambda i: (i, 0, 0))
    vec = jax.ShapeDtypeStruct((1, w), F32)
    big = lambda: pltpu.VMEM((t + 8, LANE), F32)
    return pl.pallas_call(
        body, name=name, grid=(nt,),
        in_specs=[col(0), col(nt), col(0), col(0), par(CONV_W), par(1), mat, par(1), mat, par(1), par(1), mat, mat],
        out_specs=[col(0), col(0), par(CONV_W), par(1), mat, par(1), mat, par(1), par(1)],
        out_shape=[jax.ShapeDtypeStruct((t, w), BF16), jax.ShapeDtypeStruct((t, w), BF16),
                   jax.ShapeDtypeStruct((CONV_W, w), F32), vec, jax.ShapeDtypeStruct((nt, LANE, LANE), F32), vec,
                   jax.ShapeDtypeStruct((nt, LANE, LANE), F32), vec, vec],
        scratch_shapes=[big(), big(), big(), pltpu.VMEM((t, LANE), F32), pltpu.VMEM((t, LANE), F32), big()],
        compiler_params=_cp(("parallel",)))(proj, proj, hs, dya, cw, cb, wa_t, ba, wx_t, bx, lam, wat_t, wxt_t)


def _attn_scores(q_ref, kp, bias_ref, c, hd):
    r0 = pl.multiple_of(c * CHUNK, CHUNK)
    qc = q_ref[pl.ds(r0, CHUNK), :]
    kb = kp[pl.ds(r0, ATT_BAND), :]
    s = _dot(qc, kb, NT) * (hd ** -0.5) + bias_ref[...]
    kpos = lax.broadcasted_iota(jnp.int32, (CHUNK, ATT_BAND), 1)
    s = jnp.where(kpos + (c - ATT_LEFT) * CHUNK >= 0, s, MASK_VALUE)
    m = jnp.max(s, axis=-1, keepdims=True)
    e = jnp.exp(s - m)
    p = e / jnp.sum(e, axis=-1, keepdims=True)
    return r0, qc, kb, p


def _attn_pad_copy(src_ref, dst, t, pad):
    dst[pl.ds(0, pad), :] = jnp.zeros((pad, dst.shape[1]), dst.dtype)
    rb = min(512, t)
    for r0 in range(0, t, rb):
        dst[pl.ds(pad + r0, rb), :] = src_ref[pl.ds(r0, rb), :]


def _attn_fwd(proj, bias, w, name):
    t = proj.shape[0]
    hd = w // ATT_HEADS
    pad = ATT_LEFT * CHUNK
    qo, ko, vo = 2 * w // hd, 3 * w // hd, 4 * w // hd

    def body(q_ref, k_ref, v_ref, bias_ref, o_ref, kp, vp):
        _attn_pad_copy(k_ref, kp, t, pad)
        _attn_pad_copy(v_ref, vp, t, pad)

        def chunk(c, _):
            r0, _, _, p = _attn_scores(q_ref, kp, bias_ref, c, hd)
            vb = vp[pl.ds(r0, ATT_BAND), :]
            o_ref[pl.ds(r0, CHUNK), :] = _dot(p.astype(BF16), vb).astype(BF16)
            return 0

        lax.fori_loop(0, t // CHUNK, chunk, 0, unroll=2)

    col = lambda c0: pl.BlockSpec((t, hd), functools.partial(lambda h, c0: (0, c0 + h), c0=c0))
    return pl.pallas_call(
        body, name=name, grid=(ATT_HEADS,),
        in_specs=[col(qo), col(ko), col(vo), pl.BlockSpec((None, CHUNK, ATT_BAND), lambda h: (h, 0, 0))],
        out_specs=col(0), out_shape=jax.ShapeDtypeStruct((t, w), BF16),
        scratch_shapes=[pltpu.VMEM((t + pad, hd), BF16), pltpu.VMEM((t + pad, hd), BF16)],
        compiler_params=_cp(("parallel",)))(proj, proj, proj, bias)


def _attn_bwd(proj, bias, do, w, name):
    t = proj.shape[0]
    hd = w // ATT_HEADS
    pad = ATT_LEFT * CHUNK
    qo, ko, vo = 2 * w // hd, 3 * w // hd, 4 * w // hd

    def body(q_ref, k_ref, v_ref, bias_ref, do_ref, dq_ref, dk_ref, dv_ref, db_ref, kp, vp, dkp, dvp):
        _attn_pad_copy(k_ref, kp, t, pad)
        _attn_pad_copy(v_ref, vp, t, pad)
        rb = min(512, t)
        for r0 in range(0, t + pad, rb):
            n = min(rb, t + pad - r0)
            dkp[pl.ds(r0, n), :] = jnp.zeros((n, hd), F32)
            dvp[pl.ds(r0, n), :] = jnp.zeros((n, hd), F32)
        db_ref[...] = jnp.zeros_like(db_ref)
        scale = hd ** -0.5

        def chunk(c, _):
            r0, qc, kb, p = _attn_scores(q_ref, kp, bias_ref, c, hd)
            band = pl.ds(r0, ATT_BAND)
            vb = vp[band, :]
            doc = do_ref[pl.ds(r0, CHUNK), :]
            dp = _dot(doc, vb, NT)
            ds = p * (dp - jnp.sum(dp * p, axis=-1, keepdims=True))
            db_ref[...] += ds
            dsb = ds.astype(BF16)
            dq_ref[pl.ds(r0, CHUNK), :] = (_dot(dsb, kb) * scale).astype(BF16)
            dkp[band, :] += _dot(dsb, qc, TN) * scale
            dvp[band, :] += _dot(p.astype(BF16), doc, TN)
            return 0

        lax.fori_loop(0, t // CHUNK, chunk, 0, unroll=2)
        for r0 in range(0, t, rb):
            dk_ref[pl.ds(r0, rb), :] = dkp[pl.ds(pad + r0, rb), :].astype(BF16)
            dv_ref[pl.ds(r0, rb), :] = dvp[pl.ds(pad + r0, rb), :].astype(BF16)

    col = lambda c0: pl.BlockSpec((t, hd), functools.partial(lambda h, c0: (0, c0 + h), c0=c0))
    tb = pl.BlockSpec((None, CHUNK, ATT_BAND), lambda h: (h, 0, 0))
    o = jax.ShapeDtypeStruct((t, w), BF16)
    return pl.pallas_call(
        body, name=name, grid=(ATT_HEADS,),
        in_specs=[col(qo), col(ko), col(vo), tb, col(0)],
        out_specs=[col(0), col(0), col(0), tb],
        out_shape=[o, o, o, jax.ShapeDtypeStruct((ATT_HEADS, CHUNK, ATT_BAND), F32)],
        scratch_shapes=[pltpu.VMEM((t + pad, hd), BF16), pltpu.VMEM((t + pad, hd), BF16),
                        pltpu.VMEM((t + pad, hd), F32), pltpu.VMEM((t + pad, hd), F32)],
        compiler_params=_cp(("parallel",)))(proj, proj, proj, bias, do)


def _rel_index():
    q_pos = ATT_LEFT * CHUNK + np.arange(CHUNK)
    k_pos = np.arange(ATT_BAND)
    return (np.clip(q_pos[:, None] - k_pos[None, :], -MAX_REL, MAX_REL) + MAX_REL).astype(np.int32).reshape(-1)


def _bias_expand(rel_bias, name):
    nl, nh, _ = rel_bias.shape
    n = CHUNK * ATT_BAND
    kb = n // 8
    idx = jnp.asarray(_rel_index().reshape(1, n))
    tab = jnp.pad(rel_bias, ((0, 0), (0, 0), (0, N_REL_PAD - N_REL)))

    def body(t_ref, i_ref, o_ref):
        onehot = (lax.broadcasted_iota(jnp.int32, (N_REL_PAD, kb), 0) == i_ref[...]).astype(F32)
        o_ref[...] = lax.dot_general(t_ref[...], onehot, NN, precision=lax.Precision.HIGHEST,
                                     preferred_element_type=F32)

    out = pl.pallas_call(
        body, name=name, grid=(nl, n // kb),
        in_specs=[pl.BlockSpec((None, nh, N_REL_PAD), lambda l, j: (l, 0, 0)), pl.BlockSpec((1, kb), lambda l, j: (0, j))],
        out_specs=pl.BlockSpec((None, nh, kb), lambda l, j: (l, 0, j)),
        out_shape=jax.ShapeDtypeStruct((nl, nh, n), F32), compiler_params=_cp(("parallel", "parallel")))(tab, idx)
    return out.reshape(nl, nh, CHUNK, ATT_BAND)


def _bias_reduce(dbias, name):
    nl, nh = dbias.shape[:2]
    n = CHUNK * ATT_BAND
    kb = n // 8
    idx = jnp.asarray(_rel_index().reshape(n, 1))

    def body(d_ref, i_ref, o_ref):
        @pl.when(pl.program_id(1) == 0)
        def _():
            o_ref[...] = jnp.zeros_like(o_ref)

        onehot = (lax.broadcasted_iota(jnp.int32, (kb, N_REL_PAD), 1) == i_ref[...]).astype(F32)
        o_ref[...] += lax.dot_general(d_ref[...], onehot, NN, precision=lax.Precision.HIGHEST,
                                      preferred_element_type=F32)

    out = pl.pallas_call(
        body, name=name, grid=(nl, n // kb),
        in_specs=[pl.BlockSpec((None, nh, kb), lambda l, j: (l, 0, j)), pl.BlockSpec((kb, 1), lambda l, j: (j, 0))],
        out_specs=pl.BlockSpec((None, nh, N_REL_PAD), lambda l, j: (l, 0, 0)),
        out_shape=jax.ShapeDtypeStruct((nl, nh, N_REL_PAD), F32),
        compiler_params=_cp(("parallel", "arbitrary")))(dbias.reshape(nl, nh, n), idx)
    return out[:, :, :N_REL]


def _s5_prep_math(a_re, a_im, ls, btr, bti):
    step = jnp.exp(ls)
    mag = jnp.exp(a_re * step)
    ang = a_im * step
    lr = mag * jnp.cos(ang)
    li = mag * jnp.sin(ang)
    den = a_re * a_re + a_im * a_im
    nr = lr - 1.0
    cr = (nr * a_re + li * a_im) / den
    ci = (li * a_re - nr * a_im) / den
    bbr = cr[:, None, :] * btr - ci[:, None, :] * bti
    bbi = cr[:, None, :] * bti + ci[:, None, :] * btr
    return lr, li, bbr, bbi


def _s5_prep(a_re, a_im, ls, btr, bti, name):
    nl, g, p = a_re.shape
    h = btr.shape[2]

    def body(ar, ai, l_ref, br, bi, o1, o2, o3, o4):
        r = _s5_prep_math(ar[...], ai[...], l_ref[...], br[...], bi[...])
        o1[...], o2[...], o3[...], o4[...] = r

    m2 = pl.BlockSpec((None, g, p), lambda l: (l, 0, 0))
    m1 = pl.BlockSpec((None, g, 1), lambda l: (l, 0, 0))
    m3 = pl.BlockSpec((None, g, h, p), lambda l: (l, 0, 0, 0))
    s2 = jax.ShapeDtypeStruct((nl, g, p), F32)
    s3 = jax.ShapeDtypeStruct((nl, g, h, p), F32)
    return pl.pallas_call(body, name=name, grid=(nl,), in_specs=[m2, m2, m1, m3, m3], out_specs=[m2, m2, m3, m3],
                          out_shape=[s2, s2, s3, s3], compiler_params=_cp(("parallel",)))(a_re, a_im, ls, btr, bti)


def _s5_prep_bwd(a_re, a_im, ls, btr, bti, dlr, dli, dbbr, dbbi, name):
    nl, g, p = a_re.shape
    h = btr.shape[2]

    def body(ar, ai, l_ref, br, bi, g1, g2, g3, g4, o1, o2, o3, o4, o5):
        _, vjp = jax.vjp(_s5_prep_math, ar[...], ai[...], l_ref[...], br[...], bi[...])
        o1[...], o2[...], o3[...], o4[...], o5[...] = vjp((g1[...], g2[...], g3[...], g4[...]))

    m2 = pl.BlockSpec((None, g, p), lambda l: (l, 0, 0))
    m1 = pl.BlockSpec((None, g, 1), lambda l: (l, 0, 0))
    m3 = pl.BlockSpec((None, g, h, p), lambda l: (l, 0, 0, 0))
    s2 = jax.ShapeDtypeStruct((nl, g, p), F32)
    s1 = jax.ShapeDtypeStruct((nl, g, 1), F32)
    s3 = jax.ShapeDtypeStruct((nl, g, h, p), F32)
    return pl.pallas_call(body, name=name, grid=(nl,), in_specs=[m2, m2, m1, m3, m3, m2, m2, m3, m3],
                          out_specs=[m2, m2, m1, m3, m3], out_shape=[s2, s2, s1, s3, s3],
                          compiler_params=_cp(("parallel",)))(a_re, a_im, ls, btr, bti, dlr, dli, dbbr, dbbi)


def _cpow(lr, li, n):
    rr = ri = None
    br, bi = lr, li
    while n:
        if n & 1:
            rr, ri = (br, bi) if rr is None else (rr * br - ri * bi, rr * bi + ri * br)
        n >>= 1
        if n:
            br, bi = br * br - bi * bi, 2.0 * br * bi
    return rr, ri


def _permute_rows(src, dst, seg, inverse):
    def body(k, _):
        tile = pl.ds(pl.multiple_of(k * NSEG, NSEG), NSEG)
        spread = pl.ds(k, NSEG, stride=seg)
        if inverse:
            dst[spread, :] = src[tile, :]
        else:
            dst[tile, :] = src[spread, :]
        return 0

    lax.fori_loop(0, seg, body, 0, unroll=8)


def _s5_states(up, bbt_ref, lam_ref, xs, t):
    seg = t // NSEG
    rb = min(512, t)
    nj = xs.shape[0]
    nh = nj // 2
    for r0 in range(0, t, rb):
        bu = _dot(up[pl.ds(r0, rb), :].astype(BF16), bbt_ref[...])
        for j in range(nj):
            xs[j, pl.ds(r0, rb), :] = bu[:, j * LANE:(j + 1) * LANE]
    lamv = lam_ref[...]
    lr = [jnp.broadcast_to(lamv[j:j + 1], (NSEG, LANE)) for j in range(nh)]
    li = [jnp.broadcast_to(lamv[nh + j:nh + j + 1], (NSEG, LANE)) for j in range(nh)]
    zero = jnp.zeros((NSEG, LANE), F32)

    def load(k):
        rows = pl.ds(pl.multiple_of(k * NSEG, NSEG), NSEG)
        return (tuple(xs.at[j][rows, :] for j in range(nh)), tuple(xs.at[nh + j][rows, :] for j in range(nh)))

    def step(k, c):
        xr, xi, br, bi = c
        nbr, nbi = load(jnp.minimum(k + 1, seg - 1))
        rows = pl.ds(pl.multiple_of(k * NSEG, NSEG), NSEG)
        nxr, nxi = [], []
        for j in range(nh):
            r = lr[j] * xr[j] - li[j] * xi[j] + br[j]
            i = lr[j] * xi[j] + li[j] * xr[j] + bi[j]
            xs.at[j][rows, :] = r
            xs.at[nh + j][rows, :] = i
            nxr.append(r)
            nxi.append(i)
        return tuple(nxr), tuple(nxi), nbr, nbi

    xr, xi, _, _ = lax.fori_loop(0, seg, step, ((zero,) * nh, (zero,) * nh) + load(0), unroll=4)
    pw = [_cpow(lr[j], li[j], seg) for j in range(nh)]
    sub = lax.broadcasted_iota(jnp.int32, (NSEG, LANE), 0)
    ctr, cti = [], []
    for j in range(nh):
        plr, pli = pw[j]
        cr, ci = zero, zero
        for s in range(1, NSEG):
            er = pltpu.roll(xr[j], 1, 0)
            ei = pltpu.roll(xi[j], 1, 0)
            pr_ = pltpu.roll(cr, 1, 0)
            pi_ = pltpu.roll(ci, 1, 0)
            nr = er + plr * pr_ - pli * pi_
            ni = ei + plr * pi_ + pli * pr_
            cr = jnp.where(sub == s, nr, cr)
            ci = jnp.where(sub == s, ni, ci)
        ctr.append(cr)
        cti.append(ci)

    def fix(k, c):
        rr, ri, xr_, xi_ = c
        nxr, nxi = load(jnp.minimum(k + 1, seg - 1))
        rows = pl.ds(pl.multiple_of(k * NSEG, NSEG), NSEG)
        nr, ni = [], []
        for j in range(nh):
            r = lr[j] * rr[j] - li[j] * ri[j]
            i = lr[j] * ri[j] + li[j] * rr[j]
            xs.at[j][rows, :] = xr_[j] + r
            xs.at[nh + j][rows, :] = xi_[j] + i
            nr.append(r)
            ni.append(i)
        return tuple(nr), tuple(ni), nxr, nxi

    lax.fori_loop(0, seg, fix, (tuple(ctr), tuple(cti)) + load(0), unroll=4)
    return lr, li, pw


def _s5_fwd(proj, bbt, cmat, lam, dvec, w, name):
    t = proj.shape[0]
    nt = w // LANE
    nj = bbt.shape[2] // LANE
    seg = t // NSEG
    rb = min(512, t)
    uo = 5 * nt

    def body(u_ref, bbt_ref, cm_ref, lam_ref, d_ref, s5o_ref, yc_ref, xs, uf, up):
        for r0 in range(0, t, rb):
            uf[pl.ds(r0, rb), :] = u_ref[pl.ds(r0, rb), :].astype(F32)
        _permute_rows(uf, up, seg, False)
        _s5_states(up, bbt_ref, lam_ref, xs, t)
        for r0 in range(0, t, rb):
            rows = pl.ds(r0, rb)
            y = d_ref[...] * up[rows, :]
            for j in range(nj):
                y = y + _dot(xs.at[j][rows, :].astype(BF16), cm_ref[pl.ds(j * LANE, LANE), :])
            uf[rows, :] = y
        _permute_rows(uf, s5o_ref, seg, True)
        for r0 in range(0, t, rb):
            rows = pl.ds(r0, rb)
            yc_ref[rows, :] = _gelu(s5o_ref[rows, :]).astype(BF16)

    col = lambda c0: pl.BlockSpec((t, LANE), functools.partial(lambda i, c0: (0, c0 + i), c0=c0))
    return pl.pallas_call(
        body, name=name, grid=(nt,),
        in_specs=[col(uo), pl.BlockSpec((None, LANE, nj * LANE), lambda i: (i, 0, 0)),
                  pl.BlockSpec((None, nj * LANE, LANE), lambda i: (i, 0, 0)),
                  pl.BlockSpec((None, nj, LANE), lambda i: (i, 0, 0)), pl.BlockSpec((1, LANE), lambda i: (0, i))],
        out_specs=[col(0), col(0)],
        out_shape=[jax.ShapeDtypeStruct((t, w), F32), jax.ShapeDtypeStruct((t, w), BF16)],
        scratch_shapes=[pltpu.VMEM((nj, t, LANE), F32), pltpu.VMEM((t, LANE), F32), pltpu.VMEM((t, LANE), F32)],
        compiler_params=_cp(("parallel",)))(proj, bbt, cmat, lam, dvec)


def _s5_bwd(proj, dy, bbt, bbm, cmt, lam, dvec, w, name):
    t = proj.shape[0]
    nt = w // LANE
    nj = bbt.shape[2] // LANE
    nh = nj // 2
    seg = t // NSEG
    rb = min(512, t)
    uo = 5 * nt

    def body(u_ref, dy_ref, bbt_ref, bbm_ref, cmt_ref, lam_ref, d_ref, du_ref, dbbt_ref, dcm_ref, dlam_ref, dd_ref,
             xs, gs, uf, up, dyp):
        for r0 in range(0, t, rb):
            uf[pl.ds(r0, rb), :] = u_ref[pl.ds(r0, rb), :].astype(F32)
        _permute_rows(uf, up, seg, False)
        for r0 in range(0, t, rb):
            uf[pl.ds(r0, rb), :] = dy_ref[pl.ds(r0, rb), :].astype(F32)
        _permute_rows(uf, dyp, seg, False)
        lr, li, pw = _s5_states(up, bbt_ref, lam_ref, xs, t)
        dcm = [jnp.zeros((LANE, LANE), F32) for _ in range(nj)]
        for r0 in range(0, t, rb):
            rows = pl.ds(r0, rb)
            dyb = dyp[rows, :].astype(BF16)
            dx = _dot(dyb, cmt_ref[...])
            for j in range(nj):
                gs.at[j][rows, :] = dx[:, j * LANE:(j + 1) * LANE]
                dcm[j] = dcm[j] + _dot(xs.at[j][rows, :].astype(BF16), dyb, TN)
        for j in range(nj):
            dcm_ref[pl.ds(j * LANE, LANE), :] = dcm[j]
        zero = jnp.zeros((NSEG, LANE), F32)
        sub = lax.broadcasted_iota(jnp.int32, (NSEG, LANE), 0)
        tile = lambda k: pl.ds(pl.multiple_of(k * NSEG, NSEG), NSEG)

        def gload(k):
            return (tuple(gs.at[j][tile(k), :] for j in range(nh)), tuple(gs.at[nh + j][tile(k), :] for j in range(nh)))

        def xload(k):
            return (tuple(xs.at[j][tile(k), :] for j in range(nh)), tuple(xs.at[nh + j][tile(k), :] for j in range(nh)))

        def step(kk, c):
            gr, gi, dr, di = c
            k = seg - 1 - kk
            ndr, ndi = gload(jnp.maximum(k - 1, 0))
            ngr, ngi = [], []
            for j in range(nh):
                r = lr[j] * gr[j] + li[j] * gi[j] + dr[j]
                i = lr[j] * gi[j] - li[j] * gr[j] + di[j]
                gs.at[j][tile(k), :] = r
                gs.at[nh + j][tile(k), :] = i
                ngr.append(r)
                ngi.append(i)
            return tuple(ngr), tuple(ngi), ndr, ndi

        g0r, g0i, _, _ = lax.fori_loop(0, seg, step, ((zero,) * nh, (zero,) * nh) + gload(seg - 1), unroll=4)
        ctr, cti = [], []
        for j in range(nh):
            plr, pli = pw[j]
            cr, ci = zero, zero
            for s in range(NSEG - 2, -1, -1):
                sr = pltpu.roll(g0r[j], NSEG - 1, 0)
                si = pltpu.roll(g0i[j], NSEG - 1, 0)
                pr_ = pltpu.roll(cr, NSEG - 1, 0)
                pi_ = pltpu.roll(ci, NSEG - 1, 0)
                nr = sr + plr * pr_ + pli * pi_
                ni = si + plr * pi_ - pli * pr_
                cr = jnp.where(sub == s, nr, cr)
                ci = jnp.where(sub == s, ni, ci)
            ctr.append(cr)
            cti.append(ci)

        def fix_step(k, c, xr, xi, gr, gi):
            rr, ri, ar, ai = c
            nr, ni, nar, nai = [], [], [], []
            for j in range(nh):
                r = lr[j] * rr[j] + li[j] * ri[j]
                i = lr[j] * ri[j] - li[j] * rr[j]
                g_r = gr[j] + r
                g_i = gi[j] + i
                gs.at[j][tile(k), :] = g_r
                gs.at[nh + j][tile(k), :] = g_i
                nar.append(ar[j] + g_r * xr[j] + g_i * xi[j])
                nai.append(ai[j] + g_i * xr[j] - g_r * xi[j])
                nr.append(r)
                ni.append(i)
            return tuple(nr), tuple(ni), tuple(nar), tuple(nai)

        def fix(kk, c):
            k = seg - 1 - kk
            nk = jnp.maximum(k - 1, 0)
            ngr, ngi = gload(nk)
            nxr, nxi = xload(jnp.maximum(nk - 1, 0))
            return fix_step(k, c[:4], c[6], c[7], c[4], c[5]) + (ngr, ngi, nxr, nxi)

        init = (tuple(ctr), tuple(cti), (zero,) * nh, (zero,) * nh) + gload(seg - 1) + xload(seg - 2)
        c = lax.fori_loop(0, seg - 1, fix, init, unroll=4)
        lxr, lxi = xload(seg - 1)
        shift = lambda v: jnp.where(sub == 0, 0.0, pltpu.roll(v, 1, 0))
        _, _, ar, ai = fix_step(0, c[:4], tuple(shift(v) for v in lxr), tuple(shift(v) for v in lxi), c[4], c[5])
        for j in range(nh):
            dlam_ref[pl.ds(j, 1), :] = jnp.sum(ar[j], axis=0, keepdims=True)
            dlam_ref[pl.ds(nh + j, 1), :] = jnp.sum(ai[j], axis=0, keepdims=True)
        dbb = [jnp.zeros((LANE, LANE), F32) for _ in range(nj)]
        dd = jnp.zeros((1, LANE), F32)
        for r0 in range(0, t, rb):
            rows = pl.ds(r0, rb)
            uv = up[rows, :]
            ub = uv.astype(BF16)
            dyf = dyp[rows, :]
            du = d_ref[...] * dyf
            dd = dd + jnp.sum(dyf * uv, axis=0, keepdims=True)
            for j in range(nj):
                gb = gs.at[j][rows, :].astype(BF16)
                du = du + _dot(gb, bbm_ref[pl.ds(j * LANE, LANE), :])
                dbb[j] = dbb[j] + _dot(ub, gb, TN)
            uf[rows, :] = du
        for j in range(nj):
            dbbt_ref[:, pl.ds(j * LANE, LANE)] = dbb[j]
        dd_ref[...] = dd
        _permute_rows(uf, up, seg, True)
        for r0 in range(0, t, rb):
            rows = pl.ds(r0, rb)
            du_ref[rows, :] = up[rows, :].astype(BF16)

    col = lambda c0: pl.BlockSpec((t, LANE), functools.partial(lambda i, c0: (0, c0 + i), c0=c0))
    wide = pl.BlockSpec((None, LANE, nj * LANE), lambda i: (i, 0, 0))
    tall = pl.BlockSpec((None, nj * LANE, LANE), lambda i: (i, 0, 0))
    lam_s = pl.BlockSpec((None, nj, LANE), lambda i: (i, 0, 0))
    vec = pl.BlockSpec((1, LANE), lambda i: (0, i))
    flat = lambda: pltpu.VMEM((t, LANE), F32)
    return pl.pallas_call(
        body, name=name, grid=(nt,),
        in_specs=[col(uo), col(0), wide, tall, wide, lam_s, vec],
        out_specs=[col(0), wide, tall, lam_s, vec],
        out_shape=[jax.ShapeDtypeStruct((t, w), BF16), jax.ShapeDtypeStruct((nt, LANE, nj * LANE), F32),
                   jax.ShapeDtypeStruct((nt, nj * LANE, LANE), F32), jax.ShapeDtypeStruct((nt, nj, LANE), F32),
                   jax.ShapeDtypeStruct((1, w), F32)],
        scratch_shapes=[pltpu.VMEM((nj, t, LANE), F32), pltpu.VMEM((nj, t, LANE), F32), flat(), flat(), flat()],
        compiler_params=_cp(("parallel",)))(proj, dy, bbt, bbm, cmt, lam, dvec)


def _block_diag(x, nt):
    nb, r, c = x.shape
    b = nb // nt
    eye = jnp.eye(b, dtype=x.dtype)
    return jnp.einsum("ibrc,bk->ibrkc", x.reshape(nt, b, r, c), eye).reshape(nt, b * r, b * c)


def _block_diag_extract(x, b):
    nt, br, bc = x.shape
    r, c = br // b, bc // b
    eye = jnp.eye(b, dtype=x.dtype)
    return jnp.einsum("ibrkc,bk->ibrc", x.reshape(nt, b, r, b, c), eye).reshape(nt * b, r, c)


def _adamw_math(w, g, m, v):
    m = ADAM_B1 * m + (1.0 - ADAM_B1) * g
    v = ADAM_B2 * v + (1.0 - ADAM_B2) * (g * g)
    m_hat = m / (1.0 - ADAM_B1 ** ADAM_STEP)
    v_hat = v / (1.0 - ADAM_B2 ** ADAM_STEP)
    delta = -ADAM_LR * (m_hat / (jnp.sqrt(v_hat) + ADAM_EPS) + ADAM_WD * w)
    return delta, m, v


def _adamw_reduce(parts, w, m, v, prev, l, name):
    n, r, c = parts.shape
    tr = r
    for cand in (512, 256, 128, 64, 32, 16, 8):
        if r % cand == 0 and cand * c * 4 <= 2 * 1024 * 1024:
            tr = cand
            break
    nb = r // tr
    if prev is None:
        prev = [lax.empty(w.shape, F32) for _ in range(4)]

    def body(p_ref, w_ref, m_ref, v_ref, *rest):
        g_ref, d_ref, nm_ref, nv_ref = rest[4:]
        g = p_ref[0].astype(F32)
        for q in range(1, n):
            g = g + p_ref[q].astype(F32)
        g_ref[...] = g
        d_ref[...], nm_ref[...], nv_ref[...] = _adamw_math(w_ref[...], g, m_ref[...], v_ref[...])

    row = pl.BlockSpec((tr, c), lambda i: (l * nb + i, 0))
    any_spec = pl.BlockSpec(memory_space=pl.ANY)
    o = jax.ShapeDtypeStruct(w.shape, F32)
    return pl.pallas_call(
        body, name=name, grid=(nb,),
        in_specs=[pl.BlockSpec((n, tr, c), lambda i: (0, i, 0)), row, row, row] + [any_spec] * 4,
        out_specs=[row, row, row, row], out_shape=[o, o, o, o],
        input_output_aliases={4: 0, 5: 1, 6: 2, 7: 3},
        compiler_params=_cp(("parallel",)))(parts, w, m, v, *prev)


def _adamw_flat(w, g, m, v, name):
    r, c = w.shape

    def body(w_ref, g_ref, m_ref, v_ref, d_ref, nm_ref, nv_ref):
        d_ref[...], nm_ref[...], nv_ref[...] = _adamw_math(w_ref[...], g_ref[...], m_ref[...], v_ref[...])

    o = jax.ShapeDtypeStruct((r, c), F32)
    row = pl.BlockSpec((FLAT_ROWS, c), lambda i: (i, 0))
    return pl.pallas_call(body, name=name, grid=(r // FLAT_ROWS,), in_specs=[row] * 4, out_specs=[row] * 3,
                          out_shape=[o, o, o], compiler_params=_cp(("parallel",)))(w, g, m, v)


def _sum_rows8(x, name):
    n, r, c = x.shape

    def body(x_ref, o_ref):
        acc = x_ref[0]
        for q in range(1, n):
            acc = acc + x_ref[q]
        o_ref[...] = acc

    return pl.pallas_call(
        body, name=name, grid=(r // FLAT_ROWS,), in_specs=[pl.BlockSpec((n, FLAT_ROWS, c), lambda i: (0, i, 0))],
        out_specs=pl.BlockSpec((FLAT_ROWS, c), lambda i: (i, 0)), out_shape=jax.ShapeDtypeStruct((r, c), F32),
        compiler_params=_cp(("parallel",)))(x)


def _row_tile(r, row_bytes, limit=2 * 1024 * 1024):
    for cand in (512, 256, 128, 64, 32, 16):
        if r % cand == 0 and cand * row_bytes <= limit:
            return cand
    return r


def _pair_sum(own, sib, core, chip, name):
    _, _, r, c = own.shape
    tr = _row_tile(r, 2 * c)

    def body(core_ref, chip_ref, a_ref, b_ref, o_ref, land_ref):
        s = (a_ref[...].astype(F32) + b_ref[...].astype(F32)).astype(BF16)
        o_ref[...] = s

        @pl.when(pl.program_id(1) == chip_ref[0])
        def _():
            land_ref[...] = s

    grid_spec = pltpu.PrefetchScalarGridSpec(
        num_scalar_prefetch=2, grid=(r // tr, NCHIP),
        in_specs=[pl.BlockSpec((None, None, tr, c), lambda i, p, cr, hr: (p, cr[0], i, 0)),
                  pl.BlockSpec((None, tr, c), lambda i, p, cr, hr: (p, i, 0))],
        out_specs=[pl.BlockSpec((None, tr, c), lambda i, p, cr, hr: (p, i, 0)),
                   pl.BlockSpec((None, tr, c), lambda i, p, cr, hr: (hr[0], i, 0))])
    o = jax.ShapeDtypeStruct((NCHIP, r, c), BF16)
    return pl.pallas_call(body, name=name, grid_spec=grid_spec, out_shape=[o, o],
                          compiler_params=_cp(("parallel", "arbitrary")))(core, chip, own, sib)


def _cast_place(w, l, me, after, name):
    _, r, c = w.shape
    tr = _row_tile(r, 4 * c)

    def body(m_ref, w_ref, a_ref, o_ref, land_ref):
        v = w_ref[...].astype(BF16)
        o_ref[...] = v
        land_ref[...] = v

    grid_spec = pltpu.PrefetchScalarGridSpec(
        num_scalar_prefetch=1, grid=(r // tr,),
        in_specs=[pl.BlockSpec((None, tr, c), lambda i, mr: (l, i, 0)), pl.BlockSpec(memory_space=pl.ANY)],
        out_specs=[pl.BlockSpec((tr, c), lambda i, mr: (i, 0)), pl.BlockSpec((None, tr, c), lambda i, mr: (mr[0], i, 0))])
    return pl.pallas_call(body, name=name, grid_spec=grid_spec,
                          out_shape=[jax.ShapeDtypeStruct((r, c), BF16), jax.ShapeDtypeStruct((NDEV, r, c), BF16)],
                          compiler_params=_cp(("parallel",)))(me, w, after)


def _all_gather(xs, name):
    n = len(xs)

    def body(*refs):
        x_refs, o_refs = refs[:n], refs[n:2 * n]
        send_sems, recv_sems, local_sems = refs[2 * n:]
        x, y, c = lax.axis_index("x"), lax.axis_index("y"), lax.axis_index("c")
        me, sibling = (x, y, c), (x, y, 1 - c)
        chips = [(1 - x, y), (x, 1 - y), (1 - x, 1 - y)]

        def copy(t, k, block, to, src=None):
            dst = o_refs[t].at[4 * block[0] + 2 * block[1] + block[2]]
            return pltpu.make_async_remote_copy(
                src_ref=dst if src is None else src, dst_ref=dst,
                send_sem=send_sems.at[7 * t + k], recv_sem=recv_sems.at[7 * t + k],
                device_id=to, device_id_type=MESH)

        mine, first, passed = [], [], []
        for t in range(n):
            cp = pltpu.make_async_copy(x_refs[t], o_refs[t].at[4 * x + 2 * y + c], local_sems.at[t])
            cp.start()
            mine.append(cp)
            first.append(copy(t, 0, me, sibling, src=x_refs[t]))
            first += [copy(t, 1 + j, me, (*chip, c), src=x_refs[t]) for j, chip in enumerate(chips)]
        for cp in first:
            cp.start()
        for t in range(n):
            for j, chip in enumerate(chips):
                copy(t, 1 + j, (*chip, c), me).wait_recv()
                cp = copy(t, 4 + j, (*chip, c), sibling)
                cp.start()
                passed.append(cp)
        for t in range(n):
            copy(t, 0, sibling, me).wait_recv()
            for j, chip in enumerate(chips):
                copy(t, 4 + j, (*chip, 1 - c), me).wait_recv()
        for cp in first + passed:
            cp.wait_send()
        for cp in mine:
            cp.wait()

    any_spec = pl.BlockSpec(memory_space=pl.ANY)
    return pl.pallas_call(
        body, name=name, in_specs=[any_spec] * n, out_specs=[any_spec] * n,
        out_shape=[jax.ShapeDtypeStruct((NDEV,) + a.shape, a.dtype) for a in xs],
        scratch_shapes=[pltpu.SemaphoreType.DMA((7 * n,)), pltpu.SemaphoreType.DMA((7 * n,)),
                        pltpu.SemaphoreType.DMA((n,))],
        )(*xs)


def _sibling_exchange(xs, name):
    n = len(xs)

    def body(*refs):
        x_refs, o_refs = refs[:n], refs[n:2 * n]
        send_sems, recv_sems = refs[2 * n:]
        x, y, c = lax.axis_index("x"), lax.axis_index("y"), lax.axis_index("c")
        cps = []
        for t in range(n):
            cp = pltpu.make_async_remote_copy(
                src_ref=x_refs[t].at[:, 1 - c], dst_ref=o_refs[t], send_sem=send_sems.at[t], recv_sem=recv_sems.at[t],
                device_id=(x, y, 1 - c), device_id_type=MESH)
            cp.start()
            cps.append(cp)
        for cp in cps:
            cp.wait()

    any_spec = pl.BlockSpec(memory_space=pl.ANY)
    return pl.pallas_call(
        body, name=name, in_specs=[any_spec] * n, out_specs=[any_spec] * n,
        out_shape=[jax.ShapeDtypeStruct((a.shape[0],) + a.shape[2:], a.dtype) for a in xs],
        scratch_shapes=[pltpu.SemaphoreType.DMA((n,)), pltpu.SemaphoreType.DMA((n,))],
        )(*xs)


def _split_copies(kind, srcs, lands, send_sems, recv_sems):
    x, y, c = lax.axis_index("x"), lax.axis_index("y"), lax.axis_index("c")
    flip = lambda v, b: 1 - v if b else v
    cps = []
    for t in range(len(srcs)):
        if kind == "all":
            for mask in range(1, NDEV):
                peer = (flip(x, mask & 4), flip(y, mask & 2), flip(c, mask & 1))
                cps.append(pltpu.make_async_remote_copy(
                    src_ref=srcs[t], dst_ref=lands[t].at[4 * x + 2 * y + c],
                    send_sem=send_sems.at[(NDEV - 1) * t + mask - 1], recv_sem=recv_sems.at[(NDEV - 1) * t + mask - 1],
                    device_id=peer, device_id_type=MESH))
        else:
            for j, (px, py) in enumerate([(1 - x, y), (x, 1 - y), (1 - x, 1 - y)]):
                cps.append(pltpu.make_async_remote_copy(
                    src_ref=srcs[t].at[2 * px + py], dst_ref=lands[t].at[2 * x + y],
                    send_sem=send_sems.at[3 * t + j], recv_sem=recv_sems.at[3 * t + j],
                    device_id=(px, py, c), device_id_type=MESH))
    return cps


def _split_start(kind, srcs, lands, name):
    n = len(srcs)
    ncp = n * (NDEV - 1 if kind == "all" else NCHIP - 1)
    arrs = list(srcs) + list(lands)

    def body(*refs):
        for cp in _split_copies(kind, refs[:n], refs[n:2 * n], refs[2 * n], refs[2 * n + 1]):
            cp.start()
        refs[-1][...] = jnp.zeros_like(refs[-1])

    hbm = pl.BlockSpec(memory_space=pltpu.HBM)
    sem = pl.BlockSpec(memory_space=pltpu.SEMAPHORE)
    return pl.pallas_call(
        body, name=name,
        out_shape=(pltpu.SemaphoreType.DMA((ncp,)), pltpu.SemaphoreType.DMA((ncp,)),
                   *[pltpu.HBM(a.shape, a.dtype) for a in arrs], jax.ShapeDtypeStruct((8, LANE), F32)),
        in_specs=[hbm] * (2 * n), out_specs=(sem, sem, *[hbm] * (2 * n), pl.BlockSpec(memory_space=pltpu.VMEM)),
        input_output_aliases={i: 2 + i for i in range(2 * n)},
        compiler_params=pltpu.CompilerParams(has_side_effects=pltpu.SideEffectType.DATAFLOW_SIDE_EFFECTING),
    )(*[pltpu.with_memory_space_constraint(a, pltpu.HBM) for a in arrs])


def _split_wait(kind, started, after, name):
    send_sems, recv_sems, thru = started[0], started[1], list(started[2:])
    n = len(thru) // 2

    def body(*refs):
        for cp in _split_copies(kind, refs[:n], refs[n:2 * n], refs[2 * n], refs[2 * n + 1]):
            cp.wait_send()
            cp.wait_recv()

    hbm = pl.BlockSpec(memory_space=pltpu.HBM)
    sem = pl.BlockSpec(memory_space=pltpu.SEMAPHORE)
    outs = pl.pallas_call(
        body, name=name, out_shape=tuple(pltpu.HBM(a.shape, a.dtype) for a in thru),
        in_specs=[hbm] * (2 * n) + [sem, sem, pl.BlockSpec(memory_space=pl.ANY)], out_specs=tuple([hbm] * (2 * n)),
        input_output_aliases={i: i for i in range(2 * n)},
        compiler_params=pltpu.CompilerParams(has_side_effects=pltpu.SideEffectType.DATAFLOW_SIDE_EFFECTING),
    )(*thru, send_sems, recv_sems, after)
    return list(outs[n:])


_SMALL = ["norm_mix_g", "lru_conv_b", "lru_wa", "lru_ba", "lru_wx", "lru_bx", "lru_lambda", "attn_rel_bias",
          "ssm_a_re", "ssm_a_im", "ssm_b_re", "ssm_b_im", "ssm_c_re", "ssm_c_im", "ssm_d", "ssm_log_step",
          "norm_ffn_g", "norm_final_g"]
_SMALL_SHARDED = ["gate_bias", "lru_conv_w"]
_BIG = ["w_in", "ssm_w_glu", "w_branch", "w_out", "w_ffn_gate", "w_ffn_up", "w_ffn_down"]
_WEIGHTS = ["norm_mix_g", "w_in", "gate_bias", "lru_conv_w", "lru_conv_b", "lru_wa", "lru_ba", "lru_wx", "lru_bx",
            "lru_lambda", "attn_rel_bias", "ssm_a_re", "ssm_a_im", "ssm_b_re", "ssm_b_im", "ssm_c_re", "ssm_c_im",
            "ssm_d", "ssm_log_step", "ssm_w_glu", "w_branch", "w_out", "norm_ffn_g", "w_ffn_gate", "w_ffn_up",
            "w_ffn_down", "norm_final_g"]


def _device_step(x, target, wts, small_gath, get_gath, on_grads):
    t, d = x.shape
    w = d // 2
    nl = wts["norm_mix_g"].shape[0]
    nt = w // LANE
    f = wts["w_ffn_gate"].shape[-1]
    ncol = wts["w_in"].shape[-1]
    tm = _tile(t, 1024)
    n_g = w // SSM_GROUP
    gpt = LANE // SSM_GROUP
    bw = w // LRU_BLOCKS
    bpt = LANE // bw

    gbias = jnp.transpose(small_gath["gate_bias"], (1, 2, 0, 3)).reshape(nl, N_BRANCH, d)
    convw = jnp.transpose(small_gath["lru_conv_w"], (1, 2, 0, 3)).reshape(nl, CONV_W, w)
    bias_tab = _bias_expand(wts["attn_rel_bias"], "bias_expand")
    btr = jnp.swapaxes(wts["ssm_b_re"], 2, 3)
    bti = jnp.swapaxes(wts["ssm_b_im"], 2, 3)
    ls3 = wts["ssm_log_step"][..., None]
    lam_r, lam_i, bbr, bbi = _s5_prep(wts["ssm_a_re"], wts["ssm_a_im"], ls3, btr, bti, "s5_prep")

    def layer_consts(l):
        c = {}
        c["wa_t"] = _block_diag(wts["lru_wa"][l], nt).astype(BF16)
        c["wx_t"] = _block_diag(wts["lru_wx"][l], nt).astype(BF16)
        c["wat_t"] = jnp.swapaxes(c["wa_t"], 1, 2)
        c["wxt_t"] = jnp.swapaxes(c["wx_t"], 1, 2)
        bd_r, bd_i = _block_diag(bbr[l], nt), _block_diag(bbi[l], nt)
        c["bbt"] = jnp.concatenate([bd_r, bd_i], axis=-1).astype(BF16)
        c["bbm"] = jnp.swapaxes(c["bbt"], 1, 2)
        cd_r, cd_i = _block_diag(wts["ssm_c_re"][l], nt), _block_diag(wts["ssm_c_im"][l], nt)
        c["cmt"] = jnp.concatenate([cd_r, -cd_i], axis=-1).astype(BF16)
        c["cmat"] = jnp.swapaxes(c["cmt"], 1, 2)
        nrow = gpt * SSM_P // LANE
        c["lam"] = jnp.concatenate([lam_r[l].reshape(nt, nrow, LANE), lam_i[l].reshape(nt, nrow, LANE)], axis=1)
        return c

    row1 = lambda a: a.reshape(1, -1)
    tied = lambda g, tok: g if tok is None else g + tok[0, 0]
    saved = []
    for l in range(nl):
        c = layer_consts(l)
        gath, tok = get_gath(l, x)
        h1 = _rmsnorm_fwd(x, tied(row1(wts["norm_mix_g"][l]), tok), f"norm_mix_{l}")
        tn = _tile(ncol, 768)
        nn_ = ncol // tn
        proj = _mm(f"in_proj_{l}", NN, (t // tm, NDEV, nn_), 0, [h1, gath["w_in"]],
                   [pl.BlockSpec((tm, d), lambda i, j, n: (i, 0)),
                    pl.BlockSpec((None, d, tn), lambda i, j, n: (j, 0, n))],
                   jax.ShapeDtypeStruct((t, NDEV * ncol), BF16),
                   pl.BlockSpec((tm, tn), lambda i, j, n: (i, j * nn_ + n)), (8, LANE))
        ya, hs = _lru_fwd(proj, convw[l], row1(wts["lru_conv_b"][l]), c["wa_t"], row1(wts["lru_ba"][l]), c["wx_t"],
                          row1(wts["lru_bx"][l]), row1(wts["lru_lambda"][l]), w, f"lru_fwd_{l}")
        yb = _attn_fwd(proj, bias_tab[l], w, f"attn_fwd_{l}")
        s5o, yc = _s5_fwd(proj, c["bbt"], c["cmat"], c["lam"], row1(wts["ssm_d"][l]), w, f"s5_fwd_{l}")
        cb = d // NDEV

        def branch(y, wg, idx, nm):
            return _mm(nm, NN, (t // tm, NDEV), 0, [y, wg],
                       [pl.BlockSpec((tm, w), lambda i, j: (i, 0)),
                        pl.BlockSpec((None,) * len(idx(0)[:-2]) + (w, cb), lambda i, j: idx(j))],
                       jax.ShapeDtypeStruct((t, d), BF16), pl.BlockSpec((tm, cb), lambda i, j: (i, j)), (8, LANE))

        wb_idx = lambda b: (lambda j: (j, b, 0, 0))
        br_a = branch(ya, gath["w_branch"], wb_idx(0), f"branch_a_{l}")
        br_b = branch(yb, gath["w_branch"], wb_idx(1), f"branch_b_{l}")
        br_c0 = branch(yc, gath["w_branch"], wb_idx(2), f"branch_c_{l}")
        gl = branch(yc, gath["ssm_w_glu"], lambda j: (j, 0, 0), f"branch_glu_{l}")
        merged = _merge_fwd(proj, gbias[l], br_a, br_b, br_c0, gl, d, f"merge_fwd_{l}")
        tno = _tile(d, 1024)
        w_out_full = gath["w_out"].reshape(d, d)
        x_mid = _mm(f"out_proj_{l}", NN, (t // tm, d // tno), 0, [merged, w_out_full, x],
                    [pl.BlockSpec((tm, d), lambda i, n: (i, 0)), pl.BlockSpec((d, tno), lambda i, n: (0, n)),
                     pl.BlockSpec((tm, tno), lambda i, n: (i, n))],
                    jax.ShapeDtypeStruct((t, d), F32), pl.BlockSpec((tm, tno), lambda i, n: (i, n)), (8, LANE),
                    epi=lambda acc, res: acc + res)
        h2 = _rmsnorm_fwd(x_mid, row1(wts["norm_ffn_g"][l]), f"norm_ffn_{l}")
        gp, up, act = _ffn_in(h2, gath["w_ffn_gate"], gath["w_ffn_up"], f"ffn_in_{l}")
        x_next = _mm(f"ffn_down_{l}", NN, (t // tm, d // tno, NDEV), 1, [act, gath["w_ffn_down"], x_mid],
                     [pl.BlockSpec((None, tm, f), lambda i, n, k: (k, i, 0)),
                      pl.BlockSpec((None, f, tno), lambda i, n, k: (k, 0, n)),
                      pl.BlockSpec((tm, tno), lambda i, n, k: (i, n))],
                     jax.ShapeDtypeStruct((t, d), F32), pl.BlockSpec((tm, tno), lambda i, n, k: (i, n)), (tm, tno),
                     epi=lambda acc, res: acc + res)
        saved.append(dict(x=x, h1=h1, proj=proj, ya=ya, hs=hs, yb=yb, s5o=s5o, yc=yc, br_a=br_a, br_b=br_b,
                          br_c0=br_c0, gl=gl, merged=merged, x_mid=x_mid, h2=h2, gp=gp, up=up, act=act, c=c, gath=gath))
        x = x_next

    loss, dx, dxb, dgf = _loss_head(x, row1(wts["norm_final_g"]), target, "loss_head")

    sg = {k: [None] * nl for k in _SMALL + _SMALL_SHARDED if k != "norm_final_g"}
    dbias_tabs = [None] * nl
    dlr_l, dli_l, dbbr_l, dbbi_l = [None] * nl, [None] * nl, [None] * nl, [None] * nl
    tk = _tile(d, 1024)
    for l in range(nl - 1, -1, -1):
        s = saved[l]
        c = s["c"]
        gath = s["gath"]
        gbuf = {}
        tno = _tile(d, 1024)
        gbuf["w_ffn_down"] = _mm(
            f"dw_ffn_down_{l}", TN, (NDEV, d // tno), 0, [s["act"], dxb],
            [pl.BlockSpec((None, t, f), lambda j, n: (j, 0, 0)), pl.BlockSpec((t, tno), lambda j, n: (0, n))],
            jax.ShapeDtypeStruct((NDEV, f, d), BF16), pl.BlockSpec((None, f, tno), lambda j, n: (j, 0, n)), (8, LANE))
        dgp, dup = _ffn_dact(dxb, gath["w_ffn_down"], s["gp"], s["up"], f"d_act_{l}")
        dh2 = _ffn_dh(dgp, dup, gath["w_ffn_gate"], gath["w_ffn_up"], f"d_h2_{l}")

        def ffn_dw(dz, key, nm):
            return _mm(nm, TN, (NDEV, d // tk), 0, [s["h2"], dz],
                       [pl.BlockSpec((t, tk), lambda j, k: (0, k)), pl.BlockSpec((None, t, f), lambda j, k: (j, 0, 0))],
                       jax.ShapeDtypeStruct((NDEV, d, f), BF16), pl.BlockSpec((None, tk, f), lambda j, k: (j, k, 0)),
                       (8, LANE))

        gbuf["w_ffn_gate"] = ffn_dw(dgp, "w_ffn_gate", f"dw_ffn_gate_{l}")
        gbuf["w_ffn_up"] = ffn_dw(dup, "w_ffn_up", f"dw_ffn_up_{l}")
        dx_mid, dxmb, dg2 = _rmsnorm_bwd(s["x_mid"], row1(wts["norm_ffn_g"][l]), [dh2], dx, f"norm_ffn_bwd_{l}")
        sg["norm_ffn_g"][l] = dg2.reshape(-1)
        rb_ = d // NDEV
        dmerged = _mm(f"d_merged_{l}", NT, (t // tm, d // tno), 0, [dxmb, gath["w_out"].reshape(d, d)],
                      [pl.BlockSpec((tm, d), lambda i, n: (i, 0)), pl.BlockSpec((tno, d), lambda i, n: (n, 0))],
                      jax.ShapeDtypeStruct((t, d), BF16), pl.BlockSpec((tm, tno), lambda i, n: (i, n)), (8, LANE))
        tkw = _tile(d, 512)
        gbuf["w_out"] = _mm(
            f"dw_out_{l}", TN, (d // tkw, d // tno), 0, [s["merged"], dxmb],
            [pl.BlockSpec((t, tkw), lambda k, n: (0, k)), pl.BlockSpec((t, tno), lambda k, n: (0, n))],
            jax.ShapeDtypeStruct((d, d), BF16), pl.BlockSpec((tkw, tno), lambda k, n: (k, n)),
            (8, LANE)).reshape(NDEV, rb_, d)
        dbr_a, dbr_b, dbr_c0, dgl, dgates, dgb = _merge_bwd(s["proj"], gbias[l], s["br_a"], s["br_b"], s["br_c0"], s["gl"],
                                                           dmerged, d, f"merge_bwd_{l}")
        sg["gate_bias"][l] = dgb.reshape(-1)
        cb = d // NDEV

        def branch_dy(dbr, wg, idx, nm, dt):
            nlead = len(idx(0, 0)) - 2
            return _mm(nm, NT, (t // tm, NDEV), 1, [dbr, wg],
                       [pl.BlockSpec((tm, cb), lambda i, j: (i, j)),
                        pl.BlockSpec((None,) * nlead + (w, cb), lambda i, j: idx(j, 0))],
                       jax.ShapeDtypeStruct((t, w), dt), pl.BlockSpec((tm, w), lambda i, j: (i, 0)), (tm, w))

        def branch_dw(y, dbr, buf, shape, idx, nm):
            nlead = len(idx(0)) - 2
            return _mm(nm, TN, (NDEV,), 0, [y, dbr] + ([] if buf is None else [buf]),
                       [pl.BlockSpec((t, w), lambda j: (0, 0)), pl.BlockSpec((t, cb), lambda j: (0, j))],
                       jax.ShapeDtypeStruct(shape, BF16),
                       pl.BlockSpec((None,) * nlead + (w, cb), lambda j: idx(j)), (8, LANE),
                       alias=None if buf is None else True)

        wb_i = lambda b: (lambda j, z=0: (j, b, 0, 0))
        glu_i = lambda j, z=0: (j, 0, 0)
        dya = branch_dy(dbr_a, gath["w_branch"], wb_i(0), f"d_ya_{l}", BF16)
        dyb = branch_dy(dbr_b, gath["w_branch"], wb_i(1), f"d_yb_{l}", BF16)
        dyc1 = branch_dy(dbr_c0, gath["w_branch"], wb_i(2), f"d_yc_{l}", F32)
        dyc2 = branch_dy(dgl, gath["ssm_w_glu"], glu_i, f"d_yc_glu_{l}", F32)
        wb_shape = (NDEV, N_BRANCH, w, cb)
        gwb = branch_dw(s["ya"], dbr_a, lax.empty(wb_shape, BF16), wb_shape, wb_i(0), f"dw_branch_a_{l}")
        gwb = branch_dw(s["yb"], dbr_b, gwb, wb_shape, wb_i(1), f"dw_branch_b_{l}")
        gbuf["w_branch"] = branch_dw(s["yc"], dbr_c0, gwb, wb_shape, wb_i(2), f"dw_branch_c_{l}")
        gbuf["ssm_w_glu"] = branch_dw(s["yc"], dgl, None, (NDEV, w, cb), glu_i, f"dw_glu_{l}")
        ds5o = _gelu_bwd_sum(s["s5o"], dyc1, dyc2, f"gelu_bwd_{l}")
        du, dbbt, dcm, dlam, dd = _s5_bwd(s["proj"], ds5o, c["bbt"], c["bbm"], c["cmt"], c["lam"], row1(wts["ssm_d"][l]),
                                          w, f"s5_bwd_{l}")
        sp = gpt * SSM_P
        dbbr_l[l] = _block_diag_extract(dbbt[:, :, :sp], gpt)
        dbbi_l[l] = _block_diag_extract(dbbt[:, :, sp:], gpt)
        dcmt = jnp.swapaxes(dcm, 1, 2)
        sg["ssm_c_re"][l] = _block_diag_extract(dcmt[:, :, :sp], gpt).reshape(-1)
        sg["ssm_c_im"][l] = (-_block_diag_extract(dcmt[:, :, sp:], gpt)).reshape(-1)
        nrow = sp // LANE
        dlr_l[l] = dlam[:, :nrow].reshape(n_g, SSM_P)
        dli_l[l] = dlam[:, nrow:].reshape(n_g, SSM_P)
        sg["ssm_d"][l] = dd.reshape(-1)
        dq, dk, dv, dbt = _attn_bwd(s["proj"], bias_tab[l], dyb, w, f"attn_bwd_{l}")
        dbias_tabs[l] = dbt
        dlx, dlg, dcw, dcb, dwa, dba, dwx, dbx, dlm = _lru_bwd(
            s["proj"], s["hs"], dya, convw[l], row1(wts["lru_conv_b"][l]), c["wa_t"], row1(wts["lru_ba"][l]), c["wx_t"],
            row1(wts["lru_bx"][l]), row1(wts["lru_lambda"][l]), c["wat_t"], c["wxt_t"], w, f"lru_bwd_{l}")
        sg["lru_conv_w"][l] = dcw.reshape(-1)
        sg["lru_conv_b"][l] = dcb.reshape(-1)
        sg["lru_wa"][l] = _block_diag_extract(dwa, bpt).reshape(-1)
        sg["lru_wx"][l] = _block_diag_extract(dwx, bpt).reshape(-1)
        sg["lru_ba"][l] = dba.reshape(-1)
        sg["lru_bx"][l] = dbx.reshape(-1)
        sg["lru_lambda"][l] = dlm.reshape(-1)
        dproj = jnp.concatenate([dlx, dlg, dq, dk, dv, du, dgates], axis=1)
        tn = _tile(ncol, 768)
        nn_ = ncol // tn
        dh1 = _mm(f"d_h1_{l}", NT, (t // tm, d // tk, NDEV), 1, [dproj, gath["w_in"]],
                  [pl.BlockSpec((tm, ncol), lambda i, k, j: (i, j)), pl.BlockSpec((None, tk, ncol), lambda i, k, j: (j, k, 0))],
                  jax.ShapeDtypeStruct((t, d), F32), pl.BlockSpec((tm, tk), lambda i, k, j: (i, k)), (tm, tk))
        gbuf["w_in"] = _mm(
            f"dw_in_{l}", TN, (NDEV, d // tk, nn_), 0, [s["h1"], dproj],
            [pl.BlockSpec((t, tk), lambda j, k, n: (0, k)), pl.BlockSpec((t, tn), lambda j, k, n: (0, j * nn_ + n))],
            jax.ShapeDtypeStruct((NDEV, d, ncol), BF16), pl.BlockSpec((None, tk, tn), lambda j, k, n: (j, k, n)),
            (8, LANE))
        tok = on_grads(l, gbuf)
        dx, dxb, dg1 = _rmsnorm_bwd(s["x"], tied(row1(wts["norm_mix_g"][l]), tok), [dh1], dx_mid, f"norm_mix_bwd_{l}")
        sg["norm_mix_g"][l] = dg1.reshape(-1)

    da_re, da_im, dls, dbtr, dbti = _s5_prep_bwd(wts["ssm_a_re"], wts["ssm_a_im"], ls3, btr, bti, jnp.stack(dlr_l),
                                                 jnp.stack(dli_l), jnp.stack(dbbr_l), jnp.stack(dbbi_l), "s5_prep_bwd")
    drel = _bias_reduce(jnp.stack(dbias_tabs), "bias_reduce")
    small = {k: jnp.stack(v) for k, v in sg.items() if v[0] is not None}
    small["ssm_a_re"] = da_re.reshape(nl, -1)
    small["ssm_a_im"] = da_im.reshape(nl, -1)
    small["ssm_log_step"] = dls.reshape(nl, -1)
    small["ssm_b_re"] = jnp.swapaxes(dbtr, 2, 3).reshape(nl, -1)
    small["ssm_b_im"] = jnp.swapaxes(dbti, 2, 3).reshape(nl, -1)
    small["attn_rel_bias"] = drel.reshape(nl, -1)
    small["norm_final_g"] = dgf.reshape(-1)
    return loss, dx, small


def kernel(x, norm_mix_g, w_in, gate_bias, lru_conv_w, lru_conv_b, lru_wa, lru_ba, lru_wx, lru_bx, lru_lambda, attn_rel_bias, ssm_a_re, ssm_a_im, ssm_b_re, ssm_b_im, ssm_c_re, ssm_c_im, ssm_d, ssm_log_step, ssm_w_glu, w_branch, w_out, norm_ffn_g, w_ffn_gate, w_ffn_up, w_ffn_down, norm_final_g, loss_target, m_norm_mix_g, m_w_in, m_gate_bias, m_lru_conv_w, m_lru_conv_b, m_lru_wa, m_lru_ba, m_lru_wx, m_lru_bx, m_lru_lambda, m_attn_rel_bias, m_ssm_a_re, m_ssm_a_im, m_ssm_b_re, m_ssm_b_im, m_ssm_c_re, m_ssm_c_im, m_ssm_d, m_ssm_log_step, m_ssm_w_glu, m_w_branch, m_w_out, m_norm_ffn_g, m_w_ffn_gate, m_w_ffn_up, m_w_ffn_down, m_norm_final_g, v_norm_mix_g, v_w_in, v_gate_bias, v_lru_conv_w, v_lru_conv_b, v_lru_wa, v_lru_ba, v_lru_wx, v_lru_bx, v_lru_lambda, v_attn_rel_bias, v_ssm_a_re, v_ssm_a_im, v_ssm_b_re, v_ssm_b_im, v_ssm_c_re, v_ssm_c_im, v_ssm_d, v_ssm_log_step, v_ssm_w_glu, v_w_branch, v_w_out, v_norm_ffn_g, v_w_ffn_gate, v_w_ffn_up, v_w_ffn_down, v_norm_final_g):
    args = locals()
    wts = {k: args[k] for k in _WEIGHTS}
    mom = {k: args["m_" + k] for k in _WEIGHTS}
    vel = {k: args["v_" + k] for k in _WEIGHTS}
    cx, cy, cc = lax.axis_index("x"), lax.axis_index("y"), lax.axis_index("c")
    me = 4 * cx + 2 * cy + cc
    nl = norm_mix_g.shape[0]

    first = _all_gather([wts[k][0].astype(BF16) for k in _BIG] + [wts[k] for k in _SMALL_SHARDED], "gather_weights_0")
    small_gath = dict(zip(_SMALL_SHARDED, first[len(_BIG):]))
    in_flight = {}
    me_arr = me.astype(jnp.int32).reshape(1)
    layer_shape = {k: wts[k].shape[1:] for k in _BIG}

    def get_gath(l, x_in):
        if l == 0:
            gath = dict(zip(_BIG, first[:len(_BIG)]))
        else:
            lands = _split_wait("all", in_flight.pop(("w", l)), x_in, f"gather_wait_{l}")
            gath = {k: a.reshape((NDEV,) + layer_shape[k]) for k, a in zip(_BIG, lands)}
        if l + 1 == nl:
            return gath, None
        after = first[0] if l == 0 else x_in
        cast = [_cast_place(wts[k].reshape(nl, -1, wts[k].shape[-1]), l + 1, me_arr, after, f"cast_{k}_{l + 1}")
                for k in _BIG]
        started = _split_start("all", [a for a, _ in cast], [b for _, b in cast], f"gather_start_{l + 1}")
        in_flight[("w", l + 1)] = started[:-1]
        return gath, started[-1]

    core = cc.astype(jnp.int32).reshape(1)
    chip = (2 * cx + cy).astype(jnp.int32).reshape(1)

    def on_grads(l, gbuf):
        own = [gbuf[k].reshape((NCHIP, 2) + gbuf[k].shape[1:]) for k in _BIG]
        sib = _sibling_exchange(own, f"grad_sibling_exchange_{l}")
        pair, lands = [], []
        for k, a, b in zip(_BIG, own, sib):
            c_ = a.shape[-1]
            p, q = _pair_sum(a.reshape(NCHIP, 2, -1, c_), b.reshape(NCHIP, -1, c_), core, chip, f"pair_sum_{k}_{l}")
            pair.append(p)
            lands.append(q)
        started = _split_start("chips", pair, lands, f"grad_start_{l}")
        in_flight[("g", l)] = started[:-1]
        return started[-1]

    loss, dx, small = _device_step(x[0], loss_target[0], wts, small_gath, get_gath, on_grads)

    res = {k: None for k in _BIG}
    for l in range(nl - 1, -1, -1):
        parts = _split_wait("chips", in_flight.pop(("g", l)), dx, f"grad_wait_{l}")
        for k, p in zip(_BIG, parts):
            c_ = wts[k].shape[-1]
            res[k] = _adamw_reduce(p.reshape(NCHIP, -1, c_), wts[k].reshape(-1, c_), mom[k].reshape(-1, c_),
                                   vel[k].reshape(-1, c_), res[k], l, f"adamw_{k}_{l}")
    out = {k: tuple(a.reshape(wts[k].shape) for a in res[k]) for k in _BIG}

    order = _SMALL + _SMALL_SHARDED
    flat = jnp.concatenate([small[k].reshape(-1) for k in order])
    n_flat = flat.shape[0]
    flat = jnp.pad(flat, (0, (-n_flat) % (FLAT_ROWS * LANE))).reshape(-1, LANE)
    (allflat,) = _all_gather([flat], "gather_small_grads")
    gsum = _sum_rows8(allflat, "sum_small_grads").reshape(-1)
    gs, off = {}, 0
    for k in order:
        n = small[k].size
        gs[k] = gsum[off:off + n]
        off += n
    gs_loc = {}
    for k in _SMALL:
        gs_loc[k] = gs[k].reshape(wts[k].shape)
    ncb = gate_bias.shape[-1]
    gs_loc["gate_bias"] = lax.dynamic_slice_in_dim(gs["gate_bias"].reshape(nl, N_BRANCH, -1), me * ncb, ncb, axis=2)
    ncw = lru_conv_w.shape[-1]
    gs_loc["lru_conv_w"] = lax.dynamic_slice_in_dim(gs["lru_conv_w"].reshape(nl, CONV_W, -1), me * ncw, ncw, axis=2)

    def pack(dct):
        v = jnp.concatenate([dct[k].reshape(-1) for k in order])
        return jnp.pad(v, (0, (-v.shape[0]) % (FLAT_ROWS * LANE))).reshape(-1, LANE)

    dl_f, nm_f, nv_f = _adamw_flat(pack(wts), pack(gs_loc), pack(mom), pack(vel), "adamw_small")
    off = 0
    for k in order:
        n = wts[k].size
        sl = lambda a: a.reshape(-1)[off:off + n].reshape(wts[k].shape)
        out[k] = (gs_loc[k], sl(dl_f), sl(nm_f), sl(nv_f))
        off += n

    loss_total = lax.psum(loss[0, 0], ("x", "y", "c"))
    return (loss_total, dx[None], *[out[k][0] for k in _WEIGHTS], *[out[k][1] for k in _WEIGHTS],
            *[out[k][2] for k in _WEIGHTS], *[out[k][3] for k in _WEIGHTS])
```

```python
import functools
import math

import numpy as np
import jax
import jax.numpy as jnp
from jax import lax
from jax.experimental import pallas as pl
from jax.experimental.pallas import tpu as pltpu

F32 = jnp.float32
BF16 = jnp.bfloat16
LANE = 128
NSEG = 8
NDEV = 8
NCHIP = 4
MESH = pl.DeviceIdType.MESH
VMEM_LIMIT = 56 * 1024 * 1024
FLAT_ROWS = 512

NORM_EPS = 1e-6
CHUNK = 64
ATT_LEFT = 8
ATT_BAND = (ATT_LEFT + 1) * CHUNK
MAX_REL = 128
N_REL = 2 * MAX_REL + 1
N_REL_PAD = 384
ATT_HEADS = 8
MASK_VALUE = -1e30
LRU_C = 8.0
LRU_BLOCKS = 16
SSM_GROUP = 16
SSM_P = 64
CONV_W = 4
N_BRANCH = 3

ADAM_LR = 0.001
ADAM_B1 = 0.9
ADAM_B2 = 0.999
ADAM_EPS = 1e-08
ADAM_WD = 0.01
ADAM_STEP = 10

NN = (((1,), (0,)), ((), ()))
NT = (((1,), (1,)), ((), ()))
TN = (((0,), (0,)), ((), ()))


def _cp(sem=None, vmem=VMEM_LIMIT, **kw):
    return pltpu.CompilerParams(dimension_semantics=sem, vmem_limit_bytes=vmem, **kw)


def _dot(a, b, dn=NN):
    return lax.dot_general(a, b, dn, preferred_element_type=F32)


def _gelu(x):
    c = math.sqrt(2.0 / math.pi)
    return 0.5 * x * (1.0 + jnp.tanh(c * (x + 0.044715 * x * x * x)))


def _gelu_grad(x):
    c = math.sqrt(2.0 / math.pi)
    t = jnp.tanh(c * (x + 0.044715 * x * x * x))
    return 0.5 * (1.0 + t) + 0.5 * x * (1.0 - t * t) * c * (1.0 + 3.0 * 0.044715 * x * x)


def _sigmoid(x):
    return 1.0 / (1.0 + jnp.exp(-x))


def _one_minus_exp(z):
    series = -(z * (1.0 + z * (0.5 + z * (1.0 / 6.0 + z * (1.0 / 24.0)))))
    return jnp.where(z > -0.02, series, 1.0 - jnp.exp(z))


def _softplus_neg(lam):
    e = jnp.exp(-jnp.abs(lam))
    series = e * (1.0 - e * (0.5 - e * (1.0 / 3.0 - e * 0.25)))
    log1p_e = jnp.where(e < 0.02, series, jnp.log(1.0 + e))
    return jnp.maximum(-lam, 0.0) + log1p_e


def _mm(name, dn, grid, n_red, ins, in_specs, out_shape, out_spec, acc_shape, epi=None, alias=None):
    n_extra = len(ins) - 2 - (1 if alias is not None else 0)
    red_axes = tuple(range(len(grid) - n_red, len(grid)))
    red_sizes = tuple(grid[r] for r in red_axes)

    def body(*refs):
        a_ref, b_ref = refs[0], refs[1]
        extra = refs[2:2 + n_extra]
        o_ref, acc = refs[-2], refs[-1]
        if n_red:
            first = functools.reduce(jnp.logical_and, [pl.program_id(r) == 0 for r in red_axes])
            last = functools.reduce(jnp.logical_and,
                                    [pl.program_id(r) == n - 1 for r, n in zip(red_axes, red_sizes)])

            @pl.when(first)
            def _():
                acc[...] = jnp.zeros_like(acc)

            acc[...] += _dot(a_ref[...], b_ref[...], dn)

            @pl.when(last)
            def _():
                r = acc[...]
                if epi is not None:
                    r = epi(r, *[e[...] for e in extra])
                o_ref[...] = r.astype(o_ref.dtype)
        else:
            r = _dot(a_ref[...], b_ref[...], dn)
            if epi is not None:
                r = epi(r, *[e[...] for e in extra])
            o_ref[...] = r.astype(o_ref.dtype)

    specs = list(in_specs)
    kw = {}
    if alias is not None:
        specs.append(pl.BlockSpec(memory_space=pl.ANY))
        kw["input_output_aliases"] = {len(ins) - 1: 0}
    sem = ("parallel",) * (len(grid) - n_red) + ("arbitrary",) * n_red
    return pl.pallas_call(
        body, name=name, grid=grid, in_specs=specs, out_specs=out_spec, out_shape=out_shape,
        scratch_shapes=[pltpu.VMEM(acc_shape, F32)], compiler_params=_cp(sem), **kw)(*ins)


def _loop(n, body, init, unroll=4):
    def block(kb, c):
        for i in range(unroll):
            c = body(kb * unroll + i, c)
        return c

    c = lax.fori_loop(0, n // unroll, block, init)
    for k in range(n - n % unroll, n):
        c = body(k, c)
    return c


def _tile(n, pref):
    if n <= pref:
        return n
    t = pref
    while t >= LANE:
        if n % t == 0 and t % LANE == 0:
            return t
        t -= LANE
    return n


def _rmsnorm_fwd(x, g, name):
    t, d = x.shape
    tm = min(512, t)

    def body(x_ref, g_ref, h_ref):
        xv = x_ref[...]
        r = lax.rsqrt(jnp.mean(xv * xv, axis=-1, keepdims=True) + NORM_EPS)
        h_ref[...] = (xv * r * g_ref[...]).astype(h_ref.dtype)

    return pl.pallas_call(
        body, name=name, grid=(t // tm,),
        in_specs=[pl.BlockSpec((tm, d), lambda i: (i, 0)), pl.BlockSpec((1, d), lambda i: (0, 0))],
        out_specs=pl.BlockSpec((tm, d), lambda i: (i, 0)),
        out_shape=jax.ShapeDtypeStruct((t, d), BF16), compiler_params=_cp(("parallel",)))(x, g)


def _rmsnorm_bwd(x, g, dhs, dres, name):
    t, d = x.shape
    tm = min(256, t)
    n_dh = len(dhs)

    def body(*refs):
        x_ref, g_ref = refs[0], refs[1]
        dh_refs = refs[2:2 + n_dh]
        dres_ref = refs[2 + n_dh]
        dx_ref, dxb_ref, dg_ref = refs[3 + n_dh:]
        xv = x_ref[...]
        r = lax.rsqrt(jnp.mean(xv * xv, axis=-1, keepdims=True) + NORM_EPS)
        xhat = xv * r
        dh = dh_refs[0][...].astype(F32)
        for e in dh_refs[1:]:
            dh = dh + e[...].astype(F32)
        dxh = dh * g_ref[...]
        dx = r * (dxh - xhat * jnp.mean(dxh * xhat, axis=-1, keepdims=True)) + dres_ref[...]
        dx_ref[...] = dx
        dxb_ref[...] = dx.astype(BF16)

        @pl.when(pl.program_id(0) == 0)
        def _():
            dg_ref[...] = jnp.zeros_like(dg_ref)

        dg_ref[...] += jnp.sum(dh * xhat, axis=0, keepdims=True)

    row = pl.BlockSpec((tm, d), lambda i: (i, 0))
    par = pl.BlockSpec((1, d), lambda i: (0, 0))
    return pl.pallas_call(
        body, name=name, grid=(t // tm,),
        in_specs=[row, par] + [row] * n_dh + [row],
        out_specs=[row, row, par],
        out_shape=[jax.ShapeDtypeStruct((t, d), F32), jax.ShapeDtypeStruct((t, d), BF16),
                   jax.ShapeDtypeStruct((1, d), F32)],
        compiler_params=_cp(("arbitrary",)))(x, g, *dhs, dres)


def _loss_head(x, g, target, name):
    t, d = x.shape
    tm = min(256, t)

    def body(x_ref, g_ref, t_ref, loss_ref, dx_ref, dxb_ref, dg_ref):
        xv = x_ref[...]
        r = lax.rsqrt(jnp.mean(xv * xv, axis=-1, keepdims=True) + NORM_EPS)
        xhat = xv * r
        y = xhat * g_ref[...]
        err = y - t_ref[...]
        dy = err * (1.0 / d)
        dxh = dy * g_ref[...]
        dx = r * (dxh - xhat * jnp.mean(dxh * xhat, axis=-1, keepdims=True))
        dx_ref[...] = dx
        dxb_ref[...] = dx.astype(BF16)

        @pl.when(pl.program_id(0) == 0)
        def _():
            dg_ref[...] = jnp.zeros_like(dg_ref)
            loss_ref[...] = jnp.zeros_like(loss_ref)

        dg_ref[...] += jnp.sum(dy * xhat, axis=0, keepdims=True)
        per_tok = jnp.sum(err * err, axis=-1, keepdims=True) * (0.5 / d)
        loss_ref[...] += jnp.sum(per_tok, axis=0, keepdims=True)

    row = pl.BlockSpec((tm, d), lambda i: (i, 0))
    par = pl.BlockSpec((1, d), lambda i: (0, 0))
    return pl.pallas_call(
        body, name=name, grid=(t // tm,),
        in_specs=[row, par, row],
        out_specs=[pl.BlockSpec((1, 1), lambda i: (0, 0)), row, row, par],
        out_shape=[jax.ShapeDtypeStruct((1, 1), F32), jax.ShapeDtypeStruct((t, d), F32),
                   jax.ShapeDtypeStruct((t, d), BF16), jax.ShapeDtypeStruct((1, d), F32)],
        compiler_params=_cp(("arbitrary",)))(x, g, target)


def _merge_fwd(proj, gbias, br_a, br_b, br_c0, gl, d, name):
    t = proj.shape[0]
    tm = min(128, t)
    off = proj.shape[1] // d - N_BRANCH

    def body(g0, g1, g2, gb, a_ref, b_ref, c_ref, gl_ref, o_ref):
        gbv = gb[...]
        s0 = _sigmoid(g0[...].astype(F32) + gbv[0:1])
        s1 = _sigmoid(g1[...].astype(F32) + gbv[1:2])
        s2 = _sigmoid(g2[...].astype(F32) + gbv[2:3])
        brc = c_ref[...].astype(F32) * _sigmoid(gl_ref[...].astype(F32))
        o_ref[...] = (s0 * a_ref[...].astype(F32) + s1 * b_ref[...].astype(F32) + s2 * brc).astype(BF16)

    row = pl.BlockSpec((tm, d), lambda i: (i, 0))
    gs = [pl.BlockSpec((tm, d), functools.partial(lambda i, b: (i, off + b), b=b)) for b in range(N_BRANCH)]
    return pl.pallas_call(
        body, name=name, grid=(t // tm,),
        in_specs=gs + [pl.BlockSpec((N_BRANCH, d), lambda i: (0, 0)), row, row, row, row],
        out_specs=row, out_shape=jax.ShapeDtypeStruct((t, d), BF16),
        compiler_params=_cp(("parallel",)))(proj, proj, proj, gbias, br_a, br_b, br_c0, gl)


def _merge_bwd(proj, gbias, br_a, br_b, br_c0, gl, dmerged, d, name):
    t = proj.shape[0]
    tm = min(128, t)
    off = proj.shape[1] // d - N_BRANCH

    def body(g0, g1, g2, gb, a_ref, b_ref, c_ref, gl_ref, dm_ref, da_ref, db_ref, dc_ref, dgl_ref, dg_ref, dgb_ref):
        gbv = gb[...]
        dm = dm_ref[...].astype(F32)
        s0 = _sigmoid(g0[...].astype(F32) + gbv[0:1])
        s1 = _sigmoid(g1[...].astype(F32) + gbv[1:2])
        s2 = _sigmoid(g2[...].astype(F32) + gbv[2:3])
        sg = _sigmoid(gl_ref[...].astype(F32))
        c0 = c_ref[...].astype(F32)
        brc = c0 * sg
        da_ref[...] = (dm * s0).astype(BF16)
        db_ref[...] = (dm * s1).astype(BF16)
        dbrc = dm * s2
        dc_ref[...] = (dbrc * sg).astype(BF16)
        dgl_ref[...] = (dbrc * c0 * sg * (1.0 - sg)).astype(BF16)
        dp0 = dm * a_ref[...].astype(F32) * s0 * (1.0 - s0)
        dp1 = dm * b_ref[...].astype(F32) * s1 * (1.0 - s1)
        dp2 = dm * brc * s2 * (1.0 - s2)
        dg_ref[:, 0:d] = dp0.astype(BF16)
        dg_ref[:, d:2 * d] = dp1.astype(BF16)
        dg_ref[:, 2 * d:3 * d] = dp2.astype(BF16)

        @pl.when(pl.program_id(0) == 0)
        def _():
            dgb_ref[...] = jnp.zeros_like(dgb_ref)

        dgb_ref[0:1, :] += jnp.sum(dp0, axis=0, keepdims=True)
        dgb_ref[1:2, :] += jnp.sum(dp1, axis=0, keepdims=True)
        dgb_ref[2:3, :] += jnp.sum(dp2, axis=0, keepdims=True)

    row = pl.BlockSpec((tm, d), lambda i: (i, 0))
    gs = [pl.BlockSpec((tm, d), functools.partial(lambda i, b: (i, off + b), b=b)) for b in range(N_BRANCH)]
    par = pl.BlockSpec((N_BRANCH, d), lambda i: (0, 0))
    bf = jax.ShapeDtypeStruct((t, d), BF16)
    return pl.pallas_call(
        body, name=name, grid=(t // tm,),
        in_specs=gs + [par, row, row, row, row, row],
        out_specs=[row, row, row, row, pl.BlockSpec((tm, N_BRANCH * d), lambda i: (i, 0)), par],
        out_shape=[bf, bf, bf, bf, jax.ShapeDtypeStruct((t, N_BRANCH * d), BF16),
                   jax.ShapeDtypeStruct((N_BRANCH, d), F32)],
        compiler_params=_cp(("arbitrary",)))(proj, proj, proj, gbias, br_a, br_b, br_c0, gl, dmerged)


def _ffn_in(h2, wg, wu, name):
    t, d = h2.shape
    f = wg.shape[-1]
    tm = _tile(t, 1024)

    def body(h_ref, g_ref, u_ref, gp_ref, up_ref, act_ref):
        h = h_ref[...]
        g = _dot(h, g_ref[...])
        u = _dot(h, u_ref[...])
        gp_ref[...] = g.astype(BF16)
        up_ref[...] = u.astype(BF16)
        act_ref[...] = (g * _sigmoid(g) * u).astype(BF16)

    wspec = pl.BlockSpec((None, d, f), lambda i, j: (j, 0, 0))
    ospec = pl.BlockSpec((None, tm, f), lambda i, j: (j, i, 0))
    o = jax.ShapeDtypeStruct((NDEV, t, f), BF16)
    return pl.pallas_call(
        body, name=name, grid=(t // tm, NDEV), in_specs=[pl.BlockSpec((tm, d), lambda i, j: (i, 0)), wspec, wspec],
        out_specs=[ospec, ospec, ospec], out_shape=[o, o, o],
        compiler_params=_cp(("parallel", "parallel")))(h2, wg, wu)


def _ffn_dact(dxb, wd, gp, up, name):
    t, d = dxb.shape
    f = wd.shape[1]
    tm = _tile(t, 1024)

    def body(dx_ref, w_ref, g_ref, u_ref, dg_ref, du_ref):
        da = _dot(dx_ref[...], w_ref[...], NT)
        g = g_ref[...].astype(F32)
        u = u_ref[...].astype(F32)
        sg = _sigmoid(g)
        silu = g * sg
        du_ref[...] = (da * silu).astype(BF16)
        dg_ref[...] = (da * u * (sg + silu * (1.0 - sg))).astype(BF16)

    ospec = pl.BlockSpec((None, tm, f), lambda i, j: (j, i, 0))
    o = jax.ShapeDtypeStruct((NDEV, t, f), BF16)
    return pl.pallas_call(
        body, name=name, grid=(t // tm, NDEV),
        in_specs=[pl.BlockSpec((tm, d), lambda i, j: (i, 0)), pl.BlockSpec((None, f, d), lambda i, j: (j, 0, 0)), ospec, ospec],
        out_specs=[ospec, ospec], out_shape=[o, o],
        compiler_params=_cp(("parallel", "parallel")))(dxb, wd, gp, up)


def _ffn_dh(dgp, dup, wg, wu, name):
    _, t, f = dgp.shape
    d = wg.shape[1]
    tm = _tile(t, 1024)
    tk = _tile(d, 1024)

    def body(g_ref, u_ref, wg_ref, wu_ref, o_ref, acc):
        j = pl.program_id(2)

        @pl.when(j == 0)
        def _():
            acc[...] = jnp.zeros_like(acc)

        acc[...] += _dot(g_ref[...], wg_ref[...], NT) + _dot(u_ref[...], wu_ref[...], NT)

        @pl.when(j == NDEV - 1)
        def _():
            o_ref[...] = acc[...]

    zspec = pl.BlockSpec((None, tm, f), lambda i, k, j: (j, i, 0))
    wspec = pl.BlockSpec((None, tk, f), lambda i, k, j: (j, k, 0))
    return pl.pallas_call(
        body, name=name, grid=(t // tm, d // tk, NDEV), in_specs=[zspec, zspec, wspec, wspec],
        out_specs=pl.BlockSpec((tm, tk), lambda i, k, j: (i, k)), out_shape=jax.ShapeDtypeStruct((t, d), F32),
        scratch_shapes=[pltpu.VMEM((tm, tk), F32)],
        compiler_params=_cp(("parallel", "parallel", "arbitrary")))(dgp, dup, wg, wu)


def _gelu_bwd_sum(s5o, dy1, dy2, name):
    t, w = s5o.shape
    tm = min(512, t)

    def body(s_ref, a_ref, b_ref, o_ref):
        o_ref[...] = ((a_ref[...] + b_ref[...]) * _gelu_grad(s_ref[...])).astype(BF16)

    row = pl.BlockSpec((tm, w), lambda i: (i, 0))
    return pl.pallas_call(
        body, name=name, grid=(t // tm,), in_specs=[row, row, row], out_specs=row,
        out_shape=jax.ShapeDtypeStruct((t, w), BF16), compiler_params=_cp(("parallel",)))(s5o, dy1, dy2)


def _lru_gates(xc, wa, ba, wx, bx, sp):
    xcb = xc.astype(BF16)
    r = _sigmoid(_dot(xcb, wa) + ba)
    i = _sigmoid(_dot(xcb, wx) + bx)
    la = -LRU_C * r * sp
    return xcb, r, i, la


def _lru_fwd(proj, cw, cb, wa_t, ba, wx_t, bx, lam, w, name):
    t = proj.shape[0]
    nt = w // LANE
    seg = t // NSEG
    rb = min(512, seg)

    def body(lx_ref, lg_ref, cw_ref, cb_ref, wa_ref, ba_ref, wx_ref, bx_ref, lam_ref, ya_ref, hs_ref, xp, a_s, b_s):
        xp[pl.ds(0, 8), :] = jnp.zeros((8, LANE), F32)
        for r0 in range(0, t, rb):
            xp[pl.ds(8 + r0, rb), :] = lx_ref[pl.ds(r0, rb), :].astype(F32)
        sp = _softplus_neg(lam_ref[...])
        cwv = cw_ref[...]
        for r0 in range(0, t, rb):
            xc = cb_ref[...] + sum(cwv[k:k + 1] * xp[pl.ds(8 + r0 - (CONV_W - 1) + k, rb), :] for k in range(CONV_W))
            _, _, i, la = _lru_gates(xc, wa_ref[...], ba_ref[...], wx_ref[...], bx_ref[...], sp)
            a_s[pl.ds(r0, rb), :] = jnp.exp(la)
            b_s[pl.ds(r0, rb), :] = jnp.sqrt(_one_minus_exp(2.0 * la)) * (i * xc)

        def load(k):
            rows = pl.ds(k, NSEG, stride=seg)
            return a_s[rows, :], b_s[rows, :]

        def step(k, c):
            h, p, a, b = c
            na, nb = load(jnp.minimum(k + 1, seg - 1))
            rows = pl.ds(k, NSEG, stride=seg)
            h = a * h + b
            p = a * p
            b_s[rows, :] = h
            a_s[rows, :] = p
            return h, p, na, nb

        _loop(seg, step, (jnp.zeros((NSEG, LANE), F32), jnp.ones((NSEG, LANE), F32)) + load(0))
        carry = jnp.zeros((1, LANE), F32)
        for s in range(NSEG):
            for r0 in range(s * seg, (s + 1) * seg, rb):
                rows = pl.ds(r0, rb)
                h = b_s[rows, :] + a_s[rows, :] * carry
                hs_ref[rows, :] = h
                ya_ref[rows, :] = (h * _gelu(lg_ref[rows, :].astype(F32))).astype(BF16)
            end = pl.ds((s + 1) * seg - 1, 1)
            carry = b_s[end, :] + a_s[end, :] * carry

    col = lambda c0: pl.BlockSpec((t, LANE), functools.partial(lambda i, c0: (0, c0 + i), c0=c0))
    par = lambda k: pl.BlockSpec((k, LANE), lambda i: (0, i))
    mat = pl.BlockSpec((None, LANE, LANE), lambda i: (i, 0, 0))
    return pl.pallas_call(
        body, name=name, grid=(nt,),
        in_specs=[col(0), col(nt), par(CONV_W), par(1), mat, par(1), mat, par(1), par(1)],
        out_specs=[col(0), col(0)],
        out_shape=[jax.ShapeDtypeStruct((t, w), BF16), jax.ShapeDtypeStruct((t, w), F32)],
        scratch_shapes=[pltpu.VMEM((t + 8, LANE), F32), pltpu.VMEM((t, LANE), F32), pltpu.VMEM((t, LANE), F32)],
        compiler_params=_cp(("parallel",)))(proj, proj, cw, cb, wa_t, ba, wx_t, bx, lam)


def _lru_bwd(proj, hs, dya, cw, cb, wa_t, ba, wx_t, bx, lam, wat_t, wxt_t, w, name):
    t = proj.shape[0]
    nt = w // LANE
    seg = t // NSEG
    rb = min(512, seg)

    def body(lx_ref, lg_ref, hs_ref, dy_ref, cw_ref, cb_ref, wa_ref, ba_ref, wx_ref, bx_ref, lam_ref, wat_ref, wxt_ref,
             dlx_ref, dlg_ref, dcw_ref, dcb_ref, dwa_ref, dba_ref, dwx_ref, dbx_ref, dlam_ref,
             xp, hp, a_s, g_s, q_s, dxc_s):
        z8 = jnp.zeros((8, LANE), F32)
        xp[pl.ds(0, 8), :] = z8
        hp[pl.ds(0, 8), :] = z8
        a_s[pl.ds(t, 8), :] = z8
        dxc_s[pl.ds(t, 8), :] = z8
        lamv = lam_ref[...]
        sp = _softplus_neg(lamv)
        cwv = cw_ref[...]

        def conv(r0):
            return cb_ref[...] + sum(cwv[k:k + 1] * xp[pl.ds(8 + r0 - (CONV_W - 1) + k, rb), :] for k in range(CONV_W))

        for r0 in range(0, t, rb):
            rows = pl.ds(r0, rb)
            xp[pl.ds(8 + r0, rb), :] = lx_ref[rows, :].astype(F32)
            hp[pl.ds(8 + r0, rb), :] = hs_ref[rows, :]
        for r0 in range(0, t, rb):
            rows = pl.ds(r0, rb)
            _, _, _, la = _lru_gates(conv(r0), wa_ref[...], ba_ref[...], wx_ref[...], bx_ref[...], sp)
            a_s[rows, :] = jnp.exp(la)
            g_s[rows, :] = dy_ref[rows, :].astype(F32) * _gelu(lg_ref[rows, :].astype(F32))

        def load(k):
            return a_s[pl.ds(k + 1, NSEG, stride=seg), :], g_s[pl.ds(k, NSEG, stride=seg), :]

        def step(kk, c):
            g, q, an, dh = c
            k = seg - 1 - kk
            nan, ndh = load(jnp.maximum(k - 1, 0))
            rows = pl.ds(k, NSEG, stride=seg)
            g = dh + an * g
            q = an * q
            g_s[rows, :] = g
            q_s[rows, :] = q
            return g, q, nan, ndh

        _loop(seg, step, (jnp.zeros((NSEG, LANE), F32), jnp.ones((NSEG, LANE), F32)) + load(seg - 1))
        nxt = [None] * NSEG
        carry = jnp.zeros((1, LANE), F32)
        for s in range(NSEG - 1, -1, -1):
            nxt[s] = carry
            start = pl.ds(s * seg, 1)
            carry = g_s[start, :] + q_s[start, :] * carry

        zrow = jnp.zeros((1, LANE), F32)
        dsp = zrow
        dba = zrow
        dbx = zrow
        dcb = zrow
        dwa = jnp.zeros((LANE, LANE), F32)
        dwx = jnp.zeros((LANE, LANE), F32)
        for s in range(NSEG):
            for r0 in range(s * seg, (s + 1) * seg, rb):
                rows = pl.ds(r0, rb)
                g = g_s[rows, :] + q_s[rows, :] * nxt[s]
                xc = conv(r0)
                xcb, r, i, la = _lru_gates(xc, wa_ref[...], ba_ref[...], wx_ref[...], bx_ref[...], sp)
                a = jnp.exp(la)
                om = _one_minus_exp(2.0 * la)
                mult = jnp.sqrt(om)
                hprev = hp[pl.ds(8 + r0 - 1, rb), :]
                da = g * hprev
                dmult = g * i * xc
                di = g * mult * xc
                dxc = g * mult * i
                dla = da * a - dmult * (1.0 - om) / mult
                dr = dla * (-LRU_C) * sp
                dsp = dsp + jnp.sum(dla * (-LRU_C) * r, axis=0, keepdims=True)
                dpr = dr * r * (1.0 - r)
                dpi = di * i * (1.0 - i)
                dba = dba + jnp.sum(dpr, axis=0, keepdims=True)
                dbx = dbx + jnp.sum(dpi, axis=0, keepdims=True)
                dprb = dpr.astype(BF16)
                dpib = dpi.astype(BF16)
                dxc = dxc + _dot(dprb, wat_ref[...]) + _dot(dpib, wxt_ref[...])
                dwa = dwa + _dot(xcb, dprb, TN)
                dwx = dwx + _dot(xcb, dpib, TN)
                dxc_s[rows, :] = dxc
                dcb = dcb + jnp.sum(dxc, axis=0, keepdims=True)
                lg = lg_ref[rows, :].astype(F32)
                dlg_ref[rows, :] = (dy_ref[rows, :].astype(F32) * hs_ref[rows, :] * _gelu_grad(lg)).astype(BF16)
        dcw = [zrow] * CONV_W
        for r0 in range(0, t, rb):
            rows = pl.ds(r0, rb)
            dlx = sum(cwv[k:k + 1] * dxc_s[pl.ds(r0 + (CONV_W - 1) - k, rb), :] for k in range(CONV_W))
            dlx_ref[rows, :] = dlx.astype(BF16)
            dxc = dxc_s[rows, :]
            for k in range(CONV_W):
                dcw[k] = dcw[k] + jnp.sum(dxc * xp[pl.ds(8 + r0 - (CONV_W - 1) + k, rb), :], axis=0, keepdims=True)
        dcw_ref[...] = jnp.concatenate(dcw, axis=0)
        dcb_ref[...] = dcb
        dwa_ref[...] = dwa
        dwx_ref[...] = dwx
        dba_ref[...] = dba
        dbx_ref[...] = dbx
        dlam_ref[...] = dsp * (-_sigmoid(-lamv))

    col = lambda c0: pl.BlockSpec((t, LANE), functools.partial(lambda i, c0: (0, c0 + i), c0=c0))
    par = lambda k: pl.BlockSpec((k, LANE), lambda i: (0, i))
    mat = pl.BlockSpec((None, LANE, LANE), lambda i: (i, 0, 0))
    vec = jax.ShapeDtypeStruct((1, w), F32)
    big = lambda: pltpu.VMEM((t + 8, LANE), F32)
    return pl.pallas_call(
        body, name=name, grid=(nt,),
        in_specs=[col(0), col(nt), col(0), col(0), par(CONV_W), par(1), mat, par(1), mat, par(1), par(1), mat, mat],
        out_specs=[col(0), col(0), par(CONV_W), par(1), mat, par(1), mat, par(1), par(1)],
        out_shape=[jax.ShapeDtypeStruct((t, w), BF16), jax.ShapeDtypeStruct((t, w), BF16),
                   jax.ShapeDtypeStruct((CONV_W, w), F32), vec, jax.ShapeDtypeStruct((nt, LANE, LANE), F32), vec,
                   jax.ShapeDtypeStruct((nt, LANE, LANE), F32), vec, vec],
        scratch_shapes=[big(), big(), big(), pltpu.VMEM((t, LANE), F32), pltpu.VMEM((t, LANE), F32), big()],
        compiler_params=_cp(("parallel",)))(proj, proj, hs, dya, cw, cb, wa_t, ba, wx_t, bx, lam, wat_t, wxt_t)


def _attn_scores(q_ref, kp, bias_ref, c, hd):
    r0 = pl.multiple_of(c * CHUNK, CHUNK)
    qc = q_ref[pl.ds(r0, CHUNK), :]
    kb = kp[pl.ds(r0, ATT_BAND), :]
    s = _dot(qc, kb, NT) * (hd ** -0.5) + bias_ref[...]
    kpos = lax.broadcasted_iota(jnp.int32, (CHUNK, ATT_BAND), 1)
    s = jnp.where(kpos + (c - ATT_LEFT) * CHUNK >= 0, s, MASK_VALUE)
    m = jnp.max(s, axis=-1, keepdims=True)
    e = jnp.exp(s - m)
    p = e / jnp.sum(e, axis=-1, keepdims=True)
    return r0, qc, kb, p


def _attn_pad_copy(src_ref, dst, t, pad):
    dst[pl.ds(0, pad), :] = jnp.zeros((pad, dst.shape[1]), dst.dtype)
    rb = min(512, t)
    for r0 in range(0, t, rb):
        dst[pl.ds(pad + r0, rb), :] = src_ref[pl.ds(r0, rb), :]


def _attn_fwd(proj, bias, w, name):
    t = proj.shape[0]
    hd = w // ATT_HEADS
    pad = ATT_LEFT * CHUNK
    qo, ko, vo = 2 * w // hd, 3 * w // hd, 4 * w // hd

    def body(q_ref, k_ref, v_ref, bias_ref, o_ref, kp, vp):
        _attn_pad_copy(k_ref, kp, t, pad)
        _attn_pad_copy(v_ref, vp, t, pad)

        def chunk(c, _):
            r0, _, _, p = _attn_scores(q_ref, kp, bias_ref, c, hd)
            vb = vp[pl.ds(r0, ATT_BAND), :]
            o_ref[pl.ds(r0, CHUNK), :] = _dot(p.astype(BF16), vb).astype(BF16)
            return 0

        lax.fori_loop(0, t // CHUNK, chunk, 0, unroll=2)

    col = lambda c0: pl.BlockSpec((t, hd), functools.partial(lambda h, c0: (0, c0 + h), c0=c0))
    return pl.pallas_call(
        body, name=name, grid=(ATT_HEADS,),
        in_specs=[col(qo), col(ko), col(vo), pl.BlockSpec((None, CHUNK, ATT_BAND), lambda h: (h, 0, 0))],
        out_specs=col(0), out_shape=jax.ShapeDtypeStruct((t, w), BF16),
        scratch_shapes=[pltpu.VMEM((t + pad, hd), BF16), pltpu.VMEM((t + pad, hd), BF16)],
        compiler_params=_cp(("parallel",)))(proj, proj, proj, bias)


def _attn_bwd(proj, bias, do, w, name):
    t = proj.shape[0]
    hd = w // ATT_HEADS
    pad = ATT_LEFT * CHUNK
    qo, ko, vo = 2 * w // hd, 3 * w // hd, 4 * w // hd

    def body(q_ref, k_ref, v_ref, bias_ref, do_ref, dq_ref, dk_ref, dv_ref, db_ref, kp, vp, dkp, dvp):
        _attn_pad_copy(k_ref, kp, t, pad)
        _attn_pad_copy(v_ref, vp, t, pad)
        rb = min(512, t)
        for r0 in range(0, t + pad, rb):
            n = min(rb, t + pad - r0)
            dkp[pl.ds(r0, n), :] = jnp.zeros((n, hd), F32)
            dvp[pl.ds(r0, n), :] = jnp.zeros((n, hd), F32)
        db_ref[...] = jnp.zeros_like(db_ref)
        scale = hd ** -0.5

        def chunk(c, _):
            r0, qc, kb, p = _attn_scores(q_ref, kp, bias_ref, c, hd)
            band = pl.ds(r0, ATT_BAND)
            vb = vp[band, :]
            doc = do_ref[pl.ds(r0, CHUNK), :]
            dp = _dot(doc, vb, NT)
            ds = p * (dp - jnp.sum(dp * p, axis=-1, keepdims=True))
            db_ref[...] += ds
            dsb = ds.astype(BF16)
            dq_ref[pl.ds(r0, CHUNK), :] = (_dot(dsb, kb) * scale).astype(BF16)
            dkp[band, :] += _dot(dsb, qc, TN) * scale
            dvp[band, :] += _dot(p.astype(BF16), doc, TN)
            return 0

        lax.fori_loop(0, t // CHUNK, chunk, 0, unroll=2)
        for r0 in range(0, t, rb):
            dk_ref[pl.ds(r0, rb), :] = dkp[pl.ds(pad + r0, rb), :].astype(BF16)
            dv_ref[pl.ds(r0, rb), :] = dvp[pl.ds(pad + r0, rb), :].astype(BF16)

    col = lambda c0: pl.BlockSpec((t, hd), functools.partial(lambda h, c0: (0, c0 + h), c0=c0))
    tb = pl.BlockSpec((None, CHUNK, ATT_BAND), lambda h: (h, 0, 0))
    o = jax.ShapeDtypeStruct((t, w), BF16)
    return pl.pallas_call(
        body, name=name, grid=(ATT_HEADS,),
        in_specs=[col(qo), col(ko), col(vo), tb, col(0)],
        out_specs=[col(0), col(0), col(0), tb],
        out_shape=[o, o, o, jax.ShapeDtypeStruct((ATT_HEADS, CHUNK, ATT_BAND), F32)],
        scratch_shapes=[pltpu.VMEM((t + pad, hd), BF16), pltpu.VMEM((t + pad, hd), BF16),
                        pltpu.VMEM((t + pad, hd), F32), pltpu.VMEM((t + pad, hd), F32)],
        compiler_params=_cp(("parallel",)))(proj, proj, proj, bias, do)


def _rel_index():
    q_pos = ATT_LEFT * CHUNK + np.arange(CHUNK)
    k_pos = np.arange(ATT_BAND)
    return (np.clip(q_pos[:, None] - k_pos[None, :], -MAX_REL, MAX_REL) + MAX_REL).astype(np.int32).reshape(-1)


def _bias_expand(rel_bias, name):
    nl, nh, _ = rel_bias.shape
    n = CHUNK * ATT_BAND
    kb = n // 8
    idx = jnp.asarray(_rel_index().reshape(1, n))
    tab = jnp.pad(rel_bias, ((0, 0), (0, 0), (0, N_REL_PAD - N_REL)))

    def body(t_ref, i_ref, o_ref):
        onehot = (lax.broadcasted_iota(jnp.int32, (N_REL_PAD, kb), 0) == i_ref[...]).astype(F32)
        o_ref[...] = lax.dot_general(t_ref[...], onehot, NN, precision=lax.Precision.HIGHEST,
                                     preferred_element_type=F32)

    out = pl.pallas_call(
        body, name=name, grid=(nl, n // kb),
        in_specs=[pl.BlockSpec((None, nh, N_REL_PAD), lambda l, j: (l, 0, 0)), pl.BlockSpec((1, kb), lambda l, j: (0, j))],
        out_specs=pl.BlockSpec((None, nh, kb), lambda l, j: (l, 0, j)),
        out_shape=jax.ShapeDtypeStruct((nl, nh, n), F32), compiler_params=_cp(("parallel", "parallel")))(tab, idx)
    return out.reshape(nl, nh, CHUNK, ATT_BAND)


def _bias_reduce(dbias, name):
    nl, nh = dbias.shape[:2]
    n = CHUNK * ATT_BAND
    kb = n // 8
    idx = jnp.asarray(_rel_index().reshape(n, 1))

    def body(d_ref, i_ref, o_ref):
        @pl.when(pl.program_id(1) == 0)
        def _():
            o_ref[...] = jnp.zeros_like(o_ref)

        onehot = (lax.broadcasted_iota(jnp.int32, (kb, N_REL_PAD), 1) == i_ref[...]).astype(F32)
        o_ref[...] += lax.dot_general(d_ref[...], onehot, NN, precision=lax.Precision.HIGHEST,
                                      preferred_element_type=F32)

    out = pl.pallas_call(
        body, name=name, grid=(nl, n // kb),
        in_specs=[pl.BlockSpec((None, nh, kb), lambda l, j: (l, 0, j)), pl.BlockSpec((kb, 1), lambda l, j: (j, 0))],
        out_specs=pl.BlockSpec((None, nh, N_REL_PAD), lambda l, j: (l, 0, 0)),
        out_shape=jax.ShapeDtypeStruct((nl, nh, N_REL_PAD), F32),
        compiler_params=_cp(("parallel", "arbitrary")))(dbias.reshape(nl, nh, n), idx)
    return out[:, :, :N_REL]


def _s5_prep_math(a_re, a_im, ls, btr, bti):
    step = jnp.exp(ls)
    mag = jnp.exp(a_re * step)
    ang = a_im * step
    lr = mag * jnp.cos(ang)
    li = mag * jnp.sin(ang)
    den = a_re * a_re + a_im * a_im
    nr = lr - 1.0
    cr = (nr * a_re + li * a_im) / den
    ci = (li * a_re - nr * a_im) / den
    bbr = cr[:, None, :] * btr - ci[:, None, :] * bti
    bbi = cr[:, None, :] * bti + ci[:, None, :] * btr
    return lr, li, bbr, bbi


def _s5_prep(a_re, a_im, ls, btr, bti, name):
    nl, g, p = a_re.shape
    h = btr.shape[2]

    def body(ar, ai, l_ref, br, bi, o1, o2, o3, o4):
        r = _s5_prep_math(ar[...], ai[...], l_ref[...], br[...], bi[...])
        o1[...], o2[...], o3[...], o4[...] = r

    m2 = pl.BlockSpec((None, g, p), lambda l: (l, 0, 0))
    m1 = pl.BlockSpec((None, g, 1), lambda l: (l, 0, 0))
    m3 = pl.BlockSpec((None, g, h, p), lambda l: (l, 0, 0, 0))
    s2 = jax.ShapeDtypeStruct((nl, g, p), F32)
    s3 = jax.ShapeDtypeStruct((nl, g, h, p), F32)
    return pl.pallas_call(body, name=name, grid=(nl,), in_specs=[m2, m2, m1, m3, m3], out_specs=[m2, m2, m3, m3],
                          out_shape=[s2, s2, s3, s3], compiler_params=_cp(("parallel",)))(a_re, a_im, ls, btr, bti)


def _s5_prep_bwd(a_re, a_im, ls, btr, bti, dlr, dli, dbbr, dbbi, name):
    nl, g, p = a_re.shape
    h = btr.shape[2]

    def body(ar, ai, l_ref, br, bi, g1, g2, g3, g4, o1, o2, o3, o4, o5):
        _, vjp = jax.vjp(_s5_prep_math, ar[...], ai[...], l_ref[...], br[...], bi[...])
        o1[...], o2[...], o3[...], o4[...], o5[...] = vjp((g1[...], g2[...], g3[...], g4[...]))

    m2 = pl.BlockSpec((None, g, p), lambda l: (l, 0, 0))
    m1 = pl.BlockSpec((None, g, 1), lambda l: (l, 0, 0))
    m3 = pl.BlockSpec((None, g, h, p), lambda l: (l, 0, 0, 0))
    s2 = jax.ShapeDtypeStruct((nl, g, p), F32)
    s1 = jax.ShapeDtypeStruct((nl, g, 1), F32)
    s3 = jax.ShapeDtypeStruct((nl, g, h, p), F32)
    return pl.pallas_call(body, name=name, grid=(nl,), in_specs=[m2, m2, m1, m3, m3, m2, m2, m3, m3],
                          out_specs=[m2, m2, m1, m3, m3], out_shape=[s2, s2, s1, s3, s3],
                          compiler_params=_cp(("parallel",)))(a_re, a_im, ls, btr, bti, dlr, dli, dbbr, dbbi)


def _cpow(lr, li, n):
    rr = ri = None
    br, bi = lr, li
    while n:
        if n & 1:
            rr, ri = (br, bi) if rr is None else (rr * br - ri * bi, rr * bi + ri * br)
        n >>= 1
        if n:
            br, bi = br * br - bi * bi, 2.0 * br * bi
    return rr, ri


def _permute_rows(src, dst, seg, inverse):
    def body(k, _):
        tile = pl.ds(pl.multiple_of(k * NSEG, NSEG), NSEG)
        spread = pl.ds(k, NSEG, stride=seg)
        if inverse:
            dst[spread, :] = src[tile, :]
        else:
            dst[tile, :] = src[spread, :]
        return 0

    lax.fori_loop(0, seg, body, 0, unroll=8)


def _s5_states(up, bbt_ref, lam_ref, xs, t):
    seg = t // NSEG
    rb = min(512, t)
    nj = xs.shape[0]
    nh = nj // 2
    for r0 in range(0, t, rb):
        bu = _dot(up[pl.ds(r0, rb), :].astype(BF16), bbt_ref[...])
        for j in range(nj):
            xs[j, pl.ds(r0, rb), :] = bu[:, j * LANE:(j + 1) * LANE]
    lamv = lam_ref[...]
    lr = [jnp.broadcast_to(lamv[j:j + 1], (NSEG, LANE)) for j in range(nh)]
    li = [jnp.broadcast_to(lamv[nh + j:nh + j + 1], (NSEG, LANE)) for j in range(nh)]
    zero = jnp.zeros((NSEG, LANE), F32)

    def load(k):
        rows = pl.ds(pl.multiple_of(k * NSEG, NSEG), NSEG)
        return (tuple(xs.at[j][rows, :] for j in range(nh)), tuple(xs.at[nh + j][rows, :] for j in range(nh)))

    def step(k, c):
        xr, xi, br, bi = c
        nbr, nbi = load(jnp.minimum(k + 1, seg - 1))
        rows = pl.ds(pl.multiple_of(k * NSEG, NSEG), NSEG)
        nxr, nxi = [], []
        for j in range(nh):
            r = lr[j] * xr[j] - li[j] * xi[j] + br[j]
            i = lr[j] * xi[j] + li[j] * xr[j] + bi[j]
            xs.at[j][rows, :] = r
            xs.at[nh + j][rows, :] = i
            nxr.append(r)
            nxi.append(i)
        return tuple(nxr), tuple(nxi), nbr, nbi

    xr, xi, _, _ = _loop(seg, step, ((zero,) * nh, (zero,) * nh) + load(0))
    pw = [_cpow(lr[j], li[j], seg) for j in range(nh)]
    sub = lax.broadcasted_iota(jnp.int32, (NSEG, LANE), 0)
    ctr, cti = [], []
    for j in range(nh):
        plr, pli = pw[j]
        cr, ci = zero, zero
        for s in range(1, NSEG):
            er = pltpu.roll(xr[j], 1, 0)
            ei = pltpu.roll(xi[j], 1, 0)
            pr_ = pltpu.roll(cr, 1, 0)
            pi_ = pltpu.roll(ci, 1, 0)
            nr = er + plr * pr_ - pli * pi_
            ni = ei + plr * pi_ + pli * pr_
            cr = jnp.where(sub == s, nr, cr)
            ci = jnp.where(sub == s, ni, ci)
        ctr.append(cr)
        cti.append(ci)

    def fix(k, c):
        rr, ri, xr_, xi_ = c
        nxr, nxi = load(jnp.minimum(k + 1, seg - 1))
        rows = pl.ds(pl.multiple_of(k * NSEG, NSEG), NSEG)
        nr, ni = [], []
        for j in range(nh):
            r = lr[j] * rr[j] - li[j] * ri[j]
            i = lr[j] * ri[j] + li[j] * rr[j]
            xs.at[j][rows, :] = xr_[j] + r
            xs.at[nh + j][rows, :] = xi_[j] + i
            nr.append(r)
            ni.append(i)
        return tuple(nr), tuple(ni), nxr, nxi

    _loop(seg, fix, (tuple(ctr), tuple(cti)) + load(0))
    return lr, li, pw


def _s5_fwd(proj, bbt, cmat, lam, dvec, w, name):
    t = proj.shape[0]
    nt = w // LANE
    nj = bbt.shape[2] // LANE
    seg = t // NSEG
    rb = min(512, t)
    uo = 5 * nt

    def body(u_ref, bbt_ref, cm_ref, lam_ref, d_ref, s5o_ref, yc_ref, xs, uf, up):
        for r0 in range(0, t, rb):
            uf[pl.ds(r0, rb), :] = u_ref[pl.ds(r0, rb), :].astype(F32)
        _permute_rows(uf, up, seg, False)
        _s5_states(up, bbt_ref, lam_ref, xs, t)
        for r0 in range(0, t, rb):
            rows = pl.ds(r0, rb)
            y = d_ref[...] * up[rows, :]
            for j in range(nj):
                y = y + _dot(xs.at[j][rows, :].astype(BF16), cm_ref[pl.ds(j * LANE, LANE), :])
            uf[rows, :] = y
        _permute_rows(uf, s5o_ref, seg, True)
        for r0 in range(0, t, rb):
            rows = pl.ds(r0, rb)
            yc_ref[rows, :] = _gelu(s5o_ref[rows, :]).astype(BF16)

    col = lambda c0: pl.BlockSpec((t, LANE), functools.partial(lambda i, c0: (0, c0 + i), c0=c0))
    return pl.pallas_call(
        body, name=name, grid=(nt,),
        in_specs=[col(uo), pl.BlockSpec((None, LANE, nj * LANE), lambda i: (i, 0, 0)),
                  pl.BlockSpec((None, nj * LANE, LANE), lambda i: (i, 0, 0)),
                  pl.BlockSpec((None, nj, LANE), lambda i: (i, 0, 0)), pl.BlockSpec((1, LANE), lambda i: (0, i))],
        out_specs=[col(0), col(0)],
        out_shape=[jax.ShapeDtypeStruct((t, w), F32), jax.ShapeDtypeStruct((t, w), BF16)],
        scratch_shapes=[pltpu.VMEM((nj, t, LANE), F32), pltpu.VMEM((t, LANE), F32), pltpu.VMEM((t, LANE), F32)],
        compiler_params=_cp(("parallel",)))(proj, bbt, cmat, lam, dvec)


def _s5_bwd(proj, dy, bbt, bbm, cmt, lam, dvec, w, name):
    t = proj.shape[0]
    nt = w // LANE
    nj = bbt.shape[2] // LANE
    nh = nj // 2
    seg = t // NSEG
    rb = min(512, t)
    uo = 5 * nt

    def body(u_ref, dy_ref, bbt_ref, bbm_ref, cmt_ref, lam_ref, d_ref, du_ref, dbbt_ref, dcm_ref, dlam_ref, dd_ref,
             xs, gs, uf, up, dyp):
        for r0 in range(0, t, rb):
            uf[pl.ds(r0, rb), :] = u_ref[pl.ds(r0, rb), :].astype(F32)
        _permute_rows(uf, up, seg, False)
        for r0 in range(0, t, rb):
            uf[pl.ds(r0, rb), :] = dy_ref[pl.ds(r0, rb), :].astype(F32)
        _permute_rows(uf, dyp, seg, False)
        lr, li, pw = _s5_states(up, bbt_ref, lam_ref, xs, t)
        dcm = [jnp.zeros((LANE, LANE), F32) for _ in range(nj)]
        for r0 in range(0, t, rb):
            rows = pl.ds(r0, rb)
            dyb = dyp[rows, :].astype(BF16)
            dx = _dot(dyb, cmt_ref[...])
            for j in range(nj):
                gs.at[j][rows, :] = dx[:, j * LANE:(j + 1) * LANE]
                dcm[j] = dcm[j] + _dot(xs.at[j][rows, :].astype(BF16), dyb, TN)
        for j in range(nj):
            dcm_ref[pl.ds(j * LANE, LANE), :] = dcm[j]
        zero = jnp.zeros((NSEG, LANE), F32)
        sub = lax.broadcasted_iota(jnp.int32, (NSEG, LANE), 0)
        tile = lambda k: pl.ds(pl.multiple_of(k * NSEG, NSEG), NSEG)

        def gload(k):
            return (tuple(gs.at[j][tile(k), :] for j in range(nh)), tuple(gs.at[nh + j][tile(k), :] for j in range(nh)))

        def xload(k):
            return (tuple(xs.at[j][tile(k), :] for j in range(nh)), tuple(xs.at[nh + j][tile(k), :] for j in range(nh)))

        def step(kk, c):
            gr, gi, dr, di = c
            k = seg - 1 - kk
            ndr, ndi = gload(jnp.maximum(k - 1, 0))
            ngr, ngi = [], []
            for j in range(nh):
                r = lr[j] * gr[j] + li[j] * gi[j] + dr[j]
                i = lr[j] * gi[j] - li[j] * gr[j] + di[j]
                gs.at[j][tile(k), :] = r
                gs.at[nh + j][tile(k), :] = i
                ngr.append(r)
                ngi.append(i)
            return tuple(ngr), tuple(ngi), ndr, ndi

        g0r, g0i, _, _ = _loop(seg, step, ((zero,) * nh, (zero,) * nh) + gload(seg - 1))
        ctr, cti = [], []
        for j in range(nh):
            plr, pli = pw[j]
            cr, ci = zero, zero
            for s in range(NSEG - 2, -1, -1):
                sr = pltpu.roll(g0r[j], NSEG - 1, 0)
                si = pltpu.roll(g0i[j], NSEG - 1, 0)
                pr_ = pltpu.roll(cr, NSEG - 1, 0)
                pi_ = pltpu.roll(ci, NSEG - 1, 0)
                nr = sr + plr * pr_ + pli * pi_
                ni = si + plr * pi_ - pli * pr_
                cr = jnp.where(sub == s, nr, cr)
                ci = jnp.where(sub == s, ni, ci)
            ctr.append(cr)
            cti.append(ci)

        def fix_step(k, c, xr, xi, gr, gi):
            rr, ri, ar, ai = c
            nr, ni, nar, nai = [], [], [], []
            for j in range(nh):
                r = lr[j] * rr[j] + li[j] * ri[j]
                i = lr[j] * ri[j] - li[j] * rr[j]
                g_r = gr[j] + r
                g_i = gi[j] + i
                gs.at[j][tile(k), :] = g_r
                gs.at[nh + j][tile(k), :] = g_i
                nar.append(ar[j] + g_r * xr[j] + g_i * xi[j])
                nai.append(ai[j] + g_i * xr[j] - g_r * xi[j])
                nr.append(r)
                ni.append(i)
            return tuple(nr), tuple(ni), tuple(nar), tuple(nai)

        def fix(kk, c):
            k = seg - 1 - kk
            nk = jnp.maximum(k - 1, 0)
            ngr, ngi = gload(nk)
            nxr, nxi = xload(jnp.maximum(nk - 1, 0))
            return fix_step(k, c[:4], c[6], c[7], c[4], c[5]) + (ngr, ngi, nxr, nxi)

        init = (tuple(ctr), tuple(cti), (zero,) * nh, (zero,) * nh) + gload(seg - 1) + xload(seg - 2)
        c = _loop(seg - 1, fix, init)
        lxr, lxi = xload(seg - 1)
        shift = lambda v: jnp.where(sub == 0, 0.0, pltpu.roll(v, 1, 0))
        _, _, ar, ai = fix_step(0, c[:4], tuple(shift(v) for v in lxr), tuple(shift(v) for v in lxi), c[4], c[5])
        for j in range(nh):
            dlam_ref[pl.ds(j, 1), :] = jnp.sum(ar[j], axis=0, keepdims=True)
            dlam_ref[pl.ds(nh + j, 1), :] = jnp.sum(ai[j], axis=0, keepdims=True)
        dbb = [jnp.zeros((LANE, LANE), F32) for _ in range(nj)]
        dd = jnp.zeros((1, LANE), F32)
        for r0 in range(0, t, rb):
            rows = pl.ds(r0, rb)
            uv = up[rows, :]
            ub = uv.astype(BF16)
            dyf = dyp[rows, :]
            du = d_ref[...] * dyf
            dd = dd + jnp.sum(dyf * uv, axis=0, keepdims=True)
            for j in range(nj):
                gb = gs.at[j][rows, :].astype(BF16)
                du = du + _dot(gb, bbm_ref[pl.ds(j * LANE, LANE), :])
                dbb[j] = dbb[j] + _dot(ub, gb, TN)
            uf[rows, :] = du
        for j in range(nj):
            dbbt_ref[:, pl.ds(j * LANE, LANE)] = dbb[j]
        dd_ref[...] = dd
        _permute_rows(uf, up, seg, True)
        for r0 in range(0, t, rb):
            rows = pl.ds(r0, rb)
            du_ref[rows, :] = up[rows, :].astype(BF16)

    col = lambda c0: pl.BlockSpec((t, LANE), functools.partial(lambda i, c0: (0, c0 + i), c0=c0))
    wide = pl.BlockSpec((None, LANE, nj * LANE), lambda i: (i, 0, 0))
    tall = pl.BlockSpec((None, nj * LANE, LANE), lambda i: (i, 0, 0))
    lam_s = pl.BlockSpec((None, nj, LANE), lambda i: (i, 0, 0))
    vec = pl.BlockSpec((1, LANE), lambda i: (0, i))
    flat = lambda: pltpu.VMEM((t, LANE), F32)
    return pl.pallas_call(
        body, name=name, grid=(nt,),
        in_specs=[col(uo), col(0), wide, tall, wide, lam_s, vec],
        out_specs=[col(0), wide, tall, lam_s, vec],
        out_shape=[jax.ShapeDtypeStruct((t, w), BF16), jax.ShapeDtypeStruct((nt, LANE, nj * LANE), F32),
                   jax.ShapeDtypeStruct((nt, nj * LANE, LANE), F32), jax.ShapeDtypeStruct((nt, nj, LANE), F32),
                   jax.ShapeDtypeStruct((1, w), F32)],
        scratch_shapes=[pltpu.VMEM((nj, t, LANE), F32), pltpu.VMEM((nj, t, LANE), F32), flat(), flat(), flat()],
        compiler_params=_cp(("parallel",)))(proj, dy, bbt, bbm, cmt, lam, dvec)


def _block_diag(x, nt):
    nb, r, c = x.shape
    b = nb // nt
    eye = jnp.eye(b, dtype=x.dtype)
    return jnp.einsum("ibrc,bk->ibrkc", x.reshape(nt, b, r, c), eye).reshape(nt, b * r, b * c)


def _block_diag_extract(x, b):
    nt, br, bc = x.shape
    r, c = br // b, bc // b
    eye = jnp.eye(b, dtype=x.dtype)
    return jnp.einsum("ibrkc,bk->ibrc", x.reshape(nt, b, r, b, c), eye).reshape(nt * b, r, c)


def _adamw_math(w, g, m, v):
    m = ADAM_B1 * m + (1.0 - ADAM_B1) * g
    v = ADAM_B2 * v + (1.0 - ADAM_B2) * (g * g)
    m_hat = m / (1.0 - ADAM_B1 ** ADAM_STEP)
    v_hat = v / (1.0 - ADAM_B2 ** ADAM_STEP)
    delta = -ADAM_LR * (m_hat / (jnp.sqrt(v_hat) + ADAM_EPS) + ADAM_WD * w)
    return delta, m, v


def _adamw_reduce(parts, w, m, v, prev, l, name):
    n, r, c = parts.shape
    tr = r
    for cand in (512, 256, 128, 64, 32, 16, 8):
        if r % cand == 0 and cand * c * 4 <= 2 * 1024 * 1024:
            tr = cand
            break
    nb = r // tr
    if prev is None:
        prev = [lax.empty(w.shape, F32) for _ in range(4)]

    def body(p_ref, w_ref, m_ref, v_ref, *rest):
        g_ref, d_ref, nm_ref, nv_ref = rest[4:]
        g = p_ref[0].astype(F32)
        for q in range(1, n):
            g = g + p_ref[q].astype(F32)
        g_ref[...] = g
        d_ref[...], nm_ref[...], nv_ref[...] = _adamw_math(w_ref[...], g, m_ref[...], v_ref[...])

    row = pl.BlockSpec((tr, c), lambda i: (l * nb + i, 0))
    any_spec = pl.BlockSpec(memory_space=pl.ANY)
    o = jax.ShapeDtypeStruct(w.shape, F32)
    return pl.pallas_call(
        body, name=name, grid=(nb,),
        in_specs=[pl.BlockSpec((n, tr, c), lambda i: (0, i, 0)), row, row, row] + [any_spec] * 4,
        out_specs=[row, row, row, row], out_shape=[o, o, o, o],
        input_output_aliases={4: 0, 5: 1, 6: 2, 7: 3},
        compiler_params=_cp(("parallel",)))(parts, w, m, v, *prev)


def _adamw_flat(w, g, m, v, name):
    r, c = w.shape

    def body(w_ref, g_ref, m_ref, v_ref, d_ref, nm_ref, nv_ref):
        d_ref[...], nm_ref[...], nv_ref[...] = _adamw_math(w_ref[...], g_ref[...], m_ref[...], v_ref[...])

    o = jax.ShapeDtypeStruct((r, c), F32)
    row = pl.BlockSpec((FLAT_ROWS, c), lambda i: (i, 0))
    return pl.pallas_call(body, name=name, grid=(r // FLAT_ROWS,), in_specs=[row] * 4, out_specs=[row] * 3,
                          out_shape=[o, o, o], compiler_params=_cp(("parallel",)))(w, g, m, v)


def _sum_rows8(x, name):
    n, r, c = x.shape

    def body(x_ref, o_ref):
        acc = x_ref[0]
        for q in range(1, n):
            acc = acc + x_ref[q]
        o_ref[...] = acc

    return pl.pallas_call(
        body, name=name, grid=(r // FLAT_ROWS,), in_specs=[pl.BlockSpec((n, FLAT_ROWS, c), lambda i: (0, i, 0))],
        out_specs=pl.BlockSpec((FLAT_ROWS, c), lambda i: (i, 0)), out_shape=jax.ShapeDtypeStruct((r, c), F32),
        compiler_params=_cp(("parallel",)))(x)


def _row_tile(r, row_bytes, limit=2 * 1024 * 1024):
    for cand in (512, 256, 128, 64, 32, 16):
        if r % cand == 0 and cand * row_bytes <= limit:
            return cand
    return r


def _pair_sum(own, sib, core, chip, name):
    _, _, r, c = own.shape
    tr = _row_tile(r, 2 * c)

    def body(core_ref, chip_ref, a_ref, b_ref, o_ref, land_ref):
        s = (a_ref[...].astype(F32) + b_ref[...].astype(F32)).astype(BF16)
        o_ref[...] = s

        @pl.when(pl.program_id(1) == chip_ref[0])
        def _():
            land_ref[...] = s

    grid_spec = pltpu.PrefetchScalarGridSpec(
        num_scalar_prefetch=2, grid=(r // tr, NCHIP),
        in_specs=[pl.BlockSpec((None, None, tr, c), lambda i, p, cr, hr: (p, cr[0], i, 0)),
                  pl.BlockSpec((None, tr, c), lambda i, p, cr, hr: (p, i, 0))],
        out_specs=[pl.BlockSpec((None, tr, c), lambda i, p, cr, hr: (p, i, 0)),
                   pl.BlockSpec((None, tr, c), lambda i, p, cr, hr: (hr[0], i, 0))])
    o = jax.ShapeDtypeStruct((NCHIP, r, c), BF16)
    return pl.pallas_call(body, name=name, grid_spec=grid_spec, out_shape=[o, o],
                          compiler_params=_cp(("parallel", "arbitrary")))(core, chip, own, sib)


def _cast_place(w, l, me, after, name):
    _, r, c = w.shape
    tr = _row_tile(r, 4 * c)

    def body(m_ref, w_ref, a_ref, o_ref, land_ref):
        v = w_ref[...].astype(BF16)
        o_ref[...] = v
        land_ref[...] = v

    grid_spec = pltpu.PrefetchScalarGridSpec(
        num_scalar_prefetch=1, grid=(r // tr,),
        in_specs=[pl.BlockSpec((None, tr, c), lambda i, mr: (l, i, 0)), pl.BlockSpec(memory_space=pl.ANY)],
        out_specs=[pl.BlockSpec((tr, c), lambda i, mr: (i, 0)), pl.BlockSpec((None, tr, c), lambda i, mr: (mr[0], i, 0))])
    return pl.pallas_call(body, name=name, grid_spec=grid_spec,
                          out_shape=[jax.ShapeDtypeStruct((r, c), BF16), jax.ShapeDtypeStruct((NDEV, r, c), BF16)],
                          compiler_params=_cp(("parallel",)))(me, w, after)


def _all_gather(xs, name):
    n = len(xs)

    def body(*refs):
        x_refs, o_refs = refs[:n], refs[n:2 * n]
        send_sems, recv_sems, local_sems = refs[2 * n:]
        x, y, c = lax.axis_index("x"), lax.axis_index("y"), lax.axis_index("c")
        me, sibling = (x, y, c), (x, y, 1 - c)
        chips = [(1 - x, y), (x, 1 - y), (1 - x, 1 - y)]

        def copy(t, k, block, to, src=None):
            dst = o_refs[t].at[4 * block[0] + 2 * block[1] + block[2]]
            return pltpu.make_async_remote_copy(
                src_ref=dst if src is None else src, dst_ref=dst,
                send_sem=send_sems.at[7 * t + k], recv_sem=recv_sems.at[7 * t + k],
                device_id=to, device_id_type=MESH)

        mine, first, passed = [], [], []
        for t in range(n):
            cp = pltpu.make_async_copy(x_refs[t], o_refs[t].at[4 * x + 2 * y + c], local_sems.at[t])
            cp.start()
            mine.append(cp)
            first.append(copy(t, 0, me, sibling, src=x_refs[t]))
            first += [copy(t, 1 + j, me, (*chip, c), src=x_refs[t]) for j, chip in enumerate(chips)]
        for cp in first:
            cp.start()
        for t in range(n):
            for j, chip in enumerate(chips):
                copy(t, 1 + j, (*chip, c), me).wait_recv()
                cp = copy(t, 4 + j, (*chip, c), sibling)
                cp.start()
                passed.append(cp)
        for t in range(n):
            copy(t, 0, sibling, me).wait_recv()
            for j, chip in enumerate(chips):
                copy(t, 4 + j, (*chip, 1 - c), me).wait_recv()
        for cp in first + passed:
            cp.wait_send()
        for cp in mine:
            cp.wait()

    any_spec = pl.BlockSpec(memory_space=pl.ANY)
    return pl.pallas_call(
        body, name=name, in_specs=[any_spec] * n, out_specs=[any_spec] * n,
        out_shape=[jax.ShapeDtypeStruct((NDEV,) + a.shape, a.dtype) for a in xs],
        scratch_shapes=[pltpu.SemaphoreType.DMA((7 * n,)), pltpu.SemaphoreType.DMA((7 * n,)),
                        pltpu.SemaphoreType.DMA((n,))],
        )(*xs)


def _sibling_exchange(xs, name):
    n = len(xs)

    def body(*refs):
        x_refs, o_refs = refs[:n], refs[n:2 * n]
        send_sems, recv_sems = refs[2 * n:]
        x, y, c = lax.axis_index("x"), lax.axis_index("y"), lax.axis_index("c")
        cps = []
        for t in range(n):
            cp = pltpu.make_async_remote_copy(
                src_ref=x_refs[t].at[:, 1 - c], dst_ref=o_refs[t], send_sem=send_sems.at[t], recv_sem=recv_sems.at[t],
                device_id=(x, y, 1 - c), device_id_type=MESH)
            cp.start()
            cps.append(cp)
        for cp in cps:
            cp.wait()

    any_spec = pl.BlockSpec(memory_space=pl.ANY)
    return pl.pallas_call(
        body, name=name, in_specs=[any_spec] * n, out_specs=[any_spec] * n,
        out_shape=[jax.ShapeDtypeStruct((a.shape[0],) + a.shape[2:], a.dtype) for a in xs],
        scratch_shapes=[pltpu.SemaphoreType.DMA((n,)), pltpu.SemaphoreType.DMA((n,))],
        )(*xs)


def _split_copies(kind, srcs, lands, send_sems, recv_sems):
    x, y, c = lax.axis_index("x"), lax.axis_index("y"), lax.axis_index("c")
    flip = lambda v, b: 1 - v if b else v
    cps = []
    for t in range(len(srcs)):
        if kind == "all":
            for mask in range(1, NDEV):
                peer = (flip(x, mask & 4), flip(y, mask & 2), flip(c, mask & 1))
                cps.append(pltpu.make_async_remote_copy(
                    src_ref=srcs[t], dst_ref=lands[t].at[4 * x + 2 * y + c],
                    send_sem=send_sems.at[(NDEV - 1) * t + mask - 1], recv_sem=recv_sems.at[(NDEV - 1) * t + mask - 1],
                    device_id=peer, device_id_type=MESH))
        else:
            for j, (px, py) in enumerate([(1 - x, y), (x, 1 - y), (1 - x, 1 - y)]):
                cps.append(pltpu.make_async_remote_copy(
                    src_ref=srcs[t].at[2 * px + py], dst_ref=lands[t].at[2 * x + y],
                    send_sem=send_sems.at[3 * t + j], recv_sem=recv_sems.at[3 * t + j],
                    device_id=(px, py, c), device_id_type=MESH))
    return cps


def _split_start(kind, srcs, lands, name):
    n = len(srcs)
    ncp = n * (NDEV - 1 if kind == "all" else NCHIP - 1)
    arrs = list(srcs) + list(lands)

    def body(*refs):
        for cp in _split_copies(kind, refs[:n], refs[n:2 * n], refs[2 * n], refs[2 * n + 1]):
            cp.start()
        refs[-1][...] = jnp.zeros_like(refs[-1])

    hbm = pl.BlockSpec(memory_space=pltpu.HBM)
    sem = pl.BlockSpec(memory_space=pltpu.SEMAPHORE)
    return pl.pallas_call(
        body, name=name,
        out_shape=(pltpu.SemaphoreType.DMA((ncp,)), pltpu.SemaphoreType.DMA((ncp,)),
                   *[pltpu.HBM(a.shape, a.dtype) for a in arrs], jax.ShapeDtypeStruct((8, LANE), F32)),
        in_specs=[hbm] * (2 * n), out_specs=(sem, sem, *[hbm] * (2 * n), pl.BlockSpec(memory_space=pltpu.VMEM)),
        input_output_aliases={i: 2 + i for i in range(2 * n)},
        compiler_params=pltpu.CompilerParams(has_side_effects=pltpu.SideEffectType.DATAFLOW_SIDE_EFFECTING),
    )(*[pltpu.with_memory_space_constraint(a, pltpu.HBM) for a in arrs])


def _split_wait(kind, started, after, name):
    send_sems, recv_sems, thru = started[0], started[1], list(started[2:])
    n = len(thru) // 2

    def body(*refs):
        for cp in _split_copies(kind, refs[:n], refs[n:2 * n], refs[2 * n], refs[2 * n + 1]):
            cp.wait_send()
            cp.wait_recv()

    hbm = pl.BlockSpec(memory_space=pltpu.HBM)
    sem = pl.BlockSpec(memory_space=pltpu.SEMAPHORE)
    outs = pl.pallas_call(
        body, name=name, out_shape=tuple(pltpu.HBM(a.shape, a.dtype) for a in thru),
        in_specs=[hbm] * (2 * n) + [sem, sem, pl.BlockSpec(memory_space=pl.ANY)], out_specs=tuple([hbm] * (2 * n)),
        input_output_aliases={i: i for i in range(2 * n)},
        compiler_params=pltpu.CompilerParams(has_side_effects=pltpu.SideEffectType.DATAFLOW_SIDE_EFFECTING),
    )(*thru, send_sems, recv_sems, after)
    return list(outs[n:])


_SMALL = ["norm_mix_g", "lru_conv_b", "lru_wa", "lru_ba", "lru_wx", "lru_bx", "lru_lambda", "attn_rel_bias",
          "ssm_a_re", "ssm_a_im", "ssm_b_re", "ssm_b_im", "ssm_c_re", "ssm_c_im", "ssm_d", "ssm_log_step",
          "norm_ffn_g", "norm_final_g"]
_SMALL_SHARDED = ["gate_bias", "lru_conv_w"]
_BIG = ["w_in", "ssm_w_glu", "w_branch", "w_out", "w_ffn_gate", "w_ffn_up", "w_ffn_down"]
_WEIGHTS = ["norm_mix_g", "w_in", "gate_bias", "lru_conv_w", "lru_conv_b", "lru_wa", "lru_ba", "lru_wx", "lru_bx",
            "lru_lambda", "attn_rel_bias", "ssm_a_re", "ssm_a_im", "ssm_b_re", "ssm_b_im", "ssm_c_re", "ssm_c_im",
            "ssm_d", "ssm_log_step", "ssm_w_glu", "w_branch", "w_out", "norm_ffn_g", "w_ffn_gate", "w_ffn_up",
            "w_ffn_down", "norm_final_g"]


def _device_step(x, target, wts, small_gath, get_gath, on_grads):
    t, d = x.shape
    w = d // 2
    nl = wts["norm_mix_g"].shape[0]
    nt = w // LANE
    f = wts["w_ffn_gate"].shape[-1]
    ncol = wts["w_in"].shape[-1]
    tm = _tile(t, 1024)
    n_g = w // SSM_GROUP
    gpt = LANE // SSM_GROUP
    bw = w // LRU_BLOCKS
    bpt = LANE // bw

    gbias = jnp.transpose(small_gath["gate_bias"], (1, 2, 0, 3)).reshape(nl, N_BRANCH, d)
    convw = jnp.transpose(small_gath["lru_conv_w"], (1, 2, 0, 3)).reshape(nl, CONV_W, w)
    bias_tab = _bias_expand(wts["attn_rel_bias"], "bias_expand")
    btr = jnp.swapaxes(wts["ssm_b_re"], 2, 3)
    bti = jnp.swapaxes(wts["ssm_b_im"], 2, 3)
    ls3 = wts["ssm_log_step"][..., None]
    lam_r, lam_i, bbr, bbi = _s5_prep(wts["ssm_a_re"], wts["ssm_a_im"], ls3, btr, bti, "s5_prep")

    def layer_consts(l):
        c = {}
        c["wa_t"] = _block_diag(wts["lru_wa"][l], nt).astype(BF16)
        c["wx_t"] = _block_diag(wts["lru_wx"][l], nt).astype(BF16)
        c["wat_t"] = jnp.swapaxes(c["wa_t"], 1, 2)
        c["wxt_t"] = jnp.swapaxes(c["wx_t"], 1, 2)
        bd_r, bd_i = _block_diag(bbr[l], nt), _block_diag(bbi[l], nt)
        c["bbt"] = jnp.concatenate([bd_r, bd_i], axis=-1).astype(BF16)
        c["bbm"] = jnp.swapaxes(c["bbt"], 1, 2)
        cd_r, cd_i = _block_diag(wts["ssm_c_re"][l], nt), _block_diag(wts["ssm_c_im"][l], nt)
        c["cmt"] = jnp.concatenate([cd_r, -cd_i], axis=-1).astype(BF16)
        c["cmat"] = jnp.swapaxes(c["cmt"], 1, 2)
        nrow = gpt * SSM_P // LANE
        c["lam"] = jnp.concatenate([lam_r[l].reshape(nt, nrow, LANE), lam_i[l].reshape(nt, nrow, LANE)], axis=1)
        return c

    row1 = lambda a: a.reshape(1, -1)
    tied = lambda g, tok: g if tok is None else g + tok[0, 0]
    saved = []
    for l in range(nl):
        c = layer_consts(l)
        gath, tok = get_gath(l, x)
        h1 = _rmsnorm_fwd(x, tied(row1(wts["norm_mix_g"][l]), tok), f"norm_mix_{l}")
        tn = _tile(ncol, 768)
        nn_ = ncol // tn
        proj = _mm(f"in_proj_{l}", NN, (t // tm, NDEV, nn_), 0, [h1, gath["w_in"]],
                   [pl.BlockSpec((tm, d), lambda i, j, n: (i, 0)),
                    pl.BlockSpec((None, d, tn), lambda i, j, n: (j, 0, n))],
                   jax.ShapeDtypeStruct((t, NDEV * ncol), BF16),
                   pl.BlockSpec((tm, tn), lambda i, j, n: (i, j * nn_ + n)), (8, LANE))
        ya, hs = _lru_fwd(proj, convw[l], row1(wts["lru_conv_b"][l]), c["wa_t"], row1(wts["lru_ba"][l]), c["wx_t"],
                          row1(wts["lru_bx"][l]), row1(wts["lru_lambda"][l]), w, f"lru_fwd_{l}")
        yb = _attn_fwd(proj, bias_tab[l], w, f"attn_fwd_{l}")
        s5o, yc = _s5_fwd(proj, c["bbt"], c["cmat"], c["lam"], row1(wts["ssm_d"][l]), w, f"s5_fwd_{l}")
        cb = d // NDEV

        def branch(y, wg, idx, nm):
            return _mm(nm, NN, (t // tm, NDEV), 0, [y, wg],
                       [pl.BlockSpec((tm, w), lambda i, j: (i, 0)),
                        pl.BlockSpec((None,) * len(idx(0)[:-2]) + (w, cb), lambda i, j: idx(j))],
                       jax.ShapeDtypeStruct((t, d), BF16), pl.BlockSpec((tm, cb), lambda i, j: (i, j)), (8, LANE))

        wb_idx = lambda b: (lambda j: (j, b, 0, 0))
        br_a = branch(ya, gath["w_branch"], wb_idx(0), f"branch_a_{l}")
        br_b = branch(yb, gath["w_branch"], wb_idx(1), f"branch_b_{l}")
        br_c0 = branch(yc, gath["w_branch"], wb_idx(2), f"branch_c_{l}")
        gl = branch(yc, gath["ssm_w_glu"], lambda j: (j, 0, 0), f"branch_glu_{l}")
        merged = _merge_fwd(proj, gbias[l], br_a, br_b, br_c0, gl, d, f"merge_fwd_{l}")
        tno = _tile(d, 1024)
        w_out_full = gath["w_out"].reshape(d, d)
        x_mid = _mm(f"out_proj_{l}", NN, (t // tm, d // tno), 0, [merged, w_out_full, x],
                    [pl.BlockSpec((tm, d), lambda i, n: (i, 0)), pl.BlockSpec((d, tno), lambda i, n: (0, n)),
                     pl.BlockSpec((tm, tno), lambda i, n: (i, n))],
                    jax.ShapeDtypeStruct((t, d), F32), pl.BlockSpec((tm, tno), lambda i, n: (i, n)), (8, LANE),
                    epi=lambda acc, res: acc + res)
        h2 = _rmsnorm_fwd(x_mid, row1(wts["norm_ffn_g"][l]), f"norm_ffn_{l}")
        gp, up, act = _ffn_in(h2, gath["w_ffn_gate"], gath["w_ffn_up"], f"ffn_in_{l}")
        x_next = _mm(f"ffn_down_{l}", NN, (t // tm, d // tno, NDEV), 1, [act, gath["w_ffn_down"], x_mid],
                     [pl.BlockSpec((None, tm, f), lambda i, n, k: (k, i, 0)),
                      pl.BlockSpec((None, f, tno), lambda i, n, k: (k, 0, n)),
                      pl.BlockSpec((tm, tno), lambda i, n, k: (i, n))],
                     jax.ShapeDtypeStruct((t, d), F32), pl.BlockSpec((tm, tno), lambda i, n, k: (i, n)), (tm, tno),
                     epi=lambda acc, res: acc + res)
        saved.append(dict(x=x, h1=h1, proj=proj, ya=ya, hs=hs, yb=yb, s5o=s5o, yc=yc, br_a=br_a, br_b=br_b,
                          br_c0=br_c0, gl=gl, merged=merged, x_mid=x_mid, h2=h2, gp=gp, up=up, act=act, c=c, gath=gath))
        x = x_next

    loss, dx, dxb, dgf = _loss_head(x, row1(wts["norm_final_g"]), target, "loss_head")

    sg = {k: [None] * nl for k in _SMALL + _SMALL_SHARDED if k != "norm_final_g"}
    dbias_tabs = [None] * nl
    dlr_l, dli_l, dbbr_l, dbbi_l = [None] * nl, [None] * nl, [None] * nl, [None] * nl
    tk = _tile(d, 1024)
    for l in range(nl - 1, -1, -1):
        s = saved[l]
        c = s["c"]
        gath = s["gath"]
        gbuf = {}
        tno = _tile(d, 1024)
        gbuf["w_ffn_down"] = _mm(
            f"dw_ffn_down_{l}", TN, (NDEV, d // tno), 0, [s["act"], dxb],
            [pl.BlockSpec((None, t, f), lambda j, n: (j, 0, 0)), pl.BlockSpec((t, tno), lambda j, n: (0, n))],
            jax.ShapeDtypeStruct((NDEV, f, d), BF16), pl.BlockSpec((None, f, tno), lambda j, n: (j, 0, n)), (8, LANE))
        dgp, dup = _ffn_dact(dxb, gath["w_ffn_down"], s["gp"], s["up"], f"d_act_{l}")
        dh2 = _ffn_dh(dgp, dup, gath["w_ffn_gate"], gath["w_ffn_up"], f"d_h2_{l}")

        def ffn_dw(dz, key, nm):
            return _mm(nm, TN, (NDEV, d // tk), 0, [s["h2"], dz],
                       [pl.BlockSpec((t, tk), lambda j, k: (0, k)), pl.BlockSpec((None, t, f), lambda j, k: (j, 0, 0))],
                       jax.ShapeDtypeStruct((NDEV, d, f), BF16), pl.BlockSpec((None, tk, f), lambda j, k: (j, k, 0)),
                       (8, LANE))

        gbuf["w_ffn_gate"] = ffn_dw(dgp, "w_ffn_gate", f"dw_ffn_gate_{l}")
        gbuf["w_ffn_up"] = ffn_dw(dup, "w_ffn_up", f"dw_ffn_up_{l}")
        dx_mid, dxmb, dg2 = _rmsnorm_bwd(s["x_mid"], row1(wts["norm_ffn_g"][l]), [dh2], dx, f"norm_ffn_bwd_{l}")
        sg["norm_ffn_g"][l] = dg2.reshape(-1)
        rb_ = d // NDEV
        dmerged = _mm(f"d_merged_{l}", NT, (t // tm, d // tno), 0, [dxmb, gath["w_out"].reshape(d, d)],
                      [pl.BlockSpec((tm, d), lambda i, n: (i, 0)), pl.BlockSpec((tno, d), lambda i, n: (n, 0))],
                      jax.ShapeDtypeStruct((t, d), BF16), pl.BlockSpec((tm, tno), lambda i, n: (i, n)), (8, LANE))
        tkw = _tile(d, 512)
        gbuf["w_out"] = _mm(
            f"dw_out_{l}", TN, (d // tkw, d // tno), 0, [s["merged"], dxmb],
            [pl.BlockSpec((t, tkw), lambda k, n: (0, k)), pl.BlockSpec((t, tno), lambda k, n: (0, n))],
            jax.ShapeDtypeStruct((d, d), BF16), pl.BlockSpec((tkw, tno), lambda k, n: (k, n)),
            (8, LANE)).reshape(NDEV, rb_, d)
        dbr_a, dbr_b, dbr_c0, dgl, dgates, dgb = _merge_bwd(s["proj"], gbias[l], s["br_a"], s["br_b"], s["br_c0"], s["gl"],
                                                           dmerged, d, f"merge_bwd_{l}")
        sg["gate_bias"][l] = dgb.reshape(-1)
        cb = d // NDEV

        def branch_dy(dbr, wg, idx, nm, dt):
            nlead = len(idx(0, 0)) - 2
            return _mm(nm, NT, (t // tm, NDEV), 1, [dbr, wg],
                       [pl.BlockSpec((tm, cb), lambda i, j: (i, j)),
                        pl.BlockSpec((None,) * nlead + (w, cb), lambda i, j: idx(j, 0))],
                       jax.ShapeDtypeStruct((t, w), dt), pl.BlockSpec((tm, w), lambda i, j: (i, 0)), (tm, w))

        def branch_dw(y, dbr, buf, shape, idx, nm):
            nlead = len(idx(0)) - 2
            return _mm(nm, TN, (NDEV,), 0, [y, dbr] + ([] if buf is None else [buf]),
                       [pl.BlockSpec((t, w), lambda j: (0, 0)), pl.BlockSpec((t, cb), lambda j: (0, j))],
                       jax.ShapeDtypeStruct(shape, BF16),
                       pl.BlockSpec((None,) * nlead + (w, cb), lambda j: idx(j)), (8, LANE),
                       alias=None if buf is None else True)

        wb_i = lambda b: (lambda j, z=0: (j, b, 0, 0))
        glu_i = lambda j, z=0: (j, 0, 0)
        dya = branch_dy(dbr_a, gath["w_branch"], wb_i(0), f"d_ya_{l}", BF16)
        dyb = branch_dy(dbr_b, gath["w_branch"], wb_i(1), f"d_yb_{l}", BF16)
        dyc1 = branch_dy(dbr_c0, gath["w_branch"], wb_i(2), f"d_yc_{l}", F32)
        dyc2 = branch_dy(dgl, gath["ssm_w_glu"], glu_i, f"d_yc_glu_{l}", F32)
        wb_shape = (NDEV, N_BRANCH, w, cb)
        gwb = branch_dw(s["ya"], dbr_a, lax.empty(wb_shape, BF16), wb_shape, wb_i(0), f"dw_branch_a_{l}")
        gwb = branch_dw(s["yb"], dbr_b, gwb, wb_shape, wb_i(1), f"dw_branch_b_{l}")
        gbuf["w_branch"] = branch_dw(s["yc"], dbr_c0, gwb, wb_shape, wb_i(2), f"dw_branch_c_{l}")
        gbuf["ssm_w_glu"] = branch_dw(s["yc"], dgl, None, (NDEV, w, cb), glu_i, f"dw_glu_{l}")
        ds5o = _gelu_bwd_sum(s["s5o"], dyc1, dyc2, f"gelu_bwd_{l}")
        du, dbbt, dcm, dlam, dd = _s5_bwd(s["proj"], ds5o, c["bbt"], c["bbm"], c["cmt"], c["lam"], row1(wts["ssm_d"][l]),
                                          w, f"s5_bwd_{l}")
        sp = gpt * SSM_P
        dbbr_l[l] = _block_diag_extract(dbbt[:, :, :sp], gpt)
        dbbi_l[l] = _block_diag_extract(dbbt[:, :, sp:], gpt)
        dcmt = jnp.swapaxes(dcm, 1, 2)
        sg["ssm_c_re"][l] = _block_diag_extract(dcmt[:, :, :sp], gpt).reshape(-1)
        sg["ssm_c_im"][l] = (-_block_diag_extract(dcmt[:, :, sp:], gpt)).reshape(-1)
        nrow = sp // LANE
        dlr_l[l] = dlam[:, :nrow].reshape(n_g, SSM_P)
        dli_l[l] = dlam[:, nrow:].reshape(n_g, SSM_P)
        sg["ssm_d"][l] = dd.reshape(-1)
        dq, dk, dv, dbt = _attn_bwd(s["proj"], bias_tab[l], dyb, w, f"attn_bwd_{l}")
        dbias_tabs[l] = dbt
        dlx, dlg, dcw, dcb, dwa, dba, dwx, dbx, dlm = _lru_bwd(
            s["proj"], s["hs"], dya, convw[l], row1(wts["lru_conv_b"][l]), c["wa_t"], row1(wts["lru_ba"][l]), c["wx_t"],
            row1(wts["lru_bx"][l]), row1(wts["lru_lambda"][l]), c["wat_t"], c["wxt_t"], w, f"lru_bwd_{l}")
        sg["lru_conv_w"][l] = dcw.reshape(-1)
        sg["lru_conv_b"][l] = dcb.reshape(-1)
        sg["lru_wa"][l] = _block_diag_extract(dwa, bpt).reshape(-1)
        sg["lru_wx"][l] = _block_diag_extract(dwx, bpt).reshape(-1)
        sg["lru_ba"][l] = dba.reshape(-1)
        sg["lru_bx"][l] = dbx.reshape(-1)
        sg["lru_lambda"][l] = dlm.reshape(-1)
        dproj = jnp.concatenate([dlx, dlg, dq, dk, dv, du, dgates], axis=1)
        tn = _tile(ncol, 768)
        nn_ = ncol // tn
        dh1 = _mm(f"d_h1_{l}", NT, (t // tm, d // tk, NDEV), 1, [dproj, gath["w_in"]],
                  [pl.BlockSpec((tm, ncol), lambda i, k, j: (i, j)), pl.BlockSpec((None, tk, ncol), lambda i, k, j: (j, k, 0))],
                  jax.ShapeDtypeStruct((t, d), F32), pl.BlockSpec((tm, tk), lambda i, k, j: (i, k)), (tm, tk))
        gbuf["w_in"] = _mm(
            f"dw_in_{l}", TN, (NDEV, d // tk, nn_), 0, [s["h1"], dproj],
            [pl.BlockSpec((t, tk), lambda j, k, n: (0, k)), pl.BlockSpec((t, tn), lambda j, k, n: (0, j * nn_ + n))],
            jax.ShapeDtypeStruct((NDEV, d, ncol), BF16), pl.BlockSpec((None, tk, tn), lambda j, k, n: (j, k, n)),
            (8, LANE))
        tok = on_grads(l, gbuf)
        dx, dxb, dg1 = _rmsnorm_bwd(s["x"], tied(row1(wts["norm_mix_g"][l]), tok), [dh1], dx_mid, f"norm_mix_bwd_{l}")
        sg["norm_mix_g"][l] = dg1.reshape(-1)

    da_re, da_im, dls, dbtr, dbti = _s5_prep_bwd(wts["ssm_a_re"], wts["ssm_a_im"], ls3, btr, bti, jnp.stack(dlr_l),
                                                 jnp.stack(dli_l), jnp.stack(dbbr_l), jnp.stack(dbbi_l), "s5_prep_bwd")
    drel = _bias_reduce(jnp.stack(dbias_tabs), "bias_reduce")
    small = {k: jnp.stack(v) for k, v in sg.items() if v[0] is not None}
    small["ssm_a_re"] = da_re.reshape(nl, -1)
    small["ssm_a_im"] = da_im.reshape(nl, -1)
    small["ssm_log_step"] = dls.reshape(nl, -1)
    small["ssm_b_re"] = jnp.swapaxes(dbtr, 2, 3).reshape(nl, -1)
    small["ssm_b_im"] = jnp.swapaxes(dbti, 2, 3).reshape(nl, -1)
    small["attn_rel_bias"] = drel.reshape(nl, -1)
    small["norm_final_g"] = dgf.reshape(-1)
    return loss, dx, small


def kernel(x, norm_mix_g, w_in, gate_bias, lru_conv_w, lru_conv_b, lru_wa, lru_ba, lru_wx, lru_bx, lru_lambda, attn_rel_bias, ssm_a_re, ssm_a_im, ssm_b_re, ssm_b_im, ssm_c_re, ssm_c_im, ssm_d, ssm_log_step, ssm_w_glu, w_branch, w_out, norm_ffn_g, w_ffn_gate, w_ffn_up, w_ffn_down, norm_final_g, loss_target, m_norm_mix_g, m_w_in, m_gate_bias, m_lru_conv_w, m_lru_conv_b, m_lru_wa, m_lru_ba, m_lru_wx, m_lru_bx, m_lru_lambda, m_attn_rel_bias, m_ssm_a_re, m_ssm_a_im, m_ssm_b_re, m_ssm_b_im, m_ssm_c_re, m_ssm_c_im, m_ssm_d, m_ssm_log_step, m_ssm_w_glu, m_w_branch, m_w_out, m_norm_ffn_g, m_w_ffn_gate, m_w_ffn_up, m_w_ffn_down, m_norm_final_g, v_norm_mix_g, v_w_in, v_gate_bias, v_lru_conv_w, v_lru_conv_b, v_lru_wa, v_lru_ba, v_lru_wx, v_lru_bx, v_lru_lambda, v_attn_rel_bias, v_ssm_a_re, v_ssm_a_im, v_ssm_b_re, v_ssm_b_im, v_ssm_c_re, v_ssm_c_im, v_ssm_d, v_ssm_log_step, v_ssm_w_glu, v_w_branch, v_w_out, v_norm_ffn_g, v_w_ffn_gate, v_w_ffn_up, v_w_ffn_down, v_norm_final_g):
    args = locals()
    wts = {k: args[k] for k in _WEIGHTS}
    mom = {k: args["m_" + k] for k in _WEIGHTS}
    vel = {k: args["v_" + k] for k in _WEIGHTS}
    cx, cy, cc = lax.axis_index("x"), lax.axis_index("y"), lax.axis_index("c")
    me = 4 * cx + 2 * cy + cc
    nl = norm_mix_g.shape[0]

    first = _all_gather([wts[k][0].astype(BF16) for k in _BIG] + [wts[k] for k in _SMALL_SHARDED], "gather_weights_0")
    small_gath = dict(zip(_SMALL_SHARDED, first[len(_BIG):]))
    in_flight = {}
    me_arr = me.astype(jnp.int32).reshape(1)
    layer_shape = {k: wts[k].shape[1:] for k in _BIG}

    def get_gath(l, x_in):
        if l == 0:
            gath = dict(zip(_BIG, first[:len(_BIG)]))
        else:
            lands = _split_wait("all", in_flight.pop(("w", l)), x_in, f"gather_wait_{l}")
            gath = {k: a.reshape((NDEV,) + layer_shape[k]) for k, a in zip(_BIG, lands)}
        if l + 1 == nl:
            return gath, None
        after = first[0] if l == 0 else x_in
        cast = [_cast_place(wts[k].reshape(nl, -1, wts[k].shape[-1]), l + 1, me_arr, after, f"cast_{k}_{l + 1}")
                for k in _BIG]
        started = _split_start("all", [a for a, _ in cast], [b for _, b in cast], f"gather_start_{l + 1}")
        in_flight[("w", l + 1)] = started[:-1]
        return gath, started[-1]

    core = cc.astype(jnp.int32).reshape(1)
    chip = (2 * cx + cy).astype(jnp.int32).reshape(1)

    def on_grads(l, gbuf):
        own = [gbuf[k].reshape((NCHIP, 2) + gbuf[k].shape[1:]) for k in _BIG]
        sib = _sibling_exchange(own, f"grad_sibling_exchange_{l}")
        pair, lands = [], []
        for k, a, b in zip(_BIG, own, sib):
            c_ = a.shape[-1]
            p, q = _pair_sum(a.reshape(NCHIP, 2, -1, c_), b.reshape(NCHIP, -1, c_), core, chip, f"pair_sum_{k}_{l}")
            pair.append(p)
            lands.append(q)
        started = _split_start("chips", pair, lands, f"grad_start_{l}")
        in_flight[("g", l)] = started[:-1]
        return started[-1]

    loss, dx, small = _device_step(x[0], loss_target[0], wts, small_gath, get_gath, on_grads)

    res = {k: None for k in _BIG}
    for l in range(nl - 1, -1, -1):
        parts = _split_wait("chips", in_flight.pop(("g", l)), dx, f"grad_wait_{l}")
        for k, p in zip(_BIG, parts):
            c_ = wts[k].shape[-1]
            res[k] = _adamw_reduce(p.reshape(NCHIP, -1, c_), wts[k].reshape(-1, c_), mom[k].reshape(-1, c_),
                                   vel[k].reshape(-1, c_), res[k], l, f"adamw_{k}_{l}")
    out = {k: tuple(a.reshape(wts[k].shape) for a in res[k]) for k in _BIG}

    order = _SMALL + _SMALL_SHARDED
    flat = jnp.concatenate([small[k].reshape(-1) for k in order])
    n_flat = flat.shape[0]
    flat = jnp.pad(flat, (0, (-n_flat) % (FLAT_ROWS * LANE))).reshape(-1, LANE)
    (allflat,) = _all_gather([flat], "gather_small_grads")
    gsum = _sum_rows8(allflat, "sum_small_grads").reshape(-1)
    gs, off = {}, 0
    for k in order:
        n = small[k].size
        gs[k] = gsum[off:off + n]
        off += n
    gs_loc = {}
    for k in _SMALL:
        gs_loc[k] = gs[k].reshape(wts[k].shape)
    ncb = gate_bias.shape[-1]
    gs_loc["gate_bias"] = lax.dynamic_slice_in_dim(gs["gate_bias"].reshape(nl, N_BRANCH, -1), me * ncb, ncb, axis=2)
    ncw = lru_conv_w.shape[-1]
    gs_loc["lru_conv_w"] = lax.dynamic_slice_in_dim(gs["lru_conv_w"].reshape(nl, CONV_W, -1), me * ncw, ncw, axis=2)

    def pack(dct):
        v = jnp.concatenate([dct[k].reshape(-1) for k in order])
        return jnp.pad(v, (0, (-v.shape[0]) % (FLAT_ROWS * LANE))).reshape(-1, LANE)

    dl_f, nm_f, nv_f = _adamw_flat(pack(wts), pack(gs_loc), pack(mom), pack(vel), "adamw_small")
    off = 0
    for k in order:
        n = wts[k].size
        sl = lambda a: a.reshape(-1)[off:off + n].reshape(wts[k].shape)
        out[k] = (gs_loc[k], sl(dl_f), sl(nm_f), sl(nv_f))
        off += n

    loss_total = lax.psum(loss[0, 0], ("x", "y", "c"))
    return (loss_total, dx[None], *[out[k][0] for k in _WEIGHTS], *[out[k][1] for k in _WEIGHTS],
            *[out[k][2] for k in _WEIGHTS], *[out[k][3] for k in _WEIGHTS])
```

```python
import functools
import math

import numpy as np
import jax
import jax.numpy as jnp
from jax import lax
from jax.experimental import pallas as pl
from jax.experimental.pallas import tpu as pltpu

F32 = jnp.float32
BF16 = jnp.bfloat16
LANE = 128
NSEG = 8
NDEV = 8
NCHIP = 4
MESH = pl.DeviceIdType.MESH
VMEM_LIMIT = 56 * 1024 * 1024
FLAT_ROWS = 512

NORM_EPS = 1e-6
CHUNK = 64
ATT_LEFT = 8
ATT_BAND = (ATT_LEFT + 1) * CHUNK
MAX_REL = 128
N_REL = 2 * MAX_REL + 1
N_REL_PAD = 384
ATT_HEADS = 8
MASK_VALUE = -1e30
LRU_C = 8.0
LRU_BLOCKS = 16
SSM_GROUP = 16
SSM_P = 64
CONV_W = 4
N_BRANCH = 3

ADAM_LR = 0.001
ADAM_B1 = 0.9
ADAM_B2 = 0.999
ADAM_EPS = 1e-08
ADAM_WD = 0.01
ADAM_STEP = 10

NN = (((1,), (0,)), ((), ()))
NT = (((1,), (1,)), ((), ()))
TN = (((0,), (0,)), ((), ()))


def _cp(sem=None, vmem=VMEM_LIMIT, **kw):
    return pltpu.CompilerParams(dimension_semantics=sem, vmem_limit_bytes=vmem, **kw)


def _dot(a, b, dn=NN):
    return lax.dot_general(a, b, dn, preferred_element_type=F32)


def _gelu(x):
    c = math.sqrt(2.0 / math.pi)
    return 0.5 * x * (1.0 + jnp.tanh(c * (x + 0.044715 * x * x * x)))


def _gelu_grad(x):
    c = math.sqrt(2.0 / math.pi)
    t = jnp.tanh(c * (x + 0.044715 * x * x * x))
    return 0.5 * (1.0 + t) + 0.5 * x * (1.0 - t * t) * c * (1.0 + 3.0 * 0.044715 * x * x)


def _sigmoid(x):
    return 1.0 / (1.0 + jnp.exp(-x))


def _one_minus_exp(z):
    series = -(z * (1.0 + z * (0.5 + z * (1.0 / 6.0 + z * (1.0 / 24.0)))))
    return jnp.where(z > -0.02, series, 1.0 - jnp.exp(z))


def _softplus_neg(lam):
    e = jnp.exp(-jnp.abs(lam))
    series = e * (1.0 - e * (0.5 - e * (1.0 / 3.0 - e * 0.25)))
    log1p_e = jnp.where(e < 0.02, series, jnp.log(1.0 + e))
    return jnp.maximum(-lam, 0.0) + log1p_e


def _mm(name, dn, grid, n_red, ins, in_specs, out_shape, out_spec, acc_shape, epi=None, alias=None):
    n_extra = len(ins) - 2 - (1 if alias is not None else 0)
    red_axes = tuple(range(len(grid) - n_red, len(grid)))
    red_sizes = tuple(grid[r] for r in red_axes)

    def body(*refs):
        a_ref, b_ref = refs[0], refs[1]
        extra = refs[2:2 + n_extra]
        o_ref, acc = refs[-2], refs[-1]
        if n_red:
            first = functools.reduce(jnp.logical_and, [pl.program_id(r) == 0 for r in red_axes])
            last = functools.reduce(jnp.logical_and,
                                    [pl.program_id(r) == n - 1 for r, n in zip(red_axes, red_sizes)])

            @pl.when(first)
            def _():
                acc[...] = jnp.zeros_like(acc)

            acc[...] += _dot(a_ref[...], b_ref[...], dn)

            @pl.when(last)
            def _():
                r = acc[...]
                if epi is not None:
                    r = epi(r, *[e[...] for e in extra])
                o_ref[...] = r.astype(o_ref.dtype)
        else:
            r = _dot(a_ref[...], b_ref[...], dn)
            if epi is not None:
                r = epi(r, *[e[...] for e in extra])
            o_ref[...] = r.astype(o_ref.dtype)

    specs = list(in_specs)
    kw = {}
    if alias is not None:
        specs.append(pl.BlockSpec(memory_space=pl.ANY))
        kw["input_output_aliases"] = {len(ins) - 1: 0}
    sem = ("parallel",) * (len(grid) - n_red) + ("arbitrary",) * n_red
    return pl.pallas_call(
        body, name=name, grid=grid, in_specs=specs, out_specs=out_spec, out_shape=out_shape,
        scratch_shapes=[pltpu.VMEM(acc_shape, F32)], compiler_params=_cp(sem), **kw)(*ins)


def _loop(n, body, init, unroll=4):
    def block(kb, c):
        for i in range(unroll):
            c = body(kb * unroll + i, c)
        return c

    c = lax.fori_loop(0, n // unroll, block, init)
    for k in range(n - n % unroll, n):
        c = body(k, c)
    return c


def _tile(n, pref):
    if n <= pref:
        return n
    t = pref
    while t >= LANE:
        if n % t == 0 and t % LANE == 0:
            return t
        t -= LANE
    return n


def _rmsnorm_fwd(x, g, name):
    t, d = x.shape
    tm = min(512, t)

    def body(x_ref, g_ref, h_ref):
        xv = x_ref[...]
        r = lax.rsqrt(jnp.mean(xv * xv, axis=-1, keepdims=True) + NORM_EPS)
        h_ref[...] = (xv * r * g_ref[...]).astype(h_ref.dtype)

    return pl.pallas_call(
        body, name=name, grid=(t // tm,),
        in_specs=[pl.BlockSpec((tm, d), lambda i: (i, 0)), pl.BlockSpec((1, d), lambda i: (0, 0))],
        out_specs=pl.BlockSpec((tm, d), lambda i: (i, 0)),
        out_shape=jax.ShapeDtypeStruct((t, d), BF16), compiler_params=_cp(("parallel",)))(x, g)


def _rmsnorm_bwd(x, g, dhs, dres, name):
    t, d = x.shape
    tm = min(256, t)
    n_dh = len(dhs)

    def body(*refs):
        x_ref, g_ref = refs[0], refs[1]
        dh_refs = refs[2:2 + n_dh]
        dres_ref = refs[2 + n_dh]
        dx_ref, dxb_ref, dg_ref = refs[3 + n_dh:]
        xv = x_ref[...]
        r = lax.rsqrt(jnp.mean(xv * xv, axis=-1, keepdims=True) + NORM_EPS)
        xhat = xv * r
        dh = dh_refs[0][...].astype(F32)
        for e in dh_refs[1:]:
            dh = dh + e[...].astype(F32)
        dxh = dh * g_ref[...]
        dx = r * (dxh - xhat * jnp.mean(dxh * xhat, axis=-1, keepdims=True)) + dres_ref[...]
        dx_ref[...] = dx
        dxb_ref[...] = dx.astype(BF16)

        @pl.when(pl.program_id(0) == 0)
        def _():
            dg_ref[...] = jnp.zeros_like(dg_ref)

        dg_ref[...] += jnp.sum(dh * xhat, axis=0, keepdims=True)

    row = pl.BlockSpec((tm, d), lambda i: (i, 0))
    par = pl.BlockSpec((1, d), lambda i: (0, 0))
    return pl.pallas_call(
        body, name=name, grid=(t // tm,),
        in_specs=[row, par] + [row] * n_dh + [row],
        out_specs=[row, row, par],
        out_shape=[jax.ShapeDtypeStruct((t, d), F32), jax.ShapeDtypeStruct((t, d), BF16),
                   jax.ShapeDtypeStruct((1, d), F32)],
        compiler_params=_cp(("arbitrary",)))(x, g, *dhs, dres)


def _loss_head(x, g, target, name):
    t, d = x.shape
    tm = min(256, t)

    def body(x_ref, g_ref, t_ref, loss_ref, dx_ref, dxb_ref, dg_ref):
        xv = x_ref[...]
        r = lax.rsqrt(jnp.mean(xv * xv, axis=-1, keepdims=True) + NORM_EPS)
        xhat = xv * r
        y = xhat * g_ref[...]
        err = y - t_ref[...]
        dy = err * (1.0 / d)
        dxh = dy * g_ref[...]
        dx = r * (dxh - xhat * jnp.mean(dxh * xhat, axis=-1, keepdims=True))
        dx_ref[...] = dx
        dxb_ref[...] = dx.astype(BF16)

        @pl.when(pl.program_id(0) == 0)
        def _():
            dg_ref[...] = jnp.zeros_like(dg_ref)
            loss_ref[...] = jnp.zeros_like(loss_ref)

        dg_ref[...] += jnp.sum(dy * xhat, axis=0, keepdims=True)
        per_tok = jnp.sum(err * err, axis=-1, keepdims=True) * (0.5 / d)
        loss_ref[...] += jnp.sum(per_tok, axis=0, keepdims=True)

    row = pl.BlockSpec((tm, d), lambda i: (i, 0))
    par = pl.BlockSpec((1, d), lambda i: (0, 0))
    return pl.pallas_call(
        body, name=name, grid=(t // tm,),
        in_specs=[row, par, row],
        out_specs=[pl.BlockSpec((1, 1), lambda i: (0, 0)), row, row, par],
        out_shape=[jax.ShapeDtypeStruct((1, 1), F32), jax.ShapeDtypeStruct((t, d), F32),
                   jax.ShapeDtypeStruct((t, d), BF16), jax.ShapeDtypeStruct((1, d), F32)],
        compiler_params=_cp(("arbitrary",)))(x, g, target)


def _merge_fwd(proj, gbias, br_a, br_b, br_c0, gl, d, name):
    t = proj.shape[0]
    tm = min(128, t)
    off = proj.shape[1] // d - N_BRANCH

    def body(g0, g1, g2, gb, a_ref, b_ref, c_ref, gl_ref, o_ref):
        gbv = gb[...]
        s0 = _sigmoid(g0[...].astype(F32) + gbv[0:1])
        s1 = _sigmoid(g1[...].astype(F32) + gbv[1:2])
        s2 = _sigmoid(g2[...].astype(F32) + gbv[2:3])
        brc = c_ref[...].astype(F32) * _sigmoid(gl_ref[...].astype(F32))
        o_ref[...] = (s0 * a_ref[...].astype(F32) + s1 * b_ref[...].astype(F32) + s2 * brc).astype(BF16)

    row = pl.BlockSpec((tm, d), lambda i: (i, 0))
    gs = [pl.BlockSpec((tm, d), functools.partial(lambda i, b: (i, off + b), b=b)) for b in range(N_BRANCH)]
    return pl.pallas_call(
        body, name=name, grid=(t // tm,),
        in_specs=gs + [pl.BlockSpec((N_BRANCH, d), lambda i: (0, 0)), row, row, row, row],
        out_specs=row, out_shape=jax.ShapeDtypeStruct((t, d), BF16),
        compiler_params=_cp(("parallel",)))(proj, proj, proj, gbias, br_a, br_b, br_c0, gl)


def _merge_bwd(proj, gbias, br_a, br_b, br_c0, gl, dmerged, d, name):
    t = proj.shape[0]
    tm = min(128, t)
    off = proj.shape[1] // d - N_BRANCH

    def body(g0, g1, g2, gb, a_ref, b_ref, c_ref, gl_ref, dm_ref, da_ref, db_ref, dc_ref, dgl_ref, dg_ref, dgb_ref):
        gbv = gb[...]
        dm = dm_ref[...].astype(F32)
        s0 = _sigmoid(g0[...].astype(F32) + gbv[0:1])
        s1 = _sigmoid(g1[...].astype(F32) + gbv[1:2])
        s2 = _sigmoid(g2[...].astype(F32) + gbv[2:3])
        sg = _sigmoid(gl_ref[...].astype(F32))
        c0 = c_ref[...].astype(F32)
        brc = c0 * sg
        da_ref[...] = (dm * s0).astype(BF16)
        db_ref[...] = (dm * s1).astype(BF16)
        dbrc = dm * s2
        dc_ref[...] = (dbrc * sg).astype(BF16)
        dgl_ref[...] = (dbrc * c0 * sg * (1.0 - sg)).astype(BF16)
        dp0 = dm * a_ref[...].astype(F32) * s0 * (1.0 - s0)
        dp1 = dm * b_ref[...].astype(F32) * s1 * (1.0 - s1)
        dp2 = dm * brc * s2 * (1.0 - s2)
        dg_ref[:, 0:d] = dp0.astype(BF16)
        dg_ref[:, d:2 * d] = dp1.astype(BF16)
        dg_ref[:, 2 * d:3 * d] = dp2.astype(BF16)

        @pl.when(pl.program_id(0) == 0)
        def _():
            dgb_ref[...] = jnp.zeros_like(dgb_ref)

        dgb_ref[0:1, :] += jnp.sum(dp0, axis=0, keepdims=True)
        dgb_ref[1:2, :] += jnp.sum(dp1, axis=0, keepdims=True)
        dgb_ref[2:3, :] += jnp.sum(dp2, axis=0, keepdims=True)

    row = pl.BlockSpec((tm, d), lambda i: (i, 0))
    gs = [pl.BlockSpec((tm, d), functools.partial(lambda i, b: (i, off + b), b=b)) for b in range(N_BRANCH)]
    par = pl.BlockSpec((N_BRANCH, d), lambda i: (0, 0))
    bf = jax.ShapeDtypeStruct((t, d), BF16)
    return pl.pallas_call(
        body, name=name, grid=(t // tm,),
        in_specs=gs + [par, row, row, row, row, row],
        out_specs=[row, row, row, row, pl.BlockSpec((tm, N_BRANCH * d), lambda i: (i, 0)), par],
        out_shape=[bf, bf, bf, bf, jax.ShapeDtypeStruct((t, N_BRANCH * d), BF16),
                   jax.ShapeDtypeStruct((N_BRANCH, d), F32)],
        compiler_params=_cp(("arbitrary",)))(proj, proj, proj, gbias, br_a, br_b, br_c0, gl, dmerged)


def _ffn_in(h2, wg, wu, name):
    t, d = h2.shape
    f = wg.shape[-1]
    tm = _tile(t, 1024)

    def body(h_ref, g_ref, u_ref, gp_ref, up_ref, act_ref):
        h = h_ref[...]
        g = _dot(h, g_ref[...])
        u = _dot(h, u_ref[...])
        gp_ref[...] = g.astype(BF16)
        up_ref[...] = u.astype(BF16)
        act_ref[...] = (g * _sigmoid(g) * u).astype(BF16)

    wspec = pl.BlockSpec((None, d, f), lambda i, j: (j, 0, 0))
    ospec = pl.BlockSpec((None, tm, f), lambda i, j: (j, i, 0))
    o = jax.ShapeDtypeStruct((NDEV, t, f), BF16)
    return pl.pallas_call(
        body, name=name, grid=(t // tm, NDEV), in_specs=[pl.BlockSpec((tm, d), lambda i, j: (i, 0)), wspec, wspec],
        out_specs=[ospec, ospec, ospec], out_shape=[o, o, o],
        compiler_params=_cp(("parallel", "parallel")))(h2, wg, wu)


def _ffn_dact(dxb, wd, gp, up, name):
    t, d = dxb.shape
    f = wd.shape[1]
    tm = _tile(t, 1024)

    def body(dx_ref, w_ref, g_ref, u_ref, dg_ref, du_ref):
        da = _dot(dx_ref[...], w_ref[...], NT)
        g = g_ref[...].astype(F32)
        u = u_ref[...].astype(F32)
        sg = _sigmoid(g)
        silu = g * sg
        du_ref[...] = (da * silu).astype(BF16)
        dg_ref[...] = (da * u * (sg + silu * (1.0 - sg))).astype(BF16)

    ospec = pl.BlockSpec((None, tm, f), lambda i, j: (j, i, 0))
    o = jax.ShapeDtypeStruct((NDEV, t, f), BF16)
    return pl.pallas_call(
        body, name=name, grid=(t // tm, NDEV),
        in_specs=[pl.BlockSpec((tm, d), lambda i, j: (i, 0)), pl.BlockSpec((None, f, d), lambda i, j: (j, 0, 0)), ospec, ospec],
        out_specs=[ospec, ospec], out_shape=[o, o],
        compiler_params=_cp(("parallel", "parallel")))(dxb, wd, gp, up)


def _ffn_dh(dgp, dup, wg, wu, name):
    _, t, f = dgp.shape
    d = wg.shape[1]
    tm = _tile(t, 1024)
    tk = _tile(d, 1024)

    def body(g_ref, u_ref, wg_ref, wu_ref, o_ref, acc):
        j = pl.program_id(2)

        @pl.when(j == 0)
        def _():
            acc[...] = jnp.zeros_like(acc)

        acc[...] += _dot(g_ref[...], wg_ref[...], NT) + _dot(u_ref[...], wu_ref[...], NT)

        @pl.when(j == NDEV - 1)
        def _():
            o_ref[...] = acc[...]

    zspec = pl.BlockSpec((None, tm, f), lambda i, k, j: (j, i, 0))
    wspec = pl.BlockSpec((None, tk, f), lambda i, k, j: (j, k, 0))
    return pl.pallas_call(
        body, name=name, grid=(t // tm, d // tk, NDEV), in_specs=[zspec, zspec, wspec, wspec],
        out_specs=pl.BlockSpec((tm, tk), lambda i, k, j: (i, k)), out_shape=jax.ShapeDtypeStruct((t, d), F32),
        scratch_shapes=[pltpu.VMEM((tm, tk), F32)],
        compiler_params=_cp(("parallel", "parallel", "arbitrary")))(dgp, dup, wg, wu)


def _gelu_bwd_sum(s5o, dy1, dy2, name):
    t, w = s5o.shape
    tm = min(512, t)

    def body(s_ref, a_ref, b_ref, o_ref):
        o_ref[...] = ((a_ref[...] + b_ref[...]) * _gelu_grad(s_ref[...])).astype(BF16)

    row = pl.BlockSpec((tm, w), lambda i: (i, 0))
    return pl.pallas_call(
        body, name=name, grid=(t // tm,), in_specs=[row, row, row], out_specs=row,
        out_shape=jax.ShapeDtypeStruct((t, w), BF16), compiler_params=_cp(("parallel",)))(s5o, dy1, dy2)


def _lru_gates(xc, wa, ba, wx, bx, sp):
    xcb = xc.astype(BF16)
    r = _sigmoid(_dot(xcb, wa) + ba)
    i = _sigmoid(_dot(xcb, wx) + bx)
    la = -LRU_C * r * sp
    return xcb, r, i, la


def _lru_fwd(proj, cw, cb, wa_t, ba, wx_t, bx, lam, w, name):
    t = proj.shape[0]
    nt = w // LANE
    seg = t // NSEG
    rb = min(512, seg)

    def body(lx_ref, lg_ref, cw_ref, cb_ref, wa_ref, ba_ref, wx_ref, bx_ref, lam_ref, ya_ref, hs_ref, xp, a_s, b_s):
        xp[pl.ds(0, 8), :] = jnp.zeros((8, LANE), F32)
        for r0 in range(0, t, rb):
            xp[pl.ds(8 + r0, rb), :] = lx_ref[pl.ds(r0, rb), :].astype(F32)
        sp = _softplus_neg(lam_ref[...])
        cwv = cw_ref[...]
        for r0 in range(0, t, rb):
            xc = cb_ref[...] + sum(cwv[k:k + 1] * xp[pl.ds(8 + r0 - (CONV_W - 1) + k, rb), :] for k in range(CONV_W))
            _, _, i, la = _lru_gates(xc, wa_ref[...], ba_ref[...], wx_ref[...], bx_ref[...], sp)
            a_s[pl.ds(r0, rb), :] = jnp.exp(la)
            b_s[pl.ds(r0, rb), :] = jnp.sqrt(_one_minus_exp(2.0 * la)) * (i * xc)

        def load(k):
            rows = pl.ds(k, NSEG, stride=seg)
            return a_s[rows, :], b_s[rows, :]

        def step(k, c):
            h, p, a, b = c
            na, nb = load(jnp.minimum(k + 1, seg - 1))
            rows = pl.ds(k, NSEG, stride=seg)
            h = a * h + b
            p = a * p
            b_s[rows, :] = h
            a_s[rows, :] = p
            return h, p, na, nb

        _loop(seg, step, (jnp.zeros((NSEG, LANE), F32), jnp.ones((NSEG, LANE), F32)) + load(0))
        carry = jnp.zeros((1, LANE), F32)
        for s in range(NSEG):
            for r0 in range(s * seg, (s + 1) * seg, rb):
                rows = pl.ds(r0, rb)
                h = b_s[rows, :] + a_s[rows, :] * carry
                hs_ref[rows, :] = h
                ya_ref[rows, :] = (h * _gelu(lg_ref[rows, :].astype(F32))).astype(BF16)
            end = pl.ds((s + 1) * seg - 1, 1)
            carry = b_s[end, :] + a_s[end, :] * carry

    col = lambda c0: pl.BlockSpec((t, LANE), functools.partial(lambda i, c0: (0, c0 + i), c0=c0))
    par = lambda k: pl.BlockSpec((k, LANE), lambda i: (0, i))
    mat = pl.BlockSpec((None, LANE, LANE), lambda i: (i, 0, 0))
    return pl.pallas_call(
        body, name=name, grid=(nt,),
        in_specs=[col(0), col(nt), par(CONV_W), par(1), mat, par(1), mat, par(1), par(1)],
        out_specs=[col(0), col(0)],
        out_shape=[jax.ShapeDtypeStruct((t, w), BF16), jax.ShapeDtypeStruct((t, w), F32)],
        scratch_shapes=[pltpu.VMEM((t + 8, LANE), F32), pltpu.VMEM((t, LANE), F32), pltpu.VMEM((t, LANE), F32)],
        compiler_params=_cp(("parallel",)))(proj, proj, cw, cb, wa_t, ba, wx_t, bx, lam)


def _lru_bwd(proj, hs, dya, cw, cb, wa_t, ba, wx_t, bx, lam, wat_t, wxt_t, w, name):
    t = proj.shape[0]
    nt = w // LANE
    seg = t // NSEG
    rb = min(512, seg)

    def body(lx_ref, lg_ref, hs_ref, dy_ref, cw_ref, cb_ref, wa_ref, ba_ref, wx_ref, bx_ref, lam_ref, wat_ref, wxt_ref,
             dlx_ref, dlg_ref, dcw_ref, dcb_ref, dwa_ref, dba_ref, dwx_ref, dbx_ref, dlam_ref,
             xp, hp, a_s, g_s, q_s, dxc_s):
        z8 = jnp.zeros((8, LANE), F32)
        xp[pl.ds(0, 8), :] = z8
        hp[pl.ds(0, 8), :] = z8
        a_s[pl.ds(t, 8), :] = z8
        dxc_s[pl.ds(t, 8), :] = z8
        lamv = lam_ref[...]
        sp = _softplus_neg(lamv)
        cwv = cw_ref[...]

        def conv(r0):
            return cb_ref[...] + sum(cwv[k:k + 1] * xp[pl.ds(8 + r0 - (CONV_W - 1) + k, rb), :] for k in range(CONV_W))

        for r0 in range(0, t, rb):
            rows = pl.ds(r0, rb)
            xp[pl.ds(8 + r0, rb), :] = lx_ref[rows, :].astype(F32)
            hp[pl.ds(8 + r0, rb), :] = hs_ref[rows, :]
        for r0 in range(0, t, rb):
            rows = pl.ds(r0, rb)
            _, _, _, la = _lru_gates(conv(r0), wa_ref[...], ba_ref[...], wx_ref[...], bx_ref[...], sp)
            a_s[rows, :] = jnp.exp(la)
            g_s[rows, :] = dy_ref[rows, :].astype(F32) * _gelu(lg_ref[rows, :].astype(F32))

        def load(k):
            return a_s[pl.ds(k + 1, NSEG, stride=seg), :], g_s[pl.ds(k, NSEG, stride=seg), :]

        def step(kk, c):
            g, q, an, dh = c
            k = seg - 1 - kk
            nan, ndh = load(jnp.maximum(k - 1, 0))
            rows = pl.ds(k, NSEG, stride=seg)
            g = dh + an * g
            q = an * q
            g_s[rows, :] = g
            q_s[rows, :] = q
            return g, q, nan, ndh

        _loop(seg, step, (jnp.zeros((NSEG, LANE), F32), jnp.ones((NSEG, LANE), F32)) + load(seg - 1))
        nxt = [None] * NSEG
        carry = jnp.zeros((1, LANE), F32)
        for s in range(NSEG - 1, -1, -1):
            nxt[s] = carry
            start = pl.ds(s * seg, 1)
            carry = g_s[start, :] + q_s[start, :] * carry

        zrow = jnp.zeros((1, LANE), F32)
        dsp = zrow
        dba = zrow
        dbx = zrow
        dcb = zrow
        dwa = jnp.zeros((LANE, LANE), F32)
        dwx = jnp.zeros((LANE, LANE), F32)
        for s in range(NSEG):
            for r0 in range(s * seg, (s + 1) * seg, rb):
                rows = pl.ds(r0, rb)
                g = g_s[rows, :] + q_s[rows, :] * nxt[s]
                xc = conv(r0)
                xcb, r, i, la = _lru_gates(xc, wa_ref[...], ba_ref[...], wx_ref[...], bx_ref[...], sp)
                a = jnp.exp(la)
                om = _one_minus_exp(2.0 * la)
                mult = jnp.sqrt(om)
                hprev = hp[pl.ds(8 + r0 - 1, rb), :]
                da = g * hprev
                dmult = g * i * xc
                di = g * mult * xc
                dxc = g * mult * i
                dla = da * a - dmult * (1.0 - om) / mult
                dr = dla * (-LRU_C) * sp
                dsp = dsp + jnp.sum(dla * (-LRU_C) * r, axis=0, keepdims=True)
                dpr = dr * r * (1.0 - r)
                dpi = di * i * (1.0 - i)
                dba = dba + jnp.sum(dpr, axis=0, keepdims=True)
                dbx = dbx + jnp.sum(dpi, axis=0, keepdims=True)
                dprb = dpr.astype(BF16)
                dpib = dpi.astype(BF16)
                dxc = dxc + _dot(dprb, wat_ref[...]) + _dot(dpib, wxt_ref[...])
                dwa = dwa + _dot(xcb, dprb, TN)
                dwx = dwx + _dot(xcb, dpib, TN)
                dxc_s[rows, :] = dxc
                dcb = dcb + jnp.sum(dxc, axis=0, keepdims=True)
                lg = lg_ref[rows, :].astype(F32)
                dlg_ref[rows, :] = (dy_ref[rows, :].astype(F32) * hs_ref[rows, :] * _gelu_grad(lg)).astype(BF16)
        dcw = [zrow] * CONV_W
        for r0 in range(0, t, rb):
            rows = pl.ds(r0, rb)
            dlx = sum(cwv[k:k + 1] * dxc_s[pl.ds(r0 + (CONV_W - 1) - k, rb), :] for k in range(CONV_W))
            dlx_ref[rows, :] = dlx.astype(BF16)
            dxc = dxc_s[rows, :]
            for k in range(CONV_W):
                dcw[k] = dcw[k] + jnp.sum(dxc * xp[pl.ds(8 + r0 - (CONV_W - 1) + k, rb), :], axis=0, keepdims=True)
        dcw_ref[...] = jnp.concatenate(dcw, axis=0)
        dcb_ref[...] = dcb
        dwa_ref[...] = dwa
        dwx_ref[...] = dwx
        dba_ref[...] = dba
        dbx_ref[...] = dbx
        dlam_ref[...] = dsp * (-_sigmoid(-lamv))

    col = lambda c0: pl.BlockSpec((t, LANE), functools.partial(lambda i, c0: (0, c0 + i), c0=c0))
    par = lambda k: pl.BlockSpec((k, LANE), lambda i: (0, i))
    mat = pl.BlockSpec((None, LANE, LANE), lambda i: (i, 0, 0))
    vec = jax.ShapeDtypeStruct((1, w), F32)
    big = lambda: pltpu.VMEM((t + 8, LANE), F32)
    return pl.pallas_call(
        body, name=name, grid=(nt,),
        in_specs=[col(0), col(nt), col(0), col(0), par(CONV_W), par(1), mat, par(1), mat, par(1), par(1), mat, mat],
        out_specs=[col(0), col(0), par(CONV_W), par(1), mat, par(1), mat, par(1), par(1)],
        out_shape=[jax.ShapeDtypeStruct((t, w), BF16), jax.ShapeDtypeStruct((t, w), BF16),
                   jax.ShapeDtypeStruct((CONV_W, w), F32), vec, jax.ShapeDtypeStruct((nt, LANE, LANE), F32), vec,
                   jax.ShapeDtypeStruct((nt, LANE, LANE), F32), vec, vec],
        scratch_shapes=[big(), big(), big(), pltpu.VMEM((t, LANE), F32), pltpu.VMEM((t, LANE), F32), big()],
        compiler_params=_cp(("parallel",)))(proj, proj, hs, dya, cw, cb, wa_t, ba, wx_t, bx, lam, wat_t, wxt_t)


def _attn_scores(q_ref, kp, bias_ref, c, hd):
    r0 = pl.multiple_of(c * CHUNK, CHUNK)
    qc = q_ref[pl.ds(r0, CHUNK), :]
    kb = kp[pl.ds(r0, ATT_BAND), :]
    s = _dot(qc, kb, NT) * (hd ** -0.5) + bias_ref[...]
    kpos = lax.broadcasted_iota(jnp.int32, (CHUNK, ATT_BAND), 1)
    s = jnp.where(kpos + (c - ATT_LEFT) * CHUNK >= 0, s, MASK_VALUE)
    m = jnp.max(s, axis=-1, keepdims=True)
    e = jnp.exp(s - m)
    p = e / jnp.sum(e, axis=-1, keepdims=True)
    return r0, qc, kb, p


def _attn_pad_copy(src_ref, dst, t, pad):
    dst[pl.ds(0, pad), :] = jnp.zeros((pad, dst.shape[1]), dst.dtype)
    rb = min(512, t)
    for r0 in range(0, t, rb):
        dst[pl.ds(pad + r0, rb), :] = src_ref[pl.ds(r0, rb), :]


def _attn_fwd(proj, bias, w, name):
    t = proj.shape[0]
    hd = w // ATT_HEADS
    pad = ATT_LEFT * CHUNK
    qo, ko, vo = 2 * w // hd, 3 * w // hd, 4 * w // hd

    def body(q_ref, k_ref, v_ref, bias_ref, o_ref, kp, vp):
        _attn_pad_copy(k_ref, kp, t, pad)
        _attn_pad_copy(v_ref, vp, t, pad)

        def chunk(c, _):
            r0, _, _, p = _attn_scores(q_ref, kp, bias_ref, c, hd)
            vb = vp[pl.ds(r0, ATT_BAND), :]
            o_ref[pl.ds(r0, CHUNK), :] = _dot(p.astype(BF16), vb).astype(BF16)
            return 0

        lax.fori_loop(0, t // CHUNK, chunk, 0, unroll=2)

    col = lambda c0: pl.BlockSpec((t, hd), functools.partial(lambda h, c0: (0, c0 + h), c0=c0))
    return pl.pallas_call(
        body, name=name, grid=(ATT_HEADS,),
        in_specs=[col(qo), col(ko), col(vo), pl.BlockSpec((None, CHUNK, ATT_BAND), lambda h: (h, 0, 0))],
        out_specs=col(0), out_shape=jax.ShapeDtypeStruct((t, w), BF16),
        scratch_shapes=[pltpu.VMEM((t + pad, hd), BF16), pltpu.VMEM((t + pad, hd), BF16)],
        compiler_params=_cp(("parallel",)))(proj, proj, proj, bias)


def _attn_bwd(proj, bias, do, w, name):
    t = proj.shape[0]
    hd = w // ATT_HEADS
    pad = ATT_LEFT * CHUNK
    qo, ko, vo = 2 * w // hd, 3 * w // hd, 4 * w // hd

    def body(q_ref, k_ref, v_ref, bias_ref, do_ref, dq_ref, dk_ref, dv_ref, db_ref, kp, vp, dkp, dvp):
        _attn_pad_copy(k_ref, kp, t, pad)
        _attn_pad_copy(v_ref, vp, t, pad)
        rb = min(512, t)
        for r0 in range(0, t + pad, rb):
            n = min(rb, t + pad - r0)
            dkp[pl.ds(r0, n), :] = jnp.zeros((n, hd), F32)
            dvp[pl.ds(r0, n), :] = jnp.zeros((n, hd), F32)
        db_ref[...] = jnp.zeros_like(db_ref)
        scale = hd ** -0.5

        def chunk(c, _):
            r0, qc, kb, p = _attn_scores(q_ref, kp, bias_ref, c, hd)
            band = pl.ds(r0, ATT_BAND)
            vb = vp[band, :]
            doc = do_ref[pl.ds(r0, CHUNK), :]
            dp = _dot(doc, vb, NT)
            ds = p * (dp - jnp.sum(dp * p, axis=-1, keepdims=True))
            db_ref[...] += ds
            dsb = ds.astype(BF16)
            dq_ref[pl.ds(r0, CHUNK), :] = (_dot(dsb, kb) * scale).astype(BF16)
            dkp[band, :] += _dot(dsb, qc, TN) * scale
            dvp[band, :] += _dot(p.astype(BF16), doc, TN)
            return 0

        lax.fori_loop(0, t // CHUNK, chunk, 0, unroll=2)
        for r0 in range(0, t, rb):
            dk_ref[pl.ds(r0, rb), :] = dkp[pl.ds(pad + r0, rb), :].astype(BF16)
            dv_ref[pl.ds(r0, rb), :] = dvp[pl.ds(pad + r0, rb), :].astype(BF16)

    col = lambda c0: pl.BlockSpec((t, hd), functools.partial(lambda h, c0: (0, c0 + h), c0=c0))
    tb = pl.BlockSpec((None, CHUNK, ATT_BAND), lambda h: (h, 0, 0))
    o = jax.ShapeDtypeStruct((t, w), BF16)
    return pl.pallas_call(
        body, name=name, grid=(ATT_HEADS,),
        in_specs=[col(qo), col(ko), col(vo), tb, col(0)],
        out_specs=[col(0), col(0), col(0), tb],
        out_shape=[o, o, o, jax.ShapeDtypeStruct((ATT_HEADS, CHUNK, ATT_BAND), F32)],
        scratch_shapes=[pltpu.VMEM((t + pad, hd), BF16), pltpu.VMEM((t + pad, hd), BF16),
                        pltpu.VMEM((t + pad, hd), F32), pltpu.VMEM((t + pad, hd), F32)],
        compiler_params=_cp(("parallel",)))(proj, proj, proj, bias, do)


def _rel_index():
    q_pos = ATT_LEFT * CHUNK + np.arange(CHUNK)
    k_pos = np.arange(ATT_BAND)
    return (np.clip(q_pos[:, None] - k_pos[None, :], -MAX_REL, MAX_REL) + MAX_REL).astype(np.int32).reshape(-1)


def _bias_expand(rel_bias, name):
    nl, nh, _ = rel_bias.shape
    n = CHUNK * ATT_BAND
    kb = n // 8
    idx = jnp.asarray(_rel_index().reshape(1, n))
    tab = jnp.pad(rel_bias, ((0, 0), (0, 0), (0, N_REL_PAD - N_REL)))

    def body(t_ref, i_ref, o_ref):
        onehot = (lax.broadcasted_iota(jnp.int32, (N_REL_PAD, kb), 0) == i_ref[...]).astype(F32)
        o_ref[...] = lax.dot_general(t_ref[...], onehot, NN, precision=lax.Precision.HIGHEST,
                                     preferred_element_type=F32)

    out = pl.pallas_call(
        body, name=name, grid=(nl, n // kb),
        in_specs=[pl.BlockSpec((None, nh, N_REL_PAD), lambda l, j: (l, 0, 0)), pl.BlockSpec((1, kb), lambda l, j: (0, j))],
        out_specs=pl.BlockSpec((None, nh, kb), lambda l, j: (l, 0, j)),
        out_shape=jax.ShapeDtypeStruct((nl, nh, n), F32), compiler_params=_cp(("parallel", "parallel")))(tab, idx)
    return out.reshape(nl, nh, CHUNK, ATT_BAND)


def _bias_reduce(dbias, name):
    nl, nh = dbias.shape[:2]
    n = CHUNK * ATT_BAND
    kb = n // 8
    idx = jnp.asarray(_rel_index().reshape(n, 1))

    def body(d_ref, i_ref, o_ref):
        @pl.when(pl.program_id(1) == 0)
        def _():
            o_ref[...] = jnp.zeros_like(o_ref)

        onehot = (lax.broadcasted_iota(jnp.int32, (kb, N_REL_PAD), 1) == i_ref[...]).astype(F32)
        o_ref[...] += lax.dot_general(d_ref[...], onehot, NN, precision=lax.Precision.HIGHEST,
                                      preferred_element_type=F32)

    out = pl.pallas_call(
        body, name=name, grid=(nl, n // kb),
        in_specs=[pl.BlockSpec((None, nh, kb), lambda l, j: (l, 0, j)), pl.BlockSpec((kb, 1), lambda l, j: (j, 0))],
        out_specs=pl.BlockSpec((None, nh, N_REL_PAD), lambda l, j: (l, 0, 0)),
        out_shape=jax.ShapeDtypeStruct((nl, nh, N_REL_PAD), F32),
        compiler_params=_cp(("parallel", "arbitrary")))(dbias.reshape(nl, nh, n), idx)
    return out[:, :, :N_REL]


def _s5_prep_math(a_re, a_im, ls, btr, bti):
    step = jnp.exp(ls)
    mag = jnp.exp(a_re * step)
    ang = a_im * step
    lr = mag * jnp.cos(ang)
    li = mag * jnp.sin(ang)
    den = a_re * a_re + a_im * a_im
    nr = lr - 1.0
    cr = (nr * a_re + li * a_im) / den
    ci = (li * a_re - nr * a_im) / den
    bbr = cr[:, None, :] * btr - ci[:, None, :] * bti
    bbi = cr[:, None, :] * bti + ci[:, None, :] * btr
    return lr, li, bbr, bbi


def _s5_prep(a_re, a_im, ls, btr, bti, name):
    nl, g, p = a_re.shape
    h = btr.shape[2]

    def body(ar, ai, l_ref, br, bi, o1, o2, o3, o4):
        r = _s5_prep_math(ar[...], ai[...], l_ref[...], br[...], bi[...])
        o1[...], o2[...], o3[...], o4[...] = r

    m2 = pl.BlockSpec((None, g, p), lambda l: (l, 0, 0))
    m1 = pl.BlockSpec((None, g, 1), lambda l: (l, 0, 0))
    m3 = pl.BlockSpec((None, g, h, p), lambda l: (l, 0, 0, 0))
    s2 = jax.ShapeDtypeStruct((nl, g, p), F32)
    s3 = jax.ShapeDtypeStruct((nl, g, h, p), F32)
    return pl.pallas_call(body, name=name, grid=(nl,), in_specs=[m2, m2, m1, m3, m3], out_specs=[m2, m2, m3, m3],
                          out_shape=[s2, s2, s3, s3], compiler_params=_cp(("parallel",)))(a_re, a_im, ls, btr, bti)


def _s5_prep_bwd(a_re, a_im, ls, btr, bti, dlr, dli, dbbr, dbbi, name):
    nl, g, p = a_re.shape
    h = btr.shape[2]

    def body(ar, ai, l_ref, br, bi, g1, g2, g3, g4, o1, o2, o3, o4, o5):
        _, vjp = jax.vjp(_s5_prep_math, ar[...], ai[...], l_ref[...], br[...], bi[...])
        o1[...], o2[...], o3[...], o4[...], o5[...] = vjp((g1[...], g2[...], g3[...], g4[...]))

    m2 = pl.BlockSpec((None, g, p), lambda l: (l, 0, 0))
    m1 = pl.BlockSpec((None, g, 1), lambda l: (l, 0, 0))
    m3 = pl.BlockSpec((None, g, h, p), lambda l: (l, 0, 0, 0))
    s2 = jax.ShapeDtypeStruct((nl, g, p), F32)
    s1 = jax.ShapeDtypeStruct((nl, g, 1), F32)
    s3 = jax.ShapeDtypeStruct((nl, g, h, p), F32)
    return pl.pallas_call(body, name=name, grid=(nl,), in_specs=[m2, m2, m1, m3, m3, m2, m2, m3, m3],
                          out_specs=[m2, m2, m1, m3, m3], out_shape=[s2, s2, s1, s3, s3],
                          compiler_params=_cp(("parallel",)))(a_re, a_im, ls, btr, bti, dlr, dli, dbbr, dbbi)


def _cpow(lr, li, n):
    rr = ri = None
    br, bi = lr, li
    while n:
        if n & 1:
            rr, ri = (br, bi) if rr is None else (rr * br - ri * bi, rr * bi + ri * br)
        n >>= 1
        if n:
            br, bi = br * br - bi * bi, 2.0 * br * bi
    return rr, ri


def _permute_rows(src, dst, seg, inverse):
    def body(k, _):
        tile = pl.ds(pl.multiple_of(k * NSEG, NSEG), NSEG)
        spread = pl.ds(k, NSEG, stride=seg)
        if inverse:
            dst[spread, :] = src[tile, :]
        else:
            dst[tile, :] = src[spread, :]
        return 0

    lax.fori_loop(0, seg, body, 0, unroll=8)


def _s5_states(up, bbt_ref, lam_ref, xs, t):
    seg = t // NSEG
    rb = min(512, t)
    nj = xs.shape[0]
    nh = nj // 2
    for r0 in range(0, t, rb):
        bu = _dot(up[pl.ds(r0, rb), :].astype(BF16), bbt_ref[...])
        for j in range(nj):
            xs[j, pl.ds(r0, rb), :] = bu[:, j * LANE:(j + 1) * LANE]
    lamv = lam_ref[...]
    lr = [jnp.broadcast_to(lamv[j:j + 1], (NSEG, LANE)) for j in range(nh)]
    li = [jnp.broadcast_to(lamv[nh + j:nh + j + 1], (NSEG, LANE)) for j in range(nh)]
    zero = jnp.zeros((NSEG, LANE), F32)

    def load(k):
        rows = pl.ds(pl.multiple_of(k * NSEG, NSEG), NSEG)
        return (tuple(xs.at[j][rows, :] for j in range(nh)), tuple(xs.at[nh + j][rows, :] for j in range(nh)))

    def step(k, c):
        xr, xi, br, bi = c
        nbr, nbi = load(jnp.minimum(k + 1, seg - 1))
        rows = pl.ds(pl.multiple_of(k * NSEG, NSEG), NSEG)
        nxr, nxi = [], []
        for j in range(nh):
            r = lr[j] * xr[j] - li[j] * xi[j] + br[j]
            i = lr[j] * xi[j] + li[j] * xr[j] + bi[j]
            xs.at[j][rows, :] = r
            xs.at[nh + j][rows, :] = i
            nxr.append(r)
            nxi.append(i)
        return tuple(nxr), tuple(nxi), nbr, nbi

    xr, xi, _, _ = _loop(seg, step, ((zero,) * nh, (zero,) * nh) + load(0))
    pw = [_cpow(lr[j], li[j], seg) for j in range(nh)]
    sub = lax.broadcasted_iota(jnp.int32, (NSEG, LANE), 0)
    ctr, cti = [], []
    for j in range(nh):
        plr, pli = pw[j]
        cr, ci = zero, zero
        for s in range(1, NSEG):
            er = pltpu.roll(xr[j], 1, 0)
            ei = pltpu.roll(xi[j], 1, 0)
            pr_ = pltpu.roll(cr, 1, 0)
            pi_ = pltpu.roll(ci, 1, 0)
            nr = er + plr * pr_ - pli * pi_
            ni = ei + plr * pi_ + pli * pr_
            cr = jnp.where(sub == s, nr, cr)
            ci = jnp.where(sub == s, ni, ci)
        ctr.append(cr)
        cti.append(ci)

    def fix(k, c):
        rr, ri, xr_, xi_ = c
        nxr, nxi = load(jnp.minimum(k + 1, seg - 1))
        rows = pl.ds(pl.multiple_of(k * NSEG, NSEG), NSEG)
        nr, ni = [], []
        for j in range(nh):
            r = lr[j] * rr[j] - li[j] * ri[j]
            i = lr[j] * ri[j] + li[j] * rr[j]
            xs.at[j][rows, :] = xr_[j] + r
            xs.at[nh + j][rows, :] = xi_[j] + i
            nr.append(r)
            ni.append(i)
        return tuple(nr), tuple(ni), nxr, nxi

    _loop(seg, fix, (tuple(ctr), tuple(cti)) + load(0))
    return lr, li, pw


def _s5_fwd(proj, bbt, cmat, lam, dvec, w, name):
    t = proj.shape[0]
    nt = w // LANE
    nj = bbt.shape[2] // LANE
    seg = t // NSEG
    rb = min(512, t)
    uo = 5 * nt

    def body(u_ref, bbt_ref, cm_ref, lam_ref, d_ref, s5o_ref, yc_ref, xs, uf, up):
        for r0 in range(0, t, rb):
            uf[pl.ds(r0, rb), :] = u_ref[pl.ds(r0, rb), :].astype(F32)
        _permute_rows(uf, up, seg, False)
        _s5_states(up, bbt_ref, lam_ref, xs, t)
        for r0 in range(0, t, rb):
            rows = pl.ds(r0, rb)
            y = d_ref[...] * up[rows, :]
            for j in range(nj):
                y = y + _dot(xs.at[j][rows, :].astype(BF16), cm_ref[pl.ds(j * LANE, LANE), :])
            uf[rows, :] = y
        _permute_rows(uf, s5o_ref, seg, True)
        for r0 in range(0, t, rb):
            rows = pl.ds(r0, rb)
            yc_ref[rows, :] = _gelu(s5o_ref[rows, :]).astype(BF16)

    col = lambda c0: pl.BlockSpec((t, LANE), functools.partial(lambda i, c0: (0, c0 + i), c0=c0))
    return pl.pallas_call(
        body, name=name, grid=(nt,),
        in_specs=[col(uo), pl.BlockSpec((None, LANE, nj * LANE), lambda i: (i, 0, 0)),
                  pl.BlockSpec((None, nj * LANE, LANE), lambda i: (i, 0, 0)),
                  pl.BlockSpec((None, nj, LANE), lambda i: (i, 0, 0)), pl.BlockSpec((1, LANE), lambda i: (0, i))],
        out_specs=[col(0), col(0)],
        out_shape=[jax.ShapeDtypeStruct((t, w), F32), jax.ShapeDtypeStruct((t, w), BF16)],
        scratch_shapes=[pltpu.VMEM((nj, t, LANE), F32), pltpu.VMEM((t, LANE), F32), pltpu.VMEM((t, LANE), F32)],
        compiler_params=_cp(("parallel",)))(proj, bbt, cmat, lam, dvec)


def _s5_bwd(proj, dy, bbt, bbm, cmt, lam, dvec, w, name):
    t = proj.shape[0]
    nt = w // LANE
    nj = bbt.shape[2] // LANE
    nh = nj // 2
    seg = t // NSEG
    rb = min(512, t)
    uo = 5 * nt

    def body(u_ref, dy_ref, bbt_ref, bbm_ref, cmt_ref, lam_ref, d_ref, du_ref, dbbt_ref, dcm_ref, dlam_ref, dd_ref,
             xs, gs, uf, up, dyp):
        for r0 in range(0, t, rb):
            uf[pl.ds(r0, rb), :] = u_ref[pl.ds(r0, rb), :].astype(F32)
        _permute_rows(uf, up, seg, False)
        for r0 in range(0, t, rb):
            uf[pl.ds(r0, rb), :] = dy_ref[pl.ds(r0, rb), :].astype(F32)
        _permute_rows(uf, dyp, seg, False)
        lr, li, pw = _s5_states(up, bbt_ref, lam_ref, xs, t)
        dcm = [jnp.zeros((LANE, LANE), F32) for _ in range(nj)]
        for r0 in range(0, t, rb):
            rows = pl.ds(r0, rb)
            dyb = dyp[rows, :].astype(BF16)
            dx = _dot(dyb, cmt_ref[...])
            for j in range(nj):
                gs.at[j][rows, :] = dx[:, j * LANE:(j + 1) * LANE]
                dcm[j] = dcm[j] + _dot(xs.at[j][rows, :].astype(BF16), dyb, TN)
        for j in range(nj):
            dcm_ref[pl.ds(j * LANE, LANE), :] = dcm[j]
        zero = jnp.zeros((NSEG, LANE), F32)
        sub = lax.broadcasted_iota(jnp.int32, (NSEG, LANE), 0)
        tile = lambda k: pl.ds(pl.multiple_of(k * NSEG, NSEG), NSEG)

        def gload(k):
            return (tuple(gs.at[j][tile(k), :] for j in range(nh)), tuple(gs.at[nh + j][tile(k), :] for j in range(nh)))

        def xload(k):
            return (tuple(xs.at[j][tile(k), :] for j in range(nh)), tuple(xs.at[nh + j][tile(k), :] for j in range(nh)))

        def step(kk, c):
            gr, gi, dr, di = c
            k = seg - 1 - kk
            ndr, ndi = gload(jnp.maximum(k - 1, 0))
            ngr, ngi = [], []
            for j in range(nh):
                r = lr[j] * gr[j] + li[j] * gi[j] + dr[j]
                i = lr[j] * gi[j] - li[j] * gr[j] + di[j]
                gs.at[j][tile(k), :] = r
                gs.at[nh + j][tile(k), :] = i
                ngr.append(r)
                ngi.append(i)
            return tuple(ngr), tuple(ngi), ndr, ndi

        g0r, g0i, _, _ = _loop(seg, step, ((zero,) * nh, (zero,) * nh) + gload(seg - 1))
        ctr, cti = [], []
        for j in range(nh):
            plr, pli = pw[j]
            cr, ci = zero, zero
            for s in range(NSEG - 2, -1, -1):
                sr = pltpu.roll(g0r[j], NSEG - 1, 0)
                si = pltpu.roll(g0i[j], NSEG - 1, 0)
                pr_ = pltpu.roll(cr, NSEG - 1, 0)
                pi_ = pltpu.roll(ci, NSEG - 1, 0)
                nr = sr + plr * pr_ + pli * pi_
                ni = si + plr * pi_ - pli * pr_
                cr = jnp.where(sub == s, nr, cr)
                ci = jnp.where(sub == s, ni, ci)
            ctr.append(cr)
            cti.append(ci)

        def fix_step(k, c, xr, xi, gr, gi):
            rr, ri, ar, ai = c
            nr, ni, nar, nai = [], [], [], []
            for j in range(nh):
                r = lr[j] * rr[j] + li[j] * ri[j]
                i = lr[j] * ri[j] - li[j] * rr[j]
                g_r = gr[j] + r
                g_i = gi[j] + i
                gs.at[j][tile(k), :] = g_r
                gs.at[nh + j][tile(k), :] = g_i
                nar.append(ar[j] + g_r * xr[j] + g_i * xi[j])
                nai.append(ai[j] + g_i * xr[j] - g_r * xi[j])
                nr.append(r)
                ni.append(i)
            return tuple(nr), tuple(ni), tuple(nar), tuple(nai)

        def fix(kk, c):
            k = seg - 1 - kk
            nk = jnp.maximum(k - 1, 0)
            ngr, ngi = gload(nk)
            nxr, nxi = xload(jnp.maximum(nk - 1, 0))
            return fix_step(k, c[:4], c[6], c[7], c[4], c[5]) + (ngr, ngi, nxr, nxi)

        init = (tuple(ctr), tuple(cti), (zero,) * nh, (zero,) * nh) + gload(seg - 1) + xload(seg - 2)
        c = _loop(seg - 1, fix, init)
        lxr, lxi = xload(seg - 1)
        shift = lambda v: jnp.where(sub == 0, 0.0, pltpu.roll(v, 1, 0))
        _, _, ar, ai = fix_step(0, c[:4], tuple(shift(v) for v in lxr), tuple(shift(v) for v in lxi), c[4], c[5])
        for j in range(nh):
            dlam_ref[pl.ds(j, 1), :] = jnp.sum(ar[j], axis=0, keepdims=True)
            dlam_ref[pl.ds(nh + j, 1), :] = jnp.sum(ai[j], axis=0, keepdims=True)
        dbb = [jnp.zeros((LANE, LANE), F32) for _ in range(nj)]
        dd = jnp.zeros((1, LANE), F32)
        for r0 in range(0, t, rb):
            rows = pl.ds(r0, rb)
            uv = up[rows, :]
            ub = uv.astype(BF16)
            dyf = dyp[rows, :]
            du = d_ref[...] * dyf
            dd = dd + jnp.sum(dyf * uv, axis=0, keepdims=True)
            for j in range(nj):
                gb = gs.at[j][rows, :].astype(BF16)
                du = du + _dot(gb, bbm_ref[pl.ds(j * LANE, LANE), :])
                dbb[j] = dbb[j] + _dot(ub, gb, TN)
            uf[rows, :] = du
        for j in range(nj):
            dbbt_ref[:, pl.ds(j * LANE, LANE)] = dbb[j]
        dd_ref[...] = dd
        _permute_rows(uf, up, seg, True)
        for r0 in range(0, t, rb):
            rows = pl.ds(r0, rb)
            du_ref[rows, :] = up[rows, :].astype(BF16)

    col = lambda c0: pl.BlockSpec((t, LANE), functools.partial(lambda i, c0: (0, c0 + i), c0=c0))
    wide = pl.BlockSpec((None, LANE, nj * LANE), lambda i: (i, 0, 0))
    tall = pl.BlockSpec((None, nj * LANE, LANE), lambda i: (i, 0, 0))
    lam_s = pl.BlockSpec((None, nj, LANE), lambda i: (i, 0, 0))
    vec = pl.BlockSpec((1, LANE), lambda i: (0, i))
    flat = lambda: pltpu.VMEM((t, LANE), F32)
    return pl.pallas_call(
        body, name=name, grid=(nt,),
        in_specs=[col(uo), col(0), wide, tall, wide, lam_s, vec],
        out_specs=[col(0), wide, tall, lam_s, vec],
        out_shape=[jax.ShapeDtypeStruct((t, w), BF16), jax.ShapeDtypeStruct((nt, LANE, nj * LANE), F32),
                   jax.ShapeDtypeStruct((nt, nj * LANE, LANE), F32), jax.ShapeDtypeStruct((nt, nj, LANE), F32),
                   jax.ShapeDtypeStruct((1, w), F32)],
        scratch_shapes=[pltpu.VMEM((nj, t, LANE), F32), pltpu.VMEM((nj, t, LANE), F32), flat(), flat(), flat()],
        compiler_params=_cp(("parallel",)))(proj, dy, bbt, bbm, cmt, lam, dvec)


def _block_diag(x, nt):
    nb, r, c = x.shape
    b = nb // nt
    eye = jnp.eye(b, dtype=x.dtype)
    return jnp.einsum("ibrc,bk->ibrkc", x.reshape(nt, b, r, c), eye).reshape(nt, b * r, b * c)


def _block_diag_extract(x, b):
    nt, br, bc = x.shape
    r, c = br // b, bc // b
    eye = jnp.eye(b, dtype=x.dtype)
    return jnp.einsum("ibrkc,bk->ibrc", x.reshape(nt, b, r, b, c), eye).reshape(nt * b, r, c)


def _adamw_math(w, g, m, v):
    m = ADAM_B1 * m + (1.0 - ADAM_B1) * g
    v = ADAM_B2 * v + (1.0 - ADAM_B2) * (g * g)
    m_hat = m / (1.0 - ADAM_B1 ** ADAM_STEP)
    v_hat = v / (1.0 - ADAM_B2 ** ADAM_STEP)
    delta = -ADAM_LR * (m_hat / (jnp.sqrt(v_hat) + ADAM_EPS) + ADAM_WD * w)
    return delta, m, v


def _adamw_reduce(parts, w, m, v, prev, l, name):
    n, r, c = parts.shape
    tr = r
    for cand in (512, 256, 128, 64, 32, 16, 8):
        if r % cand == 0 and cand * c * 4 <= 2 * 1024 * 1024:
            tr = cand
            break
    nb = r // tr
    if prev is None:
        prev = [lax.empty(w.shape, F32) for _ in range(4)]

    def body(p_ref, w_ref, m_ref, v_ref, *rest):
        g_ref, d_ref, nm_ref, nv_ref = rest[4:]
        g = p_ref[0].astype(F32)
        for q in range(1, n):
            g = g + p_ref[q].astype(F32)
        g_ref[...] = g
        d_ref[...], nm_ref[...], nv_ref[...] = _adamw_math(w_ref[...], g, m_ref[...], v_ref[...])

    row = pl.BlockSpec((tr, c), lambda i: (l * nb + i, 0))
    any_spec = pl.BlockSpec(memory_space=pl.ANY)
    o = jax.ShapeDtypeStruct(w.shape, F32)
    return pl.pallas_call(
        body, name=name, grid=(nb,),
        in_specs=[pl.BlockSpec((n, tr, c), lambda i: (0, i, 0)), row, row, row] + [any_spec] * 4,
        out_specs=[row, row, row, row], out_shape=[o, o, o, o],
        input_output_aliases={4: 0, 5: 1, 6: 2, 7: 3},
        compiler_params=_cp(("parallel",)))(parts, w, m, v, *prev)


def _adamw_flat(w, g, m, v, name):
    r, c = w.shape

    def body(w_ref, g_ref, m_ref, v_ref, d_ref, nm_ref, nv_ref):
        d_ref[...], nm_ref[...], nv_ref[...] = _adamw_math(w_ref[...], g_ref[...], m_ref[...], v_ref[...])

    o = jax.ShapeDtypeStruct((r, c), F32)
    row = pl.BlockSpec((FLAT_ROWS, c), lambda i: (i, 0))
    return pl.pallas_call(body, name=name, grid=(r // FLAT_ROWS,), in_specs=[row] * 4, out_specs=[row] * 3,
                          out_shape=[o, o, o], compiler_params=_cp(("parallel",)))(w, g, m, v)


def _sum_rows8(x, name):
    n, r, c = x.shape

    def body(x_ref, o_ref):
        acc = x_ref[0]
        for q in range(1, n):
            acc = acc + x_ref[q]
        o_ref[...] = acc

    return pl.pallas_call(
        body, name=name, grid=(r // FLAT_ROWS,), in_specs=[pl.BlockSpec((n, FLAT_ROWS, c), lambda i: (0, i, 0))],
        out_specs=pl.BlockSpec((FLAT_ROWS, c), lambda i: (i, 0)), out_shape=jax.ShapeDtypeStruct((r, c), F32),
        compiler_params=_cp(("parallel",)))(x)


def _row_tile(r, row_bytes, limit=2 * 1024 * 1024):
    for cand in (512, 256, 128, 64, 32, 16):
        if r % cand == 0 and cand * row_bytes <= limit:
            return cand
    return r


def _pair_sum(own, sib, core, chip, name):
    _, _, r, c = own.shape
    tr = _row_tile(r, 2 * c)

    def body(core_ref, chip_ref, a_ref, b_ref, o_ref, land_ref):
        s = (a_ref[...].astype(F32) + b_ref[...].astype(F32)).astype(BF16)
        o_ref[...] = s

        @pl.when(pl.program_id(1) == chip_ref[0])
        def _():
            land_ref[...] = s

    grid_spec = pltpu.PrefetchScalarGridSpec(
        num_scalar_prefetch=2, grid=(r // tr, NCHIP),
        in_specs=[pl.BlockSpec((None, None, tr, c), lambda i, p, cr, hr: (p, cr[0], i, 0)),
                  pl.BlockSpec((None, tr, c), lambda i, p, cr, hr: (p, i, 0))],
        out_specs=[pl.BlockSpec((None, tr, c), lambda i, p, cr, hr: (p, i, 0)),
                   pl.BlockSpec((None, tr, c), lambda i, p, cr, hr: (hr[0], i, 0))])
    o = jax.ShapeDtypeStruct((NCHIP, r, c), BF16)
    return pl.pallas_call(body, name=name, grid_spec=grid_spec, out_shape=[o, o],
                          compiler_params=_cp(("parallel", "arbitrary")))(core, chip, own, sib)


def _cast_place(w, l, me, after, name, dtype=BF16):
    _, r, c = w.shape
    tr = _row_tile(r, 4 * c)

    def body(m_ref, w_ref, a_ref, o_ref, land_ref):
        v = w_ref[...].astype(dtype)
        o_ref[...] = v
        land_ref[...] = v

    grid_spec = pltpu.PrefetchScalarGridSpec(
        num_scalar_prefetch=1, grid=(r // tr,),
        in_specs=[pl.BlockSpec((None, tr, c), lambda i, mr: (l, i, 0)), pl.BlockSpec(memory_space=pl.ANY)],
        out_specs=[pl.BlockSpec((tr, c), lambda i, mr: (i, 0)), pl.BlockSpec((None, tr, c), lambda i, mr: (mr[0], i, 0))])
    return pl.pallas_call(body, name=name, grid_spec=grid_spec,
                          out_shape=[jax.ShapeDtypeStruct((r, c), dtype), jax.ShapeDtypeStruct((NDEV, r, c), dtype)],
                          compiler_params=_cp(("parallel",)))(me, w, after)


def _all_gather(xs, name):
    n = len(xs)

    def body(*refs):
        x_refs, o_refs = refs[:n], refs[n:2 * n]
        send_sems, recv_sems, local_sems = refs[2 * n:]
        x, y, c = lax.axis_index("x"), lax.axis_index("y"), lax.axis_index("c")
        me, sibling = (x, y, c), (x, y, 1 - c)
        chips = [(1 - x, y), (x, 1 - y), (1 - x, 1 - y)]

        def copy(t, k, block, to, src=None):
            dst = o_refs[t].at[4 * block[0] + 2 * block[1] + block[2]]
            return pltpu.make_async_remote_copy(
                src_ref=dst if src is None else src, dst_ref=dst,
                send_sem=send_sems.at[7 * t + k], recv_sem=recv_sems.at[7 * t + k],
                device_id=to, device_id_type=MESH)

        mine, first, passed = [], [], []
        for t in range(n):
            cp = pltpu.make_async_copy(x_refs[t], o_refs[t].at[4 * x + 2 * y + c], local_sems.at[t])
            cp.start()
            mine.append(cp)
            first.append(copy(t, 0, me, sibling, src=x_refs[t]))
            first += [copy(t, 1 + j, me, (*chip, c), src=x_refs[t]) for j, chip in enumerate(chips)]
        for cp in first:
            cp.start()
        for t in range(n):
            for j, chip in enumerate(chips):
                copy(t, 1 + j, (*chip, c), me).wait_recv()
                cp = copy(t, 4 + j, (*chip, c), sibling)
                cp.start()
                passed.append(cp)
        for t in range(n):
            copy(t, 0, sibling, me).wait_recv()
            for j, chip in enumerate(chips):
                copy(t, 4 + j, (*chip, 1 - c), me).wait_recv()
        for cp in first + passed:
            cp.wait_send()
        for cp in mine:
            cp.wait()

    any_spec = pl.BlockSpec(memory_space=pl.ANY)
    return pl.pallas_call(
        body, name=name, in_specs=[any_spec] * n, out_specs=[any_spec] * n,
        out_shape=[jax.ShapeDtypeStruct((NDEV,) + a.shape, a.dtype) for a in xs],
        scratch_shapes=[pltpu.SemaphoreType.DMA((7 * n,)), pltpu.SemaphoreType.DMA((7 * n,)),
                        pltpu.SemaphoreType.DMA((n,))],
        )(*xs)


def _sibling_exchange(xs, name):
    n = len(xs)

    def body(*refs):
        x_refs, o_refs = refs[:n], refs[n:2 * n]
        send_sems, recv_sems = refs[2 * n:]
        x, y, c = lax.axis_index("x"), lax.axis_index("y"), lax.axis_index("c")
        cps = []
        for t in range(n):
            cp = pltpu.make_async_remote_copy(
                src_ref=x_refs[t].at[:, 1 - c], dst_ref=o_refs[t], send_sem=send_sems.at[t], recv_sem=recv_sems.at[t],
                device_id=(x, y, 1 - c), device_id_type=MESH)
            cp.start()
            cps.append(cp)
        for cp in cps:
            cp.wait()

    any_spec = pl.BlockSpec(memory_space=pl.ANY)
    return pl.pallas_call(
        body, name=name, in_specs=[any_spec] * n, out_specs=[any_spec] * n,
        out_shape=[jax.ShapeDtypeStruct((a.shape[0],) + a.shape[2:], a.dtype) for a in xs],
        scratch_shapes=[pltpu.SemaphoreType.DMA((n,)), pltpu.SemaphoreType.DMA((n,))],
        )(*xs)


def _split_copies(kind, srcs, lands, send_sems, recv_sems):
    x, y, c = lax.axis_index("x"), lax.axis_index("y"), lax.axis_index("c")
    flip = lambda v, b: 1 - v if b else v
    cps = []
    for t in range(len(srcs)):
        if kind == "all":
            for mask in range(1, NDEV):
                peer = (flip(x, mask & 4), flip(y, mask & 2), flip(c, mask & 1))
                cps.append(pltpu.make_async_remote_copy(
                    src_ref=srcs[t], dst_ref=lands[t].at[4 * x + 2 * y + c],
                    send_sem=send_sems.at[(NDEV - 1) * t + mask - 1], recv_sem=recv_sems.at[(NDEV - 1) * t + mask - 1],
                    device_id=peer, device_id_type=MESH))
        else:
            for j, (px, py) in enumerate([(1 - x, y), (x, 1 - y), (1 - x, 1 - y)]):
                cps.append(pltpu.make_async_remote_copy(
                    src_ref=srcs[t].at[2 * px + py], dst_ref=lands[t].at[2 * x + y],
                    send_sem=send_sems.at[3 * t + j], recv_sem=recv_sems.at[3 * t + j],
                    device_id=(px, py, c), device_id_type=MESH))
    return cps


def _split_start(kind, srcs, lands, name):
    n = len(srcs)
    ncp = n * (NDEV - 1 if kind == "all" else NCHIP - 1)
    arrs = list(srcs) + list(lands)

    def body(*refs):
        for cp in _split_copies(kind, refs[:n], refs[n:2 * n], refs[2 * n], refs[2 * n + 1]):
            cp.start()
        refs[-1][...] = jnp.zeros_like(refs[-1])

    hbm = pl.BlockSpec(memory_space=pltpu.HBM)
    sem = pl.BlockSpec(memory_space=pltpu.SEMAPHORE)
    return pl.pallas_call(
        body, name=name,
        out_shape=(pltpu.SemaphoreType.DMA((ncp,)), pltpu.SemaphoreType.DMA((ncp,)),
                   *[pltpu.HBM(a.shape, a.dtype) for a in arrs], jax.ShapeDtypeStruct((8, LANE), F32)),
        in_specs=[hbm] * (2 * n), out_specs=(sem, sem, *[hbm] * (2 * n), pl.BlockSpec(memory_space=pltpu.VMEM)),
        input_output_aliases={i: 2 + i for i in range(2 * n)},
        compiler_params=pltpu.CompilerParams(has_side_effects=pltpu.SideEffectType.DATAFLOW_SIDE_EFFECTING),
    )(*[pltpu.with_memory_space_constraint(a, pltpu.HBM) for a in arrs])


def _split_wait(kind, started, after, name):
    send_sems, recv_sems, thru = started[0], started[1], list(started[2:])
    n = len(thru) // 2

    def body(*refs):
        for cp in _split_copies(kind, refs[:n], refs[n:2 * n], refs[2 * n], refs[2 * n + 1]):
            cp.wait_send()
            cp.wait_recv()

    hbm = pl.BlockSpec(memory_space=pltpu.HBM)
    sem = pl.BlockSpec(memory_space=pltpu.SEMAPHORE)
    outs = pl.pallas_call(
        body, name=name, out_shape=tuple(pltpu.HBM(a.shape, a.dtype) for a in thru),
        in_specs=[hbm] * (2 * n) + [sem, sem, pl.BlockSpec(memory_space=pl.ANY)], out_specs=tuple([hbm] * (2 * n)),
        input_output_aliases={i: i for i in range(2 * n)},
        compiler_params=pltpu.CompilerParams(has_side_effects=pltpu.SideEffectType.DATAFLOW_SIDE_EFFECTING),
    )(*thru, send_sems, recv_sems, after)
    return list(outs[n:])


_SMALL = ["norm_mix_g", "lru_conv_b", "lru_wa", "lru_ba", "lru_wx", "lru_bx", "lru_lambda", "attn_rel_bias",
          "ssm_a_re", "ssm_a_im", "ssm_b_re", "ssm_b_im", "ssm_c_re", "ssm_c_im", "ssm_d", "ssm_log_step",
          "norm_ffn_g", "norm_final_g"]
_SMALL_SHARDED = ["gate_bias", "lru_conv_w"]
_BIG = ["w_in", "ssm_w_glu", "w_branch", "w_out", "w_ffn_gate", "w_ffn_up", "w_ffn_down"]
_WEIGHTS = ["norm_mix_g", "w_in", "gate_bias", "lru_conv_w", "lru_conv_b", "lru_wa", "lru_ba", "lru_wx", "lru_bx",
            "lru_lambda", "attn_rel_bias", "ssm_a_re", "ssm_a_im", "ssm_b_re", "ssm_b_im", "ssm_c_re", "ssm_c_im",
            "ssm_d", "ssm_log_step", "ssm_w_glu", "w_branch", "w_out", "norm_ffn_g", "w_ffn_gate", "w_ffn_up",
            "w_ffn_down", "norm_final_g"]


def _device_step(x, target, wts, small_gath, get_gath, on_grads):
    t, d = x.shape
    w = d // 2
    nl = wts["norm_mix_g"].shape[0]
    nt = w // LANE
    f = wts["w_ffn_gate"].shape[-1]
    ncol = wts["w_in"].shape[-1]
    tm = _tile(t, 1024)
    n_g = w // SSM_GROUP
    gpt = LANE // SSM_GROUP
    bw = w // LRU_BLOCKS
    bpt = LANE // bw

    gbias = jnp.transpose(small_gath["gate_bias"], (1, 2, 0, 3)).reshape(nl, N_BRANCH, d)
    convw = jnp.transpose(small_gath["lru_conv_w"], (1, 2, 0, 3)).reshape(nl, CONV_W, w)
    bias_tab = _bias_expand(wts["attn_rel_bias"], "bias_expand")
    btr = jnp.swapaxes(wts["ssm_b_re"], 2, 3)
    bti = jnp.swapaxes(wts["ssm_b_im"], 2, 3)
    ls3 = wts["ssm_log_step"][..., None]
    lam_r, lam_i, bbr, bbi = _s5_prep(wts["ssm_a_re"], wts["ssm_a_im"], ls3, btr, bti, "s5_prep")

    def layer_consts(l):
        c = {}
        c["wa_t"] = _block_diag(wts["lru_wa"][l], nt).astype(BF16)
        c["wx_t"] = _block_diag(wts["lru_wx"][l], nt).astype(BF16)
        c["wat_t"] = jnp.swapaxes(c["wa_t"], 1, 2)
        c["wxt_t"] = jnp.swapaxes(c["wx_t"], 1, 2)
        bd_r, bd_i = _block_diag(bbr[l], nt), _block_diag(bbi[l], nt)
        c["bbt"] = jnp.concatenate([bd_r, bd_i], axis=-1).astype(BF16)
        c["bbm"] = jnp.swapaxes(c["bbt"], 1, 2)
        cd_r, cd_i = _block_diag(wts["ssm_c_re"][l], nt), _block_diag(wts["ssm_c_im"][l], nt)
        c["cmt"] = jnp.concatenate([cd_r, -cd_i], axis=-1).astype(BF16)
        c["cmat"] = jnp.swapaxes(c["cmt"], 1, 2)
        nrow = gpt * SSM_P // LANE
        c["lam"] = jnp.concatenate([lam_r[l].reshape(nt, nrow, LANE), lam_i[l].reshape(nt, nrow, LANE)], axis=1)
        return c

    row1 = lambda a: a.reshape(1, -1)
    tied = lambda g, tok: g if tok is None else g + tok[0, 0]
    saved = []
    for l in range(nl):
        c = layer_consts(l)
        gath, tok = get_gath(l, x)
        h1 = _rmsnorm_fwd(x, tied(row1(wts["norm_mix_g"][l]), tok), f"norm_mix_{l}")
        tn = _tile(ncol, 768)
        nn_ = ncol // tn
        proj = _mm(f"in_proj_{l}", NN, (t // tm, NDEV, nn_), 0, [h1, gath["w_in"]],
                   [pl.BlockSpec((tm, d), lambda i, j, n: (i, 0)),
                    pl.BlockSpec((None, d, tn), lambda i, j, n: (j, 0, n))],
                   jax.ShapeDtypeStruct((t, NDEV * ncol), BF16),
                   pl.BlockSpec((tm, tn), lambda i, j, n: (i, j * nn_ + n)), (8, LANE))
        ya, hs = _lru_fwd(proj, convw[l], row1(wts["lru_conv_b"][l]), c["wa_t"], row1(wts["lru_ba"][l]), c["wx_t"],
                          row1(wts["lru_bx"][l]), row1(wts["lru_lambda"][l]), w, f"lru_fwd_{l}")
        yb = _attn_fwd(proj, bias_tab[l], w, f"attn_fwd_{l}")
        s5o, yc = _s5_fwd(proj, c["bbt"], c["cmat"], c["lam"], row1(wts["ssm_d"][l]), w, f"s5_fwd_{l}")
        cb = d // NDEV

        def branch(y, wg, idx, nm):
            return _mm(nm, NN, (t // tm, NDEV), 0, [y, wg],
                       [pl.BlockSpec((tm, w), lambda i, j: (i, 0)),
                        pl.BlockSpec((None,) * len(idx(0)[:-2]) + (w, cb), lambda i, j: idx(j))],
                       jax.ShapeDtypeStruct((t, d), BF16), pl.BlockSpec((tm, cb), lambda i, j: (i, j)), (8, LANE))

        wb_idx = lambda b: (lambda j: (j, b, 0, 0))
        br_a = branch(ya, gath["w_branch"], wb_idx(0), f"branch_a_{l}")
        br_b = branch(yb, gath["w_branch"], wb_idx(1), f"branch_b_{l}")
        br_c0 = branch(yc, gath["w_branch"], wb_idx(2), f"branch_c_{l}")
        gl = branch(yc, gath["ssm_w_glu"], lambda j: (j, 0, 0), f"branch_glu_{l}")
        merged = _merge_fwd(proj, gbias[l], br_a, br_b, br_c0, gl, d, f"merge_fwd_{l}")
        tno = _tile(d, 1024)
        w_out_full = gath["w_out"].reshape(d, d)
        x_mid = _mm(f"out_proj_{l}", NN, (t // tm, d // tno), 0, [merged, w_out_full, x],
                    [pl.BlockSpec((tm, d), lambda i, n: (i, 0)), pl.BlockSpec((d, tno), lambda i, n: (0, n)),
                     pl.BlockSpec((tm, tno), lambda i, n: (i, n))],
                    jax.ShapeDtypeStruct((t, d), F32), pl.BlockSpec((tm, tno), lambda i, n: (i, n)), (8, LANE),
                    epi=lambda acc, res: acc + res)
        h2 = _rmsnorm_fwd(x_mid, row1(wts["norm_ffn_g"][l]), f"norm_ffn_{l}")
        gp, up, act = _ffn_in(h2, gath["w_ffn_gate"], gath["w_ffn_up"], f"ffn_in_{l}")
        x_next = _mm(f"ffn_down_{l}", NN, (t // tm, d // tno, NDEV), 1, [act, gath["w_ffn_down"], x_mid],
                     [pl.BlockSpec((None, tm, f), lambda i, n, k: (k, i, 0)),
                      pl.BlockSpec((None, f, tno), lambda i, n, k: (k, 0, n)),
                      pl.BlockSpec((tm, tno), lambda i, n, k: (i, n))],
                     jax.ShapeDtypeStruct((t, d), F32), pl.BlockSpec((tm, tno), lambda i, n, k: (i, n)), (tm, tno),
                     epi=lambda acc, res: acc + res)
        saved.append(dict(x=x, h1=h1, proj=proj, ya=ya, hs=hs, yb=yb, s5o=s5o, yc=yc, br_a=br_a, br_b=br_b,
                          br_c0=br_c0, gl=gl, merged=merged, x_mid=x_mid, h2=h2, gp=gp, up=up, act=act, c=c, gath=gath))
        x = x_next

    loss, dx, dxb, dgf = _loss_head(x, row1(wts["norm_final_g"]), target, "loss_head")

    sg = {k: [None] * nl for k in _SMALL + _SMALL_SHARDED if k != "norm_final_g"}
    dbias_tabs = [None] * nl
    dlr_l, dli_l, dbbr_l, dbbi_l = [None] * nl, [None] * nl, [None] * nl, [None] * nl
    tk = _tile(d, 1024)
    for l in range(nl - 1, -1, -1):
        s = saved[l]
        c = s["c"]
        gath = s["gath"]
        gbuf = {}
        tno = _tile(d, 1024)
        gbuf["w_ffn_down"] = _mm(
            f"dw_ffn_down_{l}", TN, (NDEV, d // tno), 0, [s["act"], dxb],
            [pl.BlockSpec((None, t, f), lambda j, n: (j, 0, 0)), pl.BlockSpec((t, tno), lambda j, n: (0, n))],
            jax.ShapeDtypeStruct((NDEV, f, d), BF16), pl.BlockSpec((None, f, tno), lambda j, n: (j, 0, n)), (8, LANE))
        dgp, dup = _ffn_dact(dxb, gath["w_ffn_down"], s["gp"], s["up"], f"d_act_{l}")
        dh2 = _ffn_dh(dgp, dup, gath["w_ffn_gate"], gath["w_ffn_up"], f"d_h2_{l}")

        def ffn_dw(dz, key, nm):
            return _mm(nm, TN, (NDEV, d // tk), 0, [s["h2"], dz],
                       [pl.BlockSpec((t, tk), lambda j, k: (0, k)), pl.BlockSpec((None, t, f), lambda j, k: (j, 0, 0))],
                       jax.ShapeDtypeStruct((NDEV, d, f), BF16), pl.BlockSpec((None, tk, f), lambda j, k: (j, k, 0)),
                       (8, LANE))

        gbuf["w_ffn_gate"] = ffn_dw(dgp, "w_ffn_gate", f"dw_ffn_gate_{l}")
        gbuf["w_ffn_up"] = ffn_dw(dup, "w_ffn_up", f"dw_ffn_up_{l}")
        tok = on_grads(l, gbuf, "ffn")
        gbuf = {}
        dx_mid, dxmb, dg2 = _rmsnorm_bwd(s["x_mid"], tied(row1(wts["norm_ffn_g"][l]), tok), [dh2], dx,
                                         f"norm_ffn_bwd_{l}")
        sg["norm_ffn_g"][l] = dg2.reshape(-1)
        rb_ = d // NDEV
        dmerged = _mm(f"d_merged_{l}", NT, (t // tm, d // tno), 0, [dxmb, gath["w_out"].reshape(d, d)],
                      [pl.BlockSpec((tm, d), lambda i, n: (i, 0)), pl.BlockSpec((tno, d), lambda i, n: (n, 0))],
                      jax.ShapeDtypeStruct((t, d), BF16), pl.BlockSpec((tm, tno), lambda i, n: (i, n)), (8, LANE))
        tkw = _tile(d, 512)
        gbuf["w_out"] = _mm(
            f"dw_out_{l}", TN, (d // tkw, d // tno), 0, [s["merged"], dxmb],
            [pl.BlockSpec((t, tkw), lambda k, n: (0, k)), pl.BlockSpec((t, tno), lambda k, n: (0, n))],
            jax.ShapeDtypeStruct((d, d), BF16), pl.BlockSpec((tkw, tno), lambda k, n: (k, n)),
            (8, LANE)).reshape(NDEV, rb_, d)
        dbr_a, dbr_b, dbr_c0, dgl, dgates, dgb = _merge_bwd(s["proj"], gbias[l], s["br_a"], s["br_b"], s["br_c0"], s["gl"],
                                                           dmerged, d, f"merge_bwd_{l}")
        sg["gate_bias"][l] = dgb.reshape(-1)
        cb = d // NDEV

        def branch_dy(dbr, wg, idx, nm, dt):
            nlead = len(idx(0, 0)) - 2
            return _mm(nm, NT, (t // tm, NDEV), 1, [dbr, wg],
                       [pl.BlockSpec((tm, cb), lambda i, j: (i, j)),
                        pl.BlockSpec((None,) * nlead + (w, cb), lambda i, j: idx(j, 0))],
                       jax.ShapeDtypeStruct((t, w), dt), pl.BlockSpec((tm, w), lambda i, j: (i, 0)), (tm, w))

        def branch_dw(y, dbr, buf, shape, idx, nm):
            nlead = len(idx(0)) - 2
            return _mm(nm, TN, (NDEV,), 0, [y, dbr] + ([] if buf is None else [buf]),
                       [pl.BlockSpec((t, w), lambda j: (0, 0)), pl.BlockSpec((t, cb), lambda j: (0, j))],
                       jax.ShapeDtypeStruct(shape, BF16),
                       pl.BlockSpec((None,) * nlead + (w, cb), lambda j: idx(j)), (8, LANE),
                       alias=None if buf is None else True)

        wb_i = lambda b: (lambda j, z=0: (j, b, 0, 0))
        glu_i = lambda j, z=0: (j, 0, 0)
        dya = branch_dy(dbr_a, gath["w_branch"], wb_i(0), f"d_ya_{l}", BF16)
        dyb = branch_dy(dbr_b, gath["w_branch"], wb_i(1), f"d_yb_{l}", BF16)
        dyc1 = branch_dy(dbr_c0, gath["w_branch"], wb_i(2), f"d_yc_{l}", F32)
        dyc2 = branch_dy(dgl, gath["ssm_w_glu"], glu_i, f"d_yc_glu_{l}", F32)
        wb_shape = (NDEV, N_BRANCH, w, cb)
        gwb = branch_dw(s["ya"], dbr_a, lax.empty(wb_shape, BF16), wb_shape, wb_i(0), f"dw_branch_a_{l}")
        gwb = branch_dw(s["yb"], dbr_b, gwb, wb_shape, wb_i(1), f"dw_branch_b_{l}")
        gbuf["w_branch"] = branch_dw(s["yc"], dbr_c0, gwb, wb_shape, wb_i(2), f"dw_branch_c_{l}")
        gbuf["ssm_w_glu"] = branch_dw(s["yc"], dgl, None, (NDEV, w, cb), glu_i, f"dw_glu_{l}")
        ds5o = _gelu_bwd_sum(s["s5o"], dyc1, dyc2, f"gelu_bwd_{l}")
        du, dbbt, dcm, dlam, dd = _s5_bwd(s["proj"], ds5o, c["bbt"], c["bbm"], c["cmt"], c["lam"], row1(wts["ssm_d"][l]),
                                          w, f"s5_bwd_{l}")
        sp = gpt * SSM_P
        dbbr_l[l] = _block_diag_extract(dbbt[:, :, :sp], gpt)
        dbbi_l[l] = _block_diag_extract(dbbt[:, :, sp:], gpt)
        dcmt = jnp.swapaxes(dcm, 1, 2)
        sg["ssm_c_re"][l] = _block_diag_extract(dcmt[:, :, :sp], gpt).reshape(-1)
        sg["ssm_c_im"][l] = (-_block_diag_extract(dcmt[:, :, sp:], gpt)).reshape(-1)
        nrow = sp // LANE
        dlr_l[l] = dlam[:, :nrow].reshape(n_g, SSM_P)
        dli_l[l] = dlam[:, nrow:].reshape(n_g, SSM_P)
        sg["ssm_d"][l] = dd.reshape(-1)
        dq, dk, dv, dbt = _attn_bwd(s["proj"], bias_tab[l], dyb, w, f"attn_bwd_{l}")
        dbias_tabs[l] = dbt
        dlx, dlg, dcw, dcb, dwa, dba, dwx, dbx, dlm = _lru_bwd(
            s["proj"], s["hs"], dya, convw[l], row1(wts["lru_conv_b"][l]), c["wa_t"], row1(wts["lru_ba"][l]), c["wx_t"],
            row1(wts["lru_bx"][l]), row1(wts["lru_lambda"][l]), c["wat_t"], c["wxt_t"], w, f"lru_bwd_{l}")
        sg["lru_conv_w"][l] = dcw.reshape(-1)
        sg["lru_conv_b"][l] = dcb.reshape(-1)
        sg["lru_wa"][l] = _block_diag_extract(dwa, bpt).reshape(-1)
        sg["lru_wx"][l] = _block_diag_extract(dwx, bpt).reshape(-1)
        sg["lru_ba"][l] = dba.reshape(-1)
        sg["lru_bx"][l] = dbx.reshape(-1)
        sg["lru_lambda"][l] = dlm.reshape(-1)
        dproj = jnp.concatenate([dlx, dlg, dq, dk, dv, du, dgates], axis=1)
        tn = _tile(ncol, 768)
        nn_ = ncol // tn
        dh1 = _mm(f"d_h1_{l}", NT, (t // tm, d // tk, NDEV), 1, [dproj, gath["w_in"]],
                  [pl.BlockSpec((tm, ncol), lambda i, k, j: (i, j)), pl.BlockSpec((None, tk, ncol), lambda i, k, j: (j, k, 0))],
                  jax.ShapeDtypeStruct((t, d), F32), pl.BlockSpec((tm, tk), lambda i, k, j: (i, k)), (tm, tk))
        gbuf["w_in"] = _mm(
            f"dw_in_{l}", TN, (NDEV, d // tk, nn_), 0, [s["h1"], dproj],
            [pl.BlockSpec((t, tk), lambda j, k, n: (0, k)), pl.BlockSpec((t, tn), lambda j, k, n: (0, j * nn_ + n))],
            jax.ShapeDtypeStruct((NDEV, d, ncol), BF16), pl.BlockSpec((None, tk, tn), lambda j, k, n: (j, k, n)),
            (8, LANE))
        tok = on_grads(l, gbuf, "rest")
        dx, dxb, dg1 = _rmsnorm_bwd(s["x"], tied(row1(wts["norm_mix_g"][l]), tok), [dh1], dx_mid, f"norm_mix_bwd_{l}")
        sg["norm_mix_g"][l] = dg1.reshape(-1)

    da_re, da_im, dls, dbtr, dbti = _s5_prep_bwd(wts["ssm_a_re"], wts["ssm_a_im"], ls3, btr, bti, jnp.stack(dlr_l),
                                                 jnp.stack(dli_l), jnp.stack(dbbr_l), jnp.stack(dbbi_l), "s5_prep_bwd")
    drel = _bias_reduce(jnp.stack(dbias_tabs), "bias_reduce")
    small = {k: jnp.stack(v) for k, v in sg.items() if v[0] is not None}
    small["ssm_a_re"] = da_re.reshape(nl, -1)
    small["ssm_a_im"] = da_im.reshape(nl, -1)
    small["ssm_log_step"] = dls.reshape(nl, -1)
    small["ssm_b_re"] = jnp.swapaxes(dbtr, 2, 3).reshape(nl, -1)
    small["ssm_b_im"] = jnp.swapaxes(dbti, 2, 3).reshape(nl, -1)
    small["attn_rel_bias"] = drel.reshape(nl, -1)
    small["norm_final_g"] = dgf.reshape(-1)
    return loss, dx, small


def kernel(x, norm_mix_g, w_in, gate_bias, lru_conv_w, lru_conv_b, lru_wa, lru_ba, lru_wx, lru_bx, lru_lambda, attn_rel_bias, ssm_a_re, ssm_a_im, ssm_b_re, ssm_b_im, ssm_c_re, ssm_c_im, ssm_d, ssm_log_step, ssm_w_glu, w_branch, w_out, norm_ffn_g, w_ffn_gate, w_ffn_up, w_ffn_down, norm_final_g, loss_target, m_norm_mix_g, m_w_in, m_gate_bias, m_lru_conv_w, m_lru_conv_b, m_lru_wa, m_lru_ba, m_lru_wx, m_lru_bx, m_lru_lambda, m_attn_rel_bias, m_ssm_a_re, m_ssm_a_im, m_ssm_b_re, m_ssm_b_im, m_ssm_c_re, m_ssm_c_im, m_ssm_d, m_ssm_log_step, m_ssm_w_glu, m_w_branch, m_w_out, m_norm_ffn_g, m_w_ffn_gate, m_w_ffn_up, m_w_ffn_down, m_norm_final_g, v_norm_mix_g, v_w_in, v_gate_bias, v_lru_conv_w, v_lru_conv_b, v_lru_wa, v_lru_ba, v_lru_wx, v_lru_bx, v_lru_lambda, v_attn_rel_bias, v_ssm_a_re, v_ssm_a_im, v_ssm_b_re, v_ssm_b_im, v_ssm_c_re, v_ssm_c_im, v_ssm_d, v_ssm_log_step, v_ssm_w_glu, v_w_branch, v_w_out, v_norm_ffn_g, v_w_ffn_gate, v_w_ffn_up, v_w_ffn_down, v_norm_final_g):
    args = locals()
    wts = {k: args[k] for k in _WEIGHTS}
    mom = {k: args["m_" + k] for k in _WEIGHTS}
    vel = {k: args["v_" + k] for k in _WEIGHTS}
    cx, cy, cc = lax.axis_index("x"), lax.axis_index("y"), lax.axis_index("c")
    me = 4 * cx + 2 * cy + cc
    nl = norm_mix_g.shape[0]

    first = _all_gather([wts[k][0].astype(BF16) for k in _BIG] + [wts[k] for k in _SMALL_SHARDED], "gather_weights_0")
    small_gath = dict(zip(_SMALL_SHARDED, first[len(_BIG):]))
    in_flight = {}
    me_arr = me.astype(jnp.int32).reshape(1)
    layer_shape = {k: wts[k].shape[1:] for k in _BIG}

    def get_gath(l, x_in):
        if l == 0:
            gath = dict(zip(_BIG, first[:len(_BIG)]))
        else:
            lands = _split_wait("all", in_flight.pop(("w", l)), x_in, f"gather_wait_{l}")
            gath = {k: a.reshape((NDEV,) + layer_shape[k]) for k, a in zip(_BIG, lands)}
        if l + 1 == nl:
            return gath, None
        after = first[0] if l == 0 else x_in
        cast = [_cast_place(wts[k].reshape(nl, -1, wts[k].shape[-1]), l + 1, me_arr, after, f"cast_{k}_{l + 1}")
                for k in _BIG]
        started = _split_start("all", [a for a, _ in cast], [b for _, b in cast], f"gather_start_{l + 1}")
        in_flight[("w", l + 1)] = started[:-1]
        return gath, started[-1]

    core = cc.astype(jnp.int32).reshape(1)
    chip = (2 * cx + cy).astype(jnp.int32).reshape(1)

    def on_grads(l, gbuf, group):
        keys = [k for k in _BIG if k in gbuf]
        own = [gbuf[k].reshape((NCHIP, 2) + gbuf[k].shape[1:]) for k in keys]
        sib = _sibling_exchange(own, f"grad_sibling_exchange_{group}_{l}")
        pair, lands = [], []
        for k, a, b in zip(keys, own, sib):
            c_ = a.shape[-1]
            p, q = _pair_sum(a.reshape(NCHIP, 2, -1, c_), b.reshape(NCHIP, -1, c_), core, chip, f"pair_sum_{k}_{l}")
            pair.append(p)
            lands.append(q)
        started = _split_start("chips", pair, lands, f"grad_start_{group}_{l}")
        in_flight[("g", l, group)] = (keys, started[:-1])
        return started[-1]

    loss, dx, small = _device_step(x[0], loss_target[0], wts, small_gath, get_gath, on_grads)

    order = _SMALL + _SMALL_SHARDED
    flat = jnp.concatenate([small[k].reshape(-1) for k in order])
    n_flat = flat.shape[0]
    flat = jnp.pad(flat, (0, (-n_flat) % (FLAT_ROWS * LANE))).reshape(1, -1, LANE)
    flat_src, flat_land = _cast_place(flat, 0, me_arr, dx, "small_grads_place", F32)
    small_started = _split_start("all", [flat_src], [flat_land], "small_grads_start")[:-1]

    res = {k: None for k in _BIG}
    for l in range(nl - 1, -1, -1):
        for group in ("ffn", "rest"):
            keys, started = in_flight.pop(("g", l, group))
            parts = _split_wait("chips", started, dx, f"grad_wait_{group}_{l}")
            for k, p in zip(keys, parts):
                c_ = wts[k].shape[-1]
                res[k] = _adamw_reduce(p.reshape(NCHIP, -1, c_), wts[k].reshape(-1, c_), mom[k].reshape(-1, c_),
                                       vel[k].reshape(-1, c_), res[k], l, f"adamw_{k}_{l}")
    out = {k: tuple(a.reshape(wts[k].shape) for a in res[k]) for k in _BIG}

    (allflat,) = _split_wait("all", small_started, res["w_out"][0], "small_grads_wait")
    gsum = _sum_rows8(allflat, "sum_small_grads").reshape(-1)
    gs, off = {}, 0
    for k in order:
        n = small[k].size
        gs[k] = gsum[off:off + n]
        off += n
    gs_loc = {}
    for k in _SMALL:
        gs_loc[k] = gs[k].reshape(wts[k].shape)
    ncb = gate_bias.shape[-1]
    gs_loc["gate_bias"] = lax.dynamic_slice_in_dim(gs["gate_bias"].reshape(nl, N_BRANCH, -1), me * ncb, ncb, axis=2)
    ncw = lru_conv_w.shape[-1]
    gs_loc["lru_conv_w"] = lax.dynamic_slice_in_dim(gs["lru_conv_w"].reshape(nl, CONV_W, -1), me * ncw, ncw, axis=2)

    def pack(dct):
        v = jnp.concatenate([dct[k].reshape(-1) for k in order])
        return jnp.pad(v, (0, (-v.shape[0]) % (FLAT_ROWS * LANE))).reshape(-1, LANE)

    dl_f, nm_f, nv_f = _adamw_flat(pack(wts), pack(gs_loc), pack(mom), pack(vel), "adamw_small")
    off = 0
    for k in order:
        n = wts[k].size
        sl = lambda a: a.reshape(-1)[off:off + n].reshape(wts[k].shape)
        out[k] = (gs_loc[k], sl(dl_f), sl(nm_f), sl(nv_f))
        off += n

    loss_total = lax.psum(loss[0, 0], ("x", "y", "c"))
    return (loss_total, dx[None], *[out[k][0] for k in _WEIGHTS], *[out[k][1] for k in _WEIGHTS],
            *[out[k][2] for k in _WEIGHTS], *[out[k][3] for k in _WEIGHTS])
```

```python
import functools
import math

import numpy as np
import jax
import jax.numpy as jnp
from jax import lax
from jax.experimental import pallas as pl
from jax.experimental.pallas import tpu as pltpu

F32 = jnp.float32
BF16 = jnp.bfloat16
LANE = 128
NSEG = 8
NDEV = 8
NCHIP = 4
MESH = pl.DeviceIdType.MESH
VMEM_LIMIT = 56 * 1024 * 1024
FLAT_ROWS = 512

NORM_EPS = 1e-6
CHUNK = 64
ATT_LEFT = 8
ATT_BAND = (ATT_LEFT + 1) * CHUNK
MAX_REL = 128
N_REL = 2 * MAX_REL + 1
N_REL_PAD = 384
ATT_HEADS = 8
MASK_VALUE = -1e30
LRU_C = 8.0
LRU_BLOCKS = 16
SSM_GROUP = 16
SSM_P = 64
CONV_W = 4
N_BRANCH = 3

ADAM_LR = 0.001
ADAM_B1 = 0.9
ADAM_B2 = 0.999
ADAM_EPS = 1e-08
ADAM_WD = 0.01
ADAM_STEP = 10

NN = (((1,), (0,)), ((), ()))
NT = (((1,), (1,)), ((), ()))
TN = (((0,), (0,)), ((), ()))


def _cp(sem=None, vmem=VMEM_LIMIT, **kw):
    return pltpu.CompilerParams(dimension_semantics=sem, vmem_limit_bytes=vmem, **kw)


def _dot(a, b, dn=NN):
    return lax.dot_general(a, b, dn, preferred_element_type=F32)


def _gelu(x):
    c = math.sqrt(2.0 / math.pi)
    return 0.5 * x * (1.0 + jnp.tanh(c * (x + 0.044715 * x * x * x)))


def _gelu_grad(x):
    c = math.sqrt(2.0 / math.pi)
    t = jnp.tanh(c * (x + 0.044715 * x * x * x))
    return 0.5 * (1.0 + t) + 0.5 * x * (1.0 - t * t) * c * (1.0 + 3.0 * 0.044715 * x * x)


def _sigmoid(x):
    return 1.0 / (1.0 + jnp.exp(-x))


def _one_minus_exp(z):
    series = -(z * (1.0 + z * (0.5 + z * (1.0 / 6.0 + z * (1.0 / 24.0)))))
    return jnp.where(z > -0.02, series, 1.0 - jnp.exp(z))


def _softplus_neg(lam):
    e = jnp.exp(-jnp.abs(lam))
    series = e * (1.0 - e * (0.5 - e * (1.0 / 3.0 - e * 0.25)))
    log1p_e = jnp.where(e < 0.02, series, jnp.log(1.0 + e))
    return jnp.maximum(-lam, 0.0) + log1p_e


def _mm(name, dn, grid, n_red, ins, in_specs, out_shape, out_spec, acc_shape, epi=None, alias=None):
    n_extra = len(ins) - 2 - (1 if alias is not None else 0)
    red_axes = tuple(range(len(grid) - n_red, len(grid)))
    red_sizes = tuple(grid[r] for r in red_axes)

    def body(*refs):
        a_ref, b_ref = refs[0], refs[1]
        extra = refs[2:2 + n_extra]
        o_ref, acc = refs[-2], refs[-1]
        if n_red:
            first = functools.reduce(jnp.logical_and, [pl.program_id(r) == 0 for r in red_axes])
            last = functools.reduce(jnp.logical_and,
                                    [pl.program_id(r) == n - 1 for r, n in zip(red_axes, red_sizes)])

            @pl.when(first)
            def _():
                acc[...] = jnp.zeros_like(acc)

            acc[...] += _dot(a_ref[...], b_ref[...], dn)

            @pl.when(last)
            def _():
                r = acc[...]
                if epi is not None:
                    r = epi(r, *[e[...] for e in extra])
                o_ref[...] = r.astype(o_ref.dtype)
        else:
            r = _dot(a_ref[...], b_ref[...], dn)
            if epi is not None:
                r = epi(r, *[e[...] for e in extra])
            o_ref[...] = r.astype(o_ref.dtype)

    specs = list(in_specs)
    kw = {}
    if alias is not None:
        specs.append(pl.BlockSpec(memory_space=pl.ANY))
        kw["input_output_aliases"] = {len(ins) - 1: 0}
    sem = ("parallel",) * (len(grid) - n_red) + ("arbitrary",) * n_red
    return pl.pallas_call(
        body, name=name, grid=grid, in_specs=specs, out_specs=out_spec, out_shape=out_shape,
        scratch_shapes=[pltpu.VMEM(acc_shape, F32)], compiler_params=_cp(sem), **kw)(*ins)


def _loop(n, body, init, unroll=4):
    def block(kb, c):
        for i in range(unroll):
            c = body(kb * unroll + i, c)
        return c

    c = lax.fori_loop(0, n // unroll, block, init)
    for k in range(n - n % unroll, n):
        c = body(k, c)
    return c


def _tile(n, pref):
    if n <= pref:
        return n
    t = pref
    while t >= LANE:
        if n % t == 0 and t % LANE == 0:
            return t
        t -= LANE
    return n


def _rmsnorm_fwd(x, g, name):
    t, d = x.shape
    tm = min(512, t)

    def body(x_ref, g_ref, h_ref):
        xv = x_ref[...]
        r = lax.rsqrt(jnp.mean(xv * xv, axis=-1, keepdims=True) + NORM_EPS)
        h_ref[...] = (xv * r * g_ref[...]).astype(h_ref.dtype)

    return pl.pallas_call(
        body, name=name, grid=(t // tm,),
        in_specs=[pl.BlockSpec((tm, d), lambda i: (i, 0)), pl.BlockSpec((1, d), lambda i: (0, 0))],
        out_specs=pl.BlockSpec((tm, d), lambda i: (i, 0)),
        out_shape=jax.ShapeDtypeStruct((t, d), BF16), compiler_params=_cp(("parallel",)))(x, g)


def _rmsnorm_bwd(x, g, dhs, dres, name):
    t, d = x.shape
    tm = min(256, t)
    n_dh = len(dhs)

    def body(*refs):
        x_ref, g_ref = refs[0], refs[1]
        dh_refs = refs[2:2 + n_dh]
        dres_ref = refs[2 + n_dh]
        dx_ref, dxb_ref, dg_ref = refs[3 + n_dh:]
        xv = x_ref[...]
        r = lax.rsqrt(jnp.mean(xv * xv, axis=-1, keepdims=True) + NORM_EPS)
        xhat = xv * r
        dh = dh_refs[0][...].astype(F32)
        for e in dh_refs[1:]:
            dh = dh + e[...].astype(F32)
        dxh = dh * g_ref[...]
        dx = r * (dxh - xhat * jnp.mean(dxh * xhat, axis=-1, keepdims=True)) + dres_ref[...]
        dx_ref[...] = dx
        dxb_ref[...] = dx.astype(BF16)

        @pl.when(pl.program_id(0) == 0)
        def _():
            dg_ref[...] = jnp.zeros_like(dg_ref)

        dg_ref[...] += jnp.sum(dh * xhat, axis=0, keepdims=True)

    row = pl.BlockSpec((tm, d), lambda i: (i, 0))
    par = pl.BlockSpec((1, d), lambda i: (0, 0))
    return pl.pallas_call(
        body, name=name, grid=(t // tm,),
        in_specs=[row, par] + [row] * n_dh + [row],
        out_specs=[row, row, par],
        out_shape=[jax.ShapeDtypeStruct((t, d), F32), jax.ShapeDtypeStruct((t, d), BF16),
                   jax.ShapeDtypeStruct((1, d), F32)],
        compiler_params=_cp(("arbitrary",)))(x, g, *dhs, dres)


def _loss_head(x, g, target, name):
    t, d = x.shape
    tm = min(256, t)

    def body(x_ref, g_ref, t_ref, loss_ref, dx_ref, dxb_ref, dg_ref):
        xv = x_ref[...]
        r = lax.rsqrt(jnp.mean(xv * xv, axis=-1, keepdims=True) + NORM_EPS)
        xhat = xv * r
        y = xhat * g_ref[...]
        err = y - t_ref[...]
        dy = err * (1.0 / d)
        dxh = dy * g_ref[...]
        dx = r * (dxh - xhat * jnp.mean(dxh * xhat, axis=-1, keepdims=True))
        dx_ref[...] = dx
        dxb_ref[...] = dx.astype(BF16)

        @pl.when(pl.program_id(0) == 0)
        def _():
            dg_ref[...] = jnp.zeros_like(dg_ref)
            loss_ref[...] = jnp.zeros_like(loss_ref)

        dg_ref[...] += jnp.sum(dy * xhat, axis=0, keepdims=True)
        per_tok = jnp.sum(err * err, axis=-1, keepdims=True) * (0.5 / d)
        loss_ref[...] += jnp.sum(per_tok, axis=0, keepdims=True)

    row = pl.BlockSpec((tm, d), lambda i: (i, 0))
    par = pl.BlockSpec((1, d), lambda i: (0, 0))
    return pl.pallas_call(
        body, name=name, grid=(t // tm,),
        in_specs=[row, par, row],
        out_specs=[pl.BlockSpec((1, 1), lambda i: (0, 0)), row, row, par],
        out_shape=[jax.ShapeDtypeStruct((1, 1), F32), jax.ShapeDtypeStruct((t, d), F32),
                   jax.ShapeDtypeStruct((t, d), BF16), jax.ShapeDtypeStruct((1, d), F32)],
        compiler_params=_cp(("arbitrary",)))(x, g, target)


def _merge_fwd(proj, gbias, br_a, br_b, br_c0, gl, d, name):
    t = proj.shape[0]
    tm = min(128, t)
    off = proj.shape[1] // d - N_BRANCH

    def body(g0, g1, g2, gb, a_ref, b_ref, c_ref, gl_ref, o_ref):
        gbv = gb[...]
        s0 = _sigmoid(g0[...].astype(F32) + gbv[0:1])
        s1 = _sigmoid(g1[...].astype(F32) + gbv[1:2])
        s2 = _sigmoid(g2[...].astype(F32) + gbv[2:3])
        brc = c_ref[...].astype(F32) * _sigmoid(gl_ref[...].astype(F32))
        o_ref[...] = (s0 * a_ref[...].astype(F32) + s1 * b_ref[...].astype(F32) + s2 * brc).astype(BF16)

    row = pl.BlockSpec((tm, d), lambda i: (i, 0))
    gs = [pl.BlockSpec((tm, d), functools.partial(lambda i, b: (i, off + b), b=b)) for b in range(N_BRANCH)]
    return pl.pallas_call(
        body, name=name, grid=(t // tm,),
        in_specs=gs + [pl.BlockSpec((N_BRANCH, d), lambda i: (0, 0)), row, row, row, row],
        out_specs=row, out_shape=jax.ShapeDtypeStruct((t, d), BF16),
        compiler_params=_cp(("parallel",)))(proj, proj, proj, gbias, br_a, br_b, br_c0, gl)


def _merge_bwd(proj, gbias, br_a, br_b, br_c0, gl, dmerged, d, name):
    t = proj.shape[0]
    tm = min(128, t)
    off = proj.shape[1] // d - N_BRANCH

    def body(g0, g1, g2, gb, a_ref, b_ref, c_ref, gl_ref, dm_ref, da_ref, db_ref, dc_ref, dgl_ref, dg_ref, dgb_ref):
        gbv = gb[...]
        dm = dm_ref[...].astype(F32)
        s0 = _sigmoid(g0[...].astype(F32) + gbv[0:1])
        s1 = _sigmoid(g1[...].astype(F32) + gbv[1:2])
        s2 = _sigmoid(g2[...].astype(F32) + gbv[2:3])
        sg = _sigmoid(gl_ref[...].astype(F32))
        c0 = c_ref[...].astype(F32)
        brc = c0 * sg
        da_ref[...] = (dm * s0).astype(BF16)
        db_ref[...] = (dm * s1).astype(BF16)
        dbrc = dm * s2
        dc_ref[...] = (dbrc * sg).astype(BF16)
        dgl_ref[...] = (dbrc * c0 * sg * (1.0 - sg)).astype(BF16)
        dp0 = dm * a_ref[...].astype(F32) * s0 * (1.0 - s0)
        dp1 = dm * b_ref[...].astype(F32) * s1 * (1.0 - s1)
        dp2 = dm * brc * s2 * (1.0 - s2)
        dg_ref[:, 0:d] = dp0.astype(BF16)
        dg_ref[:, d:2 * d] = dp1.astype(BF16)
        dg_ref[:, 2 * d:3 * d] = dp2.astype(BF16)

        @pl.when(pl.program_id(0) == 0)
        def _():
            dgb_ref[...] = jnp.zeros_like(dgb_ref)

        dgb_ref[0:1, :] += jnp.sum(dp0, axis=0, keepdims=True)
        dgb_ref[1:2, :] += jnp.sum(dp1, axis=0, keepdims=True)
        dgb_ref[2:3, :] += jnp.sum(dp2, axis=0, keepdims=True)

    row = pl.BlockSpec((tm, d), lambda i: (i, 0))
    gs = [pl.BlockSpec((tm, d), functools.partial(lambda i, b: (i, off + b), b=b)) for b in range(N_BRANCH)]
    par = pl.BlockSpec((N_BRANCH, d), lambda i: (0, 0))
    bf = jax.ShapeDtypeStruct((t, d), BF16)
    return pl.pallas_call(
        body, name=name, grid=(t // tm,),
        in_specs=gs + [par, row, row, row, row, row],
        out_specs=[row, row, row, row, pl.BlockSpec((tm, N_BRANCH * d), lambda i: (i, 0)), par],
        out_shape=[bf, bf, bf, bf, jax.ShapeDtypeStruct((t, N_BRANCH * d), BF16),
                   jax.ShapeDtypeStruct((N_BRANCH, d), F32)],
        compiler_params=_cp(("arbitrary",)))(proj, proj, proj, gbias, br_a, br_b, br_c0, gl, dmerged)


def _ffn_in(h2, wg, wu, name):
    t, d = h2.shape
    f = wg.shape[-1]
    tm = _tile(t, 1024)

    def body(h_ref, g_ref, u_ref, gp_ref, up_ref, act_ref):
        h = h_ref[...]
        g = _dot(h, g_ref[...])
        u = _dot(h, u_ref[...])
        gp_ref[...] = g.astype(BF16)
        up_ref[...] = u.astype(BF16)
        act_ref[...] = (g * _sigmoid(g) * u).astype(BF16)

    wspec = pl.BlockSpec((None, d, f), lambda i, j: (j, 0, 0))
    ospec = pl.BlockSpec((None, tm, f), lambda i, j: (j, i, 0))
    o = jax.ShapeDtypeStruct((NDEV, t, f), BF16)
    return pl.pallas_call(
        body, name=name, grid=(t // tm, NDEV), in_specs=[pl.BlockSpec((tm, d), lambda i, j: (i, 0)), wspec, wspec],
        out_specs=[ospec, ospec, ospec], out_shape=[o, o, o],
        compiler_params=_cp(("parallel", "parallel")))(h2, wg, wu)


def _ffn_dact(dxb, wd, gp, up, name):
    t, d = dxb.shape
    f = wd.shape[1]
    tm = _tile(t, 1024)

    def body(dx_ref, w_ref, g_ref, u_ref, dg_ref, du_ref):
        da = _dot(dx_ref[...], w_ref[...], NT)
        g = g_ref[...].astype(F32)
        u = u_ref[...].astype(F32)
        sg = _sigmoid(g)
        silu = g * sg
        du_ref[...] = (da * silu).astype(BF16)
        dg_ref[...] = (da * u * (sg + silu * (1.0 - sg))).astype(BF16)

    ospec = pl.BlockSpec((None, tm, f), lambda i, j: (j, i, 0))
    o = jax.ShapeDtypeStruct((NDEV, t, f), BF16)
    return pl.pallas_call(
        body, name=name, grid=(t // tm, NDEV),
        in_specs=[pl.BlockSpec((tm, d), lambda i, j: (i, 0)), pl.BlockSpec((None, f, d), lambda i, j: (j, 0, 0)), ospec, ospec],
        out_specs=[ospec, ospec], out_shape=[o, o],
        compiler_params=_cp(("parallel", "parallel")))(dxb, wd, gp, up)


def _ffn_dh(dgp, dup, wg, wu, name):
    _, t, f = dgp.shape
    d = wg.shape[1]
    tm = _tile(t, 1024)
    tk = _tile(d, 1024)

    def body(g_ref, u_ref, wg_ref, wu_ref, o_ref, acc):
        j = pl.program_id(2)

        @pl.when(j == 0)
        def _():
            acc[...] = jnp.zeros_like(acc)

        acc[...] += _dot(g_ref[...], wg_ref[...], NT) + _dot(u_ref[...], wu_ref[...], NT)

        @pl.when(j == NDEV - 1)
        def _():
            o_ref[...] = acc[...]

    zspec = pl.BlockSpec((None, tm, f), lambda i, k, j: (j, i, 0))
    wspec = pl.BlockSpec((None, tk, f), lambda i, k, j: (j, k, 0))
    return pl.pallas_call(
        body, name=name, grid=(t // tm, d // tk, NDEV), in_specs=[zspec, zspec, wspec, wspec],
        out_specs=pl.BlockSpec((tm, tk), lambda i, k, j: (i, k)), out_shape=jax.ShapeDtypeStruct((t, d), F32),
        scratch_shapes=[pltpu.VMEM((tm, tk), F32)],
        compiler_params=_cp(("parallel", "parallel", "arbitrary")))(dgp, dup, wg, wu)


def _gelu_bwd_sum(s5o, dy1, dy2, name):
    t, w = s5o.shape
    tm = min(512, t)

    def body(s_ref, a_ref, b_ref, o_ref):
        o_ref[...] = ((a_ref[...] + b_ref[...]) * _gelu_grad(s_ref[...])).astype(BF16)

    row = pl.BlockSpec((tm, w), lambda i: (i, 0))
    return pl.pallas_call(
        body, name=name, grid=(t // tm,), in_specs=[row, row, row], out_specs=row,
        out_shape=jax.ShapeDtypeStruct((t, w), BF16), compiler_params=_cp(("parallel",)))(s5o, dy1, dy2)


def _lru_gates(xc, wa, ba, wx, bx, sp):
    xcb = xc.astype(BF16)
    r = _sigmoid(_dot(xcb, wa) + ba)
    i = _sigmoid(_dot(xcb, wx) + bx)
    la = -LRU_C * r * sp
    return xcb, r, i, la


def _lru_fwd(proj, cw, cb, wa_t, ba, wx_t, bx, lam, w, name):
    t = proj.shape[0]
    nt = w // LANE
    seg = t // NSEG
    rb = min(512, seg)

    def body(lx_ref, lg_ref, cw_ref, cb_ref, wa_ref, ba_ref, wx_ref, bx_ref, lam_ref, ya_ref, hs_ref, xp, a_s, b_s):
        xp[pl.ds(0, 8), :] = jnp.zeros((8, LANE), F32)
        for r0 in range(0, t, rb):
            xp[pl.ds(8 + r0, rb), :] = lx_ref[pl.ds(r0, rb), :].astype(F32)
        sp = _softplus_neg(lam_ref[...])
        cwv = cw_ref[...]
        for r0 in range(0, t, rb):
            xc = cb_ref[...] + sum(cwv[k:k + 1] * xp[pl.ds(8 + r0 - (CONV_W - 1) + k, rb), :] for k in range(CONV_W))
            _, _, i, la = _lru_gates(xc, wa_ref[...], ba_ref[...], wx_ref[...], bx_ref[...], sp)
            a_s[pl.ds(r0, rb), :] = jnp.exp(la)
            b_s[pl.ds(r0, rb), :] = jnp.sqrt(_one_minus_exp(2.0 * la)) * (i * xc)

        def load(k):
            rows = pl.ds(k, NSEG, stride=seg)
            return a_s[rows, :], b_s[rows, :]

        def step(k, c):
            h, p, a, b = c
            na, nb = load(jnp.minimum(k + 1, seg - 1))
            rows = pl.ds(k, NSEG, stride=seg)
            h = a * h + b
            p = a * p
            b_s[rows, :] = h
            a_s[rows, :] = p
            return h, p, na, nb

        _loop(seg, step, (jnp.zeros((NSEG, LANE), F32), jnp.ones((NSEG, LANE), F32)) + load(0))
        carry = jnp.zeros((1, LANE), F32)
        for s in range(NSEG):
            for r0 in range(s * seg, (s + 1) * seg, rb):
                rows = pl.ds(r0, rb)
                h = b_s[rows, :] + a_s[rows, :] * carry
                hs_ref[rows, :] = h
                ya_ref[rows, :] = (h * _gelu(lg_ref[rows, :].astype(F32))).astype(BF16)
            end = pl.ds((s + 1) * seg - 1, 1)
            carry = b_s[end, :] + a_s[end, :] * carry

    col = lambda c0: pl.BlockSpec((t, LANE), functools.partial(lambda i, c0: (0, c0 + i), c0=c0))
    par = lambda k: pl.BlockSpec((k, LANE), lambda i: (0, i))
    mat = pl.BlockSpec((None, LANE, LANE), lambda i: (i, 0, 0))
    return pl.pallas_call(
        body, name=name, grid=(nt,),
        in_specs=[col(0), col(nt), par(CONV_W), par(1), mat, par(1), mat, par(1), par(1)],
        out_specs=[col(0), col(0)],
        out_shape=[jax.ShapeDtypeStruct((t, w), BF16), jax.ShapeDtypeStruct((t, w), F32)],
        scratch_shapes=[pltpu.VMEM((t + 8, LANE), F32), pltpu.VMEM((t, LANE), F32), pltpu.VMEM((t, LANE), F32)],
        compiler_params=_cp(("parallel",)))(proj, proj, cw, cb, wa_t, ba, wx_t, bx, lam)


def _lru_bwd(proj, hs, dya, cw, cb, wa_t, ba, wx_t, bx, lam, wat_t, wxt_t, w, name):
    t = proj.shape[0]
    nt = w // LANE
    seg = t // NSEG
    rb = min(512, seg)

    def body(lx_ref, lg_ref, hs_ref, dy_ref, cw_ref, cb_ref, wa_ref, ba_ref, wx_ref, bx_ref, lam_ref, wat_ref, wxt_ref,
             dlx_ref, dlg_ref, dcw_ref, dcb_ref, dwa_ref, dba_ref, dwx_ref, dbx_ref, dlam_ref,
             xp, hp, a_s, g_s, q_s, dxc_s):
        z8 = jnp.zeros((8, LANE), F32)
        xp[pl.ds(0, 8), :] = z8
        hp[pl.ds(0, 8), :] = z8
        a_s[pl.ds(t, 8), :] = z8
        dxc_s[pl.ds(t, 8), :] = z8
        lamv = lam_ref[...]
        sp = _softplus_neg(lamv)
        cwv = cw_ref[...]

        def conv(r0):
            return cb_ref[...] + sum(cwv[k:k + 1] * xp[pl.ds(8 + r0 - (CONV_W - 1) + k, rb), :] for k in range(CONV_W))

        for r0 in range(0, t, rb):
            rows = pl.ds(r0, rb)
            xp[pl.ds(8 + r0, rb), :] = lx_ref[rows, :].astype(F32)
            hp[pl.ds(8 + r0, rb), :] = hs_ref[rows, :]
        for r0 in range(0, t, rb):
            rows = pl.ds(r0, rb)
            _, _, _, la = _lru_gates(conv(r0), wa_ref[...], ba_ref[...], wx_ref[...], bx_ref[...], sp)
            a_s[rows, :] = jnp.exp(la)
            g_s[rows, :] = dy_ref[rows, :].astype(F32) * _gelu(lg_ref[rows, :].astype(F32))

        def load(k):
            return a_s[pl.ds(k + 1, NSEG, stride=seg), :], g_s[pl.ds(k, NSEG, stride=seg), :]

        def step(kk, c):
            g, q, an, dh = c
            k = seg - 1 - kk
            nan, ndh = load(jnp.maximum(k - 1, 0))
            rows = pl.ds(k, NSEG, stride=seg)
            g = dh + an * g
            q = an * q
            g_s[rows, :] = g
            q_s[rows, :] = q
            return g, q, nan, ndh

        _loop(seg, step, (jnp.zeros((NSEG, LANE), F32), jnp.ones((NSEG, LANE), F32)) + load(seg - 1))
        nxt = [None] * NSEG
        carry = jnp.zeros((1, LANE), F32)
        for s in range(NSEG - 1, -1, -1):
            nxt[s] = carry
            start = pl.ds(s * seg, 1)
            carry = g_s[start, :] + q_s[start, :] * carry

        zrow = jnp.zeros((1, LANE), F32)
        dsp = zrow
        dba = zrow
        dbx = zrow
        dcb = zrow
        dwa = jnp.zeros((LANE, LANE), F32)
        dwx = jnp.zeros((LANE, LANE), F32)
        for s in range(NSEG):
            for r0 in range(s * seg, (s + 1) * seg, rb):
                rows = pl.ds(r0, rb)
                g = g_s[rows, :] + q_s[rows, :] * nxt[s]
                xc = conv(r0)
                xcb, r, i, la = _lru_gates(xc, wa_ref[...], ba_ref[...], wx_ref[...], bx_ref[...], sp)
                a = jnp.exp(la)
                om = _one_minus_exp(2.0 * la)
                mult = jnp.sqrt(om)
                hprev = hp[pl.ds(8 + r0 - 1, rb), :]
                da = g * hprev
                dmult = g * i * xc
                di = g * mult * xc
                dxc = g * mult * i
                dla = da * a - dmult * (1.0 - om) / mult
                dr = dla * (-LRU_C) * sp
                dsp = dsp + jnp.sum(dla * (-LRU_C) * r, axis=0, keepdims=True)
                dpr = dr * r * (1.0 - r)
                dpi = di * i * (1.0 - i)
                dba = dba + jnp.sum(dpr, axis=0, keepdims=True)
                dbx = dbx + jnp.sum(dpi, axis=0, keepdims=True)
                dprb = dpr.astype(BF16)
                dpib = dpi.astype(BF16)
                dxc = dxc + _dot(dprb, wat_ref[...]) + _dot(dpib, wxt_ref[...])
                dwa = dwa + _dot(xcb, dprb, TN)
                dwx = dwx + _dot(xcb, dpib, TN)
                dxc_s[rows, :] = dxc
                dcb = dcb + jnp.sum(dxc, axis=0, keepdims=True)
                lg = lg_ref[rows, :].astype(F32)
                dlg_ref[rows, :] = (dy_ref[rows, :].astype(F32) * hs_ref[rows, :] * _gelu_grad(lg)).astype(BF16)
        dcw = [zrow] * CONV_W
        for r0 in range(0, t, rb):
            rows = pl.ds(r0, rb)
            dlx = sum(cwv[k:k + 1] * dxc_s[pl.ds(r0 + (CONV_W - 1) - k, rb), :] for k in range(CONV_W))
            dlx_ref[rows, :] = dlx.astype(BF16)
            dxc = dxc_s[rows, :]
            for k in range(CONV_W):
                dcw[k] = dcw[k] + jnp.sum(dxc * xp[pl.ds(8 + r0 - (CONV_W - 1) + k, rb), :], axis=0, keepdims=True)
        dcw_ref[...] = jnp.concatenate(dcw, axis=0)
        dcb_ref[...] = dcb
        dwa_ref[...] = dwa
        dwx_ref[...] = dwx
        dba_ref[...] = dba
        dbx_ref[...] = dbx
        dlam_ref[...] = dsp * (-_sigmoid(-lamv))

    col = lambda c0: pl.BlockSpec((t, LANE), functools.partial(lambda i, c0: (0, c0 + i), c0=c0))
    par = lambda k: pl.BlockSpec((k, LANE), lambda i: (0, i))
    mat = pl.BlockSpec((None, LANE, LANE), lambda i: (i, 0, 0))
    vec = jax.ShapeDtypeStruct((1, w), F32)
    big = lambda: pltpu.VMEM((t + 8, LANE), F32)
    return pl.pallas_call(
        body, name=name, grid=(nt,),
        in_specs=[col(0), col(nt), col(0), col(0), par(CONV_W), par(1), mat, par(1), mat, par(1), par(1), mat, mat],
        out_specs=[col(0), col(0), par(CONV_W), par(1), mat, par(1), mat, par(1), par(1)],
        out_shape=[jax.ShapeDtypeStruct((t, w), BF16), jax.ShapeDtypeStruct((t, w), BF16),
                   jax.ShapeDtypeStruct((CONV_W, w), F32), vec, jax.ShapeDtypeStruct((nt, LANE, LANE), F32), vec,
                   jax.ShapeDtypeStruct((nt, LANE, LANE), F32), vec, vec],
        scratch_shapes=[big(), big(), big(), pltpu.VMEM((t, LANE), F32), pltpu.VMEM((t, LANE), F32), big()],
        compiler_params=_cp(("parallel",)))(proj, proj, hs, dya, cw, cb, wa_t, ba, wx_t, bx, lam, wat_t, wxt_t)


def _attn_scores(q_ref, kp, bias_ref, c, hd, masked):
    r0 = pl.multiple_of(c * CHUNK, CHUNK)
    qc = q_ref[pl.ds(r0, CHUNK), :]
    kb = kp[pl.ds(r0, ATT_BAND), :]
    s = _dot(qc, kb, NT) * (hd ** -0.5) + bias_ref[...]
    if masked:
        kpos = lax.broadcasted_iota(jnp.int32, (CHUNK, ATT_BAND), 1)
        s = jnp.where(kpos + (c - ATT_LEFT) * CHUNK >= 0, s, MASK_VALUE)
    m = jnp.max(s, axis=-1, keepdims=True)
    e = jnp.exp(s - m)
    p = e / jnp.sum(e, axis=-1, keepdims=True)
    return r0, qc, kb, p


def _attn_pad_copy(src_ref, dst, t, pad):
    dst[pl.ds(0, pad), :] = jnp.zeros((pad, dst.shape[1]), dst.dtype)
    rb = min(512, t)
    for r0 in range(0, t, rb):
        dst[pl.ds(pad + r0, rb), :] = src_ref[pl.ds(r0, rb), :]


def _attn_fwd(proj, bias, w, name):
    t = proj.shape[0]
    hd = w // ATT_HEADS
    pad = ATT_LEFT * CHUNK
    qo, ko, vo = 2 * w // hd, 3 * w // hd, 4 * w // hd

    def body(q_ref, k_ref, v_ref, bias_ref, o_ref, kp, vp):
        _attn_pad_copy(k_ref, kp, t, pad)
        _attn_pad_copy(v_ref, vp, t, pad)

        def chunk(c, _, masked):
            r0, _, _, p = _attn_scores(q_ref, kp, bias_ref, c, hd, masked)
            vb = vp[pl.ds(r0, ATT_BAND), :]
            o_ref[pl.ds(r0, CHUNK), :] = _dot(p.astype(BF16), vb).astype(BF16)
            return 0

        n_masked = min(ATT_LEFT, t // CHUNK)
        lax.fori_loop(0, n_masked, functools.partial(chunk, masked=True), 0)
        lax.fori_loop(n_masked, t // CHUNK, functools.partial(chunk, masked=False), 0, unroll=2)

    col = lambda c0: pl.BlockSpec((t, hd), functools.partial(lambda h, c0: (0, c0 + h), c0=c0))
    return pl.pallas_call(
        body, name=name, grid=(ATT_HEADS,),
        in_specs=[col(qo), col(ko), col(vo), pl.BlockSpec((None, CHUNK, ATT_BAND), lambda h: (h, 0, 0))],
        out_specs=col(0), out_shape=jax.ShapeDtypeStruct((t, w), BF16),
        scratch_shapes=[pltpu.VMEM((t + pad, hd), BF16), pltpu.VMEM((t + pad, hd), BF16)],
        compiler_params=_cp(("parallel",)))(proj, proj, proj, bias)


def _attn_bwd(proj, bias, do, w, name):
    t = proj.shape[0]
    hd = w // ATT_HEADS
    pad = ATT_LEFT * CHUNK
    qo, ko, vo = 2 * w // hd, 3 * w // hd, 4 * w // hd

    def body(q_ref, k_ref, v_ref, bias_ref, do_ref, dq_ref, dk_ref, dv_ref, db_ref, kp, vp, dkp, dvp):
        _attn_pad_copy(k_ref, kp, t, pad)
        _attn_pad_copy(v_ref, vp, t, pad)
        rb = min(512, t)
        for r0 in range(0, t + pad, rb):
            n = min(rb, t + pad - r0)
            dkp[pl.ds(r0, n), :] = jnp.zeros((n, hd), F32)
            dvp[pl.ds(r0, n), :] = jnp.zeros((n, hd), F32)
        db_ref[...] = jnp.zeros_like(db_ref)
        scale = hd ** -0.5

        def chunk(c, _, masked):
            r0, qc, kb, p = _attn_scores(q_ref, kp, bias_ref, c, hd, masked)
            band = pl.ds(r0, ATT_BAND)
            vb = vp[band, :]
            doc = do_ref[pl.ds(r0, CHUNK), :]
            dp = _dot(doc, vb, NT)
            ds = p * (dp - jnp.sum(dp * p, axis=-1, keepdims=True))
            db_ref[...] += ds
            dsb = ds.astype(BF16)
            dq_ref[pl.ds(r0, CHUNK), :] = (_dot(dsb, kb) * scale).astype(BF16)
            dkp[band, :] += _dot(dsb, qc, TN) * scale
            dvp[band, :] += _dot(p.astype(BF16), doc, TN)
            return 0

        n_masked = min(ATT_LEFT, t // CHUNK)
        lax.fori_loop(0, n_masked, functools.partial(chunk, masked=True), 0)
        lax.fori_loop(n_masked, t // CHUNK, functools.partial(chunk, masked=False), 0, unroll=2)
        for r0 in range(0, t, rb):
            dk_ref[pl.ds(r0, rb), :] = dkp[pl.ds(pad + r0, rb), :].astype(BF16)
            dv_ref[pl.ds(r0, rb), :] = dvp[pl.ds(pad + r0, rb), :].astype(BF16)

    col = lambda c0: pl.BlockSpec((t, hd), functools.partial(lambda h, c0: (0, c0 + h), c0=c0))
    tb = pl.BlockSpec((None, CHUNK, ATT_BAND), lambda h: (h, 0, 0))
    o = jax.ShapeDtypeStruct((t, w), BF16)
    return pl.pallas_call(
        body, name=name, grid=(ATT_HEADS,),
        in_specs=[col(qo), col(ko), col(vo), tb, col(0)],
        out_specs=[col(0), col(0), col(0), tb],
        out_shape=[o, o, o, jax.ShapeDtypeStruct((ATT_HEADS, CHUNK, ATT_BAND), F32)],
        scratch_shapes=[pltpu.VMEM((t + pad, hd), BF16), pltpu.VMEM((t + pad, hd), BF16),
                        pltpu.VMEM((t + pad, hd), F32), pltpu.VMEM((t + pad, hd), F32)],
        compiler_params=_cp(("parallel",)))(proj, proj, proj, bias, do)


def _rel_index():
    q_pos = ATT_LEFT * CHUNK + np.arange(CHUNK)
    k_pos = np.arange(ATT_BAND)
    return (np.clip(q_pos[:, None] - k_pos[None, :], -MAX_REL, MAX_REL) + MAX_REL).astype(np.int32).reshape(-1)


def _bias_expand(rel_bias, name):
    nl, nh, _ = rel_bias.shape
    n = CHUNK * ATT_BAND
    kb = n // 8
    idx = jnp.asarray(_rel_index().reshape(1, n))
    tab = jnp.pad(rel_bias, ((0, 0), (0, 0), (0, N_REL_PAD - N_REL)))

    def body(t_ref, i_ref, o_ref):
        onehot = (lax.broadcasted_iota(jnp.int32, (N_REL_PAD, kb), 0) == i_ref[...]).astype(F32)
        o_ref[...] = lax.dot_general(t_ref[...], onehot, NN, precision=lax.Precision.HIGHEST,
                                     preferred_element_type=F32)

    out = pl.pallas_call(
        body, name=name, grid=(nl, n // kb),
        in_specs=[pl.BlockSpec((None, nh, N_REL_PAD), lambda l, j: (l, 0, 0)), pl.BlockSpec((1, kb), lambda l, j: (0, j))],
        out_specs=pl.BlockSpec((None, nh, kb), lambda l, j: (l, 0, j)),
        out_shape=jax.ShapeDtypeStruct((nl, nh, n), F32), compiler_params=_cp(("parallel", "parallel")))(tab, idx)
    return out.reshape(nl, nh, CHUNK, ATT_BAND)


def _bias_reduce(dbias, name):
    nl, nh = dbias.shape[:2]
    n = CHUNK * ATT_BAND
    kb = n // 8
    idx = jnp.asarray(_rel_index().reshape(n, 1))

    def body(d_ref, i_ref, o_ref):
        @pl.when(pl.program_id(1) == 0)
        def _():
            o_ref[...] = jnp.zeros_like(o_ref)

        onehot = (lax.broadcasted_iota(jnp.int32, (kb, N_REL_PAD), 1) == i_ref[...]).astype(F32)
        o_ref[...] += lax.dot_general(d_ref[...], onehot, NN, precision=lax.Precision.HIGHEST,
                                      preferred_element_type=F32)

    out = pl.pallas_call(
        body, name=name, grid=(nl, n // kb),
        in_specs=[pl.BlockSpec((None, nh, kb), lambda l, j: (l, 0, j)), pl.BlockSpec((kb, 1), lambda l, j: (j, 0))],
        out_specs=pl.BlockSpec((None, nh, N_REL_PAD), lambda l, j: (l, 0, 0)),
        out_shape=jax.ShapeDtypeStruct((nl, nh, N_REL_PAD), F32),
        compiler_params=_cp(("parallel", "arbitrary")))(dbias.reshape(nl, nh, n), idx)
    return out[:, :, :N_REL]


def _s5_prep_math(a_re, a_im, ls, btr, bti):
    step = jnp.exp(ls)
    mag = jnp.exp(a_re * step)
    ang = a_im * step
    lr = mag * jnp.cos(ang)
    li = mag * jnp.sin(ang)
    den = a_re * a_re + a_im * a_im
    nr = lr - 1.0
    cr = (nr * a_re + li * a_im) / den
    ci = (li * a_re - nr * a_im) / den
    bbr = cr[:, None, :] * btr - ci[:, None, :] * bti
    bbi = cr[:, None, :] * bti + ci[:, None, :] * btr
    return lr, li, bbr, bbi


def _s5_prep(a_re, a_im, ls, btr, bti, name):
    nl, g, p = a_re.shape
    h = btr.shape[2]

    def body(ar, ai, l_ref, br, bi, o1, o2, o3, o4):
        r = _s5_prep_math(ar[...], ai[...], l_ref[...], br[...], bi[...])
        o1[...], o2[...], o3[...], o4[...] = r

    m2 = pl.BlockSpec((None, g, p), lambda l: (l, 0, 0))
    m1 = pl.BlockSpec((None, g, 1), lambda l: (l, 0, 0))
    m3 = pl.BlockSpec((None, g, h, p), lambda l: (l, 0, 0, 0))
    s2 = jax.ShapeDtypeStruct((nl, g, p), F32)
    s3 = jax.ShapeDtypeStruct((nl, g, h, p), F32)
    return pl.pallas_call(body, name=name, grid=(nl,), in_specs=[m2, m2, m1, m3, m3], out_specs=[m2, m2, m3, m3],
                          out_shape=[s2, s2, s3, s3], compiler_params=_cp(("parallel",)))(a_re, a_im, ls, btr, bti)


def _s5_prep_bwd(a_re, a_im, ls, btr, bti, dlr, dli, dbbr, dbbi, name):
    nl, g, p = a_re.shape
    h = btr.shape[2]

    def body(ar, ai, l_ref, br, bi, g1, g2, g3, g4, o1, o2, o3, o4, o5):
        _, vjp = jax.vjp(_s5_prep_math, ar[...], ai[...], l_ref[...], br[...], bi[...])
        o1[...], o2[...], o3[...], o4[...], o5[...] = vjp((g1[...], g2[...], g3[...], g4[...]))

    m2 = pl.BlockSpec((None, g, p), lambda l: (l, 0, 0))
    m1 = pl.BlockSpec((None, g, 1), lambda l: (l, 0, 0))
    m3 = pl.BlockSpec((None, g, h, p), lambda l: (l, 0, 0, 0))
    s2 = jax.ShapeDtypeStruct((nl, g, p), F32)
    s1 = jax.ShapeDtypeStruct((nl, g, 1), F32)
    s3 = jax.ShapeDtypeStruct((nl, g, h, p), F32)
    return pl.pallas_call(body, name=name, grid=(nl,), in_specs=[m2, m2, m1, m3, m3, m2, m2, m3, m3],
                          out_specs=[m2, m2, m1, m3, m3], out_shape=[s2, s2, s1, s3, s3],
                          compiler_params=_cp(("parallel",)))(a_re, a_im, ls, btr, bti, dlr, dli, dbbr, dbbi)


def _cpow(lr, li, n):
    rr = ri = None
    br, bi = lr, li
    while n:
        if n & 1:
            rr, ri = (br, bi) if rr is None else (rr * br - ri * bi, rr * bi + ri * br)
        n >>= 1
        if n:
            br, bi = br * br - bi * bi, 2.0 * br * bi
    return rr, ri


def _permute_rows(src, dst, seg, inverse):
    def body(k, _):
        tile = pl.ds(pl.multiple_of(k * NSEG, NSEG), NSEG)
        spread = pl.ds(k, NSEG, stride=seg)
        if inverse:
            dst[spread, :] = src[tile, :]
        else:
            dst[tile, :] = src[spread, :]
        return 0

    lax.fori_loop(0, seg, body, 0, unroll=8)


def _s5_states(up, bbt_ref, lam_ref, xs, t):
    seg = t // NSEG
    rb = min(512, t)
    nj = xs.shape[0]
    nh = nj // 2
    for r0 in range(0, t, rb):
        bu = _dot(up[pl.ds(r0, rb), :].astype(BF16), bbt_ref[...])
        for j in range(nj):
            xs[j, pl.ds(r0, rb), :] = bu[:, j * LANE:(j + 1) * LANE]
    lamv = lam_ref[...]
    lr = [jnp.broadcast_to(lamv[j:j + 1], (NSEG, LANE)) for j in range(nh)]
    li = [jnp.broadcast_to(lamv[nh + j:nh + j + 1], (NSEG, LANE)) for j in range(nh)]
    zero = jnp.zeros((NSEG, LANE), F32)

    def load(k):
        rows = pl.ds(pl.multiple_of(k * NSEG, NSEG), NSEG)
        return (tuple(xs.at[j][rows, :] for j in range(nh)), tuple(xs.at[nh + j][rows, :] for j in range(nh)))

    def step(k, c):
        xr, xi, br, bi = c
        nbr, nbi = load(jnp.minimum(k + 1, seg - 1))
        rows = pl.ds(pl.multiple_of(k * NSEG, NSEG), NSEG)
        nxr, nxi = [], []
        for j in range(nh):
            r = lr[j] * xr[j] - li[j] * xi[j] + br[j]
            i = lr[j] * xi[j] + li[j] * xr[j] + bi[j]
            xs.at[j][rows, :] = r
            xs.at[nh + j][rows, :] = i
            nxr.append(r)
            nxi.append(i)
        return tuple(nxr), tuple(nxi), nbr, nbi

    xr, xi, _, _ = _loop(seg, step, ((zero,) * nh, (zero,) * nh) + load(0))
    pw = [_cpow(lr[j], li[j], seg) for j in range(nh)]
    sub = lax.broadcasted_iota(jnp.int32, (NSEG, LANE), 0)
    ctr, cti = [], []
    for j in range(nh):
        plr, pli = pw[j]
        cr, ci = zero, zero
        for s in range(1, NSEG):
            er = pltpu.roll(xr[j], 1, 0)
            ei = pltpu.roll(xi[j], 1, 0)
            pr_ = pltpu.roll(cr, 1, 0)
            pi_ = pltpu.roll(ci, 1, 0)
            nr = er + plr * pr_ - pli * pi_
            ni = ei + plr * pi_ + pli * pr_
            cr = jnp.where(sub == s, nr, cr)
            ci = jnp.where(sub == s, ni, ci)
        ctr.append(cr)
        cti.append(ci)

    def fix(k, c):
        rr, ri, xr_, xi_ = c
        nxr, nxi = load(jnp.minimum(k + 1, seg - 1))
        rows = pl.ds(pl.multiple_of(k * NSEG, NSEG), NSEG)
        nr, ni = [], []
        for j in range(nh):
            r = lr[j] * rr[j] - li[j] * ri[j]
            i = lr[j] * ri[j] + li[j] * rr[j]
            xs.at[j][rows, :] = xr_[j] + r
            xs.at[nh + j][rows, :] = xi_[j] + i
            nr.append(r)
            ni.append(i)
        return tuple(nr), tuple(ni), nxr, nxi

    _loop(seg, fix, (tuple(ctr), tuple(cti)) + load(0))
    return lr, li, pw


def _s5_fwd(proj, bbt, cmat, lam, dvec, w, name):
    t = proj.shape[0]
    nt = w // LANE
    nj = bbt.shape[2] // LANE
    seg = t // NSEG
    rb = min(512, t)
    uo = 5 * nt

    def body(u_ref, bbt_ref, cm_ref, lam_ref, d_ref, s5o_ref, yc_ref, xs, uf, up):
        for r0 in range(0, t, rb):
            uf[pl.ds(r0, rb), :] = u_ref[pl.ds(r0, rb), :].astype(F32)
        _permute_rows(uf, up, seg, False)
        _s5_states(up, bbt_ref, lam_ref, xs, t)
        for r0 in range(0, t, rb):
            rows = pl.ds(r0, rb)
            y = d_ref[...] * up[rows, :]
            for j in range(nj):
                y = y + _dot(xs.at[j][rows, :].astype(BF16), cm_ref[pl.ds(j * LANE, LANE), :])
            uf[rows, :] = y
        _permute_rows(uf, s5o_ref, seg, True)
        for r0 in range(0, t, rb):
            rows = pl.ds(r0, rb)
            yc_ref[rows, :] = _gelu(s5o_ref[rows, :]).astype(BF16)

    col = lambda c0: pl.BlockSpec((t, LANE), functools.partial(lambda i, c0: (0, c0 + i), c0=c0))
    return pl.pallas_call(
        body, name=name, grid=(nt,),
        in_specs=[col(uo), pl.BlockSpec((None, LANE, nj * LANE), lambda i: (i, 0, 0)),
                  pl.BlockSpec((None, nj * LANE, LANE), lambda i: (i, 0, 0)),
                  pl.BlockSpec((None, nj, LANE), lambda i: (i, 0, 0)), pl.BlockSpec((1, LANE), lambda i: (0, i))],
        out_specs=[col(0), col(0)],
        out_shape=[jax.ShapeDtypeStruct((t, w), F32), jax.ShapeDtypeStruct((t, w), BF16)],
        scratch_shapes=[pltpu.VMEM((nj, t, LANE), F32), pltpu.VMEM((t, LANE), F32), pltpu.VMEM((t, LANE), F32)],
        compiler_params=_cp(("parallel",)))(proj, bbt, cmat, lam, dvec)


def _s5_bwd(proj, dy, bbt, bbm, cmt, lam, dvec, w, name):
    t = proj.shape[0]
    nt = w // LANE
    nj = bbt.shape[2] // LANE
    nh = nj // 2
    seg = t // NSEG
    rb = min(512, t)
    uo = 5 * nt

    def body(u_ref, dy_ref, bbt_ref, bbm_ref, cmt_ref, lam_ref, d_ref, du_ref, dbbt_ref, dcm_ref, dlam_ref, dd_ref,
             xs, gs, uf, up, dyp):
        for r0 in range(0, t, rb):
            uf[pl.ds(r0, rb), :] = u_ref[pl.ds(r0, rb), :].astype(F32)
        _permute_rows(uf, up, seg, False)
        for r0 in range(0, t, rb):
            uf[pl.ds(r0, rb), :] = dy_ref[pl.ds(r0, rb), :].astype(F32)
        _permute_rows(uf, dyp, seg, False)
        lr, li, pw = _s5_states(up, bbt_ref, lam_ref, xs, t)
        dcm = [jnp.zeros((LANE, LANE), F32) for _ in range(nj)]
        for r0 in range(0, t, rb):
            rows = pl.ds(r0, rb)
            dyb = dyp[rows, :].astype(BF16)
            dx = _dot(dyb, cmt_ref[...])
            for j in range(nj):
                gs.at[j][rows, :] = dx[:, j * LANE:(j + 1) * LANE]
                dcm[j] = dcm[j] + _dot(xs.at[j][rows, :].astype(BF16), dyb, TN)
        for j in range(nj):
            dcm_ref[pl.ds(j * LANE, LANE), :] = dcm[j]
        zero = jnp.zeros((NSEG, LANE), F32)
        sub = lax.broadcasted_iota(jnp.int32, (NSEG, LANE), 0)
        tile = lambda k: pl.ds(pl.multiple_of(k * NSEG, NSEG), NSEG)

        def gload(k):
            return (tuple(gs.at[j][tile(k), :] for j in range(nh)), tuple(gs.at[nh + j][tile(k), :] for j in range(nh)))

        def xload(k):
            return (tuple(xs.at[j][tile(k), :] for j in range(nh)), tuple(xs.at[nh + j][tile(k), :] for j in range(nh)))

        def step(kk, c):
            gr, gi, dr, di = c
            k = seg - 1 - kk
            ndr, ndi = gload(jnp.maximum(k - 1, 0))
            ngr, ngi = [], []
            for j in range(nh):
                r = lr[j] * gr[j] + li[j] * gi[j] + dr[j]
                i = lr[j] * gi[j] - li[j] * gr[j] + di[j]
                gs.at[j][tile(k), :] = r
                gs.at[nh + j][tile(k), :] = i
                ngr.append(r)
                ngi.append(i)
            return tuple(ngr), tuple(ngi), ndr, ndi

        g0r, g0i, _, _ = _loop(seg, step, ((zero,) * nh, (zero,) * nh) + gload(seg - 1))
        ctr, cti = [], []
        for j in range(nh):
            plr, pli = pw[j]
            cr, ci = zero, zero
            for s in range(NSEG - 2, -1, -1):
                sr = pltpu.roll(g0r[j], NSEG - 1, 0)
                si = pltpu.roll(g0i[j], NSEG - 1, 0)
                pr_ = pltpu.roll(cr, NSEG - 1, 0)
                pi_ = pltpu.roll(ci, NSEG - 1, 0)
                nr = sr + plr * pr_ + pli * pi_
                ni = si + plr * pi_ - pli * pr_
                cr = jnp.where(sub == s, nr, cr)
                ci = jnp.where(sub == s, ni, ci)
            ctr.append(cr)
            cti.append(ci)

        def fix_step(k, c, xr, xi, gr, gi):
            rr, ri, ar, ai = c
            nr, ni, nar, nai = [], [], [], []
            for j in range(nh):
                r = lr[j] * rr[j] + li[j] * ri[j]
                i = lr[j] * ri[j] - li[j] * rr[j]
                g_r = gr[j] + r
                g_i = gi[j] + i
                gs.at[j][tile(k), :] = g_r
                gs.at[nh + j][tile(k), :] = g_i
                nar.append(ar[j] + g_r * xr[j] + g_i * xi[j])
                nai.append(ai[j] + g_i * xr[j] - g_r * xi[j])
                nr.append(r)
                ni.append(i)
            return tuple(nr), tuple(ni), tuple(nar), tuple(nai)

        def fix(kk, c):
            k = seg - 1 - kk
            nk = jnp.maximum(k - 1, 0)
            ngr, ngi = gload(nk)
            nxr, nxi = xload(jnp.maximum(nk - 1, 0))
            return fix_step(k, c[:4], c[6], c[7], c[4], c[5]) + (ngr, ngi, nxr, nxi)

        init = (tuple(ctr), tuple(cti), (zero,) * nh, (zero,) * nh) + gload(seg - 1) + xload(seg - 2)
        c = _loop(seg - 1, fix, init)
        lxr, lxi = xload(seg - 1)
        shift = lambda v: jnp.where(sub == 0, 0.0, pltpu.roll(v, 1, 0))
        _, _, ar, ai = fix_step(0, c[:4], tuple(shift(v) for v in lxr), tuple(shift(v) for v in lxi), c[4], c[5])
        for j in range(nh):
            dlam_ref[pl.ds(j, 1), :] = jnp.sum(ar[j], axis=0, keepdims=True)
            dlam_ref[pl.ds(nh + j, 1), :] = jnp.sum(ai[j], axis=0, keepdims=True)
        dbb = [jnp.zeros((LANE, LANE), F32) for _ in range(nj)]
        dd = jnp.zeros((1, LANE), F32)
        for r0 in range(0, t, rb):
            rows = pl.ds(r0, rb)
            uv = up[rows, :]
            ub = uv.astype(BF16)
            dyf = dyp[rows, :]
            du = d_ref[...] * dyf
            dd = dd + jnp.sum(dyf * uv, axis=0, keepdims=True)
            for j in range(nj):
                gb = gs.at[j][rows, :].astype(BF16)
                du = du + _dot(gb, bbm_ref[pl.ds(j * LANE, LANE), :])
                dbb[j] = dbb[j] + _dot(ub, gb, TN)
            uf[rows, :] = du
        for j in range(nj):
            dbbt_ref[:, pl.ds(j * LANE, LANE)] = dbb[j]
        dd_ref[...] = dd
        _permute_rows(uf, up, seg, True)
        for r0 in range(0, t, rb):
            rows = pl.ds(r0, rb)
            du_ref[rows, :] = up[rows, :].astype(BF16)

    col = lambda c0: pl.BlockSpec((t, LANE), functools.partial(lambda i, c0: (0, c0 + i), c0=c0))
    wide = pl.BlockSpec((None, LANE, nj * LANE), lambda i: (i, 0, 0))
    tall = pl.BlockSpec((None, nj * LANE, LANE), lambda i: (i, 0, 0))
    lam_s = pl.BlockSpec((None, nj, LANE), lambda i: (i, 0, 0))
    vec = pl.BlockSpec((1, LANE), lambda i: (0, i))
    flat = lambda: pltpu.VMEM((t, LANE), F32)
    return pl.pallas_call(
        body, name=name, grid=(nt,),
        in_specs=[col(uo), col(0), wide, tall, wide, lam_s, vec],
        out_specs=[col(0), wide, tall, lam_s, vec],
        out_shape=[jax.ShapeDtypeStruct((t, w), BF16), jax.ShapeDtypeStruct((nt, LANE, nj * LANE), F32),
                   jax.ShapeDtypeStruct((nt, nj * LANE, LANE), F32), jax.ShapeDtypeStruct((nt, nj, LANE), F32),
                   jax.ShapeDtypeStruct((1, w), F32)],
        scratch_shapes=[pltpu.VMEM((nj, t, LANE), F32), pltpu.VMEM((nj, t, LANE), F32), flat(), flat(), flat()],
        compiler_params=_cp(("parallel",)))(proj, dy, bbt, bbm, cmt, lam, dvec)


def _block_diag(x, nt):
    nb, r, c = x.shape
    b = nb // nt
    eye = jnp.eye(b, dtype=x.dtype)
    return jnp.einsum("ibrc,bk->ibrkc", x.reshape(nt, b, r, c), eye).reshape(nt, b * r, b * c)


def _block_diag_extract(x, b):
    nt, br, bc = x.shape
    r, c = br // b, bc // b
    eye = jnp.eye(b, dtype=x.dtype)
    return jnp.einsum("ibrkc,bk->ibrc", x.reshape(nt, b, r, b, c), eye).reshape(nt * b, r, c)


def _adamw_math(w, g, m, v):
    m = ADAM_B1 * m + (1.0 - ADAM_B1) * g
    v = ADAM_B2 * v + (1.0 - ADAM_B2) * (g * g)
    m_hat = m / (1.0 - ADAM_B1 ** ADAM_STEP)
    v_hat = v / (1.0 - ADAM_B2 ** ADAM_STEP)
    delta = -ADAM_LR * (m_hat / (jnp.sqrt(v_hat) + ADAM_EPS) + ADAM_WD * w)
    return delta, m, v


def _adamw_reduce(parts, w, m, v, prev, l, name):
    n, r, c = parts.shape
    tr = r
    for cand in (512, 256, 128, 64, 32, 16, 8):
        if r % cand == 0 and cand * c * 4 <= 2 * 1024 * 1024:
            tr = cand
            break
    nb = r // tr
    if prev is None:
        prev = [lax.empty(w.shape, F32) for _ in range(4)]

    def body(p_ref, w_ref, m_ref, v_ref, *rest):
        g_ref, d_ref, nm_ref, nv_ref = rest[4:]
        g = p_ref[0].astype(F32)
        for q in range(1, n):
            g = g + p_ref[q].astype(F32)
        g_ref[...] = g
        d_ref[...], nm_ref[...], nv_ref[...] = _adamw_math(w_ref[...], g, m_ref[...], v_ref[...])

    row = pl.BlockSpec((tr, c), lambda i: (l * nb + i, 0))
    any_spec = pl.BlockSpec(memory_space=pl.ANY)
    o = jax.ShapeDtypeStruct(w.shape, F32)
    return pl.pallas_call(
        body, name=name, grid=(nb,),
        in_specs=[pl.BlockSpec((n, tr, c), lambda i: (0, i, 0)), row, row, row] + [any_spec] * 4,
        out_specs=[row, row, row, row], out_shape=[o, o, o, o],
        input_output_aliases={4: 0, 5: 1, 6: 2, 7: 3},
        compiler_params=_cp(("parallel",)))(parts, w, m, v, *prev)


def _adamw_flat(w, g, m, v, name):
    r, c = w.shape

    def body(w_ref, g_ref, m_ref, v_ref, d_ref, nm_ref, nv_ref):
        d_ref[...], nm_ref[...], nv_ref[...] = _adamw_math(w_ref[...], g_ref[...], m_ref[...], v_ref[...])

    o = jax.ShapeDtypeStruct((r, c), F32)
    row = pl.BlockSpec((FLAT_ROWS, c), lambda i: (i, 0))
    return pl.pallas_call(body, name=name, grid=(r // FLAT_ROWS,), in_specs=[row] * 4, out_specs=[row] * 3,
                          out_shape=[o, o, o], compiler_params=_cp(("parallel",)))(w, g, m, v)


def _sum_rows8(x, name):
    n, r, c = x.shape

    def body(x_ref, o_ref):
        acc = x_ref[0]
        for q in range(1, n):
            acc = acc + x_ref[q]
        o_ref[...] = acc

    return pl.pallas_call(
        body, name=name, grid=(r // FLAT_ROWS,), in_specs=[pl.BlockSpec((n, FLAT_ROWS, c), lambda i: (0, i, 0))],
        out_specs=pl.BlockSpec((FLAT_ROWS, c), lambda i: (i, 0)), out_shape=jax.ShapeDtypeStruct((r, c), F32),
        compiler_params=_cp(("parallel",)))(x)


def _row_tile(r, row_bytes, limit=2 * 1024 * 1024):
    for cand in (512, 256, 128, 64, 32, 16):
        if r % cand == 0 and cand * row_bytes <= limit:
            return cand
    return r


def _pair_sum(own, sib, core, chip, name):
    _, _, r, c = own.shape
    tr = _row_tile(r, 2 * c)

    def body(core_ref, chip_ref, a_ref, b_ref, o_ref, land_ref):
        s = (a_ref[...].astype(F32) + b_ref[...].astype(F32)).astype(BF16)
        o_ref[...] = s

        @pl.when(pl.program_id(1) == chip_ref[0])
        def _():
            land_ref[...] = s

    grid_spec = pltpu.PrefetchScalarGridSpec(
        num_scalar_prefetch=2, grid=(r // tr, NCHIP),
        in_specs=[pl.BlockSpec((None, None, tr, c), lambda i, p, cr, hr: (p, cr[0], i, 0)),
                  pl.BlockSpec((None, tr, c), lambda i, p, cr, hr: (p, i, 0))],
        out_specs=[pl.BlockSpec((None, tr, c), lambda i, p, cr, hr: (p, i, 0)),
                   pl.BlockSpec((None, tr, c), lambda i, p, cr, hr: (hr[0], i, 0))])
    o = jax.ShapeDtypeStruct((NCHIP, r, c), BF16)
    return pl.pallas_call(body, name=name, grid_spec=grid_spec, out_shape=[o, o],
                          compiler_params=_cp(("parallel", "arbitrary")))(core, chip, own, sib)


def _cast_place(w, l, me, after, name, dtype=BF16):
    _, r, c = w.shape
    tr = _row_tile(r, 4 * c)

    def body(m_ref, w_ref, a_ref, o_ref, land_ref):
        v = w_ref[...].astype(dtype)
        o_ref[...] = v
        land_ref[...] = v

    grid_spec = pltpu.PrefetchScalarGridSpec(
        num_scalar_prefetch=1, grid=(r // tr,),
        in_specs=[pl.BlockSpec((None, tr, c), lambda i, mr: (l, i, 0)), pl.BlockSpec(memory_space=pl.ANY)],
        out_specs=[pl.BlockSpec((tr, c), lambda i, mr: (i, 0)), pl.BlockSpec((None, tr, c), lambda i, mr: (mr[0], i, 0))])
    return pl.pallas_call(body, name=name, grid_spec=grid_spec,
                          out_shape=[jax.ShapeDtypeStruct((r, c), dtype), jax.ShapeDtypeStruct((NDEV, r, c), dtype)],
                          compiler_params=_cp(("parallel",)))(me, w, after)


def _all_gather(xs, name):
    n = len(xs)

    def body(*refs):
        x_refs, o_refs = refs[:n], refs[n:2 * n]
        send_sems, recv_sems, local_sems = refs[2 * n:]
        x, y, c = lax.axis_index("x"), lax.axis_index("y"), lax.axis_index("c")
        me, sibling = (x, y, c), (x, y, 1 - c)
        chips = [(1 - x, y), (x, 1 - y), (1 - x, 1 - y)]

        def copy(t, k, block, to, src=None):
            dst = o_refs[t].at[4 * block[0] + 2 * block[1] + block[2]]
            return pltpu.make_async_remote_copy(
                src_ref=dst if src is None else src, dst_ref=dst,
                send_sem=send_sems.at[7 * t + k], recv_sem=recv_sems.at[7 * t + k],
                device_id=to, device_id_type=MESH)

        mine, first, passed = [], [], []
        for t in range(n):
            cp = pltpu.make_async_copy(x_refs[t], o_refs[t].at[4 * x + 2 * y + c], local_sems.at[t])
            cp.start()
            mine.append(cp)
            first.append(copy(t, 0, me, sibling, src=x_refs[t]))
            first += [copy(t, 1 + j, me, (*chip, c), src=x_refs[t]) for j, chip in enumerate(chips)]
        for cp in first:
            cp.start()
        for t in range(n):
            for j, chip in enumerate(chips):
                copy(t, 1 + j, (*chip, c), me).wait_recv()
                cp = copy(t, 4 + j, (*chip, c), sibling)
                cp.start()
                passed.append(cp)
        for t in range(n):
            copy(t, 0, sibling, me).wait_recv()
            for j, chip in enumerate(chips):
                copy(t, 4 + j, (*chip, 1 - c), me).wait_recv()
        for cp in first + passed:
            cp.wait_send()
        for cp in mine:
            cp.wait()

    any_spec = pl.BlockSpec(memory_space=pl.ANY)
    return pl.pallas_call(
        body, name=name, in_specs=[any_spec] * n, out_specs=[any_spec] * n,
        out_shape=[jax.ShapeDtypeStruct((NDEV,) + a.shape, a.dtype) for a in xs],
        scratch_shapes=[pltpu.SemaphoreType.DMA((7 * n,)), pltpu.SemaphoreType.DMA((7 * n,)),
                        pltpu.SemaphoreType.DMA((n,))],
        )(*xs)


def _sibling_exchange(xs, name):
    n = len(xs)

    def body(*refs):
        x_refs, o_refs = refs[:n], refs[n:2 * n]
        send_sems, recv_sems = refs[2 * n:]
        x, y, c = lax.axis_index("x"), lax.axis_index("y"), lax.axis_index("c")
        cps = []
        for t in range(n):
            cp = pltpu.make_async_remote_copy(
                src_ref=x_refs[t].at[:, 1 - c], dst_ref=o_refs[t], send_sem=send_sems.at[t], recv_sem=recv_sems.at[t],
                device_id=(x, y, 1 - c), device_id_type=MESH)
            cp.start()
            cps.append(cp)
        for cp in cps:
            cp.wait()

    any_spec = pl.BlockSpec(memory_space=pl.ANY)
    return pl.pallas_call(
        body, name=name, in_specs=[any_spec] * n, out_specs=[any_spec] * n,
        out_shape=[jax.ShapeDtypeStruct((a.shape[0],) + a.shape[2:], a.dtype) for a in xs],
        scratch_shapes=[pltpu.SemaphoreType.DMA((n,)), pltpu.SemaphoreType.DMA((n,))],
        )(*xs)


def _split_copies(kind, srcs, lands, send_sems, recv_sems):
    x, y, c = lax.axis_index("x"), lax.axis_index("y"), lax.axis_index("c")
    flip = lambda v, b: 1 - v if b else v
    cps = []
    for t in range(len(srcs)):
        if kind == "all":
            for mask in range(1, NDEV):
                peer = (flip(x, mask & 4), flip(y, mask & 2), flip(c, mask & 1))
                cps.append(pltpu.make_async_remote_copy(
                    src_ref=srcs[t], dst_ref=lands[t].at[4 * x + 2 * y + c],
                    send_sem=send_sems.at[(NDEV - 1) * t + mask - 1], recv_sem=recv_sems.at[(NDEV - 1) * t + mask - 1],
                    device_id=peer, device_id_type=MESH))
        else:
            for j, (px, py) in enumerate([(1 - x, y), (x, 1 - y), (1 - x, 1 - y)]):
                cps.append(pltpu.make_async_remote_copy(
                    src_ref=srcs[t].at[2 * px + py], dst_ref=lands[t].at[2 * x + y],
                    send_sem=send_sems.at[3 * t + j], recv_sem=recv_sems.at[3 * t + j],
                    device_id=(px, py, c), device_id_type=MESH))
    return cps


def _split_start(kind, srcs, lands, name):
    n = len(srcs)
    ncp = n * (NDEV - 1 if kind == "all" else NCHIP - 1)
    arrs = list(srcs) + list(lands)

    def body(*refs):
        for cp in _split_copies(kind, refs[:n], refs[n:2 * n], refs[2 * n], refs[2 * n + 1]):
            cp.start()
        refs[-1][...] = jnp.zeros_like(refs[-1])

    hbm = pl.BlockSpec(memory_space=pltpu.HBM)
    sem = pl.BlockSpec(memory_space=pltpu.SEMAPHORE)
    return pl.pallas_call(
        body, name=name,
        out_shape=(pltpu.SemaphoreType.DMA((ncp,)), pltpu.SemaphoreType.DMA((ncp,)),
                   *[pltpu.HBM(a.shape, a.dtype) for a in arrs], jax.ShapeDtypeStruct((8, LANE), F32)),
        in_specs=[hbm] * (2 * n), out_specs=(sem, sem, *[hbm] * (2 * n), pl.BlockSpec(memory_space=pltpu.VMEM)),
        input_output_aliases={i: 2 + i for i in range(2 * n)},
        compiler_params=pltpu.CompilerParams(has_side_effects=pltpu.SideEffectType.DATAFLOW_SIDE_EFFECTING),
    )(*[pltpu.with_memory_space_constraint(a, pltpu.HBM) for a in arrs])


def _split_wait(kind, started, after, name):
    send_sems, recv_sems, thru = started[0], started[1], list(started[2:])
    n = len(thru) // 2

    def body(*refs):
        for cp in _split_copies(kind, refs[:n], refs[n:2 * n], refs[2 * n], refs[2 * n + 1]):
            cp.wait_send()
            cp.wait_recv()

    hbm = pl.BlockSpec(memory_space=pltpu.HBM)
    sem = pl.BlockSpec(memory_space=pltpu.SEMAPHORE)
    outs = pl.pallas_call(
        body, name=name, out_shape=tuple(pltpu.HBM(a.shape, a.dtype) for a in thru),
        in_specs=[hbm] * (2 * n) + [sem, sem, pl.BlockSpec(memory_space=pl.ANY)], out_specs=tuple([hbm] * (2 * n)),
        input_output_aliases={i: i for i in range(2 * n)},
        compiler_params=pltpu.CompilerParams(has_side_effects=pltpu.SideEffectType.DATAFLOW_SIDE_EFFECTING),
    )(*thru, send_sems, recv_sems, after)
    return list(outs[n:])


_SMALL = ["norm_mix_g", "lru_conv_b", "lru_wa", "lru_ba", "lru_wx", "lru_bx", "lru_lambda", "attn_rel_bias",
          "ssm_a_re", "ssm_a_im", "ssm_b_re", "ssm_b_im", "ssm_c_re", "ssm_c_im", "ssm_d", "ssm_log_step",
          "norm_ffn_g", "norm_final_g"]
_SMALL_SHARDED = ["gate_bias", "lru_conv_w"]
_BIG = ["w_in", "ssm_w_glu", "w_branch", "w_out", "w_ffn_gate", "w_ffn_up", "w_ffn_down"]
_WEIGHTS = ["norm_mix_g", "w_in", "gate_bias", "lru_conv_w", "lru_conv_b", "lru_wa", "lru_ba", "lru_wx", "lru_bx",
            "lru_lambda", "attn_rel_bias", "ssm_a_re", "ssm_a_im", "ssm_b_re", "ssm_b_im", "ssm_c_re", "ssm_c_im",
            "ssm_d", "ssm_log_step", "ssm_w_glu", "w_branch", "w_out", "norm_ffn_g", "w_ffn_gate", "w_ffn_up",
            "w_ffn_down", "norm_final_g"]


def _device_step(x, target, wts, small_gath, get_gath, on_grads):
    t, d = x.shape
    w = d // 2
    nl = wts["norm_mix_g"].shape[0]
    nt = w // LANE
    f = wts["w_ffn_gate"].shape[-1]
    ncol = wts["w_in"].shape[-1]
    tm = _tile(t, 1024)
    n_g = w // SSM_GROUP
    gpt = LANE // SSM_GROUP
    bw = w // LRU_BLOCKS
    bpt = LANE // bw

    gbias = jnp.transpose(small_gath["gate_bias"], (1, 2, 0, 3)).reshape(nl, N_BRANCH, d)
    convw = jnp.transpose(small_gath["lru_conv_w"], (1, 2, 0, 3)).reshape(nl, CONV_W, w)
    bias_tab = _bias_expand(wts["attn_rel_bias"], "bias_expand")
    btr = jnp.swapaxes(wts["ssm_b_re"], 2, 3)
    bti = jnp.swapaxes(wts["ssm_b_im"], 2, 3)
    ls3 = wts["ssm_log_step"][..., None]
    lam_r, lam_i, bbr, bbi = _s5_prep(wts["ssm_a_re"], wts["ssm_a_im"], ls3, btr, bti, "s5_prep")

    def layer_consts(l):
        c = {}
        c["wa_t"] = _block_diag(wts["lru_wa"][l], nt).astype(BF16)
        c["wx_t"] = _block_diag(wts["lru_wx"][l], nt).astype(BF16)
        c["wat_t"] = jnp.swapaxes(c["wa_t"], 1, 2)
        c["wxt_t"] = jnp.swapaxes(c["wx_t"], 1, 2)
        bd_r, bd_i = _block_diag(bbr[l], nt), _block_diag(bbi[l], nt)
        c["bbt"] = jnp.concatenate([bd_r, bd_i], axis=-1).astype(BF16)
        c["bbm"] = jnp.swapaxes(c["bbt"], 1, 2)
        cd_r, cd_i = _block_diag(wts["ssm_c_re"][l], nt), _block_diag(wts["ssm_c_im"][l], nt)
        c["cmt"] = jnp.concatenate([cd_r, -cd_i], axis=-1).astype(BF16)
        c["cmat"] = jnp.swapaxes(c["cmt"], 1, 2)
        nrow = gpt * SSM_P // LANE
        c["lam"] = jnp.concatenate([lam_r[l].reshape(nt, nrow, LANE), lam_i[l].reshape(nt, nrow, LANE)], axis=1)
        return c

    row1 = lambda a: a.reshape(1, -1)
    tied = lambda g, tok: g if tok is None else g + tok[0, 0]
    saved = []
    for l in range(nl):
        c = layer_consts(l)
        gath, tok = get_gath(l, x)
        h1 = _rmsnorm_fwd(x, tied(row1(wts["norm_mix_g"][l]), tok), f"norm_mix_{l}")
        tn = _tile(ncol, 768)
        nn_ = ncol // tn
        proj = _mm(f"in_proj_{l}", NN, (t // tm, NDEV, nn_), 0, [h1, gath["w_in"]],
                   [pl.BlockSpec((tm, d), lambda i, j, n: (i, 0)),
                    pl.BlockSpec((None, d, tn), lambda i, j, n: (j, 0, n))],
                   jax.ShapeDtypeStruct((t, NDEV * ncol), BF16),
                   pl.BlockSpec((tm, tn), lambda i, j, n: (i, j * nn_ + n)), (8, LANE))
        ya, hs = _lru_fwd(proj, convw[l], row1(wts["lru_conv_b"][l]), c["wa_t"], row1(wts["lru_ba"][l]), c["wx_t"],
                          row1(wts["lru_bx"][l]), row1(wts["lru_lambda"][l]), w, f"lru_fwd_{l}")
        yb = _attn_fwd(proj, bias_tab[l], w, f"attn_fwd_{l}")
        s5o, yc = _s5_fwd(proj, c["bbt"], c["cmat"], c["lam"], row1(wts["ssm_d"][l]), w, f"s5_fwd_{l}")
        cb = d // NDEV

        def branch(y, wg, idx, nm):
            return _mm(nm, NN, (t // tm, NDEV), 0, [y, wg],
                       [pl.BlockSpec((tm, w), lambda i, j: (i, 0)),
                        pl.BlockSpec((None,) * len(idx(0)[:-2]) + (w, cb), lambda i, j: idx(j))],
                       jax.ShapeDtypeStruct((t, d), BF16), pl.BlockSpec((tm, cb), lambda i, j: (i, j)), (8, LANE))

        wb_idx = lambda b: (lambda j: (j, b, 0, 0))
        br_a = branch(ya, gath["w_branch"], wb_idx(0), f"branch_a_{l}")
        br_b = branch(yb, gath["w_branch"], wb_idx(1), f"branch_b_{l}")
        br_c0 = branch(yc, gath["w_branch"], wb_idx(2), f"branch_c_{l}")
        gl = branch(yc, gath["ssm_w_glu"], lambda j: (j, 0, 0), f"branch_glu_{l}")
        merged = _merge_fwd(proj, gbias[l], br_a, br_b, br_c0, gl, d, f"merge_fwd_{l}")
        tno = _tile(d, 1024)
        w_out_full = gath["w_out"].reshape(d, d)
        x_mid = _mm(f"out_proj_{l}", NN, (t // tm, d // tno), 0, [merged, w_out_full, x],
                    [pl.BlockSpec((tm, d), lambda i, n: (i, 0)), pl.BlockSpec((d, tno), lambda i, n: (0, n)),
                     pl.BlockSpec((tm, tno), lambda i, n: (i, n))],
                    jax.ShapeDtypeStruct((t, d), F32), pl.BlockSpec((tm, tno), lambda i, n: (i, n)), (8, LANE),
                    epi=lambda acc, res: acc + res)
        h2 = _rmsnorm_fwd(x_mid, row1(wts["norm_ffn_g"][l]), f"norm_ffn_{l}")
        gp, up, act = _ffn_in(h2, gath["w_ffn_gate"], gath["w_ffn_up"], f"ffn_in_{l}")
        x_next = _mm(f"ffn_down_{l}", NN, (t // tm, d // tno, NDEV), 1, [act, gath["w_ffn_down"], x_mid],
                     [pl.BlockSpec((None, tm, f), lambda i, n, k: (k, i, 0)),
                      pl.BlockSpec((None, f, tno), lambda i, n, k: (k, 0, n)),
                      pl.BlockSpec((tm, tno), lambda i, n, k: (i, n))],
                     jax.ShapeDtypeStruct((t, d), F32), pl.BlockSpec((tm, tno), lambda i, n, k: (i, n)), (tm, tno),
                     epi=lambda acc, res: acc + res)
        saved.append(dict(x=x, h1=h1, proj=proj, ya=ya, hs=hs, yb=yb, s5o=s5o, yc=yc, br_a=br_a, br_b=br_b,
                          br_c0=br_c0, gl=gl, merged=merged, x_mid=x_mid, h2=h2, gp=gp, up=up, act=act, c=c, gath=gath))
        x = x_next

    loss, dx, dxb, dgf = _loss_head(x, row1(wts["norm_final_g"]), target, "loss_head")

    sg = {k: [None] * nl for k in _SMALL + _SMALL_SHARDED if k != "norm_final_g"}
    dbias_tabs = [None] * nl
    dlr_l, dli_l, dbbr_l, dbbi_l = [None] * nl, [None] * nl, [None] * nl, [None] * nl
    tk = _tile(d, 1024)
    for l in range(nl - 1, -1, -1):
        s = saved[l]
        c = s["c"]
        gath = s["gath"]
        gbuf = {}
        tno = _tile(d, 1024)
        gbuf["w_ffn_down"] = _mm(
            f"dw_ffn_down_{l}", TN, (NDEV, d // tno), 0, [s["act"], dxb],
            [pl.BlockSpec((None, t, f), lambda j, n: (j, 0, 0)), pl.BlockSpec((t, tno), lambda j, n: (0, n))],
            jax.ShapeDtypeStruct((NDEV, f, d), BF16), pl.BlockSpec((None, f, tno), lambda j, n: (j, 0, n)), (8, LANE))
        dgp, dup = _ffn_dact(dxb, gath["w_ffn_down"], s["gp"], s["up"], f"d_act_{l}")
        dh2 = _ffn_dh(dgp, dup, gath["w_ffn_gate"], gath["w_ffn_up"], f"d_h2_{l}")

        def ffn_dw(dz, key, nm):
            return _mm(nm, TN, (NDEV, d // tk), 0, [s["h2"], dz],
                       [pl.BlockSpec((t, tk), lambda j, k: (0, k)), pl.BlockSpec((None, t, f), lambda j, k: (j, 0, 0))],
                       jax.ShapeDtypeStruct((NDEV, d, f), BF16), pl.BlockSpec((None, tk, f), lambda j, k: (j, k, 0)),
                       (8, LANE))

        gbuf["w_ffn_gate"] = ffn_dw(dgp, "w_ffn_gate", f"dw_ffn_gate_{l}")
        gbuf["w_ffn_up"] = ffn_dw(dup, "w_ffn_up", f"dw_ffn_up_{l}")
        tok = on_grads(l, gbuf, "ffn")
        gbuf = {}
        dx_mid, dxmb, dg2 = _rmsnorm_bwd(s["x_mid"], tied(row1(wts["norm_ffn_g"][l]), tok), [dh2], dx,
                                         f"norm_ffn_bwd_{l}")
        sg["norm_ffn_g"][l] = dg2.reshape(-1)
        rb_ = d // NDEV
        dmerged = _mm(f"d_merged_{l}", NT, (t // tm, d // tno), 0, [dxmb, gath["w_out"].reshape(d, d)],
                      [pl.BlockSpec((tm, d), lambda i, n: (i, 0)), pl.BlockSpec((tno, d), lambda i, n: (n, 0))],
                      jax.ShapeDtypeStruct((t, d), BF16), pl.BlockSpec((tm, tno), lambda i, n: (i, n)), (8, LANE))
        tkw = _tile(d, 512)
        gbuf["w_out"] = _mm(
            f"dw_out_{l}", TN, (d // tkw, d // tno), 0, [s["merged"], dxmb],
            [pl.BlockSpec((t, tkw), lambda k, n: (0, k)), pl.BlockSpec((t, tno), lambda k, n: (0, n))],
            jax.ShapeDtypeStruct((d, d), BF16), pl.BlockSpec((tkw, tno), lambda k, n: (k, n)),
            (8, LANE)).reshape(NDEV, rb_, d)
        dbr_a, dbr_b, dbr_c0, dgl, dgates, dgb = _merge_bwd(s["proj"], gbias[l], s["br_a"], s["br_b"], s["br_c0"], s["gl"],
                                                           dmerged, d, f"merge_bwd_{l}")
        sg["gate_bias"][l] = dgb.reshape(-1)
        cb = d // NDEV

        def branch_dy(dbr, wg, idx, nm, dt):
            nlead = len(idx(0, 0)) - 2
            return _mm(nm, NT, (t // tm, NDEV), 1, [dbr, wg],
                       [pl.BlockSpec((tm, cb), lambda i, j: (i, j)),
                        pl.BlockSpec((None,) * nlead + (w, cb), lambda i, j: idx(j, 0))],
                       jax.ShapeDtypeStruct((t, w), dt), pl.BlockSpec((tm, w), lambda i, j: (i, 0)), (tm, w))

        def branch_dw(y, dbr, buf, shape, idx, nm):
            nlead = len(idx(0)) - 2
            return _mm(nm, TN, (NDEV,), 0, [y, dbr] + ([] if buf is None else [buf]),
                       [pl.BlockSpec((t, w), lambda j: (0, 0)), pl.BlockSpec((t, cb), lambda j: (0, j))],
                       jax.ShapeDtypeStruct(shape, BF16),
                       pl.BlockSpec((None,) * nlead + (w, cb), lambda j: idx(j)), (8, LANE),
                       alias=None if buf is None else True)

        wb_i = lambda b: (lambda j, z=0: (j, b, 0, 0))
        glu_i = lambda j, z=0: (j, 0, 0)
        dya = branch_dy(dbr_a, gath["w_branch"], wb_i(0), f"d_ya_{l}", BF16)
        dyb = branch_dy(dbr_b, gath["w_branch"], wb_i(1), f"d_yb_{l}", BF16)
        dyc1 = branch_dy(dbr_c0, gath["w_branch"], wb_i(2), f"d_yc_{l}", F32)
        dyc2 = branch_dy(dgl, gath["ssm_w_glu"], glu_i, f"d_yc_glu_{l}", F32)
        wb_shape = (NDEV, N_BRANCH, w, cb)
        gwb = branch_dw(s["ya"], dbr_a, lax.empty(wb_shape, BF16), wb_shape, wb_i(0), f"dw_branch_a_{l}")
        gwb = branch_dw(s["yb"], dbr_b, gwb, wb_shape, wb_i(1), f"dw_branch_b_{l}")
        gbuf["w_branch"] = branch_dw(s["yc"], dbr_c0, gwb, wb_shape, wb_i(2), f"dw_branch_c_{l}")
        gbuf["ssm_w_glu"] = branch_dw(s["yc"], dgl, None, (NDEV, w, cb), glu_i, f"dw_glu_{l}")
        ds5o = _gelu_bwd_sum(s["s5o"], dyc1, dyc2, f"gelu_bwd_{l}")
        du, dbbt, dcm, dlam, dd = _s5_bwd(s["proj"], ds5o, c["bbt"], c["bbm"], c["cmt"], c["lam"], row1(wts["ssm_d"][l]),
                                          w, f"s5_bwd_{l}")
        sp = gpt * SSM_P
        dbbr_l[l] = _block_diag_extract(dbbt[:, :, :sp], gpt)
        dbbi_l[l] = _block_diag_extract(dbbt[:, :, sp:], gpt)
        dcmt = jnp.swapaxes(dcm, 1, 2)
        sg["ssm_c_re"][l] = _block_diag_extract(dcmt[:, :, :sp], gpt).reshape(-1)
        sg["ssm_c_im"][l] = (-_block_diag_extract(dcmt[:, :, sp:], gpt)).reshape(-1)
        nrow = sp // LANE
        dlr_l[l] = dlam[:, :nrow].reshape(n_g, SSM_P)
        dli_l[l] = dlam[:, nrow:].reshape(n_g, SSM_P)
        sg["ssm_d"][l] = dd.reshape(-1)
        dq, dk, dv, dbt = _attn_bwd(s["proj"], bias_tab[l], dyb, w, f"attn_bwd_{l}")
        dbias_tabs[l] = dbt
        dlx, dlg, dcw, dcb, dwa, dba, dwx, dbx, dlm = _lru_bwd(
            s["proj"], s["hs"], dya, convw[l], row1(wts["lru_conv_b"][l]), c["wa_t"], row1(wts["lru_ba"][l]), c["wx_t"],
            row1(wts["lru_bx"][l]), row1(wts["lru_lambda"][l]), c["wat_t"], c["wxt_t"], w, f"lru_bwd_{l}")
        sg["lru_conv_w"][l] = dcw.reshape(-1)
        sg["lru_conv_b"][l] = dcb.reshape(-1)
        sg["lru_wa"][l] = _block_diag_extract(dwa, bpt).reshape(-1)
        sg["lru_wx"][l] = _block_diag_extract(dwx, bpt).reshape(-1)
        sg["lru_ba"][l] = dba.reshape(-1)
        sg["lru_bx"][l] = dbx.reshape(-1)
        sg["lru_lambda"][l] = dlm.reshape(-1)
        dproj = jnp.concatenate([dlx, dlg, dq, dk, dv, du, dgates], axis=1)
        tn = _tile(ncol, 768)
        nn_ = ncol // tn
        dh1 = _mm(f"d_h1_{l}", NT, (t // tm, d // tk, NDEV), 1, [dproj, gath["w_in"]],
                  [pl.BlockSpec((tm, ncol), lambda i, k, j: (i, j)), pl.BlockSpec((None, tk, ncol), lambda i, k, j: (j, k, 0))],
                  jax.ShapeDtypeStruct((t, d), F32), pl.BlockSpec((tm, tk), lambda i, k, j: (i, k)), (tm, tk))
        gbuf["w_in"] = _mm(
            f"dw_in_{l}", TN, (NDEV, d // tk, nn_), 0, [s["h1"], dproj],
            [pl.BlockSpec((t, tk), lambda j, k, n: (0, k)), pl.BlockSpec((t, tn), lambda j, k, n: (0, j * nn_ + n))],
            jax.ShapeDtypeStruct((NDEV, d, ncol), BF16), pl.BlockSpec((None, tk, tn), lambda j, k, n: (j, k, n)),
            (8, LANE))
        tok = on_grads(l, gbuf, "rest")
        dx, dxb, dg1 = _rmsnorm_bwd(s["x"], tied(row1(wts["norm_mix_g"][l]), tok), [dh1], dx_mid, f"norm_mix_bwd_{l}")
        sg["norm_mix_g"][l] = dg1.reshape(-1)

    da_re, da_im, dls, dbtr, dbti = _s5_prep_bwd(wts["ssm_a_re"], wts["ssm_a_im"], ls3, btr, bti, jnp.stack(dlr_l),
                                                 jnp.stack(dli_l), jnp.stack(dbbr_l), jnp.stack(dbbi_l), "s5_prep_bwd")
    drel = _bias_reduce(jnp.stack(dbias_tabs), "bias_reduce")
    small = {k: jnp.stack(v) for k, v in sg.items() if v[0] is not None}
    small["ssm_a_re"] = da_re.reshape(nl, -1)
    small["ssm_a_im"] = da_im.reshape(nl, -1)
    small["ssm_log_step"] = dls.reshape(nl, -1)
    small["ssm_b_re"] = jnp.swapaxes(dbtr, 2, 3).reshape(nl, -1)
    small["ssm_b_im"] = jnp.swapaxes(dbti, 2, 3).reshape(nl, -1)
    small["attn_rel_bias"] = drel.reshape(nl, -1)
    small["norm_final_g"] = dgf.reshape(-1)
    return loss, dx, small


def kernel(x, norm_mix_g, w_in, gate_bias, lru_conv_w, lru_conv_b, lru_wa, lru_ba, lru_wx, lru_bx, lru_lambda, attn_rel_bias, ssm_a_re, ssm_a_im, ssm_b_re, ssm_b_im, ssm_c_re, ssm_c_im, ssm_d, ssm_log_step, ssm_w_glu, w_branch, w_out, norm_ffn_g, w_ffn_gate, w_ffn_up, w_ffn_down, norm_final_g, loss_target, m_norm_mix_g, m_w_in, m_gate_bias, m_lru_conv_w, m_lru_conv_b, m_lru_wa, m_lru_ba, m_lru_wx, m_lru_bx, m_lru_lambda, m_attn_rel_bias, m_ssm_a_re, m_ssm_a_im, m_ssm_b_re, m_ssm_b_im, m_ssm_c_re, m_ssm_c_im, m_ssm_d, m_ssm_log_step, m_ssm_w_glu, m_w_branch, m_w_out, m_norm_ffn_g, m_w_ffn_gate, m_w_ffn_up, m_w_ffn_down, m_norm_final_g, v_norm_mix_g, v_w_in, v_gate_bias, v_lru_conv_w, v_lru_conv_b, v_lru_wa, v_lru_ba, v_lru_wx, v_lru_bx, v_lru_lambda, v_attn_rel_bias, v_ssm_a_re, v_ssm_a_im, v_ssm_b_re, v_ssm_b_im, v_ssm_c_re, v_ssm_c_im, v_ssm_d, v_ssm_log_step, v_ssm_w_glu, v_w_branch, v_w_out, v_norm_ffn_g, v_w_ffn_gate, v_w_ffn_up, v_w_ffn_down, v_norm_final_g):
    args = locals()
    wts = {k: args[k] for k in _WEIGHTS}
    mom = {k: args["m_" + k] for k in _WEIGHTS}
    vel = {k: args["v_" + k] for k in _WEIGHTS}
    cx, cy, cc = lax.axis_index("x"), lax.axis_index("y"), lax.axis_index("c")
    me = 4 * cx + 2 * cy + cc
    nl = norm_mix_g.shape[0]

    first = _all_gather([wts[k][0].astype(BF16) for k in _BIG] + [wts[k] for k in _SMALL_SHARDED], "gather_weights_0")
    small_gath = dict(zip(_SMALL_SHARDED, first[len(_BIG):]))
    in_flight = {}
    me_arr = me.astype(jnp.int32).reshape(1)
    layer_shape = {k: wts[k].shape[1:] for k in _BIG}

    def get_gath(l, x_in):
        if l == 0:
            gath = dict(zip(_BIG, first[:len(_BIG)]))
        else:
            lands = _split_wait("all", in_flight.pop(("w", l)), x_in, f"gather_wait_{l}")
            gath = {k: a.reshape((NDEV,) + layer_shape[k]) for k, a in zip(_BIG, lands)}
        if l + 1 == nl:
            return gath, None
        after = first[0] if l == 0 else x_in
        cast = [_cast_place(wts[k].reshape(nl, -1, wts[k].shape[-1]), l + 1, me_arr, after, f"cast_{k}_{l + 1}")
                for k in _BIG]
        started = _split_start("all", [a for a, _ in cast], [b for _, b in cast], f"gather_start_{l + 1}")
        in_flight[("w", l + 1)] = started[:-1]
        return gath, started[-1]

    core = cc.astype(jnp.int32).reshape(1)
    chip = (2 * cx + cy).astype(jnp.int32).reshape(1)

    def on_grads(l, gbuf, group):
        keys = [k for k in _BIG if k in gbuf]
        own = [gbuf[k].reshape((NCHIP, 2) + gbuf[k].shape[1:]) for k in keys]
        sib = _sibling_exchange(own, f"grad_sibling_exchange_{group}_{l}")
        pair, lands = [], []
        for k, a, b in zip(keys, own, sib):
            c_ = a.shape[-1]
            p, q = _pair_sum(a.reshape(NCHIP, 2, -1, c_), b.reshape(NCHIP, -1, c_), core, chip, f"pair_sum_{k}_{l}")
            pair.append(p)
            lands.append(q)
        started = _split_start("chips", pair, lands, f"grad_start_{group}_{l}")
        in_flight[("g", l, group)] = (keys, started[:-1])
        return started[-1]

    loss, dx, small = _device_step(x[0], loss_target[0], wts, small_gath, get_gath, on_grads)

    order = _SMALL + _SMALL_SHARDED
    flat = jnp.concatenate([small[k].reshape(-1) for k in order])
    n_flat = flat.shape[0]
    flat = jnp.pad(flat, (0, (-n_flat) % (FLAT_ROWS * LANE))).reshape(1, -1, LANE)
    flat_src, flat_land = _cast_place(flat, 0, me_arr, dx, "small_grads_place", F32)
    small_started = _split_start("all", [flat_src], [flat_land], "small_grads_start")[:-1]

    res = {k: None for k in _BIG}
    for l in range(nl - 1, -1, -1):
        for group in ("ffn", "rest"):
            keys, started = in_flight.pop(("g", l, group))
            parts = _split_wait("chips", started, dx, f"grad_wait_{group}_{l}")
            for k, p in zip(keys, parts):
                c_ = wts[k].shape[-1]
                res[k] = _adamw_reduce(p.reshape(NCHIP, -1, c_), wts[k].reshape(-1, c_), mom[k].reshape(-1, c_),
                                       vel[k].reshape(-1, c_), res[k], l, f"adamw_{k}_{l}")
    out = {k: tuple(a.reshape(wts[k].shape) for a in res[k]) for k in _BIG}

    (allflat,) = _split_wait("all", small_started, res["w_out"][0], "small_grads_wait")
    gsum = _sum_rows8(allflat, "sum_small_grads").reshape(-1)
    gs, off = {}, 0
    for k in order:
        n = small[k].size
        gs[k] = gsum[off:off + n]
        off += n
    gs_loc = {}
    for k in _SMALL:
        gs_loc[k] = gs[k].reshape(wts[k].shape)
    ncb = gate_bias.shape[-1]
    gs_loc["gate_bias"] = lax.dynamic_slice_in_dim(gs["gate_bias"].reshape(nl, N_BRANCH, -1), me * ncb, ncb, axis=2)
    ncw = lru_conv_w.shape[-1]
    gs_loc["lru_conv_w"] = lax.dynamic_slice_in_dim(gs["lru_conv_w"].reshape(nl, CONV_W, -1), me * ncw, ncw, axis=2)

    def pack(dct):
        v = jnp.concatenate([dct[k].reshape(-1) for k in order])
        return jnp.pad(v, (0, (-v.shape[0]) % (FLAT_ROWS * LANE))).reshape(-1, LANE)

    dl_f, nm_f, nv_f = _adamw_flat(pack(wts), pack(gs_loc), pack(mom), pack(vel), "adamw_small")
    off = 0
    for k in order:
        n = wts[k].size
        sl = lambda a: a.reshape(-1)[off:off + n].reshape(wts[k].shape)
        out[k] = (gs_loc[k], sl(dl_f), sl(nm_f), sl(nv_f))
        off += n

    loss_total = lax.psum(loss[0, 0], ("x", "y", "c"))
    return (loss_total, dx[None], *[out[k][0] for k in _WEIGHTS], *[out[k][1] for k in _WEIGHTS],
            *[out[k][2] for k in _WEIGHTS], *[out[k][3] for k in _WEIGHTS])
```
